```python
import math
import jax, jax.numpy as jnp
from jax import lax
import numpy as np

D_MODEL = 1024
BATCH = 2
SEQ = 8192
DEPTH = 1

D_MIX = D_MODEL
SB_HEADS = 8
SB_HEAD_DIM = 64
SB_WIDTH = SB_HEADS * SB_HEAD_DIM
DF_HEADS = 4
DF_HEAD_DIM = 64
DF_V_DIM = 2 * DF_HEAD_DIM
DF_QK_WIDTH = DF_HEADS * 2 * DF_HEAD_DIM
DF_WIDTH = DF_HEADS * DF_V_DIM
IN_COLS = 4 * SB_WIDTH + 2 * DF_QK_WIDTH + 2 * DF_WIDTH
BLOCK_Q = 128
EPS = 1e-6

kernel_name = "hybrid_stickbreak_diffattn_adaln_block"


def _rmsnorm(x, g):
    xf = x.astype(jnp.float32)
    y = xf * lax.rsqrt(jnp.mean(xf * xf, axis=-1, keepdims=True) + EPS)
    return (y * g.astype(jnp.float32)).astype(x.dtype)


def _alibi_slopes(n_heads):
    return jnp.asarray([2.0 ** (-8.0 * (h + 1) / n_heads) for h in range(n_heads)], dtype=jnp.float32)


def _to_blocks(t, nb):
    b, hh, s, d = t.shape
    return t.reshape(b, hh, nb, BLOCK_Q, d).transpose(2, 0, 1, 3, 4)


def _from_blocks(o):
    nb, b, hh, q, d = o.shape
    return o.transpose(1, 0, 3, 2, 4).reshape(b, nb * q, hh * d)


def _stick_breaking(q, k, v):
    s_len = k.shape[2]
    nb = s_len // BLOCK_Q
    inv = 1.0 / math.sqrt(q.shape[-1])
    kf = k.astype(jnp.float32)
    vf = v.astype(jnp.float32)
    spos = jnp.arange(s_len, dtype=jnp.int32)

    def block(args):
        qb, t0 = args
        z = jnp.einsum('bhqd,bhkd->bhqk', qb.astype(jnp.float32), kf) * inv
        tpos = t0 + jnp.arange(BLOCK_Q, dtype=jnp.int32)
        mask = spos[None, :] < tpos[:, None]
        log_1m = jnp.where(mask, jax.nn.log_sigmoid(-z), 0.0)
        rem = lax.cumsum(log_1m, axis=3, reverse=True) - log_1m
        a = jnp.where(mask, jnp.exp(jax.nn.log_sigmoid(z) + rem), 0.0)
        return jnp.einsum('bhqk,bhkd->bhqd', a, vf)

    starts = jnp.arange(nb, dtype=jnp.int32) * BLOCK_Q
    o = lax.map(block, (_to_blocks(q, nb), starts))
    return _from_blocks(o)


def _diff_attention(q1, q2, k1, k2, v, lam, slopes):
    s_len = k1.shape[2]
    nb = s_len // BLOCK_Q
    inv = 1.0 / math.sqrt(q1.shape[-1])
    k1f = k1.astype(jnp.float32)
    k2f = k2.astype(jnp.float32)
    vf = v.astype(jnp.float32)
    spos = jnp.arange(s_len, dtype=jnp.int32)

    def block(args):
        qb1, qb2, t0 = args
        tpos = t0 + jnp.arange(BLOCK_Q, dtype=jnp.int32)
        dist = (tpos[:, None] - spos[None, :]).astype(jnp.float32)
        mask = dist >= 0.0
        bias = -slopes[:, None, None] * dist
        s1 = jnp.einsum('bhqd,bhkd->bhqk', qb1.astype(jnp.float32), k1f) * inv + bias
        s2 = jnp.einsum('bhqd,bhkd->bhqk', qb2.astype(jnp.float32), k2f) * inv + bias
        p1 = jax.nn.softmax(jnp.where(mask, s1, -jnp.inf), axis=-1)
        p2 = jax.nn.softmax(jnp.where(mask, s2, -jnp.inf), axis=-1)
        return jnp.einsum('bhqk,bhkd->bhqd', p1 - lam * p2, vf)

    starts = jnp.arange(nb, dtype=jnp.int32) * BLOCK_Q
    o = lax.map(block, (_to_blocks(q1, nb), _to_blocks(q2, nb), starts))
    return o


def _layer(x, c, layer_idx, norm_g, w_ada, b_ada, w_in, q_norm_g, k_norm_g,
           lambda_q1, lambda_k1, lambda_q2, lambda_k2, subln_g, w_out):
    b, s, _ = x.shape
    mod = (c @ w_ada + b_ada).astype(jnp.float32)
    shift, scale, gate = jnp.split(mod, 3, axis=-1)
    h = (_rmsnorm(x, norm_g).astype(jnp.float32) * (1.0 + scale[:, None, :]) + shift[:, None, :]).astype(x.dtype)

    proj = h @ w_in
    cuts = np.cumsum([SB_WIDTH, SB_WIDTH, SB_WIDTH, SB_WIDTH, DF_QK_WIDTH, DF_QK_WIDTH, DF_WIDTH])
    sb_q, sb_k, sb_v, sb_g, df_q, df_k, df_v, df_g = jnp.split(proj, [int(i) for i in cuts], axis=-1)

    def heads(t, nh, d):
        return t.reshape(b, s, nh, d).transpose(0, 2, 1, 3)
    sb_out = _stick_breaking(heads(sb_q, SB_HEADS, SB_HEAD_DIM),
                             heads(sb_k, SB_HEADS, SB_HEAD_DIM),
                             heads(sb_v, SB_HEADS, SB_HEAD_DIM))
    sb_out = sb_out * jax.nn.silu(sb_g.astype(jnp.float32))

    qd = _rmsnorm(df_q.reshape(b, s, DF_HEADS, 2, DF_HEAD_DIM), q_norm_g)
    kd = _rmsnorm(df_k.reshape(b, s, DF_HEADS, 2, DF_HEAD_DIM), k_norm_g)
    q1 = qd[:, :, :, 0].transpose(0, 2, 1, 3)
    q2 = qd[:, :, :, 1].transpose(0, 2, 1, 3)
    k1 = kd[:, :, :, 0].transpose(0, 2, 1, 3)
    k2 = kd[:, :, :, 1].transpose(0, 2, 1, 3)
    vd = heads(df_v, DF_HEADS, DF_V_DIM)
    lam_init = 0.8 - 0.6 * math.exp(-0.3 * layer_idx)
    lam = (jnp.exp(jnp.sum(lambda_q1.astype(jnp.float32) * lambda_k1.astype(jnp.float32)))
           - jnp.exp(jnp.sum(lambda_q2.astype(jnp.float32) * lambda_k2.astype(jnp.float32)))
           + lam_init)
    df_o = _diff_attention(q1, q2, k1, k2, vd, lam, _alibi_slopes(DF_HEADS))
    df_o = _rmsnorm(df_o, subln_g) * (1.0 - lam_init)
    df_out = _from_blocks(df_o) * jax.nn.silu(df_g.astype(jnp.float32))

    mixed = jnp.concatenate([sb_out, df_out], axis=-1).astype(x.dtype)
    out = (mixed @ w_out).astype(jnp.float32)
    return (x.astype(jnp.float32) + gate[:, None, :] * out).astype(x.dtype)


def setup_inputs(seed: int = 0) -> dict:
    key = jax.random.key(seed)
    ks = jax.random.split(key, 16)
    f32 = jnp.float32
    d = D_MODEL
    return {
        "x": jax.random.normal(ks[0], (BATCH, SEQ, d), f32),
        "c": jax.random.normal(ks[1], (BATCH, d), f32),
        "norm_g": 1.0 + 0.02 * jax.random.normal(ks[2], (DEPTH, d), f32),
        "w_ada": 0.5 * d ** -0.5 * jax.random.normal(ks[3], (DEPTH, d, 3 * d), f32),
        "b_ada": 0.01 * jax.random.normal(ks[4], (DEPTH, 3 * d), f32),
        "w_in": d ** -0.5 * jax.random.normal(ks[5], (DEPTH, d, IN_COLS), f32),
        "q_norm_g": 1.0 + 0.02 * jax.random.normal(ks[6], (DEPTH, DF_HEAD_DIM), f32),
        "k_norm_g": 1.0 + 0.02 * jax.random.normal(ks[7], (DEPTH, DF_HEAD_DIM), f32),
        "lambda_q1": 0.1 * jax.random.normal(ks[8], (DEPTH, DF_HEAD_DIM), f32),
        "lambda_k1": 0.1 * jax.random.normal(ks[9], (DEPTH, DF_HEAD_DIM), f32),
        "lambda_q2": 0.1 * jax.random.normal(ks[10], (DEPTH, DF_HEAD_DIM), f32),
        "lambda_k2": 0.1 * jax.random.normal(ks[11], (DEPTH, DF_HEAD_DIM), f32),
        "subln_g": 1.0 + 0.02 * jax.random.normal(ks[12], (DEPTH, DF_V_DIM), f32),
        "w_out": D_MIX ** -0.5 * jax.random.normal(ks[13], (DEPTH, D_MIX, d), f32),
    }


def reference(x, c, norm_g, w_ada, b_ada, w_in, q_norm_g, k_norm_g,
              lambda_q1, lambda_k1, lambda_q2, lambda_k2, subln_g, w_out):
    for l in range(DEPTH):
        x = _layer(x, c, l, norm_g[l], w_ada[l], b_ada[l], w_in[l], q_norm_g[l], k_norm_g[l],
                   lambda_q1[l], lambda_k1[l], lambda_q2[l], lambda_k2[l], subln_g[l], w_out[l])
    return x
```

```python
import functools
import math

import jax
import jax.numpy as jnp
from jax import lax
from jax.experimental import pallas as pl
from jax.experimental.pallas import tpu as pltpu

D_MODEL = 1024
SB_HEADS = 8
DF_HEADS = 4
HEAD_DIM = 64
LANES = 128
SB_WIDTH = SB_HEADS * HEAD_DIM
DF_WIDTH = DF_HEADS * 2 * HEAD_DIM
IN_COLS = 4 * SB_WIDTH + 4 * DF_WIDTH
EPS = 1e-6
NEG_BIG = -1e30

CB_SB_Q, CB_SB_K, CB_SB_V, CB_SB_G = 0, 4, 8, 12
CB_DF_Q, CB_DF_K, CB_DF_V, CB_DF_G = 16, 20, 24, 28

ROWS_PROJ = 512
TQ = 256
TK = 256
PROJ_CHUNK = 512
VMEM_LIMIT = 48 * 1024 * 1024

_NT = (((1,), (1,)), ((), ()))


def _silu(g):
    return g / (1.0 + jnp.exp(-g))


def _adaln_kernel(ct_ref, w_ref, b_ref, o_ref):
    w = w_ref[...]
    rows = []
    for b in range(o_ref.shape[0]):
        col = ct_ref[:, b:b + 1]
        rows.append(jnp.sum(col * w, axis=0, keepdims=True))
    o_ref[...] = jnp.concatenate(rows, axis=0) + b_ref[...]


def _adaln_mod(c, w_ada, b_ada):
    bsz, d = c.shape
    n = w_ada.shape[1]
    tn = 512
    return pl.pallas_call(
        _adaln_kernel,
        out_shape=jax.ShapeDtypeStruct((bsz, n), jnp.float32),
        grid=(n // tn,),
        in_specs=[pl.BlockSpec((d, bsz), lambda j: (0, 0)),
                  pl.BlockSpec((d, tn), lambda j: (0, j)),
                  pl.BlockSpec((1, tn), lambda j: (0, j))],
        out_specs=pl.BlockSpec((bsz, tn), lambda j: (0, j)),
        name="adaln_mod",
    )(c.T, w_ada, b_ada.reshape(1, n))


def _group_rmsnorm(y, gain):
    outs = []
    lane = lax.broadcasted_iota(jnp.int32, (1, LANES), 1)
    lo = lane < HEAD_DIM
    for j in range(y.shape[1] // LANES):
        blk = y[:, j * LANES:(j + 1) * LANES]
        sq = blk * blk
        s_lo = jnp.sum(jnp.where(lo, sq, 0.0), axis=-1, keepdims=True)
        s_hi = jnp.sum(jnp.where(lo, 0.0, sq), axis=-1, keepdims=True)
        ms = jnp.where(lo, s_lo, s_hi) * (1.0 / HEAD_DIM)
        outs.append(blk * lax.rsqrt(ms + EPS) * gain)
    return jnp.concatenate(outs, axis=1)


def _in_proj_kernel(x_ref, ng_ref, shift_ref, scale_ref, w_ref, qg_ref, kg_ref, o_ref):
    x = x_ref[0]
    ms = jnp.mean(x * x, axis=-1, keepdims=True)
    h = x * lax.rsqrt(ms + EPS) * ng_ref[...]
    h = (h * (1.0 + scale_ref[0]) + shift_ref[0]).astype(jnp.bfloat16)
    inv = 1.0 / math.sqrt(HEAD_DIM)
    for ci in range(IN_COLS // PROJ_CHUNK):
        c0 = ci * PROJ_CHUNK
        y = jnp.dot(h, w_ref[:, c0:c0 + PROJ_CHUNK], preferred_element_type=jnp.float32)
        cb = c0 // LANES
        if cb == CB_SB_Q:
            y = y * inv
        elif cb == CB_DF_Q:
            y = _group_rmsnorm(y, qg_ref[...]) * inv
        elif cb == CB_DF_K:
            y = _group_rmsnorm(y, kg_ref[...])
        o_ref[0, :, c0:c0 + PROJ_CHUNK] = y.astype(o_ref.dtype)


def _in_proj(x, norm_g, shift, scale, w_in_bf16, q_norm_g, k_norm_g):
    bsz, s, d = x.shape
    tm = ROWS_PROJ
    qg = jnp.tile(q_norm_g.reshape(1, HEAD_DIM), (1, 2))
    kg = jnp.tile(k_norm_g.reshape(1, HEAD_DIM), (1, 2))
    return pl.pallas_call(
        _in_proj_kernel,
        out_shape=jax.ShapeDtypeStruct((bsz, s, IN_COLS), jnp.bfloat16),
        grid=(bsz, s // tm),
        in_specs=[pl.BlockSpec((1, tm, d), lambda b, i: (b, i, 0)),
                  pl.BlockSpec((1, d), lambda b, i: (0, 0)),
                  pl.BlockSpec((1, 1, d), lambda b, i: (b, 0, 0)),
                  pl.BlockSpec((1, 1, d), lambda b, i: (b, 0, 0)),
                  pl.BlockSpec((d, IN_COLS), lambda b, i: (0, 0)),
                  pl.BlockSpec((1, LANES), lambda b, i: (0, 0)),
                  pl.BlockSpec((1, LANES), lambda b, i: (0, 0))],
        out_specs=pl.BlockSpec((1, tm, IN_COLS), lambda b, i: (b, i, 0)),
        compiler_params=pltpu.CompilerParams(vmem_limit_bytes=VMEM_LIMIT),
        name="in_proj",
    )(x, norm_g.reshape(1, d), shift.reshape(bsz, 1, d), scale.reshape(bsz, 1, d),
      w_in_bf16, qg, kg)


def _sb_tile(qm, kblk, vblk, cum, carry, mask):
    z = lax.dot_general(qm, kblk, _NT, preferred_element_type=jnp.float32)
    l1p = jnp.log(1.0 + jnp.exp(-jnp.abs(z)))
    lg = jnp.minimum(-z, 0.0) - l1p
    if mask is not None:
        lg = jnp.where(mask, lg, 0.0)
    lg_hi = lg.astype(jnp.bfloat16)
    lg_lo = (lg - lg_hi.astype(jnp.float32)).astype(jnp.bfloat16)
    rem = (jnp.dot(lg_hi, cum, preferred_element_type=jnp.float32)
           + jnp.dot(lg_lo, cum, preferred_element_type=jnp.float32))
    a = jnp.exp(lg + z + rem + carry)
    if mask is not None:
        a = jnp.where(mask, a, 0.0)
    o = jnp.dot(a.astype(jnp.bfloat16), vblk, preferred_element_type=jnp.float32)
    return o, carry + jnp.sum(lg, axis=-1, keepdims=True)


def _sb_kernel(q_ref, k_ref, v_ref, g_ref, o_ref, acc_a, acc_b):
    qi = pl.program_id(2)
    q = q_ref[0]
    lane = lax.broadcasted_iota(jnp.int32, (1, LANES), 1)
    first = lane < HEAD_DIM
    zero = jnp.zeros_like(q)
    qa = jnp.where(first, q, zero)
    qb = jnp.where(first, zero, q)
    row = lax.broadcasted_iota(jnp.int32, (TK, TK), 0)
    col = lax.broadcasted_iota(jnp.int32, (TK, TK), 1)
    cum = (row > col).astype(jnp.bfloat16)
    diag_mask = col < row

    def tile(kb, ca, cb, mask):
        start = pl.multiple_of(kb * TK, TK)
        kblk = k_ref[0, pl.ds(start, TK), :]
        vblk = v_ref[0, pl.ds(start, TK), :]
        oa, ca = _sb_tile(qa, kblk, vblk, cum, ca, mask)
        ob, cb = _sb_tile(qb, kblk, vblk, cum, cb, mask)
        return oa, ob, ca, cb

    c0 = jnp.zeros((TQ, 1), jnp.float32)
    oa, ob, ca, cb = tile(qi, c0, c0, diag_mask)
    acc_a[...] = oa
    acc_b[...] = ob

    def body(j, carry):
        ca, cb = carry
        oa, ob, ca, cb = tile(qi - j, ca, cb, None)
        acc_a[...] += oa
        acc_b[...] += ob
        return ca, cb

    lax.fori_loop(1, qi + 1, body, (ca, cb))
    out = jnp.where(first, acc_a[...], acc_b[...])
    g = g_ref[0].astype(jnp.float32)
    o_ref[0] = (out * _silu(g)).astype(o_ref.dtype)


def _sb_attention(proj):
    bsz, s, _ = proj.shape
    nq = s // TQ
    return pl.pallas_call(
        _sb_kernel,
        out_shape=jax.ShapeDtypeStruct((bsz, s, SB_WIDTH), jnp.bfloat16),
        grid=(bsz, SB_WIDTH // LANES, nq),
        in_specs=[pl.BlockSpec((1, TQ, LANES), lambda b, h, i: (b, i, CB_SB_Q + h)),
                  pl.BlockSpec((1, s, LANES), lambda b, h, i: (b, 0, CB_SB_K + h)),
                  pl.BlockSpec((1, s, LANES), lambda b, h, i: (b, 0, CB_SB_V + h)),
                  pl.BlockSpec((1, TQ, LANES), lambda b, h, i: (b, i, CB_SB_G + h))],
        out_specs=pl.BlockSpec((1, TQ, LANES), lambda b, h, i: (b, i, h)),
        scratch_shapes=[pltpu.VMEM((TQ, LANES), jnp.float32),
                        pltpu.VMEM((TQ, LANES), jnp.float32)],
        compiler_params=pltpu.CompilerParams(vmem_limit_bytes=VMEM_LIMIT),
        name="sb_attn",
    )(proj, proj, proj, proj)


def _df_kernel(q_ref, k_ref, v_ref, g_ref, slope_ref, lq1_ref, lk1_ref, lq2_ref, lk2_ref, sg_ref,
               o_ref, m1_s, l1_s, acc1, m2_s, l2_s, acc2, *, lam_init):
    qi = pl.program_id(2)
    q = q_ref[0]
    lane = lax.broadcasted_iota(jnp.int32, (1, LANES), 1)
    first = lane < HEAD_DIM
    zero = jnp.zeros_like(q)
    q1 = jnp.where(first, q, zero)
    q2 = jnp.where(first, zero, q)
    slope = slope_ref[0]
    rel = lax.broadcasted_iota(jnp.int32, (1, TK), 1).astype(jnp.float32)
    slope_row = jnp.concatenate([slope] * (TK // LANES), axis=1)
    row = lax.broadcasted_iota(jnp.int32, (TQ, TK), 0)
    col = lax.broadcasted_iota(jnp.int32, (TQ, TK), 1)
    diag_mask = col <= row

    m1_s[...] = jnp.full((TQ, 1), NEG_BIG, jnp.float32)
    m2_s[...] = jnp.full((TQ, 1), NEG_BIG, jnp.float32)
    l1_s[...] = jnp.zeros((TQ, 1), jnp.float32)
    l2_s[...] = jnp.zeros((TQ, 1), jnp.float32)
    acc1[...] = jnp.zeros((TQ, LANES), jnp.float32)
    acc2[...] = jnp.zeros((TQ, LANES), jnp.float32)

    def one_map(qm, kblk, vblk, bias, mask, m_s, l_s, acc):
        s = lax.dot_general(qm, kblk, _NT, preferred_element_type=jnp.float32) + bias
        if mask is not None:
            s = jnp.where(mask, s, NEG_BIG)
        m_old = m_s[...]
        m_new = jnp.maximum(m_old, jnp.max(s, axis=-1, keepdims=True))
        p = jnp.exp(s - m_new)
        alpha = jnp.exp(m_old - m_new)
        l_s[...] = alpha * l_s[...] + jnp.sum(p, axis=-1, keepdims=True)
        acc[...] = alpha * acc[...] + jnp.dot(p.astype(jnp.bfloat16), vblk,
                                              preferred_element_type=jnp.float32)
        m_s[...] = m_new

    def tile(j, mask):
        kb = qi - j
        start = pl.multiple_of(kb * TK, TK)
        kblk = k_ref[0, pl.ds(start, TK), :]
        vblk = v_ref[0, pl.ds(start, TK), :]
        off = (-(j * TK)).astype(jnp.float32)
        bias = slope_row * (rel + off)
        one_map(q1, kblk, vblk, bias, mask, m1_s, l1_s, acc1)
        one_map(q2, kblk, vblk, bias, mask, m2_s, l2_s, acc2)

    tile(jnp.int32(0), diag_mask)

    def body(j, carry):
        tile(j, None)
        return carry

    lax.fori_loop(1, qi + 1, body, 0)

    lam = (jnp.exp(jnp.sum(lq1_ref[...] * lk1_ref[...], keepdims=True))
           - jnp.exp(jnp.sum(lq2_ref[...] * lk2_ref[...], keepdims=True)) + lam_init)
    o = acc1[...] / l1_s[...] - lam * (acc2[...] / l2_s[...])
    ms = jnp.mean(o * o, axis=-1, keepdims=True)
    o = o * lax.rsqrt(ms + EPS) * sg_ref[...] * (1.0 - lam_init)
    g = g_ref[0].astype(jnp.float32)
    o_ref[0] = (o * _silu(g)).astype(o_ref.dtype)


def _df_attention(proj, slopes, lq1, lk1, lq2, lk2, subln_g, lam_init):
    bsz, s, _ = proj.shape
    nq = s // TQ
    vec = pl.BlockSpec((1, HEAD_DIM), lambda b, h, i: (0, 0))
    col = pltpu.VMEM((TQ, 1), jnp.float32)
    wide = pltpu.VMEM((TQ, LANES), jnp.float32)
    return pl.pallas_call(
        functools.partial(_df_kernel, lam_init=lam_init),
        out_shape=jax.ShapeDtypeStruct((bsz, s, DF_WIDTH), jnp.bfloat16),
        grid=(bsz, DF_HEADS, nq),
        in_specs=[pl.BlockSpec((1, TQ, LANES), lambda b, h, i: (b, i, CB_DF_Q + h)),
                  pl.BlockSpec((1, s, LANES), lambda b, h, i: (b, 0, CB_DF_K + h)),
                  pl.BlockSpec((1, s, LANES), lambda b, h, i: (b, 0, CB_DF_V + h)),
                  pl.BlockSpec((1, TQ, LANES), lambda b, h, i: (b, i, CB_DF_G + h)),
                  pl.BlockSpec((1, 1, LANES), lambda b, h, i: (h, 0, 0)),
                  vec, vec, vec, vec,
                  pl.BlockSpec((1, LANES), lambda b, h, i: (0, 0))],
        out_specs=pl.BlockSpec((1, TQ, LANES), lambda b, h, i: (b, i, h)),
        scratch_shapes=[col, col, wide, col, col, wide],
        compiler_params=pltpu.CompilerParams(vmem_limit_bytes=VMEM_LIMIT),
        name="df_attn",
    )(proj, proj, proj, proj, slopes, lq1, lk1, lq2, lk2, subln_g)


def _out_proj_kernel(x_ref, sb_ref, df_ref, w_ref, gate_ref, o_ref):
    y = (jnp.dot(sb_ref[0], w_ref[:SB_WIDTH, :], preferred_element_type=jnp.float32)
         + jnp.dot(df_ref[0], w_ref[SB_WIDTH:, :], preferred_element_type=jnp.float32))
    o_ref[0] = x_ref[0] + gate_ref[0] * y


def _out_proj(x, sb_o, df_o, w_out_bf16, gate):
    bsz, s, d = x.shape
    tm = ROWS_PROJ
    return pl.pallas_call(
        _out_proj_kernel,
        out_shape=jax.ShapeDtypeStruct((bsz, s, d), jnp.float32),
        grid=(bsz, s // tm),
        in_specs=[pl.BlockSpec((1, tm, d), lambda b, i: (b, i, 0)),
                  pl.BlockSpec((1, tm, SB_WIDTH), lambda b, i: (b, i, 0)),
                  pl.BlockSpec((1, tm, DF_WIDTH), lambda b, i: (b, i, 0)),
                  pl.BlockSpec((SB_WIDTH + DF_WIDTH, d), lambda b, i: (0, 0)),
                  pl.BlockSpec((1, 1, d), lambda b, i: (b, 0, 0))],
        out_specs=pl.BlockSpec((1, tm, d), lambda b, i: (b, i, 0)),
        compiler_params=pltpu.CompilerParams(vmem_limit_bytes=VMEM_LIMIT),
        name="out_proj",
    )(x, sb_o, df_o, w_out_bf16, gate.reshape(bsz, 1, d))


def _layer(x, c, layer_idx, norm_g, w_ada, b_ada, w_in, q_norm_g, k_norm_g,
           lambda_q1, lambda_k1, lambda_q2, lambda_k2, subln_g, w_out):
    d = x.shape[-1]
    lam_init = 0.8 - 0.6 * math.exp(-0.3 * layer_idx)
    mod = _adaln_mod(c, w_ada, b_ada)
    shift, scale, gate = mod[:, :d], mod[:, d:2 * d], mod[:, 2 * d:]
    proj = _in_proj(x, norm_g, shift, scale, w_in.astype(jnp.bfloat16), q_norm_g, k_norm_g)
    sb_o = _sb_attention(proj)
    slopes = jnp.asarray([2.0 ** (-8.0 * (h + 1) / DF_HEADS) for h in range(DF_HEADS)], jnp.float32)
    slopes = jnp.broadcast_to(slopes[:, None, None], (DF_HEADS, 1, LANES))
    row = lambda v: v.reshape(1, -1)
    df_o = _df_attention(proj, slopes, row(lambda_q1), row(lambda_k1), row(lambda_q2),
                         row(lambda_k2), row(subln_g), lam_init)
    return _out_proj(x, sb_o, df_o, w_out.astype(jnp.bfloat16), gate)


@jax.jit
def kernel(x, c, norm_g, w_ada, b_ada, w_in, q_norm_g, k_norm_g, lambda_q1, lambda_k1,
           lambda_q2, lambda_k2, subln_g, w_out):
    for l in range(norm_g.shape[0]):
        x = _layer(x, c, l, norm_g[l], w_ada[l], b_ada[l], w_in[l], q_norm_g[l], k_norm_g[l],
                   lambda_q1[l], lambda_k1[l], lambda_q2[l], lambda_k2[l], subln_g[l], w_out[l])
    return x
```

```python
import functools
import math

import jax
import jax.numpy as jnp
from jax import lax
from jax.experimental import pallas as pl
from jax.experimental.pallas import tpu as pltpu

D_MODEL = 1024
SB_HEADS = 8
DF_HEADS = 4
HEAD_DIM = 64
LANES = 128
SB_WIDTH = SB_HEADS * HEAD_DIM
DF_WIDTH = DF_HEADS * 2 * HEAD_DIM
IN_COLS = 4 * SB_WIDTH + 4 * DF_WIDTH
EPS = 1e-6
NEG_BIG = -1e30

CB_SB_Q, CB_SB_K, CB_SB_V, CB_SB_G = 0, 4, 8, 12
CB_DF_Q, CB_DF_K, CB_DF_V, CB_DF_G = 16, 20, 24, 28

ROWS_PROJ = 512
TQ = 256
TK = 256
SB_GROUP = 4
DF_GROUP = 4
LOG2E = math.log2(math.e)
PROJ_CHUNK = 512
VMEM_LIMIT = 48 * 1024 * 1024

_NT = (((1,), (1,)), ((), ()))


def _silu(g):
    return g / (1.0 + jnp.exp(-g))


def _adaln_kernel(ct_ref, w_ref, b_ref, o_ref):
    w = w_ref[...]
    rows = []
    for b in range(o_ref.shape[0]):
        col = ct_ref[:, b:b + 1]
        rows.append(jnp.sum(col * w, axis=0, keepdims=True))
    o_ref[...] = jnp.concatenate(rows, axis=0) + b_ref[...]


def _adaln_mod(c, w_ada, b_ada):
    bsz, d = c.shape
    n = w_ada.shape[1]
    tn = 512
    return pl.pallas_call(
        _adaln_kernel,
        out_shape=jax.ShapeDtypeStruct((bsz, n), jnp.float32),
        grid=(n // tn,),
        in_specs=[pl.BlockSpec((d, bsz), lambda j: (0, 0)),
                  pl.BlockSpec((d, tn), lambda j: (0, j)),
                  pl.BlockSpec((1, tn), lambda j: (0, j))],
        out_specs=pl.BlockSpec((bsz, tn), lambda j: (0, j)),
        name="adaln_mod",
    )(c.T, w_ada, b_ada.reshape(1, n))


def _group_rmsnorm(y, gain):
    outs = []
    lane = lax.broadcasted_iota(jnp.int32, (1, LANES), 1)
    lo = lane < HEAD_DIM
    for j in range(y.shape[1] // LANES):
        blk = y[:, j * LANES:(j + 1) * LANES]
        sq = blk * blk
        s_lo = jnp.sum(jnp.where(lo, sq, 0.0), axis=-1, keepdims=True)
        s_hi = jnp.sum(jnp.where(lo, 0.0, sq), axis=-1, keepdims=True)
        ms = jnp.where(lo, s_lo, s_hi) * (1.0 / HEAD_DIM)
        outs.append(blk * lax.rsqrt(ms + EPS) * gain)
    return jnp.concatenate(outs, axis=1)


def _in_proj_kernel(x_ref, ng_ref, shift_ref, scale_ref, w_ref, qg_ref, kg_ref, o_ref):
    x = x_ref[0]
    ms = jnp.mean(x * x, axis=-1, keepdims=True)
    h = x * lax.rsqrt(ms + EPS) * ng_ref[...]
    h = (h * (1.0 + scale_ref[0]) + shift_ref[0]).astype(jnp.bfloat16)
    inv = LOG2E / math.sqrt(HEAD_DIM)
    for ci in range(IN_COLS // PROJ_CHUNK):
        c0 = ci * PROJ_CHUNK
        y = jnp.dot(h, w_ref[:, c0:c0 + PROJ_CHUNK], preferred_element_type=jnp.float32)
        cb = c0 // LANES
        if cb == CB_SB_Q:
            y = y * inv
        elif cb == CB_DF_Q:
            y = _group_rmsnorm(y, qg_ref[...]) * inv
        elif cb == CB_DF_K:
            y = _group_rmsnorm(y, kg_ref[...])
        o_ref[0, :, c0:c0 + PROJ_CHUNK] = y.astype(o_ref.dtype)


def _in_proj(x, norm_g, shift, scale, w_in_bf16, q_norm_g, k_norm_g):
    bsz, s, d = x.shape
    tm = ROWS_PROJ
    qg = jnp.tile(q_norm_g.reshape(1, HEAD_DIM), (1, 2))
    kg = jnp.tile(k_norm_g.reshape(1, HEAD_DIM), (1, 2))
    return pl.pallas_call(
        _in_proj_kernel,
        out_shape=jax.ShapeDtypeStruct((bsz, s, IN_COLS), jnp.bfloat16),
        grid=(bsz, s // tm),
        in_specs=[pl.BlockSpec((1, tm, d), lambda b, i: (b, i, 0)),
                  pl.BlockSpec((1, d), lambda b, i: (0, 0)),
                  pl.BlockSpec((1, 1, d), lambda b, i: (b, 0, 0)),
                  pl.BlockSpec((1, 1, d), lambda b, i: (b, 0, 0)),
                  pl.BlockSpec((d, IN_COLS), lambda b, i: (0, 0)),
                  pl.BlockSpec((1, LANES), lambda b, i: (0, 0)),
                  pl.BlockSpec((1, LANES), lambda b, i: (0, 0))],
        out_specs=pl.BlockSpec((1, tm, IN_COLS), lambda b, i: (b, i, 0)),
        compiler_params=pltpu.CompilerParams(vmem_limit_bytes=VMEM_LIMIT),
        name="in_proj",
    )(x, norm_g.reshape(1, d), shift.reshape(bsz, 1, d), scale.reshape(bsz, 1, d),
      w_in_bf16, qg, kg)


def _sb_tile(qm, kblk, vblk, cum, carry, mask):
    z = lax.dot_general(qm, kblk, _NT, preferred_element_type=jnp.float32)
    nz = -z
    l1p = jnp.log(1.0 + jnp.exp2(jnp.minimum(z, nz))) * LOG2E
    lg = jnp.minimum(nz, 0.0) - l1p
    if mask is not None:
        lg = jnp.where(mask, lg, 0.0)
    rem = jnp.dot(lg.astype(jnp.bfloat16), cum, preferred_element_type=jnp.float32)
    a = jnp.exp2(lg + z + rem + carry)
    if mask is not None:
        a = jnp.where(mask, a, 0.0)
    o = jnp.dot(a.astype(jnp.bfloat16), vblk, preferred_element_type=jnp.float32)
    return o, carry + jnp.sum(lg, axis=-1, keepdims=True)


def _sb_kernel(q_ref, k_ref, v_ref, g_ref, o_ref, acc):
    qi = pl.program_id(2)
    q = q_ref[0]
    lane = lax.broadcasted_iota(jnp.int32, (1, LANES), 1)
    first = lane < HEAD_DIM
    zero = jnp.zeros_like(q)
    qs = jnp.concatenate([jnp.where(first, q, zero), jnp.where(first, zero, q)], axis=0)
    row = lax.broadcasted_iota(jnp.int32, (TK, TK), 0)
    col = lax.broadcasted_iota(jnp.int32, (TK, TK), 1)
    cum = (row > col).astype(jnp.bfloat16)
    strict = col < row
    diag_mask = jnp.concatenate([strict, strict], axis=0)

    def tiles(kbs, carry, mask):
        total = None
        for kb in kbs:
            start = pl.multiple_of(kb * TK, TK)
            o, carry = _sb_tile(qs, k_ref[0, pl.ds(start, TK), :], v_ref[0, pl.ds(start, TK), :],
                                cum, carry, mask)
            total = o if total is None else total + o
        return total, carry

    o, carry = tiles([qi], jnp.zeros((2 * TQ, 1), jnp.float32), diag_mask)
    acc[...] = o

    def single(j, carry):
        o, carry = tiles([qi - j], carry, None)
        acc[...] += o
        return carry

    rem = lax.rem(qi, SB_GROUP)
    carry = lax.fori_loop(1, rem + 1, single, carry)

    def group(g, carry):
        base = qi - rem - g * SB_GROUP
        o, carry = tiles([base - 1 - u for u in range(SB_GROUP)], carry, None)
        acc[...] += o
        return carry

    lax.fori_loop(0, qi // SB_GROUP, group, carry)
    out = jnp.where(first, acc[:TQ, :], acc[TQ:, :])
    g = g_ref[0].astype(jnp.float32)
    o_ref[0] = (out * _silu(g)).astype(o_ref.dtype)


def _sb_attention(proj):
    bsz, s, _ = proj.shape
    nq = s // TQ
    return pl.pallas_call(
        _sb_kernel,
        out_shape=jax.ShapeDtypeStruct((bsz, s, SB_WIDTH), jnp.bfloat16),
        grid=(bsz, SB_WIDTH // LANES, nq),
        in_specs=[pl.BlockSpec((1, TQ, LANES), lambda b, h, i: (b, i, CB_SB_Q + h)),
                  pl.BlockSpec((1, s, LANES), lambda b, h, i: (b, 0, CB_SB_K + h)),
                  pl.BlockSpec((1, s, LANES), lambda b, h, i: (b, 0, CB_SB_V + h)),
                  pl.BlockSpec((1, TQ, LANES), lambda b, h, i: (b, i, CB_SB_G + h))],
        out_specs=pl.BlockSpec((1, TQ, LANES), lambda b, h, i: (b, i, h)),
        scratch_shapes=[pltpu.VMEM((2 * TQ, LANES), jnp.float32)],
        compiler_params=pltpu.CompilerParams(vmem_limit_bytes=VMEM_LIMIT),
        name="sb_attn",
    )(proj, proj, proj, proj)


def _df_kernel(q_ref, k_ref, v_ref, g_ref, slope_ref, lq1_ref, lk1_ref, lq2_ref, lk2_ref, sg_ref,
               o_ref, m_s, l_s, acc, *, lam_init):
    qi = pl.program_id(2)
    q = q_ref[0]
    lane = lax.broadcasted_iota(jnp.int32, (1, LANES), 1)
    first = lane < HEAD_DIM
    zero = jnp.zeros_like(q)
    qs = jnp.concatenate([jnp.where(first, q, zero), jnp.where(first, zero, q)], axis=0)
    slope = slope_ref[0] * LOG2E
    row = lax.broadcasted_iota(jnp.int32, (TQ, TK), 0)
    col = lax.broadcasted_iota(jnp.int32, (TQ, TK), 1)
    causal = col <= row
    diag_mask = jnp.concatenate([causal, causal], axis=0)

    m_s[...] = jnp.full((2 * TQ, LANES), NEG_BIG, jnp.float32)
    l_s[...] = jnp.zeros((2 * TQ, LANES), jnp.float32)
    acc[...] = jnp.zeros((2 * TQ, LANES), jnp.float32)

    def tile(first_kb, width, mask):
        reps = width // LANES
        start = pl.multiple_of(first_kb * TK, TK)
        kblk = k_ref[0, pl.ds(start, width), :]
        vblk = v_ref[0, pl.ds(start, width), :]
        rel = lax.broadcasted_iota(jnp.int32, (1, width), 1).astype(jnp.float32)
        off = ((first_kb - qi) * TK).astype(jnp.float32)
        bias = jnp.concatenate([slope] * reps, axis=1) * (rel + off)
        s = lax.dot_general(qs, kblk, _NT, preferred_element_type=jnp.float32) + bias
        if mask is not None:
            s = jnp.where(mask, s, NEG_BIG)
        m_old = m_s[...]
        m_new = jnp.maximum(m_old, jnp.max(s, axis=-1, keepdims=True))
        p = jnp.exp2(s - jnp.concatenate([m_new] * reps, axis=1))
        alpha = jnp.exp2(m_old - m_new)
        l_s[...] = alpha * l_s[...] + jnp.sum(p, axis=-1, keepdims=True)
        acc[...] = alpha * acc[...] + jnp.dot(p.astype(jnp.bfloat16), vblk,
                                              preferred_element_type=jnp.float32)
        m_s[...] = m_new

    tile(qi, TK, diag_mask)

    def single(j, carry):
        tile(qi - j, TK, None)
        return carry

    rem = lax.rem(qi, DF_GROUP)
    lax.fori_loop(1, rem + 1, single, 0)

    def group(g, carry):
        tile(qi - rem - (g + 1) * DF_GROUP, DF_GROUP * TK, None)
        return carry

    lax.fori_loop(0, qi // DF_GROUP, group, 0)

    lam = (jnp.exp(jnp.sum(lq1_ref[...] * lk1_ref[...], keepdims=True))
           - jnp.exp(jnp.sum(lq2_ref[...] * lk2_ref[...], keepdims=True)) + lam_init)
    o = acc[:TQ, :] / l_s[:TQ, :] - lam * (acc[TQ:, :] / l_s[TQ:, :])
    ms = jnp.mean(o * o, axis=-1, keepdims=True)
    o = o * lax.rsqrt(ms + EPS) * sg_ref[...] * (1.0 - lam_init)
    g = g_ref[0].astype(jnp.float32)
    o_ref[0] = (o * _silu(g)).astype(o_ref.dtype)


def _df_attention(proj, slopes, lq1, lk1, lq2, lk2, subln_g, lam_init):
    bsz, s, _ = proj.shape
    nq = s // TQ
    vec = pl.BlockSpec((1, HEAD_DIM), lambda b, h, i: (0, 0))
    stat = pltpu.VMEM((2 * TQ, LANES), jnp.float32)
    return pl.pallas_call(
        functools.partial(_df_kernel, lam_init=lam_init),
        out_shape=jax.ShapeDtypeStruct((bsz, s, DF_WIDTH), jnp.bfloat16),
        grid=(bsz, DF_HEADS, nq),
        in_specs=[pl.BlockSpec((1, TQ, LANES), lambda b, h, i: (b, i, CB_DF_Q + h)),
                  pl.BlockSpec((1, s, LANES), lambda b, h, i: (b, 0, CB_DF_K + h)),
                  pl.BlockSpec((1, s, LANES), lambda b, h, i: (b, 0, CB_DF_V + h)),
                  pl.BlockSpec((1, TQ, LANES), lambda b, h, i: (b, i, CB_DF_G + h)),
                  pl.BlockSpec((1, 1, LANES), lambda b, h, i: (h, 0, 0)),
                  vec, vec, vec, vec,
                  pl.BlockSpec((1, LANES), lambda b, h, i: (0, 0))],
        out_specs=pl.BlockSpec((1, TQ, LANES), lambda b, h, i: (b, i, h)),
        scratch_shapes=[stat, stat, stat],
        compiler_params=pltpu.CompilerParams(vmem_limit_bytes=VMEM_LIMIT),
        name="df_attn",
    )(proj, proj, proj, proj, slopes, lq1, lk1, lq2, lk2, subln_g)


def _out_proj_kernel(x_ref, sb_ref, df_ref, w_ref, gate_ref, o_ref):
    y = (jnp.dot(sb_ref[0], w_ref[:SB_WIDTH, :], preferred_element_type=jnp.float32)
         + jnp.dot(df_ref[0], w_ref[SB_WIDTH:, :], preferred_element_type=jnp.float32))
    o_ref[0] = x_ref[0] + gate_ref[0] * y


def _out_proj(x, sb_o, df_o, w_out_bf16, gate):
    bsz, s, d = x.shape
    tm = ROWS_PROJ
    return pl.pallas_call(
        _out_proj_kernel,
        out_shape=jax.ShapeDtypeStruct((bsz, s, d), jnp.float32),
        grid=(bsz, s // tm),
        in_specs=[pl.BlockSpec((1, tm, d), lambda b, i: (b, i, 0)),
                  pl.BlockSpec((1, tm, SB_WIDTH), lambda b, i: (b, i, 0)),
                  pl.BlockSpec((1, tm, DF_WIDTH), lambda b, i: (b, i, 0)),
                  pl.BlockSpec((SB_WIDTH + DF_WIDTH, d), lambda b, i: (0, 0)),
                  pl.BlockSpec((1, 1, d), lambda b, i: (b, 0, 0))],
        out_specs=pl.BlockSpec((1, tm, d), lambda b, i: (b, i, 0)),
        compiler_params=pltpu.CompilerParams(vmem_limit_bytes=VMEM_LIMIT),
        name="out_proj",
    )(x, sb_o, df_o, w_out_bf16, gate.reshape(bsz, 1, d))


def _layer(x, c, layer_idx, norm_g, w_ada, b_ada, w_in, q_norm_g, k_norm_g,
           lambda_q1, lambda_k1, lambda_q2, lambda_k2, subln_g, w_out):
    d = x.shape[-1]
    lam_init = 0.8 - 0.6 * math.exp(-0.3 * layer_idx)
    mod = _adaln_mod(c, w_ada, b_ada)
    shift, scale, gate = mod[:, :d], mod[:, d:2 * d], mod[:, 2 * d:]
    proj = _in_proj(x, norm_g, shift, scale, w_in.astype(jnp.bfloat16), q_norm_g, k_norm_g)
    sb_o = _sb_attention(proj)
    slopes = jnp.asarray([2.0 ** (-8.0 * (h + 1) / DF_HEADS) for h in range(DF_HEADS)], jnp.float32)
    slopes = jnp.broadcast_to(slopes[:, None, None], (DF_HEADS, 1, LANES))
    row = lambda v: v.reshape(1, -1)
    df_o = _df_attention(proj, slopes, row(lambda_q1), row(lambda_k1), row(lambda_q2),
                         row(lambda_k2), row(subln_g), lam_init)
    return _out_proj(x, sb_o, df_o, w_out.astype(jnp.bfloat16), gate)


@jax.jit
def kernel(x, c, norm_g, w_ada, b_ada, w_in, q_norm_g, k_norm_g, lambda_q1, lambda_k1,
           lambda_q2, lambda_k2, subln_g, w_out):
    for l in range(norm_g.shape[0]):
        x = _layer(x, c, l, norm_g[l], w_ada[l], b_ada[l], w_in[l], q_norm_g[l], k_norm_g[l],
                   lambda_q1[l], lambda_k1[l], lambda_q2[l], lambda_k2[l], subln_g[l], w_out[l])
    return x
```

```python
import functools
import math

import jax
import jax.numpy as jnp
from jax import lax
from jax.experimental import pallas as pl
from jax.experimental.pallas import tpu as pltpu

D_MODEL = 1024
SB_HEADS = 8
DF_HEADS = 4
HEAD_DIM = 64
LANES = 128
SB_WIDTH = SB_HEADS * HEAD_DIM
DF_WIDTH = DF_HEADS * 2 * HEAD_DIM
IN_COLS = 4 * SB_WIDTH + 4 * DF_WIDTH
EPS = 1e-6
NEG_BIG = -1e30

CB_SB_Q, CB_SB_K, CB_SB_V, CB_SB_G = 0, 4, 8, 12
CB_DF_Q, CB_DF_K, CB_DF_V, CB_DF_G = 16, 20, 24, 28

ROWS_PROJ = 512
TQ = 256
TK = 256
SB_GROUP = 4
DF_GROUP = 4
LOG2E = math.log2(math.e)
SB_DEAD_LOG2 = -150.0
DF_DEAD_LOG2 = -150.0
DF_BOUND_MARGIN = 1.02
PROJ_CHUNK = 512
VMEM_LIMIT = 48 * 1024 * 1024

_NT = (((1,), (1,)), ((), ()))


def _silu(g):
    return g / (1.0 + jnp.exp(-g))


def _adaln_kernel(ct_ref, w_ref, b_ref, o_ref):
    w = w_ref[...]
    rows = []
    for b in range(o_ref.shape[0]):
        col = ct_ref[:, b:b + 1]
        rows.append(jnp.sum(col * w, axis=0, keepdims=True))
    o_ref[...] = jnp.concatenate(rows, axis=0) + b_ref[...]


def _adaln_mod(c, w_ada, b_ada):
    bsz, d = c.shape
    n = w_ada.shape[1]
    tn = 512
    return pl.pallas_call(
        _adaln_kernel,
        out_shape=jax.ShapeDtypeStruct((bsz, n), jnp.float32),
        grid=(n // tn,),
        in_specs=[pl.BlockSpec((d, bsz), lambda j: (0, 0)),
                  pl.BlockSpec((d, tn), lambda j: (0, j)),
                  pl.BlockSpec((1, tn), lambda j: (0, j))],
        out_specs=pl.BlockSpec((bsz, tn), lambda j: (0, j)),
        name="adaln_mod",
    )(c.T, w_ada, b_ada.reshape(1, n))


def _group_rmsnorm(y, gain):
    outs = []
    lane = lax.broadcasted_iota(jnp.int32, (1, LANES), 1)
    lo = lane < HEAD_DIM
    for j in range(y.shape[1] // LANES):
        blk = y[:, j * LANES:(j + 1) * LANES]
        sq = blk * blk
        s_lo = jnp.sum(jnp.where(lo, sq, 0.0), axis=-1, keepdims=True)
        s_hi = jnp.sum(jnp.where(lo, 0.0, sq), axis=-1, keepdims=True)
        ms = jnp.where(lo, s_lo, s_hi) * (1.0 / HEAD_DIM)
        outs.append(blk * lax.rsqrt(ms + EPS) * gain)
    return jnp.concatenate(outs, axis=1)


def _in_proj_kernel(x_ref, ng_ref, shift_ref, scale_ref, w_ref, qg_ref, kg_ref, o_ref):
    x = x_ref[0]
    ms = jnp.mean(x * x, axis=-1, keepdims=True)
    h = x * lax.rsqrt(ms + EPS) * ng_ref[...]
    h = (h * (1.0 + scale_ref[0]) + shift_ref[0]).astype(jnp.bfloat16)
    inv = LOG2E / math.sqrt(HEAD_DIM)
    for ci in range(IN_COLS // PROJ_CHUNK):
        c0 = ci * PROJ_CHUNK
        y = jnp.dot(h, w_ref[:, c0:c0 + PROJ_CHUNK], preferred_element_type=jnp.float32)
        cb = c0 // LANES
        if cb == CB_SB_Q:
            y = y * inv
        elif cb == CB_DF_Q:
            y = _group_rmsnorm(y, qg_ref[...]) * inv
        elif cb == CB_DF_K:
            y = _group_rmsnorm(y, kg_ref[...])
        o_ref[0, :, c0:c0 + PROJ_CHUNK] = y.astype(o_ref.dtype)


def _in_proj(x, norm_g, shift, scale, w_in_bf16, q_norm_g, k_norm_g):
    bsz, s, d = x.shape
    tm = ROWS_PROJ
    qg = jnp.tile(q_norm_g.reshape(1, HEAD_DIM), (1, 2))
    kg = jnp.tile(k_norm_g.reshape(1, HEAD_DIM), (1, 2))
    return pl.pallas_call(
        _in_proj_kernel,
        out_shape=jax.ShapeDtypeStruct((bsz, s, IN_COLS), jnp.bfloat16),
        grid=(bsz, s // tm),
        in_specs=[pl.BlockSpec((1, tm, d), lambda b, i: (b, i, 0)),
                  pl.BlockSpec((1, d), lambda b, i: (0, 0)),
                  pl.BlockSpec((1, 1, d), lambda b, i: (b, 0, 0)),
                  pl.BlockSpec((1, 1, d), lambda b, i: (b, 0, 0)),
                  pl.BlockSpec((d, IN_COLS), lambda b, i: (0, 0)),
                  pl.BlockSpec((1, LANES), lambda b, i: (0, 0)),
                  pl.BlockSpec((1, LANES), lambda b, i: (0, 0))],
        out_specs=pl.BlockSpec((1, tm, IN_COLS), lambda b, i: (b, i, 0)),
        compiler_params=pltpu.CompilerParams(vmem_limit_bytes=VMEM_LIMIT),
        name="in_proj",
    )(x, norm_g.reshape(1, d), shift.reshape(bsz, 1, d), scale.reshape(bsz, 1, d),
      w_in_bf16, qg, kg)


def _sb_tile(qm, kblk, vblk, cum, carry, mask):
    z = lax.dot_general(qm, kblk, _NT, preferred_element_type=jnp.float32)
    nz = -z
    l1p = jnp.log(1.0 + jnp.exp2(jnp.minimum(z, nz))) * LOG2E
    lg = jnp.minimum(nz, 0.0) - l1p
    if mask is not None:
        lg = jnp.where(mask, lg, 0.0)
    rem = jnp.dot(lg.astype(jnp.bfloat16), cum, preferred_element_type=jnp.float32)
    a = jnp.exp2(lg + z + rem + carry)
    if mask is not None:
        a = jnp.where(mask, a, 0.0)
    o = jnp.dot(a.astype(jnp.bfloat16), vblk, preferred_element_type=jnp.float32)
    return o, carry + jnp.sum(lg, axis=-1, keepdims=True)


def _sb_kernel(q_ref, k_ref, v_ref, g_ref, o_ref, acc):
    qi = pl.program_id(2)
    q = q_ref[0]
    lane = lax.broadcasted_iota(jnp.int32, (1, LANES), 1)
    first = lane < HEAD_DIM
    zero = jnp.zeros_like(q)
    qs = jnp.concatenate([jnp.where(first, q, zero), jnp.where(first, zero, q)], axis=0)
    row = lax.broadcasted_iota(jnp.int32, (TK, TK), 0)
    col = lax.broadcasted_iota(jnp.int32, (TK, TK), 1)
    cum = (row > col).astype(jnp.bfloat16)
    strict = col < row
    diag_mask = jnp.concatenate([strict, strict], axis=0)

    def tiles(kbs, carry, masks):
        total = None
        for kb, mask in zip(kbs, masks):
            start = pl.multiple_of(kb * TK, TK)
            o, carry = _sb_tile(qs, k_ref[0, pl.ds(start, TK), :], v_ref[0, pl.ds(start, TK), :],
                                cum, carry, mask)
            total = o if total is None else total + o
        return total, carry

    def alive(carry):
        return jnp.max(carry) > SB_DEAD_LOG2

    zero_carry = jnp.zeros((2 * TQ, 1), jnp.float32)

    def first_pair():
        o, carry = tiles([qi, qi - 1], zero_carry, [diag_mask, None])
        acc[...] = o
        return carry

    def first_alone():
        o, carry = tiles([qi], zero_carry, [diag_mask])
        acc[...] = o
        return carry

    carry = lax.cond(qi > 0, first_pair, first_alone)
    left = jnp.maximum(qi - 1, 0)

    def sweep(n_steps, kbs_of_step, carry):
        def cond(state):
            i, _, live = state
            return jnp.logical_and(i < n_steps, live)

        def body(state):
            i, carry, _ = state
            kbs = kbs_of_step(i)
            o, carry = tiles(kbs, carry, [None] * len(kbs))
            acc[...] += o
            return i + 1, carry, alive(carry)

        return lax.while_loop(cond, body, (jnp.int32(0), carry, alive(carry)))[1]

    rem = lax.rem(left, SB_GROUP)
    carry = sweep(rem, lambda i: [left - 1 - i], carry)
    sweep(left // SB_GROUP,
          lambda i: [left - rem - i * SB_GROUP - 1 - u for u in range(SB_GROUP)], carry)
    out = jnp.where(first, acc[:TQ, :], acc[TQ:, :])
    g = g_ref[0].astype(jnp.float32)
    o_ref[0] = (out * _silu(g)).astype(o_ref.dtype)


def _sb_attention(proj):
    bsz, s, _ = proj.shape
    nq = s // TQ
    return pl.pallas_call(
        _sb_kernel,
        out_shape=jax.ShapeDtypeStruct((bsz, s, SB_WIDTH), jnp.bfloat16),
        grid=(bsz, SB_WIDTH // LANES, nq),
        in_specs=[pl.BlockSpec((1, TQ, LANES), lambda b, h, i: (b, i, CB_SB_Q + h)),
                  pl.BlockSpec((1, s, LANES), lambda b, h, i: (b, 0, CB_SB_K + h)),
                  pl.BlockSpec((1, s, LANES), lambda b, h, i: (b, 0, CB_SB_V + h)),
                  pl.BlockSpec((1, TQ, LANES), lambda b, h, i: (b, i, CB_SB_G + h))],
        out_specs=pl.BlockSpec((1, TQ, LANES), lambda b, h, i: (b, i, h)),
        scratch_shapes=[pltpu.VMEM((2 * TQ, LANES), jnp.float32)],
        compiler_params=pltpu.CompilerParams(vmem_limit_bytes=VMEM_LIMIT),
        name="sb_attn",
    )(proj, proj, proj, proj)


def _df_kernel(q_ref, k_ref, v_ref, g_ref, slope_ref, lq1_ref, lk1_ref, lq2_ref, lk2_ref, sg_ref,
               qg_ref, kg_ref, o_ref, m_s, l_s, acc, *, lam_init):
    qi = pl.program_id(2)
    q = q_ref[0]
    lane = lax.broadcasted_iota(jnp.int32, (1, LANES), 1)
    first = lane < HEAD_DIM
    zero = jnp.zeros_like(q)
    qs = jnp.concatenate([jnp.where(first, q, zero), jnp.where(first, zero, q)], axis=0)
    slope = slope_ref[0] * LOG2E
    row = lax.broadcasted_iota(jnp.int32, (TQ, TK), 0)
    col = lax.broadcasted_iota(jnp.int32, (TQ, TK), 1)
    causal = col <= row
    diag_mask = jnp.concatenate([causal, causal], axis=0)

    m_s[...] = jnp.full((2 * TQ, LANES), NEG_BIG, jnp.float32)
    l_s[...] = jnp.zeros((2 * TQ, LANES), jnp.float32)
    acc[...] = jnp.zeros((2 * TQ, LANES), jnp.float32)

    def tile(first_kb, width, mask):
        reps = width // LANES
        start = pl.multiple_of(first_kb * TK, TK)
        kblk = k_ref[0, pl.ds(start, width), :]
        vblk = v_ref[0, pl.ds(start, width), :]
        rel = lax.broadcasted_iota(jnp.int32, (1, width), 1).astype(jnp.float32)
        off = ((first_kb - qi) * TK).astype(jnp.float32)
        bias = jnp.concatenate([slope] * reps, axis=1) * (rel + off)
        s = lax.dot_general(qs, kblk, _NT, preferred_element_type=jnp.float32) + bias
        if mask is not None:
            s = jnp.where(mask, s, NEG_BIG)
        m_old = m_s[...]
        m_new = jnp.maximum(m_old, jnp.max(s, axis=-1, keepdims=True))
        p = jnp.exp2(s - jnp.concatenate([m_new] * reps, axis=1))
        alpha = jnp.exp2(m_old - m_new)
        l_s[...] = alpha * l_s[...] + jnp.sum(p, axis=-1, keepdims=True)
        acc[...] = alpha * acc[...] + jnp.dot(p.astype(jnp.bfloat16), vblk,
                                              preferred_element_type=jnp.float32)
        m_s[...] = m_new

    tile(qi, TK, diag_mask)

    qk_bound = (jnp.max(jnp.abs(qg_ref[...])) * jnp.max(jnp.abs(kg_ref[...]))
                * (math.sqrt(HEAD_DIM) * LOG2E * DF_BOUND_MARGIN))
    slope_s = jnp.max(slope)

    def needed(first_kb, width):
        last_rel = ((first_kb - qi) * TK + (width - 1)).astype(jnp.float32)
        return qk_bound + slope_s * last_rel - jnp.min(m_s[...]) > DF_DEAD_LOG2

    def sweep(n_steps, first_kb_of_step, width):
        def cond(state):
            i, live = state
            return jnp.logical_and(i < n_steps, live)

        def body(state):
            i, _ = state
            tile(first_kb_of_step(i), width, None)
            return i + 1, needed(first_kb_of_step(i + 1), width)

        lax.while_loop(cond, body, (jnp.int32(0), needed(first_kb_of_step(jnp.int32(0)), width)))

    rem = lax.rem(qi, DF_GROUP)
    sweep(rem, lambda i: qi - 1 - i, TK)
    sweep(qi // DF_GROUP, lambda i: qi - rem - (i + 1) * DF_GROUP, DF_GROUP * TK)

    lam = (jnp.exp(jnp.sum(lq1_ref[...] * lk1_ref[...], keepdims=True))
           - jnp.exp(jnp.sum(lq2_ref[...] * lk2_ref[...], keepdims=True)) + lam_init)
    o = acc[:TQ, :] / l_s[:TQ, :] - lam * (acc[TQ:, :] / l_s[TQ:, :])
    ms = jnp.mean(o * o, axis=-1, keepdims=True)
    o = o * lax.rsqrt(ms + EPS) * sg_ref[...] * (1.0 - lam_init)
    g = g_ref[0].astype(jnp.float32)
    o_ref[0] = (o * _silu(g)).astype(o_ref.dtype)


def _df_attention(proj, slopes, lq1, lk1, lq2, lk2, subln_g, q_norm_g, k_norm_g, lam_init):
    bsz, s, _ = proj.shape
    nq = s // TQ
    vec = pl.BlockSpec((1, HEAD_DIM), lambda b, h, i: (0, 0))
    stat = pltpu.VMEM((2 * TQ, LANES), jnp.float32)
    return pl.pallas_call(
        functools.partial(_df_kernel, lam_init=lam_init),
        out_shape=jax.ShapeDtypeStruct((bsz, s, DF_WIDTH), jnp.bfloat16),
        grid=(bsz, DF_HEADS, nq),
        in_specs=[pl.BlockSpec((1, TQ, LANES), lambda b, h, i: (b, i, CB_DF_Q + h)),
                  pl.BlockSpec((1, s, LANES), lambda b, h, i: (b, 0, CB_DF_K + h)),
                  pl.BlockSpec((1, s, LANES), lambda b, h, i: (b, 0, CB_DF_V + h)),
                  pl.BlockSpec((1, TQ, LANES), lambda b, h, i: (b, i, CB_DF_G + h)),
                  pl.BlockSpec((1, 1, LANES), lambda b, h, i: (h, 0, 0)),
                  vec, vec, vec, vec,
                  pl.BlockSpec((1, LANES), lambda b, h, i: (0, 0)),
                  vec, vec],
        out_specs=pl.BlockSpec((1, TQ, LANES), lambda b, h, i: (b, i, h)),
        scratch_shapes=[stat, stat, stat],
        compiler_params=pltpu.CompilerParams(vmem_limit_bytes=VMEM_LIMIT),
        name="df_attn",
    )(proj, proj, proj, proj, slopes, lq1, lk1, lq2, lk2, subln_g, q_norm_g, k_norm_g)


def _out_proj_kernel(x_ref, sb_ref, df_ref, w_ref, gate_ref, o_ref):
    y = (jnp.dot(sb_ref[0], w_ref[:SB_WIDTH, :], preferred_element_type=jnp.float32)
         + jnp.dot(df_ref[0], w_ref[SB_WIDTH:, :], preferred_element_type=jnp.float32))
    o_ref[0] = x_ref[0] + gate_ref[0] * y


def _out_proj(x, sb_o, df_o, w_out_bf16, gate):
    bsz, s, d = x.shape
    tm = ROWS_PROJ
    return pl.pallas_call(
        _out_proj_kernel,
        out_shape=jax.ShapeDtypeStruct((bsz, s, d), jnp.float32),
        grid=(bsz, s // tm),
        in_specs=[pl.BlockSpec((1, tm, d), lambda b, i: (b, i, 0)),
                  pl.BlockSpec((1, tm, SB_WIDTH), lambda b, i: (b, i, 0)),
                  pl.BlockSpec((1, tm, DF_WIDTH), lambda b, i: (b, i, 0)),
                  pl.BlockSpec((SB_WIDTH + DF_WIDTH, d), lambda b, i: (0, 0)),
                  pl.BlockSpec((1, 1, d), lambda b, i: (b, 0, 0))],
        out_specs=pl.BlockSpec((1, tm, d), lambda b, i: (b, i, 0)),
        compiler_params=pltpu.CompilerParams(vmem_limit_bytes=VMEM_LIMIT),
        name="out_proj",
    )(x, sb_o, df_o, w_out_bf16, gate.reshape(bsz, 1, d))


def _layer(x, c, layer_idx, norm_g, w_ada, b_ada, w_in, q_norm_g, k_norm_g,
           lambda_q1, lambda_k1, lambda_q2, lambda_k2, subln_g, w_out):
    d = x.shape[-1]
    lam_init = 0.8 - 0.6 * math.exp(-0.3 * layer_idx)
    mod = _adaln_mod(c, w_ada, b_ada)
    shift, scale, gate = mod[:, :d], mod[:, d:2 * d], mod[:, 2 * d:]
    proj = _in_proj(x, norm_g, shift, scale, w_in.astype(jnp.bfloat16), q_norm_g, k_norm_g)
    sb_o = _sb_attention(proj)
    slopes = jnp.asarray([2.0 ** (-8.0 * (h + 1) / DF_HEADS) for h in range(DF_HEADS)], jnp.float32)
    slopes = jnp.broadcast_to(slopes[:, None, None], (DF_HEADS, 1, LANES))
    row = lambda v: v.reshape(1, -1)
    df_o = _df_attention(proj, slopes, row(lambda_q1), row(lambda_k1), row(lambda_q2),
                         row(lambda_k2), row(subln_g), row(q_norm_g), row(k_norm_g), lam_init)
    return _out_proj(x, sb_o, df_o, w_out.astype(jnp.bfloat16), gate)


@jax.jit
def kernel(x, c, norm_g, w_ada, b_ada, w_in, q_norm_g, k_norm_g, lambda_q1, lambda_k1,
           lambda_q2, lambda_k2, subln_g, w_out):
    for l in range(norm_g.shape[0]):
        x = _layer(x, c, l, norm_g[l], w_ada[l], b_ada[l], w_in[l], q_norm_g[l], k_norm_g[l],
                   lambda_q1[l], lambda_k1[l], lambda_q2[l], lambda_k2[l], subln_g[l], w_out[l])
    return x
```

```python
import functools
import math

import jax
import jax.numpy as jnp
from jax import lax
from jax.experimental import pallas as pl
from jax.experimental.pallas import tpu as pltpu

D_MODEL = 1024
SB_HEADS = 8
DF_HEADS = 4
HEAD_DIM = 64
LANES = 128
SB_WIDTH = SB_HEADS * HEAD_DIM
DF_WIDTH = DF_HEADS * 2 * HEAD_DIM
IN_COLS = 4 * SB_WIDTH + 4 * DF_WIDTH
EPS = 1e-6
NEG_BIG = -1e30

CB_SB_Q, CB_SB_K, CB_SB_V, CB_SB_G = 0, 4, 8, 12
CB_DF_Q, CB_DF_K, CB_DF_V, CB_DF_G = 16, 20, 24, 28

ROWS_PROJ = 512
TQ = 256
TK = 256
SB_GROUP = 4
DF_GROUP = 4
LOG2E = math.log2(math.e)
SB_DEAD_LOG2 = -150.0
DF_DEAD_LOG2 = -150.0
DF_FIXED_SHIFT_MAX = 48.0
DF_BOUND_MARGIN = 1.02
PROJ_CHUNK = 512
VMEM_LIMIT = 48 * 1024 * 1024

_NT = (((1,), (1,)), ((), ()))


def _silu(g):
    return g / (1.0 + jnp.exp(-g))


def _adaln_kernel(ct_ref, w_ref, b_ref, o_ref):
    w = w_ref[...]
    rows = []
    for b in range(o_ref.shape[0]):
        col = ct_ref[:, b:b + 1]
        rows.append(jnp.sum(col * w, axis=0, keepdims=True))
    o_ref[...] = jnp.concatenate(rows, axis=0) + b_ref[...]


def _adaln_mod(c, w_ada, b_ada):
    bsz, d = c.shape
    n = w_ada.shape[1]
    tn = 512
    return pl.pallas_call(
        _adaln_kernel,
        out_shape=jax.ShapeDtypeStruct((bsz, n), jnp.float32),
        grid=(n // tn,),
        in_specs=[pl.BlockSpec((d, bsz), lambda j: (0, 0)),
                  pl.BlockSpec((d, tn), lambda j: (0, j)),
                  pl.BlockSpec((1, tn), lambda j: (0, j))],
        out_specs=pl.BlockSpec((bsz, tn), lambda j: (0, j)),
        name="adaln_mod",
    )(c.T, w_ada, b_ada.reshape(1, n))


def _group_rmsnorm(y, gain):
    outs = []
    lane = lax.broadcasted_iota(jnp.int32, (1, LANES), 1)
    lo = lane < HEAD_DIM
    for j in range(y.shape[1] // LANES):
        blk = y[:, j * LANES:(j + 1) * LANES]
        sq = blk * blk
        s_lo = jnp.sum(jnp.where(lo, sq, 0.0), axis=-1, keepdims=True)
        s_hi = jnp.sum(jnp.where(lo, 0.0, sq), axis=-1, keepdims=True)
        ms = jnp.where(lo, s_lo, s_hi) * (1.0 / HEAD_DIM)
        outs.append(blk * lax.rsqrt(ms + EPS) * gain)
    return jnp.concatenate(outs, axis=1)


def _in_proj_kernel(x_ref, ng_ref, shift_ref, scale_ref, w_ref, qg_ref, kg_ref, o_ref):
    x = x_ref[0]
    ms = jnp.mean(x * x, axis=-1, keepdims=True)
    h = x * lax.rsqrt(ms + EPS) * ng_ref[...]
    h = (h * (1.0 + scale_ref[0]) + shift_ref[0]).astype(jnp.bfloat16)
    inv = LOG2E / math.sqrt(HEAD_DIM)
    for ci in range(IN_COLS // PROJ_CHUNK):
        c0 = ci * PROJ_CHUNK
        y = jnp.dot(h, w_ref[:, c0:c0 + PROJ_CHUNK], preferred_element_type=jnp.float32)
        cb = c0 // LANES
        if cb == CB_SB_Q:
            y = y * inv
        elif cb == CB_DF_Q:
            y = _group_rmsnorm(y, qg_ref[...]) * inv
        elif cb == CB_DF_K:
            y = _group_rmsnorm(y, kg_ref[...])
        o_ref[0, :, c0:c0 + PROJ_CHUNK] = y.astype(o_ref.dtype)


def _in_proj(x, norm_g, shift, scale, w_in_bf16, q_norm_g, k_norm_g):
    bsz, s, d = x.shape
    tm = ROWS_PROJ
    qg = jnp.tile(q_norm_g.reshape(1, HEAD_DIM), (1, 2))
    kg = jnp.tile(k_norm_g.reshape(1, HEAD_DIM), (1, 2))
    return pl.pallas_call(
        _in_proj_kernel,
        out_shape=jax.ShapeDtypeStruct((bsz, s, IN_COLS), jnp.bfloat16),
        grid=(bsz, s // tm),
        in_specs=[pl.BlockSpec((1, tm, d), lambda b, i: (b, i, 0)),
                  pl.BlockSpec((1, d), lambda b, i: (0, 0)),
                  pl.BlockSpec((1, 1, d), lambda b, i: (b, 0, 0)),
                  pl.BlockSpec((1, 1, d), lambda b, i: (b, 0, 0)),
                  pl.BlockSpec((d, IN_COLS), lambda b, i: (0, 0)),
                  pl.BlockSpec((1, LANES), lambda b, i: (0, 0)),
                  pl.BlockSpec((1, LANES), lambda b, i: (0, 0))],
        out_specs=pl.BlockSpec((1, tm, IN_COLS), lambda b, i: (b, i, 0)),
        compiler_params=pltpu.CompilerParams(vmem_limit_bytes=VMEM_LIMIT),
        name="in_proj",
    )(x, norm_g.reshape(1, d), shift.reshape(bsz, 1, d), scale.reshape(bsz, 1, d),
      w_in_bf16, qg, kg)


def _sb_tile(qm, kblk, vblk, cum, carry, mask):
    z = lax.dot_general(qm, kblk, _NT, preferred_element_type=jnp.float32)
    nz = -z
    l1p = jnp.log(1.0 + jnp.exp2(jnp.minimum(z, nz))) * LOG2E
    lg = jnp.minimum(nz, 0.0) - l1p
    if mask is not None:
        lg = jnp.where(mask, lg, 0.0)
    rem = jnp.dot(lg.astype(jnp.bfloat16), cum, preferred_element_type=jnp.float32)
    a = jnp.exp2(lg + z + rem + carry)
    if mask is not None:
        a = jnp.where(mask, a, 0.0)
    o = jnp.dot(a.astype(jnp.bfloat16), vblk, preferred_element_type=jnp.float32)
    return o, carry + jnp.sum(lg, axis=-1, keepdims=True)


def _sb_kernel(q_ref, k_ref, v_ref, g_ref, o_ref, acc):
    qi = pl.program_id(2)
    q = q_ref[0]
    lane = lax.broadcasted_iota(jnp.int32, (1, LANES), 1)
    first = lane < HEAD_DIM
    zero = jnp.zeros_like(q)
    qs = jnp.concatenate([jnp.where(first, q, zero), jnp.where(first, zero, q)], axis=0)
    row = lax.broadcasted_iota(jnp.int32, (TK, TK), 0)
    col = lax.broadcasted_iota(jnp.int32, (TK, TK), 1)
    cum = (row > col).astype(jnp.bfloat16)
    strict = col < row
    diag_mask = jnp.concatenate([strict, strict], axis=0)

    def tiles(kbs, carry, masks):
        total = None
        for kb, mask in zip(kbs, masks):
            start = pl.multiple_of(kb * TK, TK)
            o, carry = _sb_tile(qs, k_ref[0, pl.ds(start, TK), :], v_ref[0, pl.ds(start, TK), :],
                                cum, carry, mask)
            total = o if total is None else total + o
        return total, carry

    def alive(carry):
        return jnp.max(carry) > SB_DEAD_LOG2

    zero_carry = jnp.zeros((2 * TQ, 1), jnp.float32)

    def first_pair():
        o, carry = tiles([qi, qi - 1], zero_carry, [diag_mask, None])
        acc[...] = o
        return carry

    def first_alone():
        o, carry = tiles([qi], zero_carry, [diag_mask])
        acc[...] = o
        return carry

    carry = lax.cond(qi > 0, first_pair, first_alone)
    left = jnp.maximum(qi - 1, 0)

    def sweep(n_steps, kbs_of_step, carry):
        def cond(state):
            i, _, live = state
            return jnp.logical_and(i < n_steps, live)

        def body(state):
            i, carry, _ = state
            kbs = kbs_of_step(i)
            o, carry = tiles(kbs, carry, [None] * len(kbs))
            acc[...] += o
            return i + 1, carry, alive(carry)

        return lax.while_loop(cond, body, (jnp.int32(0), carry, alive(carry)))[1]

    rem = lax.rem(left, SB_GROUP)
    carry = sweep(rem, lambda i: [left - 1 - i], carry)
    sweep(left // SB_GROUP,
          lambda i: [left - rem - i * SB_GROUP - 1 - u for u in range(SB_GROUP)], carry)
    out = jnp.where(first, acc[:TQ, :], acc[TQ:, :])
    g = g_ref[0].astype(jnp.float32)
    o_ref[0] = (out * _silu(g)).astype(o_ref.dtype)


def _sb_attention(proj):
    bsz, s, _ = proj.shape
    nq = s // TQ
    return pl.pallas_call(
        _sb_kernel,
        out_shape=jax.ShapeDtypeStruct((bsz, s, SB_WIDTH), jnp.bfloat16),
        grid=(bsz, SB_WIDTH // LANES, nq),
        in_specs=[pl.BlockSpec((1, TQ, LANES), lambda b, h, i: (b, i, CB_SB_Q + h)),
                  pl.BlockSpec((1, s, LANES), lambda b, h, i: (b, 0, CB_SB_K + h)),
                  pl.BlockSpec((1, s, LANES), lambda b, h, i: (b, 0, CB_SB_V + h)),
                  pl.BlockSpec((1, TQ, LANES), lambda b, h, i: (b, i, CB_SB_G + h))],
        out_specs=pl.BlockSpec((1, TQ, LANES), lambda b, h, i: (b, i, h)),
        scratch_shapes=[pltpu.VMEM((2 * TQ, LANES), jnp.float32)],
        compiler_params=pltpu.CompilerParams(vmem_limit_bytes=VMEM_LIMIT),
        name="sb_attn",
    )(proj, proj, proj, proj)


def _df_kernel(q_ref, k_ref, v_ref, g_ref, slope_ref, lq1_ref, lk1_ref, lq2_ref, lk2_ref, sg_ref,
               qg_ref, kg_ref, o_ref, m_s, l_s, acc, *, lam_init):
    qi = pl.program_id(2)
    q = q_ref[0]
    lane = lax.broadcasted_iota(jnp.int32, (1, LANES), 1)
    first = lane < HEAD_DIM
    zero = jnp.zeros_like(q)
    qs = jnp.concatenate([jnp.where(first, q, zero), jnp.where(first, zero, q)], axis=0)
    slope = slope_ref[0] * LOG2E
    row = lax.broadcasted_iota(jnp.int32, (TQ, TK), 0)
    col = lax.broadcasted_iota(jnp.int32, (TQ, TK), 1)
    causal = col <= row
    diag_mask = jnp.concatenate([causal, causal], axis=0)

    qk_bound = (jnp.max(jnp.abs(qg_ref[...])) * jnp.max(jnp.abs(kg_ref[...]))
                * (math.sqrt(HEAD_DIM) * LOG2E * DF_BOUND_MARGIN))
    slope_s = jnp.max(slope)

    def load(first_kb, width):
        start = pl.multiple_of(first_kb * TK, TK)
        rel = lax.broadcasted_iota(jnp.int32, (1, width), 1).astype(jnp.float32)
        off = ((first_kb - qi) * TK).astype(jnp.float32)
        bias = jnp.concatenate([slope] * (width // LANES), axis=1) * (rel + off)
        return k_ref[0, pl.ds(start, width), :], v_ref[0, pl.ds(start, width), :], bias

    def last_rel(first_kb, width):
        return ((first_kb - qi) * TK + (width - 1)).astype(jnp.float32)

    def sweep(n_steps, first_kb_of_step, width, tile, needed):
        def cond(state):
            i, live = state
            return jnp.logical_and(i < n_steps, live)

        def body(state):
            i, _ = state
            tile(first_kb_of_step(i), width, None)
            return i + 1, needed(first_kb_of_step(i + 1), width)

        lax.while_loop(cond, body, (jnp.int32(0), needed(first_kb_of_step(jnp.int32(0)), width)))

    def sweep_all(tile, needed):
        tile(qi, TK, diag_mask)
        rem = lax.rem(qi, DF_GROUP)
        sweep(rem, lambda i: qi - 1 - i, TK, tile, needed)
        sweep(qi // DF_GROUP, lambda i: qi - rem - (i + 1) * DF_GROUP, DF_GROUP * TK, tile, needed)

    def fixed_shift():
        rowpos = lax.broadcasted_iota(jnp.int32, (TQ, LANES), 0).astype(jnp.float32)
        shift = qk_bound + slope * jnp.concatenate([rowpos, rowpos], axis=0)
        l_s[...] = jnp.zeros((2 * TQ, LANES), jnp.float32)
        acc[...] = jnp.zeros((2 * TQ, LANES), jnp.float32)

        def tile(first_kb, width, mask):
            reps = width // LANES
            kblk, vblk, bias = load(first_kb, width)
            s = (lax.dot_general(qs, kblk, _NT, preferred_element_type=jnp.float32)
                 + bias - jnp.concatenate([shift] * reps, axis=1))
            if mask is not None:
                s = jnp.where(mask, s, NEG_BIG)
            p = jnp.exp2(s)
            part = p[:, :LANES]
            for r in range(1, reps):
                part = part + p[:, r * LANES:(r + 1) * LANES]
            l_s[...] += part
            acc[...] += jnp.dot(p.astype(jnp.bfloat16), vblk, preferred_element_type=jnp.float32)

        def needed(first_kb, width):
            return slope_s * last_rel(first_kb, width) > DF_DEAD_LOG2

        sweep_all(tile, needed)
        l = jnp.sum(l_s[...], axis=-1, keepdims=True)
        return l[:TQ], l[TQ:]

    def running_max():
        m_s[...] = jnp.full((2 * TQ, LANES), NEG_BIG, jnp.float32)
        l_s[...] = jnp.zeros((2 * TQ, LANES), jnp.float32)
        acc[...] = jnp.zeros((2 * TQ, LANES), jnp.float32)

        def tile(first_kb, width, mask):
            reps = width // LANES
            kblk, vblk, bias = load(first_kb, width)
            s = lax.dot_general(qs, kblk, _NT, preferred_element_type=jnp.float32) + bias
            if mask is not None:
                s = jnp.where(mask, s, NEG_BIG)
            m_old = m_s[...]
            m_new = jnp.maximum(m_old, jnp.max(s, axis=-1, keepdims=True))
            p = jnp.exp2(s - jnp.concatenate([m_new] * reps, axis=1))
            alpha = jnp.exp2(m_old - m_new)
            l_s[...] = alpha * l_s[...] + jnp.sum(p, axis=-1, keepdims=True)
            acc[...] = alpha * acc[...] + jnp.dot(p.astype(jnp.bfloat16), vblk,
                                                  preferred_element_type=jnp.float32)
            m_s[...] = m_new

        def needed(first_kb, width):
            return (qk_bound + slope_s * last_rel(first_kb, width) - jnp.min(m_s[...])
                    > DF_DEAD_LOG2)

        sweep_all(tile, needed)
        return l_s[:TQ, :1], l_s[TQ:, :1]

    l1, l2 = lax.cond(qk_bound < DF_FIXED_SHIFT_MAX, fixed_shift, running_max)

    lam = (jnp.exp(jnp.sum(lq1_ref[...] * lk1_ref[...], keepdims=True))
           - jnp.exp(jnp.sum(lq2_ref[...] * lk2_ref[...], keepdims=True)) + lam_init)
    o = acc[:TQ, :] / l1 - lam * (acc[TQ:, :] / l2)
    ms = jnp.mean(o * o, axis=-1, keepdims=True)
    o = o * lax.rsqrt(ms + EPS) * sg_ref[...] * (1.0 - lam_init)
    g = g_ref[0].astype(jnp.float32)
    o_ref[0] = (o * _silu(g)).astype(o_ref.dtype)


def _df_attention(proj, slopes, lq1, lk1, lq2, lk2, subln_g, q_norm_g, k_norm_g, lam_init):
    bsz, s, _ = proj.shape
    nq = s // TQ
    vec = pl.BlockSpec((1, HEAD_DIM), lambda b, h, i: (0, 0))
    stat = pltpu.VMEM((2 * TQ, LANES), jnp.float32)
    return pl.pallas_call(
        functools.partial(_df_kernel, lam_init=lam_init),
        out_shape=jax.ShapeDtypeStruct((bsz, s, DF_WIDTH), jnp.bfloat16),
        grid=(bsz, DF_HEADS, nq),
        in_specs=[pl.BlockSpec((1, TQ, LANES), lambda b, h, i: (b, i, CB_DF_Q + h)),
                  pl.BlockSpec((1, s, LANES), lambda b, h, i: (b, 0, CB_DF_K + h)),
                  pl.BlockSpec((1, s, LANES), lambda b, h, i: (b, 0, CB_DF_V + h)),
                  pl.BlockSpec((1, TQ, LANES), lambda b, h, i: (b, i, CB_DF_G + h)),
                  pl.BlockSpec((1, 1, LANES), lambda b, h, i: (h, 0, 0)),
                  vec, vec, vec, vec,
                  pl.BlockSpec((1, LANES), lambda b, h, i: (0, 0)),
                  vec, vec],
        out_specs=pl.BlockSpec((1, TQ, LANES), lambda b, h, i: (b, i, h)),
        scratch_shapes=[stat, stat, stat],
        compiler_params=pltpu.CompilerParams(vmem_limit_bytes=VMEM_LIMIT),
        name="df_attn",
    )(proj, proj, proj, proj, slopes, lq1, lk1, lq2, lk2, subln_g, q_norm_g, k_norm_g)


def _out_proj_kernel(x_ref, sb_ref, df_ref, w_ref, gate_ref, o_ref):
    y = (jnp.dot(sb_ref[0], w_ref[:SB_WIDTH, :], preferred_element_type=jnp.float32)
         + jnp.dot(df_ref[0], w_ref[SB_WIDTH:, :], preferred_element_type=jnp.float32))
    o_ref[0] = x_ref[0] + gate_ref[0] * y


def _out_proj(x, sb_o, df_o, w_out_bf16, gate):
    bsz, s, d = x.shape
    tm = ROWS_PROJ
    return pl.pallas_call(
        _out_proj_kernel,
        out_shape=jax.ShapeDtypeStruct((bsz, s, d), jnp.float32),
        grid=(bsz, s // tm),
        in_specs=[pl.BlockSpec((1, tm, d), lambda b, i: (b, i, 0)),
                  pl.BlockSpec((1, tm, SB_WIDTH), lambda b, i: (b, i, 0)),
                  pl.BlockSpec((1, tm, DF_WIDTH), lambda b, i: (b, i, 0)),
                  pl.BlockSpec((SB_WIDTH + DF_WIDTH, d), lambda b, i: (0, 0)),
                  pl.BlockSpec((1, 1, d), lambda b, i: (b, 0, 0))],
        out_specs=pl.BlockSpec((1, tm, d), lambda b, i: (b, i, 0)),
        compiler_params=pltpu.CompilerParams(vmem_limit_bytes=VMEM_LIMIT),
        name="out_proj",
    )(x, sb_o, df_o, w_out_bf16, gate.reshape(bsz, 1, d))


def _layer(x, c, layer_idx, norm_g, w_ada, b_ada, w_in, q_norm_g, k_norm_g,
           lambda_q1, lambda_k1, lambda_q2, lambda_k2, subln_g, w_out):
    d = x.shape[-1]
    lam_init = 0.8 - 0.6 * math.exp(-0.3 * layer_idx)
    mod = _adaln_mod(c, w_ada, b_ada)
    shift, scale, gate = mod[:, :d], mod[:, d:2 * d], mod[:, 2 * d:]
    proj = _in_proj(x, norm_g, shift, scale, w_in.astype(jnp.bfloat16), q_norm_g, k_norm_g)
    sb_o = _sb_attention(proj)
    slopes = jnp.asarray([2.0 ** (-8.0 * (h + 1) / DF_HEADS) for h in range(DF_HEADS)], jnp.float32)
    slopes = jnp.broadcast_to(slopes[:, None, None], (DF_HEADS, 1, LANES))
    row = lambda v: v.reshape(1, -1)
    df_o = _df_attention(proj, slopes, row(lambda_q1), row(lambda_k1), row(lambda_q2),
                         row(lambda_k2), row(subln_g), row(q_norm_g), row(k_norm_g), lam_init)
    return _out_proj(x, sb_o, df_o, w_out.astype(jnp.bfloat16), gate)


@jax.jit
def kernel(x, c, norm_g, w_ada, b_ada, w_in, q_norm_g, k_norm_g, lambda_q1, lambda_k1,
           lambda_q2, lambda_k2, subln_g, w_out):
    for l in range(norm_g.shape[0]):
        x = _layer(x, c, l, norm_g[l], w_ada[l], b_ada[l], w_in[l], q_norm_g[l], k_norm_g[l],
                   lambda_q1[l], lambda_k1[l], lambda_q2[l], lambda_k2[l], subln_g[l], w_out[l])
    return x
```

```python
import functools
import math

import jax
import jax.numpy as jnp
from jax import lax
from jax.experimental import pallas as pl
from jax.experimental.pallas import tpu as pltpu

D_MODEL = 1024
SB_HEADS = 8
DF_HEADS = 4
HEAD_DIM = 64
LANES = 128
SB_WIDTH = SB_HEADS * HEAD_DIM
DF_WIDTH = DF_HEADS * 2 * HEAD_DIM
IN_COLS = 4 * SB_WIDTH + 4 * DF_WIDTH
EPS = 1e-6
NEG_BIG = -1e30

CB_SB_Q, CB_SB_K, CB_SB_V, CB_SB_G = 0, 4, 8, 12
CB_DF_Q, CB_DF_K, CB_DF_V, CB_DF_G = 16, 20, 24, 28

ROWS_PROJ = 512
TQ = 256
TK = 256
DF_TQ = 512
DF_TK = 512
SB_GROUP = 4
DF_GROUP = 2
LOG2E = math.log2(math.e)
SB_DEAD_LOG2 = -150.0
DF_DEAD_LOG2 = -150.0
DF_FIXED_SHIFT_MAX = 48.0
DF_BOUND_MARGIN = 1.02
PROJ_CHUNK = 512
VMEM_LIMIT = 48 * 1024 * 1024

_NT = (((1,), (1,)), ((), ()))


def _silu(g):
    return g / (1.0 + jnp.exp(-g))


def _adaln_kernel(ct_ref, w_ref, b_ref, o_ref):
    w = w_ref[...]
    rows = []
    for b in range(o_ref.shape[0]):
        col = ct_ref[:, b:b + 1]
        rows.append(jnp.sum(col * w, axis=0, keepdims=True))
    o_ref[...] = jnp.concatenate(rows, axis=0) + b_ref[...]


def _adaln_mod(c, w_ada, b_ada):
    bsz, d = c.shape
    n = w_ada.shape[1]
    tn = 512
    return pl.pallas_call(
        _adaln_kernel,
        out_shape=jax.ShapeDtypeStruct((bsz, n), jnp.float32),
        grid=(n // tn,),
        in_specs=[pl.BlockSpec((d, bsz), lambda j: (0, 0)),
                  pl.BlockSpec((d, tn), lambda j: (0, j)),
                  pl.BlockSpec((1, tn), lambda j: (0, j))],
        out_specs=pl.BlockSpec((bsz, tn), lambda j: (0, j)),
        name="adaln_mod",
    )(c.T, w_ada, b_ada.reshape(1, n))


def _group_rmsnorm(y, gain):
    outs = []
    lane = lax.broadcasted_iota(jnp.int32, (1, LANES), 1)
    lo = lane < HEAD_DIM
    for j in range(y.shape[1] // LANES):
        blk = y[:, j * LANES:(j + 1) * LANES]
        sq = blk * blk
        s_lo = jnp.sum(jnp.where(lo, sq, 0.0), axis=-1, keepdims=True)
        s_hi = jnp.sum(jnp.where(lo, 0.0, sq), axis=-1, keepdims=True)
        ms = jnp.where(lo, s_lo, s_hi) * (1.0 / HEAD_DIM)
        outs.append(blk * lax.rsqrt(ms + EPS) * gain)
    return jnp.concatenate(outs, axis=1)


def _in_proj_kernel(x_ref, ng_ref, shift_ref, scale_ref, w_ref, qg_ref, kg_ref, o_ref):
    x = x_ref[0]
    ms = jnp.mean(x * x, axis=-1, keepdims=True)
    h = x * lax.rsqrt(ms + EPS) * ng_ref[...]
    h = (h * (1.0 + scale_ref[0]) + shift_ref[0]).astype(jnp.bfloat16)
    inv = LOG2E / math.sqrt(HEAD_DIM)
    for ci in range(IN_COLS // PROJ_CHUNK):
        c0 = ci * PROJ_CHUNK
        y = jnp.dot(h, w_ref[:, c0:c0 + PROJ_CHUNK], preferred_element_type=jnp.float32)
        cb = c0 // LANES
        if cb == CB_SB_Q:
            y = y * inv
        elif cb == CB_DF_Q:
            y = _group_rmsnorm(y, qg_ref[...]) * inv
        elif cb == CB_DF_K:
            y = _group_rmsnorm(y, kg_ref[...])
        o_ref[0, :, c0:c0 + PROJ_CHUNK] = y.astype(o_ref.dtype)


def _in_proj(x, norm_g, shift, scale, w_in_bf16, q_norm_g, k_norm_g):
    bsz, s, d = x.shape
    tm = ROWS_PROJ
    qg = jnp.tile(q_norm_g.reshape(1, HEAD_DIM), (1, 2))
    kg = jnp.tile(k_norm_g.reshape(1, HEAD_DIM), (1, 2))
    return pl.pallas_call(
        _in_proj_kernel,
        out_shape=jax.ShapeDtypeStruct((bsz, s, IN_COLS), jnp.bfloat16),
        grid=(bsz, s // tm),
        in_specs=[pl.BlockSpec((1, tm, d), lambda b, i: (b, i, 0)),
                  pl.BlockSpec((1, d), lambda b, i: (0, 0)),
                  pl.BlockSpec((1, 1, d), lambda b, i: (b, 0, 0)),
                  pl.BlockSpec((1, 1, d), lambda b, i: (b, 0, 0)),
                  pl.BlockSpec((d, IN_COLS), lambda b, i: (0, 0)),
                  pl.BlockSpec((1, LANES), lambda b, i: (0, 0)),
                  pl.BlockSpec((1, LANES), lambda b, i: (0, 0))],
        out_specs=pl.BlockSpec((1, tm, IN_COLS), lambda b, i: (b, i, 0)),
        compiler_params=pltpu.CompilerParams(vmem_limit_bytes=VMEM_LIMIT),
        name="in_proj",
    )(x, norm_g.reshape(1, d), shift.reshape(bsz, 1, d), scale.reshape(bsz, 1, d),
      w_in_bf16, qg, kg)


def _sb_tile(qm, kblk, vblk, cum, carry, mask):
    z = lax.dot_general(qm, kblk, _NT, preferred_element_type=jnp.float32)
    nz = -z
    l1p = jnp.log(1.0 + jnp.exp2(jnp.minimum(z, nz))) * LOG2E
    lg = jnp.minimum(nz, 0.0) - l1p
    if mask is not None:
        lg = jnp.where(mask, lg, 0.0)
    rem = jnp.dot(lg.astype(jnp.bfloat16), cum, preferred_element_type=jnp.float32)
    a = jnp.exp2(lg + z + rem + carry)
    if mask is not None:
        a = jnp.where(mask, a, 0.0)
    o = jnp.dot(a.astype(jnp.bfloat16), vblk, preferred_element_type=jnp.float32)
    return o, carry + jnp.sum(lg, axis=-1, keepdims=True)


def _sb_kernel(q_ref, k_ref, v_ref, g_ref, o_ref, acc):
    qi = pl.program_id(2)
    q = q_ref[0]
    lane = lax.broadcasted_iota(jnp.int32, (1, LANES), 1)
    first = lane < HEAD_DIM
    zero = jnp.zeros_like(q)
    qs = jnp.concatenate([jnp.where(first, q, zero), jnp.where(first, zero, q)], axis=0)
    row = lax.broadcasted_iota(jnp.int32, (TK, TK), 0)
    col = lax.broadcasted_iota(jnp.int32, (TK, TK), 1)
    cum = (row > col).astype(jnp.bfloat16)
    strict = col < row
    diag_mask = jnp.concatenate([strict, strict], axis=0)

    def tiles(kbs, carry, masks):
        total = None
        for kb, mask in zip(kbs, masks):
            start = pl.multiple_of(kb * TK, TK)
            o, carry = _sb_tile(qs, k_ref[0, pl.ds(start, TK), :], v_ref[0, pl.ds(start, TK), :],
                                cum, carry, mask)
            total = o if total is None else total + o
        return total, carry

    def alive(carry):
        return jnp.max(carry) > SB_DEAD_LOG2

    zero_carry = jnp.zeros((2 * TQ, 1), jnp.float32)

    def first_pair():
        o, carry = tiles([qi, qi - 1], zero_carry, [diag_mask, None])
        acc[...] = o
        return carry

    def first_alone():
        o, carry = tiles([qi], zero_carry, [diag_mask])
        acc[...] = o
        return carry

    carry = lax.cond(qi > 0, first_pair, first_alone)
    left = jnp.maximum(qi - 1, 0)

    def sweep(n_steps, kbs_of_step, carry):
        def cond(state):
            i, _, live = state
            return jnp.logical_and(i < n_steps, live)

        def body(state):
            i, carry, _ = state
            kbs = kbs_of_step(i)
            o, carry = tiles(kbs, carry, [None] * len(kbs))
            acc[...] += o
            return i + 1, carry, alive(carry)

        return lax.while_loop(cond, body, (jnp.int32(0), carry, alive(carry)))[1]

    rem = lax.rem(left, SB_GROUP)
    carry = sweep(rem, lambda i: [left - 1 - i], carry)
    sweep(left // SB_GROUP,
          lambda i: [left - rem - i * SB_GROUP - 1 - u for u in range(SB_GROUP)], carry)
    out = jnp.where(first, acc[:TQ, :], acc[TQ:, :])
    g = g_ref[0].astype(jnp.float32)
    o_ref[0] = (out * _silu(g)).astype(o_ref.dtype)


def _sb_attention(proj):
    bsz, s, _ = proj.shape
    nq = s // TQ
    return pl.pallas_call(
        _sb_kernel,
        out_shape=jax.ShapeDtypeStruct((bsz, s, SB_WIDTH), jnp.bfloat16),
        grid=(bsz, SB_WIDTH // LANES, nq),
        in_specs=[pl.BlockSpec((1, TQ, LANES), lambda b, h, i: (b, i, CB_SB_Q + h)),
                  pl.BlockSpec((1, s, LANES), lambda b, h, i: (b, 0, CB_SB_K + h)),
                  pl.BlockSpec((1, s, LANES), lambda b, h, i: (b, 0, CB_SB_V + h)),
                  pl.BlockSpec((1, TQ, LANES), lambda b, h, i: (b, i, CB_SB_G + h))],
        out_specs=pl.BlockSpec((1, TQ, LANES), lambda b, h, i: (b, i, h)),
        scratch_shapes=[pltpu.VMEM((2 * TQ, LANES), jnp.float32)],
        compiler_params=pltpu.CompilerParams(vmem_limit_bytes=VMEM_LIMIT),
        name="sb_attn",
    )(proj, proj, proj, proj)


def _df_kernel(q_ref, k_ref, v_ref, g_ref, slope_ref, lq1_ref, lk1_ref, lq2_ref, lk2_ref, sg_ref,
               qg_ref, kg_ref, o_ref, m_s, l_s, acc, *, lam_init):
    qi = pl.program_id(2)
    q = q_ref[0]
    lane = lax.broadcasted_iota(jnp.int32, (1, LANES), 1)
    first = lane < HEAD_DIM
    zero = jnp.zeros_like(q)
    qs = jnp.concatenate([jnp.where(first, q, zero), jnp.where(first, zero, q)], axis=0)
    slope = slope_ref[0] * LOG2E
    row = lax.broadcasted_iota(jnp.int32, (DF_TQ, DF_TK), 0)
    col = lax.broadcasted_iota(jnp.int32, (DF_TQ, DF_TK), 1)
    causal = col <= row
    diag_mask = jnp.concatenate([causal, causal], axis=0)

    qk_bound = (jnp.max(jnp.abs(qg_ref[...])) * jnp.max(jnp.abs(kg_ref[...]))
                * (math.sqrt(HEAD_DIM) * LOG2E * DF_BOUND_MARGIN))
    slope_s = jnp.max(slope)

    def load(first_kb, width):
        start = pl.multiple_of(first_kb * DF_TK, DF_TK)
        rel = lax.broadcasted_iota(jnp.int32, (1, width), 1).astype(jnp.float32)
        off = ((first_kb - qi) * DF_TK).astype(jnp.float32)
        bias = jnp.concatenate([slope] * (width // LANES), axis=1) * (rel + off)
        return k_ref[0, pl.ds(start, width), :], v_ref[0, pl.ds(start, width), :], bias

    def last_rel(first_kb, width):
        return ((first_kb - qi) * DF_TK + (width - 1)).astype(jnp.float32)

    def sweep(n_steps, first_kb_of_step, width, tile, needed):
        def cond(state):
            i, live = state
            return jnp.logical_and(i < n_steps, live)

        def body(state):
            i, _ = state
            tile(first_kb_of_step(i), width, None)
            return i + 1, needed(first_kb_of_step(i + 1), width)

        lax.while_loop(cond, body, (jnp.int32(0), needed(first_kb_of_step(jnp.int32(0)), width)))

    def sweep_all(tile, needed):
        tile(qi, DF_TK, diag_mask)
        rem = lax.rem(qi, DF_GROUP)
        sweep(rem, lambda i: qi - 1 - i, DF_TK, tile, needed)
        sweep(qi // DF_GROUP, lambda i: qi - rem - (i + 1) * DF_GROUP, DF_GROUP * DF_TK, tile, needed)

    def fixed_shift():
        rowpos = lax.broadcasted_iota(jnp.int32, (DF_TQ, LANES), 0).astype(jnp.float32)
        shift = qk_bound + slope * jnp.concatenate([rowpos, rowpos], axis=0)
        l_s[...] = jnp.zeros((2 * DF_TQ, LANES), jnp.float32)
        acc[...] = jnp.zeros((2 * DF_TQ, LANES), jnp.float32)

        def tile(first_kb, width, mask):
            reps = width // LANES
            kblk, vblk, bias = load(first_kb, width)
            s = (lax.dot_general(qs, kblk, _NT, preferred_element_type=jnp.float32)
                 + bias - jnp.concatenate([shift] * reps, axis=1))
            if mask is not None:
                s = jnp.where(mask, s, NEG_BIG)
            p = jnp.exp2(s)
            part = p[:, :LANES]
            for r in range(1, reps):
                part = part + p[:, r * LANES:(r + 1) * LANES]
            l_s[...] += part
            acc[...] += jnp.dot(p.astype(jnp.bfloat16), vblk, preferred_element_type=jnp.float32)

        def needed(first_kb, width):
            return slope_s * last_rel(first_kb, width) > DF_DEAD_LOG2

        sweep_all(tile, needed)
        l = jnp.sum(l_s[...], axis=-1, keepdims=True)
        return l[:DF_TQ], l[DF_TQ:]

    def running_max():
        m_s[...] = jnp.full((2 * DF_TQ, LANES), NEG_BIG, jnp.float32)
        l_s[...] = jnp.zeros((2 * DF_TQ, LANES), jnp.float32)
        acc[...] = jnp.zeros((2 * DF_TQ, LANES), jnp.float32)

        def tile(first_kb, width, mask):
            reps = width // LANES
            kblk, vblk, bias = load(first_kb, width)
            s = lax.dot_general(qs, kblk, _NT, preferred_element_type=jnp.float32) + bias
            if mask is not None:
                s = jnp.where(mask, s, NEG_BIG)
            m_old = m_s[...]
            m_new = jnp.maximum(m_old, jnp.max(s, axis=-1, keepdims=True))
            p = jnp.exp2(s - jnp.concatenate([m_new] * reps, axis=1))
            alpha = jnp.exp2(m_old - m_new)
            l_s[...] = alpha * l_s[...] + jnp.sum(p, axis=-1, keepdims=True)
            acc[...] = alpha * acc[...] + jnp.dot(p.astype(jnp.bfloat16), vblk,
                                                  preferred_element_type=jnp.float32)
            m_s[...] = m_new

        def needed(first_kb, width):
            return (qk_bound + slope_s * last_rel(first_kb, width) - jnp.min(m_s[...])
                    > DF_DEAD_LOG2)

        sweep_all(tile, needed)
        return l_s[:DF_TQ, :1], l_s[DF_TQ:, :1]

    l1, l2 = lax.cond(qk_bound < DF_FIXED_SHIFT_MAX, fixed_shift, running_max)

    lam = (jnp.exp(jnp.sum(lq1_ref[...] * lk1_ref[...], keepdims=True))
           - jnp.exp(jnp.sum(lq2_ref[...] * lk2_ref[...], keepdims=True)) + lam_init)
    o = acc[:DF_TQ, :] / l1 - lam * (acc[DF_TQ:, :] / l2)
    ms = jnp.mean(o * o, axis=-1, keepdims=True)
    o = o * lax.rsqrt(ms + EPS) * sg_ref[...] * (1.0 - lam_init)
    g = g_ref[0].astype(jnp.float32)
    o_ref[0] = (o * _silu(g)).astype(o_ref.dtype)


def _df_attention(proj, slopes, lq1, lk1, lq2, lk2, subln_g, q_norm_g, k_norm_g, lam_init):
    bsz, s, _ = proj.shape
    nq = s // DF_TQ
    vec = pl.BlockSpec((1, HEAD_DIM), lambda b, h, i: (0, 0))
    stat = pltpu.VMEM((2 * DF_TQ, LANES), jnp.float32)
    return pl.pallas_call(
        functools.partial(_df_kernel, lam_init=lam_init),
        out_shape=jax.ShapeDtypeStruct((bsz, s, DF_WIDTH), jnp.bfloat16),
        grid=(bsz, DF_HEADS, nq),
        in_specs=[pl.BlockSpec((1, DF_TQ, LANES), lambda b, h, i: (b, i, CB_DF_Q + h)),
                  pl.BlockSpec((1, s, LANES), lambda b, h, i: (b, 0, CB_DF_K + h)),
                  pl.BlockSpec((1, s, LANES), lambda b, h, i: (b, 0, CB_DF_V + h)),
                  pl.BlockSpec((1, DF_TQ, LANES), lambda b, h, i: (b, i, CB_DF_G + h)),
                  pl.BlockSpec((1, 1, LANES), lambda b, h, i: (h, 0, 0)),
                  vec, vec, vec, vec,
                  pl.BlockSpec((1, LANES), lambda b, h, i: (0, 0)),
                  vec, vec],
        out_specs=pl.BlockSpec((1, DF_TQ, LANES), lambda b, h, i: (b, i, h)),
        scratch_shapes=[stat, stat, stat],
        compiler_params=pltpu.CompilerParams(vmem_limit_bytes=VMEM_LIMIT),
        name="df_attn",
    )(proj, proj, proj, proj, slopes, lq1, lk1, lq2, lk2, subln_g, q_norm_g, k_norm_g)


def _out_proj_kernel(x_ref, sb_ref, df_ref, w_ref, gate_ref, o_ref):
    y = (jnp.dot(sb_ref[0], w_ref[:SB_WIDTH, :], preferred_element_type=jnp.float32)
         + jnp.dot(df_ref[0], w_ref[SB_WIDTH:, :], preferred_element_type=jnp.float32))
    o_ref[0] = x_ref[0] + gate_ref[0] * y


def _out_proj(x, sb_o, df_o, w_out_bf16, gate):
    bsz, s, d = x.shape
    tm = ROWS_PROJ
    return pl.pallas_call(
        _out_proj_kernel,
        out_shape=jax.ShapeDtypeStruct((bsz, s, d), jnp.float32),
        grid=(bsz, s // tm),
        in_specs=[pl.BlockSpec((1, tm, d), lambda b, i: (b, i, 0)),
                  pl.BlockSpec((1, tm, SB_WIDTH), lambda b, i: (b, i, 0)),
                  pl.BlockSpec((1, tm, DF_WIDTH), lambda b, i: (b, i, 0)),
                  pl.BlockSpec((SB_WIDTH + DF_WIDTH, d), lambda b, i: (0, 0)),
                  pl.BlockSpec((1, 1, d), lambda b, i: (b, 0, 0))],
        out_specs=pl.BlockSpec((1, tm, d), lambda b, i: (b, i, 0)),
        compiler_params=pltpu.CompilerParams(vmem_limit_bytes=VMEM_LIMIT),
        name="out_proj",
    )(x, sb_o, df_o, w_out_bf16, gate.reshape(bsz, 1, d))


def _layer(x, c, layer_idx, norm_g, w_ada, b_ada, w_in, q_norm_g, k_norm_g,
           lambda_q1, lambda_k1, lambda_q2, lambda_k2, subln_g, w_out):
    d = x.shape[-1]
    lam_init = 0.8 - 0.6 * math.exp(-0.3 * layer_idx)
    mod = _adaln_mod(c, w_ada, b_ada)
    shift, scale, gate = mod[:, :d], mod[:, d:2 * d], mod[:, 2 * d:]
    proj = _in_proj(x, norm_g, shift, scale, w_in.astype(jnp.bfloat16), q_norm_g, k_norm_g)
    sb_o = _sb_attention(proj)
    slopes = jnp.asarray([2.0 ** (-8.0 * (h + 1) / DF_HEADS) for h in range(DF_HEADS)], jnp.float32)
    slopes = jnp.broadcast_to(slopes[:, None, None], (DF_HEADS, 1, LANES))
    row = lambda v: v.reshape(1, -1)
    df_o = _df_attention(proj, slopes, row(lambda_q1), row(lambda_k1), row(lambda_q2),
                         row(lambda_k2), row(subln_g), row(q_norm_g), row(k_norm_g), lam_init)
    return _out_proj(x, sb_o, df_o, w_out.astype(jnp.bfloat16), gate)


@jax.jit
def kernel(x, c, norm_g, w_ada, b_ada, w_in, q_norm_g, k_norm_g, lambda_q1, lambda_k1,
           lambda_q2, lambda_k2, subln_g, w_out):
    for l in range(norm_g.shape[0]):
        x = _layer(x, c, l, norm_g[l], w_ada[l], b_ada[l], w_in[l], q_norm_g[l], k_norm_g[l],
                   lambda_q1[l], lambda_k1[l], lambda_q2[l], lambda_k2[l], subln_g[l], w_out[l])
    return x
```

```python
import functools
import math

import jax
import jax.numpy as jnp
from jax import lax
from jax.experimental import pallas as pl
from jax.experimental.pallas import tpu as pltpu

D_MODEL = 1024
SB_HEADS = 8
DF_HEADS = 4
HEAD_DIM = 64
LANES = 128
SB_WIDTH = SB_HEADS * HEAD_DIM
DF_WIDTH = DF_HEADS * 2 * HEAD_DIM
IN_COLS = 4 * SB_WIDTH + 4 * DF_WIDTH
EPS = 1e-6
NEG_BIG = -1e30

CB_SB_Q, CB_SB_K, CB_SB_V, CB_SB_G = 0, 4, 8, 12
CB_DF_Q, CB_DF_K, CB_DF_V, CB_DF_G = 16, 20, 24, 28

ROWS_PROJ = 512
TQ = 256
TK = 256
DF_TQ = 512
DF_TK = 512
SB_QT = 2
SB_GROUP = 4
DF_GROUP = 2
LOG2E = math.log2(math.e)
SB_DEAD_LOG2 = -150.0
DF_DEAD_LOG2 = -150.0
DF_FIXED_SHIFT_MAX = 48.0
DF_BOUND_MARGIN = 1.02
PROJ_CHUNK = 512
VMEM_LIMIT = 48 * 1024 * 1024

_NT = (((1,), (1,)), ((), ()))


def _silu(g):
    return g / (1.0 + jnp.exp(-g))


def _adaln_kernel(ct_ref, w_ref, b_ref, o_ref):
    w = w_ref[...]
    rows = []
    for b in range(o_ref.shape[0]):
        col = ct_ref[:, b:b + 1]
        rows.append(jnp.sum(col * w, axis=0, keepdims=True))
    o_ref[...] = jnp.concatenate(rows, axis=0) + b_ref[...]


def _adaln_mod(c, w_ada, b_ada):
    bsz, d = c.shape
    n = w_ada.shape[1]
    tn = 512
    return pl.pallas_call(
        _adaln_kernel,
        out_shape=jax.ShapeDtypeStruct((bsz, n), jnp.float32),
        grid=(n // tn,),
        in_specs=[pl.BlockSpec((d, bsz), lambda j: (0, 0)),
                  pl.BlockSpec((d, tn), lambda j: (0, j)),
                  pl.BlockSpec((1, tn), lambda j: (0, j))],
        out_specs=pl.BlockSpec((bsz, tn), lambda j: (0, j)),
        name="adaln_mod",
    )(c.T, w_ada, b_ada.reshape(1, n))


def _group_rmsnorm(y, gain):
    outs = []
    lane = lax.broadcasted_iota(jnp.int32, (1, LANES), 1)
    lo = lane < HEAD_DIM
    for j in range(y.shape[1] // LANES):
        blk = y[:, j * LANES:(j + 1) * LANES]
        sq = blk * blk
        s_lo = jnp.sum(jnp.where(lo, sq, 0.0), axis=-1, keepdims=True)
        s_hi = jnp.sum(jnp.where(lo, 0.0, sq), axis=-1, keepdims=True)
        ms = jnp.where(lo, s_lo, s_hi) * (1.0 / HEAD_DIM)
        outs.append(blk * lax.rsqrt(ms + EPS) * gain)
    return jnp.concatenate(outs, axis=1)


def _in_proj_kernel(x_ref, ng_ref, shift_ref, scale_ref, w_ref, qg_ref, kg_ref, o_ref):
    x = x_ref[0]
    ms = jnp.mean(x * x, axis=-1, keepdims=True)
    h = x * lax.rsqrt(ms + EPS) * ng_ref[...]
    h = (h * (1.0 + scale_ref[0]) + shift_ref[0]).astype(jnp.bfloat16)
    inv = LOG2E / math.sqrt(HEAD_DIM)
    for ci in range(IN_COLS // PROJ_CHUNK):
        c0 = ci * PROJ_CHUNK
        y = jnp.dot(h, w_ref[:, c0:c0 + PROJ_CHUNK], preferred_element_type=jnp.float32)
        cb = c0 // LANES
        if cb == CB_SB_Q:
            y = y * inv
        elif cb == CB_DF_Q:
            y = _group_rmsnorm(y, qg_ref[...]) * inv
        elif cb == CB_DF_K:
            y = _group_rmsnorm(y, kg_ref[...])
        o_ref[0, :, c0:c0 + PROJ_CHUNK] = y.astype(o_ref.dtype)


def _in_proj(x, norm_g, shift, scale, w_in_bf16, q_norm_g, k_norm_g):
    bsz, s, d = x.shape
    tm = ROWS_PROJ
    qg = jnp.tile(q_norm_g.reshape(1, HEAD_DIM), (1, 2))
    kg = jnp.tile(k_norm_g.reshape(1, HEAD_DIM), (1, 2))
    return pl.pallas_call(
        _in_proj_kernel,
        out_shape=jax.ShapeDtypeStruct((bsz, s, IN_COLS), jnp.bfloat16),
        grid=(bsz, s // tm),
        in_specs=[pl.BlockSpec((1, tm, d), lambda b, i: (b, i, 0)),
                  pl.BlockSpec((1, d), lambda b, i: (0, 0)),
                  pl.BlockSpec((1, 1, d), lambda b, i: (b, 0, 0)),
                  pl.BlockSpec((1, 1, d), lambda b, i: (b, 0, 0)),
                  pl.BlockSpec((d, IN_COLS), lambda b, i: (0, 0)),
                  pl.BlockSpec((1, LANES), lambda b, i: (0, 0)),
                  pl.BlockSpec((1, LANES), lambda b, i: (0, 0))],
        out_specs=pl.BlockSpec((1, tm, IN_COLS), lambda b, i: (b, i, 0)),
        compiler_params=pltpu.CompilerParams(vmem_limit_bytes=VMEM_LIMIT),
        name="in_proj",
    )(x, norm_g.reshape(1, d), shift.reshape(bsz, 1, d), scale.reshape(bsz, 1, d),
      w_in_bf16, qg, kg)


def _sb_tile(qm, kblk, vblk, cum, carry, mask):
    z = lax.dot_general(qm, kblk, _NT, preferred_element_type=jnp.float32)
    nz = -z
    l1p = jnp.log(1.0 + jnp.exp2(jnp.minimum(z, nz))) * LOG2E
    lg = jnp.minimum(nz, 0.0) - l1p
    if mask is not None:
        lg = jnp.where(mask, lg, 0.0)
    rem = jnp.dot(lg.astype(jnp.bfloat16), cum, preferred_element_type=jnp.float32)
    a = jnp.exp2(lg + z + rem + carry)
    if mask is not None:
        a = jnp.where(mask, a, 0.0)
    o = jnp.dot(a.astype(jnp.bfloat16), vblk, preferred_element_type=jnp.float32)
    return o, carry + jnp.sum(lg, axis=-1, keepdims=True)


def _sb_kernel(q_ref, k_ref, v_ref, g_ref, o_ref, acc):
    step = pl.program_id(2)
    lane = lax.broadcasted_iota(jnp.int32, (1, LANES), 1)
    first = lane < HEAD_DIM
    row = lax.broadcasted_iota(jnp.int32, (TK, TK), 0)
    col = lax.broadcasted_iota(jnp.int32, (TK, TK), 1)
    cum = (row > col).astype(jnp.bfloat16)
    strict = col < row
    diag_mask = jnp.concatenate([strict, strict], axis=0)

    def stacked(u):
        q = q_ref[0, u * TQ:(u + 1) * TQ, :]
        zero = jnp.zeros_like(q)
        return jnp.concatenate([jnp.where(first, q, zero), jnp.where(first, zero, q)], axis=0)

    qs = [stacked(u) for u in range(SB_QT)]
    acc_of = [acc.at[u * 2 * TQ:(u + 1) * 2 * TQ, :] for u in range(SB_QT)]

    def tiles(qm, kbs, carry, masks):
        total = None
        for kb, mask in zip(kbs, masks):
            start = pl.multiple_of(kb * TK, TK)
            o, carry = _sb_tile(qm, k_ref[0, pl.ds(start, TK), :], v_ref[0, pl.ds(start, TK), :],
                                cum, carry, mask)
            total = o if total is None else total + o
        return total, carry

    zero_carry = jnp.zeros((2 * TQ, 1), jnp.float32)

    def first_block(tile0_has_left):
        carries = []
        for u in range(SB_QT):
            qt = step * SB_QT + u
            if u == 0 and not tile0_has_left:
                o, carry = tiles(qs[u], [qt], zero_carry, [diag_mask])
            else:
                o, carry = tiles(qs[u], [qt, qt - 1], zero_carry, [diag_mask, None])
            acc_of[u][...] = o
            carries.append(carry)
        return tuple(carries)

    carries = lax.cond(step > 0, lambda: first_block(True), lambda: first_block(False))

    def alive(carry):
        return jnp.max(carry) > SB_DEAD_LOG2

    def sweep(u, n_steps, kbs_of_step, carry):
        def cond(state):
            i, _, live = state
            return jnp.logical_and(i < n_steps, live)

        def body(state):
            i, carry, _ = state
            kbs = kbs_of_step(i)
            o, carry = tiles(qs[u], kbs, carry, [None] * len(kbs))
            acc_of[u][...] += o
            return i + 1, carry, alive(carry)

        return lax.while_loop(cond, body, (jnp.int32(0), carry, alive(carry)))[1]

    any_alive = alive(carries[0])
    for carry in carries[1:]:
        any_alive = jnp.logical_or(any_alive, alive(carry))

    @pl.when(any_alive)
    def _():
        for u in range(SB_QT):
            left = jnp.maximum(step * SB_QT + u - 1, 0)
            rem = lax.rem(left, SB_GROUP)
            carry = sweep(u, rem, lambda i, left=left: [left - 1 - i], carries[u])
            sweep(u, left // SB_GROUP,
                  lambda i, left=left, rem=rem: [left - rem - i * SB_GROUP - 1 - t
                                                 for t in range(SB_GROUP)], carry)

    for u in range(SB_QT):
        out = jnp.where(first, acc_of[u][:TQ, :], acc_of[u][TQ:, :])
        g = g_ref[0, u * TQ:(u + 1) * TQ, :].astype(jnp.float32)
        o_ref[0, u * TQ:(u + 1) * TQ, :] = (out * _silu(g)).astype(o_ref.dtype)


def _sb_attention(proj):
    bsz, s, _ = proj.shape
    rows = SB_QT * TQ
    return pl.pallas_call(
        _sb_kernel,
        out_shape=jax.ShapeDtypeStruct((bsz, s, SB_WIDTH), jnp.bfloat16),
        grid=(bsz, SB_WIDTH // LANES, s // rows),
        in_specs=[pl.BlockSpec((1, rows, LANES), lambda b, h, i: (b, i, CB_SB_Q + h)),
                  pl.BlockSpec((1, s, LANES), lambda b, h, i: (b, 0, CB_SB_K + h)),
                  pl.BlockSpec((1, s, LANES), lambda b, h, i: (b, 0, CB_SB_V + h)),
                  pl.BlockSpec((1, rows, LANES), lambda b, h, i: (b, i, CB_SB_G + h))],
        out_specs=pl.BlockSpec((1, rows, LANES), lambda b, h, i: (b, i, h)),
        scratch_shapes=[pltpu.VMEM((SB_QT * 2 * TQ, LANES), jnp.float32)],
        compiler_params=pltpu.CompilerParams(vmem_limit_bytes=VMEM_LIMIT),
        name="sb_attn",
    )(proj, proj, proj, proj)


def _df_kernel(q_ref, k_ref, v_ref, g_ref, slope_ref, lq1_ref, lk1_ref, lq2_ref, lk2_ref, sg_ref,
               qg_ref, kg_ref, o_ref, m_s, l_s, acc, *, lam_init):
    qi = pl.program_id(2)
    q = q_ref[0]
    lane = lax.broadcasted_iota(jnp.int32, (1, LANES), 1)
    first = lane < HEAD_DIM
    zero = jnp.zeros_like(q)
    qs = jnp.concatenate([jnp.where(first, q, zero), jnp.where(first, zero, q)], axis=0)
    slope = slope_ref[0] * LOG2E
    row = lax.broadcasted_iota(jnp.int32, (DF_TQ, DF_TK), 0)
    col = lax.broadcasted_iota(jnp.int32, (DF_TQ, DF_TK), 1)
    causal = col <= row
    diag_mask = jnp.concatenate([causal, causal], axis=0)

    qk_bound = (jnp.max(jnp.abs(qg_ref[...])) * jnp.max(jnp.abs(kg_ref[...]))
                * (math.sqrt(HEAD_DIM) * LOG2E * DF_BOUND_MARGIN))
    slope_s = jnp.max(slope)

    def load(first_kb, width):
        start = pl.multiple_of(first_kb * DF_TK, DF_TK)
        rel = lax.broadcasted_iota(jnp.int32, (1, width), 1).astype(jnp.float32)
        off = ((first_kb - qi) * DF_TK).astype(jnp.float32)
        bias = jnp.concatenate([slope] * (width // LANES), axis=1) * (rel + off)
        return k_ref[0, pl.ds(start, width), :], v_ref[0, pl.ds(start, width), :], bias

    def last_rel(first_kb, width):
        return ((first_kb - qi) * DF_TK + (width - 1)).astype(jnp.float32)

    def sweep(n_steps, first_kb_of_step, width, tile, needed):
        def cond(state):
            i, live = state
            return jnp.logical_and(i < n_steps, live)

        def body(state):
            i, _ = state
            tile(first_kb_of_step(i), width, None)
            return i + 1, needed(first_kb_of_step(i + 1), width)

        lax.while_loop(cond, body, (jnp.int32(0), needed(first_kb_of_step(jnp.int32(0)), width)))

    def sweep_all(tile, needed):
        tile(qi, DF_TK, diag_mask)
        rem = lax.rem(qi, DF_GROUP)
        sweep(rem, lambda i: qi - 1 - i, DF_TK, tile, needed)
        sweep(qi // DF_GROUP, lambda i: qi - rem - (i + 1) * DF_GROUP, DF_GROUP * DF_TK, tile, needed)

    def fixed_shift():
        rowpos = lax.broadcasted_iota(jnp.int32, (DF_TQ, LANES), 0).astype(jnp.float32)
        shift = qk_bound + slope * jnp.concatenate([rowpos, rowpos], axis=0)
        l_s[...] = jnp.zeros((2 * DF_TQ, LANES), jnp.float32)
        acc[...] = jnp.zeros((2 * DF_TQ, LANES), jnp.float32)

        def tile(first_kb, width, mask):
            reps = width // LANES
            kblk, vblk, bias = load(first_kb, width)
            s = (lax.dot_general(qs, kblk, _NT, preferred_element_type=jnp.float32)
                 + bias - jnp.concatenate([shift] * reps, axis=1))
            if mask is not None:
                s = jnp.where(mask, s, NEG_BIG)
            p = jnp.exp2(s)
            part = p[:, :LANES]
            for r in range(1, reps):
                part = part + p[:, r * LANES:(r + 1) * LANES]
            l_s[...] += part
            acc[...] += jnp.dot(p.astype(jnp.bfloat16), vblk, preferred_element_type=jnp.float32)

        def needed(first_kb, width):
            return slope_s * last_rel(first_kb, width) > DF_DEAD_LOG2

        sweep_all(tile, needed)
        l = jnp.sum(l_s[...], axis=-1, keepdims=True)
        return l[:DF_TQ], l[DF_TQ:]

    def running_max():
        m_s[...] = jnp.full((2 * DF_TQ, LANES), NEG_BIG, jnp.float32)
        l_s[...] = jnp.zeros((2 * DF_TQ, LANES), jnp.float32)
        acc[...] = jnp.zeros((2 * DF_TQ, LANES), jnp.float32)

        def tile(first_kb, width, mask):
            reps = width // LANES
            kblk, vblk, bias = load(first_kb, width)
            s = lax.dot_general(qs, kblk, _NT, preferred_element_type=jnp.float32) + bias
            if mask is not None:
                s = jnp.where(mask, s, NEG_BIG)
            m_old = m_s[...]
            m_new = jnp.maximum(m_old, jnp.max(s, axis=-1, keepdims=True))
            p = jnp.exp2(s - jnp.concatenate([m_new] * reps, axis=1))
            alpha = jnp.exp2(m_old - m_new)
            l_s[...] = alpha * l_s[...] + jnp.sum(p, axis=-1, keepdims=True)
            acc[...] = alpha * acc[...] + jnp.dot(p.astype(jnp.bfloat16), vblk,
                                                  preferred_element_type=jnp.float32)
            m_s[...] = m_new

        def needed(first_kb, width):
            return (qk_bound + slope_s * last_rel(first_kb, width) - jnp.min(m_s[...])
                    > DF_DEAD_LOG2)

        sweep_all(tile, needed)
        return l_s[:DF_TQ, :1], l_s[DF_TQ:, :1]

    l1, l2 = lax.cond(qk_bound < DF_FIXED_SHIFT_MAX, fixed_shift, running_max)

    lam = (jnp.exp(jnp.sum(lq1_ref[...] * lk1_ref[...], keepdims=True))
           - jnp.exp(jnp.sum(lq2_ref[...] * lk2_ref[...], keepdims=True)) + lam_init)
    o = acc[:DF_TQ, :] / l1 - lam * (acc[DF_TQ:, :] / l2)
    ms = jnp.mean(o * o, axis=-1, keepdims=True)
    o = o * lax.rsqrt(ms + EPS) * sg_ref[...] * (1.0 - lam_init)
    g = g_ref[0].astype(jnp.float32)
    o_ref[0] = (o * _silu(g)).astype(o_ref.dtype)


def _df_attention(proj, slopes, lq1, lk1, lq2, lk2, subln_g, q_norm_g, k_norm_g, lam_init):
    bsz, s, _ = proj.shape
    nq = s // DF_TQ
    vec = pl.BlockSpec((1, HEAD_DIM), lambda b, h, i: (0, 0))
    stat = pltpu.VMEM((2 * DF_TQ, LANES), jnp.float32)
    return pl.pallas_call(
        functools.partial(_df_kernel, lam_init=lam_init),
        out_shape=jax.ShapeDtypeStruct((bsz, s, DF_WIDTH), jnp.bfloat16),
        grid=(bsz, DF_HEADS, nq),
        in_specs=[pl.BlockSpec((1, DF_TQ, LANES), lambda b, h, i: (b, i, CB_DF_Q + h)),
                  pl.BlockSpec((1, s, LANES), lambda b, h, i: (b, 0, CB_DF_K + h)),
                  pl.BlockSpec((1, s, LANES), lambda b, h, i: (b, 0, CB_DF_V + h)),
                  pl.BlockSpec((1, DF_TQ, LANES), lambda b, h, i: (b, i, CB_DF_G + h)),
                  pl.BlockSpec((1, 1, LANES), lambda b, h, i: (h, 0, 0)),
                  vec, vec, vec, vec,
                  pl.BlockSpec((1, LANES), lambda b, h, i: (0, 0)),
                  vec, vec],
        out_specs=pl.BlockSpec((1, DF_TQ, LANES), lambda b, h, i: (b, i, h)),
        scratch_shapes=[stat, stat, stat],
        compiler_params=pltpu.CompilerParams(vmem_limit_bytes=VMEM_LIMIT),
        name="df_attn",
    )(proj, proj, proj, proj, slopes, lq1, lk1, lq2, lk2, subln_g, q_norm_g, k_norm_g)


def _out_proj_kernel(x_ref, sb_ref, df_ref, w_ref, gate_ref, o_ref):
    y = (jnp.dot(sb_ref[0], w_ref[:SB_WIDTH, :], preferred_element_type=jnp.float32)
         + jnp.dot(df_ref[0], w_ref[SB_WIDTH:, :], preferred_element_type=jnp.float32))
    o_ref[0] = x_ref[0] + gate_ref[0] * y


def _out_proj(x, sb_o, df_o, w_out_bf16, gate):
    bsz, s, d = x.shape
    tm = ROWS_PROJ
    return pl.pallas_call(
        _out_proj_kernel,
        out_shape=jax.ShapeDtypeStruct((bsz, s, d), jnp.float32),
        grid=(bsz, s // tm),
        in_specs=[pl.BlockSpec((1, tm, d), lambda b, i: (b, i, 0)),
                  pl.BlockSpec((1, tm, SB_WIDTH), lambda b, i: (b, i, 0)),
                  pl.BlockSpec((1, tm, DF_WIDTH), lambda b, i: (b, i, 0)),
                  pl.BlockSpec((SB_WIDTH + DF_WIDTH, d), lambda b, i: (0, 0)),
                  pl.BlockSpec((1, 1, d), lambda b, i: (b, 0, 0))],
        out_specs=pl.BlockSpec((1, tm, d), lambda b, i: (b, i, 0)),
        compiler_params=pltpu.CompilerParams(vmem_limit_bytes=VMEM_LIMIT),
        name="out_proj",
    )(x, sb_o, df_o, w_out_bf16, gate.reshape(bsz, 1, d))


def _layer(x, c, layer_idx, norm_g, w_ada, b_ada, w_in, q_norm_g, k_norm_g,
           lambda_q1, lambda_k1, lambda_q2, lambda_k2, subln_g, w_out):
    d = x.shape[-1]
    lam_init = 0.8 - 0.6 * math.exp(-0.3 * layer_idx)
    mod = _adaln_mod(c, w_ada, b_ada)
    shift, scale, gate = mod[:, :d], mod[:, d:2 * d], mod[:, 2 * d:]
    proj = _in_proj(x, norm_g, shift, scale, w_in.astype(jnp.bfloat16), q_norm_g, k_norm_g)
    sb_o = _sb_attention(proj)
    slopes = jnp.asarray([2.0 ** (-8.0 * (h + 1) / DF_HEADS) for h in range(DF_HEADS)], jnp.float32)
    slopes = jnp.broadcast_to(slopes[:, None, None], (DF_HEADS, 1, LANES))
    row = lambda v: v.reshape(1, -1)
    df_o = _df_attention(proj, slopes, row(lambda_q1), row(lambda_k1), row(lambda_q2),
                         row(lambda_k2), row(subln_g), row(q_norm_g), row(k_norm_g), lam_init)
    return _out_proj(x, sb_o, df_o, w_out.astype(jnp.bfloat16), gate)


@jax.jit
def kernel(x, c, norm_g, w_ada, b_ada, w_in, q_norm_g, k_norm_g, lambda_q1, lambda_k1,
           lambda_q2, lambda_k2, subln_g, w_out):
    for l in range(norm_g.shape[0]):
        x = _layer(x, c, l, norm_g[l], w_ada[l], b_ada[l], w_in[l], q_norm_g[l], k_norm_g[l],
                   lambda_q1[l], lambda_k1[l], lambda_q2[l], lambda_k2[l], subln_g[l], w_out[l])
    return x
```

```python
import functools
import math

import jax
import jax.numpy as jnp
from jax import lax
from jax.experimental import pallas as pl
from jax.experimental.pallas import tpu as pltpu

D_MODEL = 1024
SB_HEADS = 8
DF_HEADS = 4
HEAD_DIM = 64
LANES = 128
SB_WIDTH = SB_HEADS * HEAD_DIM
DF_WIDTH = DF_HEADS * 2 * HEAD_DIM
IN_COLS = 4 * SB_WIDTH + 4 * DF_WIDTH
EPS = 1e-6
NEG_BIG = -1e30

CB_SB_Q, CB_SB_K, CB_SB_V, CB_SB_G = 0, 4, 8, 12
CB_DF_Q, CB_DF_K, CB_DF_V, CB_DF_G = 16, 20, 24, 28

ROWS_PROJ = 512
ROWS_OUT = 1024
TQ = 256
TK = 256
DF_TQ = 512
DF_TK = 512
SB_QT = 4
SB_GROUP = 4
DF_GROUP = 2
LOG2E = math.log2(math.e)
SB_DEAD_LOG2 = -150.0
DF_DEAD_LOG2 = -150.0
DF_FIXED_SHIFT_MAX = 48.0
DF_BOUND_MARGIN = 1.02
PROJ_CHUNK = 512
VMEM_LIMIT = 48 * 1024 * 1024

_NT = (((1,), (1,)), ((), ()))


def _silu(g):
    return g / (1.0 + jnp.exp(-g))


def _adaln_kernel(ct_ref, w_ref, b_ref, o_ref):
    w = w_ref[...]
    rows = []
    for b in range(o_ref.shape[0]):
        col = ct_ref[:, b:b + 1]
        rows.append(jnp.sum(col * w, axis=0, keepdims=True))
    o_ref[...] = jnp.concatenate(rows, axis=0) + b_ref[...]


def _adaln_mod(c, w_ada, b_ada):
    bsz, d = c.shape
    n = w_ada.shape[1]
    tn = 512
    return pl.pallas_call(
        _adaln_kernel,
        out_shape=jax.ShapeDtypeStruct((bsz, n), jnp.float32),
        grid=(n // tn,),
        in_specs=[pl.BlockSpec((d, bsz), lambda j: (0, 0)),
                  pl.BlockSpec((d, tn), lambda j: (0, j)),
                  pl.BlockSpec((1, tn), lambda j: (0, j))],
        out_specs=pl.BlockSpec((bsz, tn), lambda j: (0, j)),
        name="adaln_mod",
    )(c.T, w_ada, b_ada.reshape(1, n))


def _group_rmsnorm(y, gain):
    outs = []
    lane = lax.broadcasted_iota(jnp.int32, (1, LANES), 1)
    lo = lane < HEAD_DIM
    for j in range(y.shape[1] // LANES):
        blk = y[:, j * LANES:(j + 1) * LANES]
        sq = blk * blk
        s_lo = jnp.sum(jnp.where(lo, sq, 0.0), axis=-1, keepdims=True)
        s_hi = jnp.sum(jnp.where(lo, 0.0, sq), axis=-1, keepdims=True)
        ms = jnp.where(lo, s_lo, s_hi) * (1.0 / HEAD_DIM)
        outs.append(blk * lax.rsqrt(ms + EPS) * gain)
    return jnp.concatenate(outs, axis=1)


def _in_proj_kernel(x_ref, ng_ref, shift_ref, scale_ref, w_ref, qg_ref, kg_ref, o_ref):
    x = x_ref[0]
    ms = jnp.mean(x * x, axis=-1, keepdims=True)
    h = x * lax.rsqrt(ms + EPS) * ng_ref[...]
    h = (h * (1.0 + scale_ref[0]) + shift_ref[0]).astype(jnp.bfloat16)
    inv = LOG2E / math.sqrt(HEAD_DIM)
    for ci in range(IN_COLS // PROJ_CHUNK):
        c0 = ci * PROJ_CHUNK
        y = jnp.dot(h, w_ref[:, c0:c0 + PROJ_CHUNK], preferred_element_type=jnp.float32)
        cb = c0 // LANES
        if cb == CB_SB_Q:
            y = y * inv
        elif cb == CB_DF_Q:
            y = _group_rmsnorm(y, qg_ref[...]) * inv
        elif cb == CB_DF_K:
            y = _group_rmsnorm(y, kg_ref[...])
        o_ref[0, :, c0:c0 + PROJ_CHUNK] = y.astype(o_ref.dtype)


def _in_proj(x, norm_g, shift, scale, w_in_bf16, q_norm_g, k_norm_g):
    bsz, s, d = x.shape
    tm = ROWS_PROJ
    qg = jnp.tile(q_norm_g.reshape(1, HEAD_DIM), (1, 2))
    kg = jnp.tile(k_norm_g.reshape(1, HEAD_DIM), (1, 2))
    return pl.pallas_call(
        _in_proj_kernel,
        out_shape=jax.ShapeDtypeStruct((bsz, s, IN_COLS), jnp.bfloat16),
        grid=(bsz, s // tm),
        in_specs=[pl.BlockSpec((1, tm, d), lambda b, i: (b, i, 0)),
                  pl.BlockSpec((1, d), lambda b, i: (0, 0)),
                  pl.BlockSpec((1, 1, d), lambda b, i: (b, 0, 0)),
                  pl.BlockSpec((1, 1, d), lambda b, i: (b, 0, 0)),
                  pl.BlockSpec((d, IN_COLS), lambda b, i: (0, 0)),
                  pl.BlockSpec((1, LANES), lambda b, i: (0, 0)),
                  pl.BlockSpec((1, LANES), lambda b, i: (0, 0))],
        out_specs=pl.BlockSpec((1, tm, IN_COLS), lambda b, i: (b, i, 0)),
        compiler_params=pltpu.CompilerParams(vmem_limit_bytes=VMEM_LIMIT),
        name="in_proj",
    )(x, norm_g.reshape(1, d), shift.reshape(bsz, 1, d), scale.reshape(bsz, 1, d),
      w_in_bf16, qg, kg)


def _sb_tile(qm, kblk, vblk, cum, carry, mask):
    z = lax.dot_general(qm, kblk, _NT, preferred_element_type=jnp.float32)
    nz = -z
    l1p = jnp.log(1.0 + jnp.exp2(jnp.minimum(z, nz))) * LOG2E
    lg = jnp.minimum(nz, 0.0) - l1p
    if mask is not None:
        lg = jnp.where(mask, lg, 0.0)
    rem = jnp.dot(lg.astype(jnp.bfloat16), cum, preferred_element_type=jnp.float32)
    a = jnp.exp2(lg + z + rem + carry)
    if mask is not None:
        a = jnp.where(mask, a, 0.0)
    o = jnp.dot(a.astype(jnp.bfloat16), vblk, preferred_element_type=jnp.float32)
    return o, carry + jnp.sum(lg, axis=-1, keepdims=True)


def _sb_kernel(q_ref, k_ref, v_ref, g_ref, o_ref, acc):
    step = pl.program_id(2)
    lane = lax.broadcasted_iota(jnp.int32, (1, LANES), 1)
    first = lane < HEAD_DIM
    row = lax.broadcasted_iota(jnp.int32, (TK, TK), 0)
    col = lax.broadcasted_iota(jnp.int32, (TK, TK), 1)
    cum = (row > col).astype(jnp.bfloat16)
    strict = col < row
    diag_mask = jnp.concatenate([strict, strict], axis=0)

    def stacked(u):
        q = q_ref[0, u * TQ:(u + 1) * TQ, :]
        zero = jnp.zeros_like(q)
        return jnp.concatenate([jnp.where(first, q, zero), jnp.where(first, zero, q)], axis=0)

    qs = [stacked(u) for u in range(SB_QT)]
    acc_of = [acc.at[u * 2 * TQ:(u + 1) * 2 * TQ, :] for u in range(SB_QT)]

    def tiles(qm, kbs, carry, masks):
        total = None
        for kb, mask in zip(kbs, masks):
            start = pl.multiple_of(kb * TK, TK)
            o, carry = _sb_tile(qm, k_ref[0, pl.ds(start, TK), :], v_ref[0, pl.ds(start, TK), :],
                                cum, carry, mask)
            total = o if total is None else total + o
        return total, carry

    zero_carry = jnp.zeros((2 * TQ, 1), jnp.float32)

    def first_block(tile0_has_left):
        carries = []
        for u in range(SB_QT):
            qt = step * SB_QT + u
            if u == 0 and not tile0_has_left:
                o, carry = tiles(qs[u], [qt], zero_carry, [diag_mask])
            else:
                o, carry = tiles(qs[u], [qt, qt - 1], zero_carry, [diag_mask, None])
            acc_of[u][...] = o
            carries.append(carry)
        return tuple(carries)

    carries = lax.cond(step > 0, lambda: first_block(True), lambda: first_block(False))

    def alive(carry):
        return jnp.max(carry) > SB_DEAD_LOG2

    def sweep(u, n_steps, kbs_of_step, carry):
        def cond(state):
            i, _, live = state
            return jnp.logical_and(i < n_steps, live)

        def body(state):
            i, carry, _ = state
            kbs = kbs_of_step(i)
            o, carry = tiles(qs[u], kbs, carry, [None] * len(kbs))
            acc_of[u][...] += o
            return i + 1, carry, alive(carry)

        return lax.while_loop(cond, body, (jnp.int32(0), carry, alive(carry)))[1]

    any_alive = alive(carries[0])
    for carry in carries[1:]:
        any_alive = jnp.logical_or(any_alive, alive(carry))

    @pl.when(any_alive)
    def _():
        for u in range(SB_QT):
            left = jnp.maximum(step * SB_QT + u - 1, 0)
            rem = lax.rem(left, SB_GROUP)
            carry = sweep(u, rem, lambda i, left=left: [left - 1 - i], carries[u])
            sweep(u, left // SB_GROUP,
                  lambda i, left=left, rem=rem: [left - rem - i * SB_GROUP - 1 - t
                                                 for t in range(SB_GROUP)], carry)

    for u in range(SB_QT):
        out = jnp.where(first, acc_of[u][:TQ, :], acc_of[u][TQ:, :])
        g = g_ref[0, u * TQ:(u + 1) * TQ, :].astype(jnp.float32)
        o_ref[0, u * TQ:(u + 1) * TQ, :] = (out * _silu(g)).astype(o_ref.dtype)


def _sb_attention(proj):
    bsz, s, _ = proj.shape
    rows = SB_QT * TQ
    return pl.pallas_call(
        _sb_kernel,
        out_shape=jax.ShapeDtypeStruct((bsz, s, SB_WIDTH), jnp.bfloat16),
        grid=(bsz, SB_WIDTH // LANES, s // rows),
        in_specs=[pl.BlockSpec((1, rows, LANES), lambda b, h, i: (b, i, CB_SB_Q + h)),
                  pl.BlockSpec((1, s, LANES), lambda b, h, i: (b, 0, CB_SB_K + h)),
                  pl.BlockSpec((1, s, LANES), lambda b, h, i: (b, 0, CB_SB_V + h)),
                  pl.BlockSpec((1, rows, LANES), lambda b, h, i: (b, i, CB_SB_G + h))],
        out_specs=pl.BlockSpec((1, rows, LANES), lambda b, h, i: (b, i, h)),
        scratch_shapes=[pltpu.VMEM((SB_QT * 2 * TQ, LANES), jnp.float32)],
        compiler_params=pltpu.CompilerParams(vmem_limit_bytes=VMEM_LIMIT),
        name="sb_attn",
    )(proj, proj, proj, proj)


def _df_kernel(q_ref, k_ref, v_ref, g_ref, slope_ref, lq1_ref, lk1_ref, lq2_ref, lk2_ref, sg_ref,
               qg_ref, kg_ref, o_ref, m_s, l_s, acc, *, lam_init):
    qi = pl.program_id(2)
    q = q_ref[0]
    lane = lax.broadcasted_iota(jnp.int32, (1, LANES), 1)
    first = lane < HEAD_DIM
    zero = jnp.zeros_like(q)
    qs = jnp.concatenate([jnp.where(first, q, zero), jnp.where(first, zero, q)], axis=0)
    slope = slope_ref[0] * LOG2E
    row = lax.broadcasted_iota(jnp.int32, (DF_TQ, DF_TK), 0)
    col = lax.broadcasted_iota(jnp.int32, (DF_TQ, DF_TK), 1)
    causal = col <= row
    diag_mask = jnp.concatenate([causal, causal], axis=0)

    qk_bound = (jnp.max(jnp.abs(qg_ref[...])) * jnp.max(jnp.abs(kg_ref[...]))
                * (math.sqrt(HEAD_DIM) * LOG2E * DF_BOUND_MARGIN))
    slope_s = jnp.max(slope)

    def load(first_kb, width):
        start = pl.multiple_of(first_kb * DF_TK, DF_TK)
        rel = lax.broadcasted_iota(jnp.int32, (1, width), 1).astype(jnp.float32)
        off = ((first_kb - qi) * DF_TK).astype(jnp.float32)
        bias = jnp.concatenate([slope] * (width // LANES), axis=1) * (rel + off)
        return k_ref[0, pl.ds(start, width), :], v_ref[0, pl.ds(start, width), :], bias

    def last_rel(first_kb, width):
        return ((first_kb - qi) * DF_TK + (width - 1)).astype(jnp.float32)

    def sweep(n_steps, first_kb_of_step, width, tile, needed):
        def cond(state):
            i, live = state
            return jnp.logical_and(i < n_steps, live)

        def body(state):
            i, _ = state
            tile(first_kb_of_step(i), width, None)
            return i + 1, needed(first_kb_of_step(i + 1), width)

        lax.while_loop(cond, body, (jnp.int32(0), needed(first_kb_of_step(jnp.int32(0)), width)))

    def sweep_all(tile, needed):
        tile(qi, DF_TK, diag_mask)
        rem = lax.rem(qi, DF_GROUP)
        sweep(rem, lambda i: qi - 1 - i, DF_TK, tile, needed)
        sweep(qi // DF_GROUP, lambda i: qi - rem - (i + 1) * DF_GROUP, DF_GROUP * DF_TK, tile, needed)

    def fixed_shift():
        rowpos = lax.broadcasted_iota(jnp.int32, (DF_TQ, LANES), 0).astype(jnp.float32)
        shift = qk_bound + slope * jnp.concatenate([rowpos, rowpos], axis=0)
        l_s[...] = jnp.zeros((2 * DF_TQ, LANES), jnp.float32)
        acc[...] = jnp.zeros((2 * DF_TQ, LANES), jnp.float32)

        def tile(first_kb, width, mask):
            reps = width // LANES
            kblk, vblk, bias = load(first_kb, width)
            s = (lax.dot_general(qs, kblk, _NT, preferred_element_type=jnp.float32)
                 + bias - jnp.concatenate([shift] * reps, axis=1))
            if mask is not None:
                s = jnp.where(mask, s, NEG_BIG)
            p = jnp.exp2(s)
            part = p[:, :LANES]
            for r in range(1, reps):
                part = part + p[:, r * LANES:(r + 1) * LANES]
            l_s[...] += part
            acc[...] += jnp.dot(p.astype(jnp.bfloat16), vblk, preferred_element_type=jnp.float32)

        def needed(first_kb, width):
            return slope_s * last_rel(first_kb, width) > DF_DEAD_LOG2

        sweep_all(tile, needed)
        l = jnp.sum(l_s[...], axis=-1, keepdims=True)
        return l[:DF_TQ], l[DF_TQ:]

    def running_max():
        m_s[...] = jnp.full((2 * DF_TQ, LANES), NEG_BIG, jnp.float32)
        l_s[...] = jnp.zeros((2 * DF_TQ, LANES), jnp.float32)
        acc[...] = jnp.zeros((2 * DF_TQ, LANES), jnp.float32)

        def tile(first_kb, width, mask):
            reps = width // LANES
            kblk, vblk, bias = load(first_kb, width)
            s = lax.dot_general(qs, kblk, _NT, preferred_element_type=jnp.float32) + bias
            if mask is not None:
                s = jnp.where(mask, s, NEG_BIG)
            m_old = m_s[...]
            m_new = jnp.maximum(m_old, jnp.max(s, axis=-1, keepdims=True))
            p = jnp.exp2(s - jnp.concatenate([m_new] * reps, axis=1))
            alpha = jnp.exp2(m_old - m_new)
            l_s[...] = alpha * l_s[...] + jnp.sum(p, axis=-1, keepdims=True)
            acc[...] = alpha * acc[...] + jnp.dot(p.astype(jnp.bfloat16), vblk,
                                                  preferred_element_type=jnp.float32)
            m_s[...] = m_new

        def needed(first_kb, width):
            return (qk_bound + slope_s * last_rel(first_kb, width) - jnp.min(m_s[...])
                    > DF_DEAD_LOG2)

        sweep_all(tile, needed)
        return l_s[:DF_TQ, :1], l_s[DF_TQ:, :1]

    l1, l2 = lax.cond(qk_bound < DF_FIXED_SHIFT_MAX, fixed_shift, running_max)

    lam = (jnp.exp(jnp.sum(lq1_ref[...] * lk1_ref[...], keepdims=True))
           - jnp.exp(jnp.sum(lq2_ref[...] * lk2_ref[...], keepdims=True)) + lam_init)
    o = acc[:DF_TQ, :] / l1 - lam * (acc[DF_TQ:, :] / l2)
    ms = jnp.mean(o * o, axis=-1, keepdims=True)
    o = o * lax.rsqrt(ms + EPS) * sg_ref[...] * (1.0 - lam_init)
    g = g_ref[0].astype(jnp.float32)
    o_ref[0] = (o * _silu(g)).astype(o_ref.dtype)


def _df_attention(proj, slopes, lq1, lk1, lq2, lk2, subln_g, q_norm_g, k_norm_g, lam_init):
    bsz, s, _ = proj.shape
    nq = s // DF_TQ
    vec = pl.BlockSpec((1, HEAD_DIM), lambda b, h, i: (0, 0))
    stat = pltpu.VMEM((2 * DF_TQ, LANES), jnp.float32)
    return pl.pallas_call(
        functools.partial(_df_kernel, lam_init=lam_init),
        out_shape=jax.ShapeDtypeStruct((bsz, s, DF_WIDTH), jnp.bfloat16),
        grid=(bsz, DF_HEADS, nq),
        in_specs=[pl.BlockSpec((1, DF_TQ, LANES), lambda b, h, i: (b, i, CB_DF_Q + h)),
                  pl.BlockSpec((1, s, LANES), lambda b, h, i: (b, 0, CB_DF_K + h)),
                  pl.BlockSpec((1, s, LANES), lambda b, h, i: (b, 0, CB_DF_V + h)),
                  pl.BlockSpec((1, DF_TQ, LANES), lambda b, h, i: (b, i, CB_DF_G + h)),
                  pl.BlockSpec((1, 1, LANES), lambda b, h, i: (h, 0, 0)),
                  vec, vec, vec, vec,
                  pl.BlockSpec((1, LANES), lambda b, h, i: (0, 0)),
                  vec, vec],
        out_specs=pl.BlockSpec((1, DF_TQ, LANES), lambda b, h, i: (b, i, h)),
        scratch_shapes=[stat, stat, stat],
        compiler_params=pltpu.CompilerParams(vmem_limit_bytes=VMEM_LIMIT),
        name="df_attn",
    )(proj, proj, proj, proj, slopes, lq1, lk1, lq2, lk2, subln_g, q_norm_g, k_norm_g)


def _out_proj_kernel(x_ref, sb_ref, df_ref, w_ref, gate_ref, o_ref):
    y = (jnp.dot(sb_ref[0], w_ref[:SB_WIDTH, :], preferred_element_type=jnp.float32)
         + jnp.dot(df_ref[0], w_ref[SB_WIDTH:, :], preferred_element_type=jnp.float32))
    o_ref[0] = x_ref[0] + gate_ref[0] * y


def _out_proj(x, sb_o, df_o, w_out_bf16, gate):
    bsz, s, d = x.shape
    tm = ROWS_OUT
    return pl.pallas_call(
        _out_proj_kernel,
        out_shape=jax.ShapeDtypeStruct((bsz, s, d), jnp.float32),
        grid=(bsz, s // tm),
        in_specs=[pl.BlockSpec((1, tm, d), lambda b, i: (b, i, 0)),
                  pl.BlockSpec((1, tm, SB_WIDTH), lambda b, i: (b, i, 0)),
                  pl.BlockSpec((1, tm, DF_WIDTH), lambda b, i: (b, i, 0)),
                  pl.BlockSpec((SB_WIDTH + DF_WIDTH, d), lambda b, i: (0, 0)),
                  pl.BlockSpec((1, 1, d), lambda b, i: (b, 0, 0))],
        out_specs=pl.BlockSpec((1, tm, d), lambda b, i: (b, i, 0)),
        compiler_params=pltpu.CompilerParams(vmem_limit_bytes=VMEM_LIMIT),
        name="out_proj",
    )(x, sb_o, df_o, w_out_bf16, gate.reshape(bsz, 1, d))


def _layer(x, c, layer_idx, norm_g, w_ada, b_ada, w_in, q_norm_g, k_norm_g,
           lambda_q1, lambda_k1, lambda_q2, lambda_k2, subln_g, w_out):
    d = x.shape[-1]
    lam_init = 0.8 - 0.6 * math.exp(-0.3 * layer_idx)
    mod = _adaln_mod(c, w_ada, b_ada)
    shift, scale, gate = mod[:, :d], mod[:, d:2 * d], mod[:, 2 * d:]
    proj = _in_proj(x, norm_g, shift, scale, w_in.astype(jnp.bfloat16), q_norm_g, k_norm_g)
    sb_o = _sb_attention(proj)
    slopes = jnp.asarray([2.0 ** (-8.0 * (h + 1) / DF_HEADS) for h in range(DF_HEADS)], jnp.float32)
    slopes = jnp.broadcast_to(slopes[:, None, None], (DF_HEADS, 1, LANES))
    row = lambda v: v.reshape(1, -1)
    df_o = _df_attention(proj, slopes, row(lambda_q1), row(lambda_k1), row(lambda_q2),
                         row(lambda_k2), row(subln_g), row(q_norm_g), row(k_norm_g), lam_init)
    return _out_proj(x, sb_o, df_o, w_out.astype(jnp.bfloat16), gate)


@jax.jit
def kernel(x, c, norm_g, w_ada, b_ada, w_in, q_norm_g, k_norm_g, lambda_q1, lambda_k1,
           lambda_q2, lambda_k2, subln_g, w_out):
    for l in range(norm_g.shape[0]):
        x = _layer(x, c, l, norm_g[l], w_ada[l], b_ada[l], w_in[l], q_norm_g[l], k_norm_g[l],
                   lambda_q1[l], lambda_k1[l], lambda_q2[l], lambda_k2[l], subln_g[l], w_out[l])
    return x
```

```python
import functools
import math

import jax
import jax.numpy as jnp
from jax import lax
from jax.experimental import pallas as pl
from jax.experimental.pallas import tpu as pltpu

D_MODEL = 1024
SB_HEADS = 8
DF_HEADS = 4
HEAD_DIM = 64
LANES = 128
SB_WIDTH = SB_HEADS * HEAD_DIM
DF_WIDTH = DF_HEADS * 2 * HEAD_DIM
IN_COLS = 4 * SB_WIDTH + 4 * DF_WIDTH
EPS = 1e-6
NEG_BIG = -1e30

CB_SB_Q, CB_SB_K, CB_SB_V, CB_SB_G = 0, 4, 8, 12
CB_DF_Q, CB_DF_K, CB_DF_V, CB_DF_G = 16, 20, 24, 28

ROWS_PROJ = 512
ROWS_OUT = 1024
TQ = 256
TK = 256
DF_TQ = 512
DF_TK = 512
SB_QT = 4
SB_GROUP = 4
DF_GROUP = 2
LOG2E = math.log2(math.e)
SB_DEAD_LOG2 = -150.0
DF_DEAD_LOG2 = -150.0
DF_FIXED_SHIFT_MAX = 48.0
DF_BOUND_MARGIN = 1.02
PROJ_CHUNK = 512
VMEM_LIMIT = 48 * 1024 * 1024

_NT = (((1,), (1,)), ((), ()))


def _silu(g):
    return g / (1.0 + jnp.exp(-g))


def _adaln_kernel(ct_ref, w_ref, b_ref, o_ref):
    w = w_ref[...]
    rows = []
    for b in range(o_ref.shape[0]):
        col = ct_ref[:, b:b + 1]
        rows.append(jnp.sum(col * w, axis=0, keepdims=True))
    o_ref[...] = jnp.concatenate(rows, axis=0) + b_ref[...]


def _adaln_mod(c, w_ada, b_ada):
    bsz, d = c.shape
    n = w_ada.shape[1]
    tn = 512
    return pl.pallas_call(
        _adaln_kernel,
        out_shape=jax.ShapeDtypeStruct((bsz, n), jnp.float32),
        grid=(n // tn,),
        in_specs=[pl.BlockSpec((d, bsz), lambda j: (0, 0)),
                  pl.BlockSpec((d, tn), lambda j: (0, j)),
                  pl.BlockSpec((1, tn), lambda j: (0, j))],
        out_specs=pl.BlockSpec((bsz, tn), lambda j: (0, j)),
        name="adaln_mod",
    )(c.T, w_ada, b_ada.reshape(1, n))


def _group_rmsnorm(y, gain):
    outs = []
    lane = lax.broadcasted_iota(jnp.int32, (1, LANES), 1)
    lo = lane < HEAD_DIM
    for j in range(y.shape[1] // LANES):
        blk = y[:, j * LANES:(j + 1) * LANES]
        sq = blk * blk
        s_lo = jnp.sum(jnp.where(lo, sq, 0.0), axis=-1, keepdims=True)
        s_hi = jnp.sum(jnp.where(lo, 0.0, sq), axis=-1, keepdims=True)
        ms = jnp.where(lo, s_lo, s_hi) * (1.0 / HEAD_DIM)
        outs.append(blk * lax.rsqrt(ms + EPS) * gain)
    return jnp.concatenate(outs, axis=1)


def _in_proj_kernel(x_ref, ng_ref, shift_ref, scale_ref, w_ref, qg_ref, kg_ref, o_ref):
    x = x_ref[0]
    ms = jnp.mean(x * x, axis=-1, keepdims=True)
    h = x * lax.rsqrt(ms + EPS) * ng_ref[...]
    h = (h * (1.0 + scale_ref[0]) + shift_ref[0]).astype(jnp.bfloat16)
    inv = LOG2E / math.sqrt(HEAD_DIM)
    for ci in range(IN_COLS // PROJ_CHUNK):
        c0 = ci * PROJ_CHUNK
        y = jnp.dot(h, w_ref[:, c0:c0 + PROJ_CHUNK], preferred_element_type=jnp.float32)
        cb = c0 // LANES
        if cb == CB_SB_Q:
            y = y * inv
        elif cb == CB_DF_Q:
            y = _group_rmsnorm(y, qg_ref[...]) * inv
        elif cb == CB_DF_K:
            y = _group_rmsnorm(y, kg_ref[...])
        o_ref[0, :, c0:c0 + PROJ_CHUNK] = y.astype(o_ref.dtype)


def _in_proj(x, norm_g, shift, scale, w_in_bf16, q_norm_g, k_norm_g):
    bsz, s, d = x.shape
    tm = ROWS_PROJ
    qg = jnp.tile(q_norm_g.reshape(1, HEAD_DIM), (1, 2))
    kg = jnp.tile(k_norm_g.reshape(1, HEAD_DIM), (1, 2))
    return pl.pallas_call(
        _in_proj_kernel,
        out_shape=jax.ShapeDtypeStruct((bsz, s, IN_COLS), jnp.bfloat16),
        grid=(bsz, s // tm),
        in_specs=[pl.BlockSpec((1, tm, d), lambda b, i: (b, i, 0)),
                  pl.BlockSpec((1, d), lambda b, i: (0, 0)),
                  pl.BlockSpec((1, 1, d), lambda b, i: (b, 0, 0)),
                  pl.BlockSpec((1, 1, d), lambda b, i: (b, 0, 0)),
                  pl.BlockSpec((d, IN_COLS), lambda b, i: (0, 0)),
                  pl.BlockSpec((1, LANES), lambda b, i: (0, 0)),
                  pl.BlockSpec((1, LANES), lambda b, i: (0, 0))],
        out_specs=pl.BlockSpec((1, tm, IN_COLS), lambda b, i: (b, i, 0)),
        compiler_params=pltpu.CompilerParams(vmem_limit_bytes=VMEM_LIMIT),
        name="in_proj",
    )(x, norm_g.reshape(1, d), shift.reshape(bsz, 1, d), scale.reshape(bsz, 1, d),
      w_in_bf16, qg, kg)


def _sb_tile(qm, kblk, vblk, cum, carry, mask):
    z = lax.dot_general(qm, kblk, _NT, preferred_element_type=jnp.float32)
    if mask is not None:
        z = jnp.where(mask, z, NEG_BIG)
    nz = -z
    l1p = jnp.log2(1.0 + jnp.exp2(jnp.minimum(z, nz)))
    lg = jnp.minimum(nz, 0.0) - l1p
    rem = jnp.dot(lg.astype(jnp.bfloat16), cum, preferred_element_type=jnp.float32)
    tail = rem + lg
    a = jnp.exp2(tail + z + carry)
    o = jnp.dot(a.astype(jnp.bfloat16), vblk, preferred_element_type=jnp.float32)
    return o, carry + tail[:, :1]


def _sb_kernel(q_ref, k_ref, v_ref, g_ref, o_ref, acc):
    step = pl.program_id(2)
    lane = lax.broadcasted_iota(jnp.int32, (1, LANES), 1)
    first = lane < HEAD_DIM
    row = lax.broadcasted_iota(jnp.int32, (TK, TK), 0)
    col = lax.broadcasted_iota(jnp.int32, (TK, TK), 1)
    cum = (row > col).astype(jnp.bfloat16)
    strict = col < row
    diag_mask = jnp.concatenate([strict, strict], axis=0)

    def stacked(u):
        q = q_ref[0, u * TQ:(u + 1) * TQ, :]
        zero = jnp.zeros_like(q)
        return jnp.concatenate([jnp.where(first, q, zero), jnp.where(first, zero, q)], axis=0)

    qs = [stacked(u) for u in range(SB_QT)]
    acc_of = [acc.at[u * 2 * TQ:(u + 1) * 2 * TQ, :] for u in range(SB_QT)]

    def tiles(qm, kbs, carry, masks):
        total = None
        for kb, mask in zip(kbs, masks):
            start = pl.multiple_of(kb * TK, TK)
            o, carry = _sb_tile(qm, k_ref[0, pl.ds(start, TK), :], v_ref[0, pl.ds(start, TK), :],
                                cum, carry, mask)
            total = o if total is None else total + o
        return total, carry

    zero_carry = jnp.zeros((2 * TQ, 1), jnp.float32)

    def first_block(tile0_has_left):
        carries = []
        for u in range(SB_QT):
            qt = step * SB_QT + u
            if u == 0 and not tile0_has_left:
                o, carry = tiles(qs[u], [qt], zero_carry, [diag_mask])
            else:
                o, carry = tiles(qs[u], [qt, qt - 1], zero_carry, [diag_mask, None])
            acc_of[u][...] = o
            carries.append(carry)
        return tuple(carries)

    carries = lax.cond(step > 0, lambda: first_block(True), lambda: first_block(False))

    def alive(carry):
        return jnp.max(carry) > SB_DEAD_LOG2

    def sweep(u, n_steps, kbs_of_step, carry):
        def cond(state):
            i, _, live = state
            return jnp.logical_and(i < n_steps, live)

        def body(state):
            i, carry, _ = state
            kbs = kbs_of_step(i)
            o, carry = tiles(qs[u], kbs, carry, [None] * len(kbs))
            acc_of[u][...] += o
            return i + 1, carry, alive(carry)

        return lax.while_loop(cond, body, (jnp.int32(0), carry, alive(carry)))[1]

    any_alive = alive(carries[0])
    for carry in carries[1:]:
        any_alive = jnp.logical_or(any_alive, alive(carry))

    @pl.when(any_alive)
    def _():
        for u in range(SB_QT):
            left = jnp.maximum(step * SB_QT + u - 1, 0)
            rem = lax.rem(left, SB_GROUP)
            carry = sweep(u, rem, lambda i, left=left: [left - 1 - i], carries[u])
            sweep(u, left // SB_GROUP,
                  lambda i, left=left, rem=rem: [left - rem - i * SB_GROUP - 1 - t
                                                 for t in range(SB_GROUP)], carry)

    for u in range(SB_QT):
        out = jnp.where(first, acc_of[u][:TQ, :], acc_of[u][TQ:, :])
        g = g_ref[0, u * TQ:(u + 1) * TQ, :].astype(jnp.float32)
        o_ref[0, u * TQ:(u + 1) * TQ, :] = (out * _silu(g)).astype(o_ref.dtype)


def _sb_attention(proj):
    bsz, s, _ = proj.shape
    rows = SB_QT * TQ
    return pl.pallas_call(
        _sb_kernel,
        out_shape=jax.ShapeDtypeStruct((bsz, s, SB_WIDTH), jnp.bfloat16),
        grid=(bsz, SB_WIDTH // LANES, s // rows),
        in_specs=[pl.BlockSpec((1, rows, LANES), lambda b, h, i: (b, i, CB_SB_Q + h)),
                  pl.BlockSpec((1, s, LANES), lambda b, h, i: (b, 0, CB_SB_K + h)),
                  pl.BlockSpec((1, s, LANES), lambda b, h, i: (b, 0, CB_SB_V + h)),
                  pl.BlockSpec((1, rows, LANES), lambda b, h, i: (b, i, CB_SB_G + h))],
        out_specs=pl.BlockSpec((1, rows, LANES), lambda b, h, i: (b, i, h)),
        scratch_shapes=[pltpu.VMEM((SB_QT * 2 * TQ, LANES), jnp.float32)],
        compiler_params=pltpu.CompilerParams(vmem_limit_bytes=VMEM_LIMIT),
        name="sb_attn",
    )(proj, proj, proj, proj)


def _df_kernel(q_ref, k_ref, v_ref, g_ref, slope_ref, lq1_ref, lk1_ref, lq2_ref, lk2_ref, sg_ref,
               qg_ref, kg_ref, o_ref, m_s, l_s, acc, *, lam_init):
    qi = pl.program_id(2)
    q = q_ref[0]
    lane = lax.broadcasted_iota(jnp.int32, (1, LANES), 1)
    first = lane < HEAD_DIM
    zero = jnp.zeros_like(q)
    qs = jnp.concatenate([jnp.where(first, q, zero), jnp.where(first, zero, q)], axis=0)
    slope = slope_ref[0] * LOG2E
    row = lax.broadcasted_iota(jnp.int32, (DF_TQ, DF_TK), 0)
    col = lax.broadcasted_iota(jnp.int32, (DF_TQ, DF_TK), 1)
    causal = col <= row
    diag_mask = jnp.concatenate([causal, causal], axis=0)

    qk_bound = (jnp.max(jnp.abs(qg_ref[...])) * jnp.max(jnp.abs(kg_ref[...]))
                * (math.sqrt(HEAD_DIM) * LOG2E * DF_BOUND_MARGIN))
    slope_s = jnp.max(slope)

    def load(first_kb, width):
        start = pl.multiple_of(first_kb * DF_TK, DF_TK)
        rel = lax.broadcasted_iota(jnp.int32, (1, width), 1).astype(jnp.float32)
        off = ((first_kb - qi) * DF_TK).astype(jnp.float32)
        bias = jnp.concatenate([slope] * (width // LANES), axis=1) * (rel + off)
        return k_ref[0, pl.ds(start, width), :], v_ref[0, pl.ds(start, width), :], bias

    def last_rel(first_kb, width):
        return ((first_kb - qi) * DF_TK + (width - 1)).astype(jnp.float32)

    def sweep(n_steps, first_kb_of_step, width, tile, needed):
        def cond(state):
            i, live = state
            return jnp.logical_and(i < n_steps, live)

        def body(state):
            i, _ = state
            tile(first_kb_of_step(i), width, None)
            return i + 1, needed(first_kb_of_step(i + 1), width)

        lax.while_loop(cond, body, (jnp.int32(0), needed(first_kb_of_step(jnp.int32(0)), width)))

    def sweep_all(tile, needed):
        tile(qi, DF_TK, diag_mask)
        rem = lax.rem(qi, DF_GROUP)
        sweep(rem, lambda i: qi - 1 - i, DF_TK, tile, needed)
        sweep(qi // DF_GROUP, lambda i: qi - rem - (i + 1) * DF_GROUP, DF_GROUP * DF_TK, tile, needed)

    def fixed_shift():
        rowpos = lax.broadcasted_iota(jnp.int32, (DF_TQ, LANES), 0).astype(jnp.float32)
        shift = qk_bound + slope * jnp.concatenate([rowpos, rowpos], axis=0)
        l_s[...] = jnp.zeros((2 * DF_TQ, LANES), jnp.float32)
        acc[...] = jnp.zeros((2 * DF_TQ, LANES), jnp.float32)

        def tile(first_kb, width, mask):
            reps = width // LANES
            kblk, vblk, bias = load(first_kb, width)
            s = (lax.dot_general(qs, kblk, _NT, preferred_element_type=jnp.float32)
                 + bias - jnp.concatenate([shift] * reps, axis=1))
            if mask is not None:
                s = jnp.where(mask, s, NEG_BIG)
            p = jnp.exp2(s)
            part = p[:, :LANES]
            for r in range(1, reps):
                part = part + p[:, r * LANES:(r + 1) * LANES]
            l_s[...] += part
            acc[...] += jnp.dot(p.astype(jnp.bfloat16), vblk, preferred_element_type=jnp.float32)

        def needed(first_kb, width):
            return slope_s * last_rel(first_kb, width) > DF_DEAD_LOG2

        sweep_all(tile, needed)
        l = jnp.sum(l_s[...], axis=-1, keepdims=True)
        return l[:DF_TQ], l[DF_TQ:]

    def running_max():
        m_s[...] = jnp.full((2 * DF_TQ, LANES), NEG_BIG, jnp.float32)
        l_s[...] = jnp.zeros((2 * DF_TQ, LANES), jnp.float32)
        acc[...] = jnp.zeros((2 * DF_TQ, LANES), jnp.float32)

        def tile(first_kb, width, mask):
            reps = width // LANES
            kblk, vblk, bias = load(first_kb, width)
            s = lax.dot_general(qs, kblk, _NT, preferred_element_type=jnp.float32) + bias
            if mask is not None:
                s = jnp.where(mask, s, NEG_BIG)
            m_old = m_s[...]
            m_new = jnp.maximum(m_old, jnp.max(s, axis=-1, keepdims=True))
            p = jnp.exp2(s - jnp.concatenate([m_new] * reps, axis=1))
            alpha = jnp.exp2(m_old - m_new)
            l_s[...] = alpha * l_s[...] + jnp.sum(p, axis=-1, keepdims=True)
            acc[...] = alpha * acc[...] + jnp.dot(p.astype(jnp.bfloat16), vblk,
                                                  preferred_element_type=jnp.float32)
            m_s[...] = m_new

        def needed(first_kb, width):
            return (qk_bound + slope_s * last_rel(first_kb, width) - jnp.min(m_s[...])
                    > DF_DEAD_LOG2)

        sweep_all(tile, needed)
        return l_s[:DF_TQ, :1], l_s[DF_TQ:, :1]

    l1, l2 = lax.cond(qk_bound < DF_FIXED_SHIFT_MAX, fixed_shift, running_max)

    lam = (jnp.exp(jnp.sum(lq1_ref[...] * lk1_ref[...], keepdims=True))
           - jnp.exp(jnp.sum(lq2_ref[...] * lk2_ref[...], keepdims=True)) + lam_init)
    o = acc[:DF_TQ, :] / l1 - lam * (acc[DF_TQ:, :] / l2)
    ms = jnp.mean(o * o, axis=-1, keepdims=True)
    o = o * lax.rsqrt(ms + EPS) * sg_ref[...] * (1.0 - lam_init)
    g = g_ref[0].astype(jnp.float32)
    o_ref[0] = (o * _silu(g)).astype(o_ref.dtype)


def _df_attention(proj, slopes, lq1, lk1, lq2, lk2, subln_g, q_norm_g, k_norm_g, lam_init):
    bsz, s, _ = proj.shape
    nq = s // DF_TQ
    vec = pl.BlockSpec((1, HEAD_DIM), lambda b, h, i: (0, 0))
    stat = pltpu.VMEM((2 * DF_TQ, LANES), jnp.float32)
    return pl.pallas_call(
        functools.partial(_df_kernel, lam_init=lam_init),
        out_shape=jax.ShapeDtypeStruct((bsz, s, DF_WIDTH), jnp.bfloat16),
        grid=(bsz, DF_HEADS, nq),
        in_specs=[pl.BlockSpec((1, DF_TQ, LANES), lambda b, h, i: (b, i, CB_DF_Q + h)),
                  pl.BlockSpec((1, s, LANES), lambda b, h, i: (b, 0, CB_DF_K + h)),
                  pl.BlockSpec((1, s, LANES), lambda b, h, i: (b, 0, CB_DF_V + h)),
                  pl.BlockSpec((1, DF_TQ, LANES), lambda b, h, i: (b, i, CB_DF_G + h)),
                  pl.BlockSpec((1, 1, LANES), lambda b, h, i: (h, 0, 0)),
                  vec, vec, vec, vec,
                  pl.BlockSpec((1, LANES), lambda b, h, i: (0, 0)),
                  vec, vec],
        out_specs=pl.BlockSpec((1, DF_TQ, LANES), lambda b, h, i: (b, i, h)),
        scratch_shapes=[stat, stat, stat],
        compiler_params=pltpu.CompilerParams(vmem_limit_bytes=VMEM_LIMIT),
        name="df_attn",
    )(proj, proj, proj, proj, slopes, lq1, lk1, lq2, lk2, subln_g, q_norm_g, k_norm_g)


def _out_proj_kernel(x_ref, sb_ref, df_ref, w_ref, gate_ref, o_ref):
    y = (jnp.dot(sb_ref[0], w_ref[:SB_WIDTH, :], preferred_element_type=jnp.float32)
         + jnp.dot(df_ref[0], w_ref[SB_WIDTH:, :], preferred_element_type=jnp.float32))
    o_ref[0] = x_ref[0] + gate_ref[0] * y


def _out_proj(x, sb_o, df_o, w_out_bf16, gate):
    bsz, s, d = x.shape
    tm = ROWS_OUT
    return pl.pallas_call(
        _out_proj_kernel,
        out_shape=jax.ShapeDtypeStruct((bsz, s, d), jnp.float32),
        grid=(bsz, s // tm),
        in_specs=[pl.BlockSpec((1, tm, d), lambda b, i: (b, i, 0)),
                  pl.BlockSpec((1, tm, SB_WIDTH), lambda b, i: (b, i, 0)),
                  pl.BlockSpec((1, tm, DF_WIDTH), lambda b, i: (b, i, 0)),
                  pl.BlockSpec((SB_WIDTH + DF_WIDTH, d), lambda b, i: (0, 0)),
                  pl.BlockSpec((1, 1, d), lambda b, i: (b, 0, 0))],
        out_specs=pl.BlockSpec((1, tm, d), lambda b, i: (b, i, 0)),
        compiler_params=pltpu.CompilerParams(vmem_limit_bytes=VMEM_LIMIT),
        name="out_proj",
    )(x, sb_o, df_o, w_out_bf16, gate.reshape(bsz, 1, d))


def _layer(x, c, layer_idx, norm_g, w_ada, b_ada, w_in, q_norm_g, k_norm_g,
           lambda_q1, lambda_k1, lambda_q2, lambda_k2, subln_g, w_out):
    d = x.shape[-1]
    lam_init = 0.8 - 0.6 * math.exp(-0.3 * layer_idx)
    mod = _adaln_mod(c, w_ada, b_ada)
    shift, scale, gate = mod[:, :d], mod[:, d:2 * d], mod[:, 2 * d:]
    proj = _in_proj(x, norm_g, shift, scale, w_in.astype(jnp.bfloat16), q_norm_g, k_norm_g)
    sb_o = _sb_attention(proj)
    slopes = jnp.asarray([2.0 ** (-8.0 * (h + 1) / DF_HEADS) for h in range(DF_HEADS)], jnp.float32)
    slopes = jnp.broadcast_to(slopes[:, None, None], (DF_HEADS, 1, LANES))
    row = lambda v: v.reshape(1, -1)
    df_o = _df_attention(proj, slopes, row(lambda_q1), row(lambda_k1), row(lambda_q2),
                         row(lambda_k2), row(subln_g), row(q_norm_g), row(k_norm_g), lam_init)
    return _out_proj(x, sb_o, df_o, w_out.astype(jnp.bfloat16), gate)


@jax.jit
def kernel(x, c, norm_g, w_ada, b_ada, w_in, q_norm_g, k_norm_g, lambda_q1, lambda_k1,
           lambda_q2, lambda_k2, subln_g, w_out):
    for l in range(norm_g.shape[0]):
        x = _layer(x, c, l, norm_g[l], w_ada[l], b_ada[l], w_in[l], q_norm_g[l], k_norm_g[l],
                   lambda_q1[l], lambda_k1[l], lambda_q2[l], lambda_k2[l], subln_g[l], w_out[l])
    return x
```

```python
import functools
import math

import jax
import jax.numpy as jnp
from jax import lax
from jax.experimental import pallas as pl
from jax.experimental.pallas import tpu as pltpu

D_MODEL = 1024
SB_HEADS = 8
DF_HEADS = 4
HEAD_DIM = 64
LANES = 128
SB_WIDTH = SB_HEADS * HEAD_DIM
DF_WIDTH = DF_HEADS * 2 * HEAD_DIM
IN_COLS = 4 * SB_WIDTH + 4 * DF_WIDTH
EPS = 1e-6
NEG_BIG = -1e30

CB_SB_Q, CB_SB_K, CB_SB_V, CB_SB_G = 0, 4, 8, 12
CB_DF_Q, CB_DF_K, CB_DF_V, CB_DF_G = 16, 20, 24, 28

ROWS_PROJ = 512
ROWS_OUT = 1024
TQ = 256
TK = 256
DF_TQ = 512
DF_TK = 512
SB_QT = DF_TQ // TQ
SB_GROUP = 4
DF_GROUP = 2
LOG2E = math.log2(math.e)
SB_DEAD_LOG2 = -150.0
DF_DEAD_LOG2 = -150.0
DF_FIXED_SHIFT_MAX = 48.0
DF_BOUND_MARGIN = 1.02
PROJ_CHUNK = 512
VMEM_LIMIT = 48 * 1024 * 1024

_NT = (((1,), (1,)), ((), ()))


def _silu(g):
    return g / (1.0 + jnp.exp(-g))


def _adaln_kernel(ct_ref, w_ref, b_ref, o_ref):
    w = w_ref[...]
    rows = []
    for b in range(o_ref.shape[0]):
        col = ct_ref[:, b:b + 1]
        rows.append(jnp.sum(col * w, axis=0, keepdims=True))
    o_ref[...] = jnp.concatenate(rows, axis=0) + b_ref[...]


def _adaln_mod(c, w_ada, b_ada):
    bsz, d = c.shape
    n = w_ada.shape[1]
    tn = 512
    return pl.pallas_call(
        _adaln_kernel,
        out_shape=jax.ShapeDtypeStruct((bsz, n), jnp.float32),
        grid=(n // tn,),
        in_specs=[pl.BlockSpec((d, bsz), lambda j: (0, 0)),
                  pl.BlockSpec((d, tn), lambda j: (0, j)),
                  pl.BlockSpec((1, tn), lambda j: (0, j))],
        out_specs=pl.BlockSpec((bsz, tn), lambda j: (0, j)),
        name="adaln_mod",
    )(c.T, w_ada, b_ada.reshape(1, n))


def _group_rmsnorm(y, gain):
    outs = []
    lane = lax.broadcasted_iota(jnp.int32, (1, LANES), 1)
    lo = lane < HEAD_DIM
    for j in range(y.shape[1] // LANES):
        blk = y[:, j * LANES:(j + 1) * LANES]
        sq = blk * blk
        s_lo = jnp.sum(jnp.where(lo, sq, 0.0), axis=-1, keepdims=True)
        s_hi = jnp.sum(jnp.where(lo, 0.0, sq), axis=-1, keepdims=True)
        ms = jnp.where(lo, s_lo, s_hi) * (1.0 / HEAD_DIM)
        outs.append(blk * lax.rsqrt(ms + EPS) * gain)
    return jnp.concatenate(outs, axis=1)


def _in_proj_kernel(x_ref, ng_ref, shift_ref, scale_ref, w_ref, qg_ref, kg_ref, o_ref):
    x = x_ref[0]
    ms = jnp.mean(x * x, axis=-1, keepdims=True)
    h = x * lax.rsqrt(ms + EPS) * ng_ref[...]
    h = (h * (1.0 + scale_ref[0]) + shift_ref[0]).astype(jnp.bfloat16)
    inv = LOG2E / math.sqrt(HEAD_DIM)
    for ci in range(IN_COLS // PROJ_CHUNK):
        c0 = ci * PROJ_CHUNK
        y = jnp.dot(h, w_ref[:, c0:c0 + PROJ_CHUNK], preferred_element_type=jnp.float32)
        cb = c0 // LANES
        if cb == CB_SB_Q:
            y = y * inv
        elif cb == CB_DF_Q:
            y = _group_rmsnorm(y, qg_ref[...]) * inv
        elif cb == CB_DF_K:
            y = _group_rmsnorm(y, kg_ref[...])
        o_ref[0, :, c0:c0 + PROJ_CHUNK] = y.astype(o_ref.dtype)


def _in_proj(x, norm_g, shift, scale, w_in_bf16, q_norm_g, k_norm_g):
    bsz, s, d = x.shape
    tm = ROWS_PROJ
    qg = jnp.tile(q_norm_g.reshape(1, HEAD_DIM), (1, 2))
    kg = jnp.tile(k_norm_g.reshape(1, HEAD_DIM), (1, 2))
    return pl.pallas_call(
        _in_proj_kernel,
        out_shape=jax.ShapeDtypeStruct((bsz, s, IN_COLS), jnp.bfloat16),
        grid=(bsz, s // tm),
        in_specs=[pl.BlockSpec((1, tm, d), lambda b, i: (b, i, 0)),
                  pl.BlockSpec((1, d), lambda b, i: (0, 0)),
                  pl.BlockSpec((1, 1, d), lambda b, i: (b, 0, 0)),
                  pl.BlockSpec((1, 1, d), lambda b, i: (b, 0, 0)),
                  pl.BlockSpec((d, IN_COLS), lambda b, i: (0, 0)),
                  pl.BlockSpec((1, LANES), lambda b, i: (0, 0)),
                  pl.BlockSpec((1, LANES), lambda b, i: (0, 0))],
        out_specs=pl.BlockSpec((1, tm, IN_COLS), lambda b, i: (b, i, 0)),
        compiler_params=pltpu.CompilerParams(vmem_limit_bytes=VMEM_LIMIT),
        name="in_proj",
    )(x, norm_g.reshape(1, d), shift.reshape(bsz, 1, d), scale.reshape(bsz, 1, d),
      w_in_bf16, qg, kg)


def _sb_tile(qm, kblk, vblk, cum, carry, mask):
    z = lax.dot_general(qm, kblk, _NT, preferred_element_type=jnp.float32)
    nz = -z
    l1p = jnp.log(1.0 + jnp.exp2(jnp.minimum(z, nz))) * LOG2E
    lg = jnp.minimum(nz, 0.0) - l1p
    if mask is not None:
        lg = jnp.where(mask, lg, 0.0)
    rem = jnp.dot(lg.astype(jnp.bfloat16), cum, preferred_element_type=jnp.float32)
    a = jnp.exp2(lg + z + rem + carry)
    if mask is not None:
        a = jnp.where(mask, a, 0.0)
    o = jnp.dot(a.astype(jnp.bfloat16), vblk, preferred_element_type=jnp.float32)
    return o, carry + jnp.sum(lg, axis=-1, keepdims=True)


def _attn_kernel(sq_ref, sk_ref, sv_ref, sg_ref, dq_ref, dk_ref, dv_ref, dg_ref, slope_ref,
                 lq1_ref, lk1_ref, lq2_ref, lk2_ref, subg_ref, qg_ref, kg_ref,
                 sbo_ref, dfo_ref, sb_acc, m_s, l_s, df_acc, *, lam_init):
    step = pl.program_id(2)
    lane = lax.broadcasted_iota(jnp.int32, (1, LANES), 1)
    first = lane < HEAD_DIM

    def stacked(q):
        zero = jnp.zeros_like(q)
        return jnp.concatenate([jnp.where(first, q, zero), jnp.where(first, zero, q)], axis=0)

    row = lax.broadcasted_iota(jnp.int32, (TK, TK), 0)
    col = lax.broadcasted_iota(jnp.int32, (TK, TK), 1)
    cum = (row > col).astype(jnp.bfloat16)
    strict = col < row
    sb_diag_mask = jnp.concatenate([strict, strict], axis=0)
    sb_qs = [stacked(sq_ref[0, u * TQ:(u + 1) * TQ, :]) for u in range(SB_QT)]
    acc_of = [sb_acc.at[u * 2 * TQ:(u + 1) * 2 * TQ, :] for u in range(SB_QT)]

    def sb_tiles(qm, kbs, carry, masks):
        total = None
        for kb, mask in zip(kbs, masks):
            start = pl.multiple_of(kb * TK, TK)
            o, carry = _sb_tile(qm, sk_ref[0, pl.ds(start, TK), :], sv_ref[0, pl.ds(start, TK), :],
                                cum, carry, mask)
            total = o if total is None else total + o
        return total, carry

    zero_carry = jnp.zeros((2 * TQ, 1), jnp.float32)

    def sb_first_block(tile0_has_left):
        carries = []
        for u in range(SB_QT):
            qt = step * SB_QT + u
            if u == 0 and not tile0_has_left:
                o, carry = sb_tiles(sb_qs[u], [qt], zero_carry, [sb_diag_mask])
            else:
                o, carry = sb_tiles(sb_qs[u], [qt, qt - 1], zero_carry, [sb_diag_mask, None])
            acc_of[u][...] = o
            carries.append(carry)
        return tuple(carries)

    def sb_alive(carry):
        return jnp.max(carry) > SB_DEAD_LOG2

    def sb_sweep(u, n_steps, kbs_of_step, carry):
        def cond(state):
            i, _, live = state
            return jnp.logical_and(i < n_steps, live)

        def body(state):
            i, carry, _ = state
            kbs = kbs_of_step(i)
            o, carry = sb_tiles(sb_qs[u], kbs, carry, [None] * len(kbs))
            acc_of[u][...] += o
            return i + 1, carry, sb_alive(carry)

        return lax.while_loop(cond, body, (jnp.int32(0), carry, sb_alive(carry)))[1]

    qi = step
    df_qs = stacked(dq_ref[0])
    slope = slope_ref[0] * LOG2E
    drow = lax.broadcasted_iota(jnp.int32, (DF_TQ, DF_TK), 0)
    dcol = lax.broadcasted_iota(jnp.int32, (DF_TQ, DF_TK), 1)
    causal = dcol <= drow
    df_diag_mask = jnp.concatenate([causal, causal], axis=0)

    qk_bound = (jnp.max(jnp.abs(qg_ref[...])) * jnp.max(jnp.abs(kg_ref[...]))
                * (math.sqrt(HEAD_DIM) * LOG2E * DF_BOUND_MARGIN))
    slope_s = jnp.max(slope)

    def load(first_kb, width):
        start = pl.multiple_of(first_kb * DF_TK, DF_TK)
        rel = lax.broadcasted_iota(jnp.int32, (1, width), 1).astype(jnp.float32)
        off = ((first_kb - qi) * DF_TK).astype(jnp.float32)
        bias = jnp.concatenate([slope] * (width // LANES), axis=1) * (rel + off)
        return dk_ref[0, pl.ds(start, width), :], dv_ref[0, pl.ds(start, width), :], bias

    def last_rel(first_kb, width):
        return ((first_kb - qi) * DF_TK + (width - 1)).astype(jnp.float32)

    def df_sweep(n_steps, first_kb_of_step, width, tile, needed):
        def cond(state):
            i, live = state
            return jnp.logical_and(i < n_steps, live)

        def body(state):
            i, _ = state
            tile(first_kb_of_step(i), width, None)
            return i + 1, needed(first_kb_of_step(i + 1), width)

        lax.while_loop(cond, body, (jnp.int32(0), needed(first_kb_of_step(jnp.int32(0)), width)))

    def run(tile, needed):
        def first_block(tile0_has_left):
            tile(qi, DF_TK, df_diag_mask)
            return sb_first_block(tile0_has_left)

        carries = lax.cond(step > 0, lambda: first_block(True), lambda: first_block(False))
        rem = lax.rem(qi, DF_GROUP)
        df_sweep(rem, lambda i: qi - 1 - i, DF_TK, tile, needed)
        df_sweep(qi // DF_GROUP, lambda i: qi - rem - (i + 1) * DF_GROUP, DF_GROUP * DF_TK,
                 tile, needed)
        return carries

    def fixed_shift():
        rowpos = lax.broadcasted_iota(jnp.int32, (DF_TQ, LANES), 0).astype(jnp.float32)
        shift = qk_bound + slope * jnp.concatenate([rowpos, rowpos], axis=0)
        l_s[...] = jnp.zeros((2 * DF_TQ, LANES), jnp.float32)
        df_acc[...] = jnp.zeros((2 * DF_TQ, LANES), jnp.float32)

        def tile(first_kb, width, mask):
            reps = width // LANES
            kblk, vblk, bias = load(first_kb, width)
            s = (lax.dot_general(df_qs, kblk, _NT, preferred_element_type=jnp.float32)
                 + bias - jnp.concatenate([shift] * reps, axis=1))
            if mask is not None:
                s = jnp.where(mask, s, NEG_BIG)
            p = jnp.exp2(s)
            part = p[:, :LANES]
            for r in range(1, reps):
                part = part + p[:, r * LANES:(r + 1) * LANES]
            l_s[...] += part
            df_acc[...] += jnp.dot(p.astype(jnp.bfloat16), vblk, preferred_element_type=jnp.float32)

        def needed(first_kb, width):
            return slope_s * last_rel(first_kb, width) > DF_DEAD_LOG2

        carries = run(tile, needed)
        l = jnp.sum(l_s[...], axis=-1, keepdims=True)
        return (l[:DF_TQ], l[DF_TQ:]) + carries

    def running_max():
        m_s[...] = jnp.full((2 * DF_TQ, LANES), NEG_BIG, jnp.float32)
        l_s[...] = jnp.zeros((2 * DF_TQ, LANES), jnp.float32)
        df_acc[...] = jnp.zeros((2 * DF_TQ, LANES), jnp.float32)

        def tile(first_kb, width, mask):
            reps = width // LANES
            kblk, vblk, bias = load(first_kb, width)
            s = lax.dot_general(df_qs, kblk, _NT, preferred_element_type=jnp.float32) + bias
            if mask is not None:
                s = jnp.where(mask, s, NEG_BIG)
            m_old = m_s[...]
            m_new = jnp.maximum(m_old, jnp.max(s, axis=-1, keepdims=True))
            p = jnp.exp2(s - jnp.concatenate([m_new] * reps, axis=1))
            alpha = jnp.exp2(m_old - m_new)
            l_s[...] = alpha * l_s[...] + jnp.sum(p, axis=-1, keepdims=True)
            df_acc[...] = alpha * df_acc[...] + jnp.dot(p.astype(jnp.bfloat16), vblk,
                                                        preferred_element_type=jnp.float32)
            m_s[...] = m_new

        def needed(first_kb, width):
            return (qk_bound + slope_s * last_rel(first_kb, width) - jnp.min(m_s[...])
                    > DF_DEAD_LOG2)

        carries = run(tile, needed)
        return (l_s[:DF_TQ, :1], l_s[DF_TQ:, :1]) + carries

    res = lax.cond(qk_bound < DF_FIXED_SHIFT_MAX, fixed_shift, running_max)
    l1, l2, carries = res[0], res[1], res[2:]

    any_alive = sb_alive(carries[0])
    for carry in carries[1:]:
        any_alive = jnp.logical_or(any_alive, sb_alive(carry))

    @pl.when(any_alive)
    def _():
        for u in range(SB_QT):
            left = jnp.maximum(step * SB_QT + u - 1, 0)
            rem = lax.rem(left, SB_GROUP)
            carry = sb_sweep(u, rem, lambda i, left=left: [left - 1 - i], carries[u])
            sb_sweep(u, left // SB_GROUP,
                     lambda i, left=left, rem=rem: [left - rem - i * SB_GROUP - 1 - t
                                                    for t in range(SB_GROUP)], carry)

    for u in range(SB_QT):
        out = jnp.where(first, acc_of[u][:TQ, :], acc_of[u][TQ:, :])
        g = sg_ref[0, u * TQ:(u + 1) * TQ, :].astype(jnp.float32)
        sbo_ref[0, u * TQ:(u + 1) * TQ, :] = (out * _silu(g)).astype(sbo_ref.dtype)

    lam = (jnp.exp(jnp.sum(lq1_ref[...] * lk1_ref[...], keepdims=True))
           - jnp.exp(jnp.sum(lq2_ref[...] * lk2_ref[...], keepdims=True)) + lam_init)
    o = df_acc[:DF_TQ, :] / l1 - lam * (df_acc[DF_TQ:, :] / l2)
    ms = jnp.mean(o * o, axis=-1, keepdims=True)
    o = o * lax.rsqrt(ms + EPS) * subg_ref[...] * (1.0 - lam_init)
    g = dg_ref[0].astype(jnp.float32)
    dfo_ref[0] = (o * _silu(g)).astype(dfo_ref.dtype)


def _attention(proj, slopes, lq1, lk1, lq2, lk2, subln_g, q_norm_g, k_norm_g, lam_init):
    bsz, s, _ = proj.shape
    rows = DF_TQ

    def tile_spec(cb):
        return pl.BlockSpec((1, rows, LANES), lambda b, h, i: (b, i, cb + h))

    def full_spec(cb):
        return pl.BlockSpec((1, s, LANES), lambda b, h, i: (b, 0, cb + h))

    vec = pl.BlockSpec((1, HEAD_DIM), lambda b, h, i: (0, 0))
    stat = pltpu.VMEM((2 * DF_TQ, LANES), jnp.float32)
    out_spec = pl.BlockSpec((1, rows, LANES), lambda b, h, i: (b, i, h))
    return pl.pallas_call(
        functools.partial(_attn_kernel, lam_init=lam_init),
        out_shape=(jax.ShapeDtypeStruct((bsz, s, SB_WIDTH), jnp.bfloat16),
                   jax.ShapeDtypeStruct((bsz, s, DF_WIDTH), jnp.bfloat16)),
        grid=(bsz, DF_HEADS, s // rows),
        in_specs=[tile_spec(CB_SB_Q), full_spec(CB_SB_K), full_spec(CB_SB_V), tile_spec(CB_SB_G),
                  tile_spec(CB_DF_Q), full_spec(CB_DF_K), full_spec(CB_DF_V), tile_spec(CB_DF_G),
                  pl.BlockSpec((1, 1, LANES), lambda b, h, i: (h, 0, 0)),
                  vec, vec, vec, vec,
                  pl.BlockSpec((1, LANES), lambda b, h, i: (0, 0)),
                  vec, vec],
        out_specs=(out_spec, out_spec),
        scratch_shapes=[pltpu.VMEM((SB_QT * 2 * TQ, LANES), jnp.float32), stat, stat, stat],
        compiler_params=pltpu.CompilerParams(vmem_limit_bytes=VMEM_LIMIT),
        name="attention",
    )(proj, proj, proj, proj, proj, proj, proj, proj, slopes, lq1, lk1, lq2, lk2, subln_g,
      q_norm_g, k_norm_g)


def _out_proj_kernel(x_ref, sb_ref, df_ref, w_ref, gate_ref, o_ref):
    y = (jnp.dot(sb_ref[0], w_ref[:SB_WIDTH, :], preferred_element_type=jnp.float32)
         + jnp.dot(df_ref[0], w_ref[SB_WIDTH:, :], preferred_element_type=jnp.float32))
    o_ref[0] = x_ref[0] + gate_ref[0] * y


def _out_proj(x, sb_o, df_o, w_out_bf16, gate):
    bsz, s, d = x.shape
    tm = ROWS_OUT
    return pl.pallas_call(
        _out_proj_kernel,
        out_shape=jax.ShapeDtypeStruct((bsz, s, d), jnp.float32),
        grid=(bsz, s // tm),
        in_specs=[pl.BlockSpec((1, tm, d), lambda b, i: (b, i, 0)),
                  pl.BlockSpec((1, tm, SB_WIDTH), lambda b, i: (b, i, 0)),
                  pl.BlockSpec((1, tm, DF_WIDTH), lambda b, i: (b, i, 0)),
                  pl.BlockSpec((SB_WIDTH + DF_WIDTH, d), lambda b, i: (0, 0)),
                  pl.BlockSpec((1, 1, d), lambda b, i: (b, 0, 0))],
        out_specs=pl.BlockSpec((1, tm, d), lambda b, i: (b, i, 0)),
        compiler_params=pltpu.CompilerParams(vmem_limit_bytes=VMEM_LIMIT),
        name="out_proj",
    )(x, sb_o, df_o, w_out_bf16, gate.reshape(bsz, 1, d))


def _layer(x, c, layer_idx, norm_g, w_ada, b_ada, w_in, q_norm_g, k_norm_g,
           lambda_q1, lambda_k1, lambda_q2, lambda_k2, subln_g, w_out):
    d = x.shape[-1]
    lam_init = 0.8 - 0.6 * math.exp(-0.3 * layer_idx)
    mod = _adaln_mod(c, w_ada, b_ada)
    shift, scale, gate = mod[:, :d], mod[:, d:2 * d], mod[:, 2 * d:]
    proj = _in_proj(x, norm_g, shift, scale, w_in.astype(jnp.bfloat16), q_norm_g, k_norm_g)
    slopes = jnp.asarray([2.0 ** (-8.0 * (h + 1) / DF_HEADS) for h in range(DF_HEADS)], jnp.float32)
    slopes = jnp.broadcast_to(slopes[:, None, None], (DF_HEADS, 1, LANES))
    row = lambda v: v.reshape(1, -1)
    sb_o, df_o = _attention(proj, slopes, row(lambda_q1), row(lambda_k1), row(lambda_q2),
                            row(lambda_k2), row(subln_g), row(q_norm_g), row(k_norm_g), lam_init)
    return _out_proj(x, sb_o, df_o, w_out.astype(jnp.bfloat16), gate)


@jax.jit
def kernel(x, c, norm_g, w_ada, b_ada, w_in, q_norm_g, k_norm_g, lambda_q1, lambda_k1,
           lambda_q2, lambda_k2, subln_g, w_out):
    for l in range(norm_g.shape[0]):
        x = _layer(x, c, l, norm_g[l], w_ada[l], b_ada[l], w_in[l], q_norm_g[l], k_norm_g[l],
                   lambda_q1[l], lambda_k1[l], lambda_q2[l], lambda_k2[l], subln_g[l], w_out[l])
    return x
```

```python
import functools
import math

import jax
import jax.numpy as jnp
from jax import lax
from jax.experimental import pallas as pl
from jax.experimental.pallas import tpu as pltpu

D_MODEL = 1024
SB_HEADS = 8
DF_HEADS = 4
HEAD_DIM = 64
LANES = 128
SB_WIDTH = SB_HEADS * HEAD_DIM
DF_WIDTH = DF_HEADS * 2 * HEAD_DIM
IN_COLS = 4 * SB_WIDTH + 4 * DF_WIDTH
EPS = 1e-6
NEG_BIG = -1e30

CB_SB_Q, CB_SB_K, CB_SB_V, CB_SB_G = 0, 4, 8, 12
CB_DF_Q, CB_DF_K, CB_DF_V, CB_DF_G = 16, 20, 24, 28

ROWS_PROJ = 512
ROWS_OUT = 1024
TQ = 256
TK = 256
DF_TQ = 512
DF_TK = 512
SB_QT = 8
SB_PAIR = 2
SB_GROUP = 4
DF_GROUP = 2
LOG2E = math.log2(math.e)
SB_DEAD_LOG2 = -150.0
DF_DEAD_LOG2 = -150.0
DF_FIXED_SHIFT_MAX = 48.0
DF_BOUND_MARGIN = 1.02
PROJ_CHUNK = 512
VMEM_LIMIT = 48 * 1024 * 1024

_NT = (((1,), (1,)), ((), ()))


def _silu(g):
    return g / (1.0 + jnp.exp(-g))


def _adaln_kernel(ct_ref, w_ref, b_ref, o_ref):
    w = w_ref[...]
    rows = []
    for b in range(o_ref.shape[0]):
        col = ct_ref[:, b:b + 1]
        rows.append(jnp.sum(col * w, axis=0, keepdims=True))
    o_ref[...] = jnp.concatenate(rows, axis=0) + b_ref[...]


def _adaln_mod(c, w_ada, b_ada):
    bsz, d = c.shape
    n = w_ada.shape[1]
    tn = 512
    return pl.pallas_call(
        _adaln_kernel,
        out_shape=jax.ShapeDtypeStruct((bsz, n), jnp.float32),
        grid=(n // tn,),
        in_specs=[pl.BlockSpec((d, bsz), lambda j: (0, 0)),
                  pl.BlockSpec((d, tn), lambda j: (0, j)),
                  pl.BlockSpec((1, tn), lambda j: (0, j))],
        out_specs=pl.BlockSpec((bsz, tn), lambda j: (0, j)),
        name="adaln_mod",
    )(c.T, w_ada, b_ada.reshape(1, n))


def _group_rmsnorm(y, gain):
    outs = []
    lane = lax.broadcasted_iota(jnp.int32, (1, LANES), 1)
    lo = lane < HEAD_DIM
    for j in range(y.shape[1] // LANES):
        blk = y[:, j * LANES:(j + 1) * LANES]
        sq = blk * blk
        s_lo = jnp.sum(jnp.where(lo, sq, 0.0), axis=-1, keepdims=True)
        s_hi = jnp.sum(jnp.where(lo, 0.0, sq), axis=-1, keepdims=True)
        ms = jnp.where(lo, s_lo, s_hi) * (1.0 / HEAD_DIM)
        outs.append(blk * lax.rsqrt(ms + EPS) * gain)
    return jnp.concatenate(outs, axis=1)


def _in_proj_kernel(x_ref, ng_ref, shift_ref, scale_ref, w_ref, qg_ref, kg_ref, o_ref):
    x = x_ref[0]
    ms = jnp.mean(x * x, axis=-1, keepdims=True)
    h = x * lax.rsqrt(ms + EPS) * ng_ref[...]
    h = (h * (1.0 + scale_ref[0]) + shift_ref[0]).astype(jnp.bfloat16)
    inv = LOG2E / math.sqrt(HEAD_DIM)
    for ci in range(IN_COLS // PROJ_CHUNK):
        c0 = ci * PROJ_CHUNK
        y = jnp.dot(h, w_ref[:, c0:c0 + PROJ_CHUNK], preferred_element_type=jnp.float32)
        cb = c0 // LANES
        if cb == CB_SB_Q:
            y = y * inv
        elif cb == CB_DF_Q:
            y = _group_rmsnorm(y, qg_ref[...]) * inv
        elif cb == CB_DF_K:
            y = _group_rmsnorm(y, kg_ref[...])
        o_ref[0, :, c0:c0 + PROJ_CHUNK] = y.astype(o_ref.dtype)


def _in_proj(x, norm_g, shift, scale, w_in_bf16, q_norm_g, k_norm_g):
    bsz, s, d = x.shape
    tm = ROWS_PROJ
    qg = jnp.tile(q_norm_g.reshape(1, HEAD_DIM), (1, 2))
    kg = jnp.tile(k_norm_g.reshape(1, HEAD_DIM), (1, 2))
    return pl.pallas_call(
        _in_proj_kernel,
        out_shape=jax.ShapeDtypeStruct((bsz, s, IN_COLS), jnp.bfloat16),
        grid=(bsz, s // tm),
        in_specs=[pl.BlockSpec((1, tm, d), lambda b, i: (b, i, 0)),
                  pl.BlockSpec((1, d), lambda b, i: (0, 0)),
                  pl.BlockSpec((1, 1, d), lambda b, i: (b, 0, 0)),
                  pl.BlockSpec((1, 1, d), lambda b, i: (b, 0, 0)),
                  pl.BlockSpec((d, IN_COLS), lambda b, i: (0, 0)),
                  pl.BlockSpec((1, LANES), lambda b, i: (0, 0)),
                  pl.BlockSpec((1, LANES), lambda b, i: (0, 0))],
        out_specs=pl.BlockSpec((1, tm, IN_COLS), lambda b, i: (b, i, 0)),
        compiler_params=pltpu.CompilerParams(vmem_limit_bytes=VMEM_LIMIT),
        name="in_proj",
    )(x, norm_g.reshape(1, d), shift.reshape(bsz, 1, d), scale.reshape(bsz, 1, d),
      w_in_bf16, qg, kg)


def _sb_tile(qm, kblk, vblk, cum, carry, mask):
    z = lax.dot_general(qm, kblk, _NT, preferred_element_type=jnp.float32)
    nz = -z
    l1p = jnp.log(1.0 + jnp.exp2(jnp.minimum(z, nz))) * LOG2E
    lg = jnp.minimum(nz, 0.0) - l1p
    if mask is not None:
        lg = jnp.where(mask, lg, 0.0)
    rem = jnp.dot(lg.astype(jnp.bfloat16), cum, preferred_element_type=jnp.float32)
    a = jnp.exp2(lg + z + rem + carry)
    if mask is not None:
        a = jnp.where(mask, a, 0.0)
    o = jnp.dot(a.astype(jnp.bfloat16), vblk, preferred_element_type=jnp.float32)
    return o, carry + jnp.sum(lg, axis=-1, keepdims=True)


def _sb_kernel(q_ref, k_ref, v_ref, g_ref, o_ref, acc, carry_s):
    step = pl.program_id(2)
    lane = lax.broadcasted_iota(jnp.int32, (1, LANES), 1)
    first = lane < HEAD_DIM
    row = lax.broadcasted_iota(jnp.int32, (TK, TK), 0)
    col = lax.broadcasted_iota(jnp.int32, (TK, TK), 1)
    cum = (row > col).astype(jnp.bfloat16)
    strict = col < row
    diag_mask = jnp.concatenate([strict, strict], axis=0)

    def q_rows(u):
        return pl.ds(pl.multiple_of(u * TQ, TQ), TQ)

    def acc_rows(u):
        return pl.ds(pl.multiple_of(u * (2 * TQ), 2 * TQ), 2 * TQ)

    def stacked(u):
        q = q_ref[0, q_rows(u), :]
        zero = jnp.zeros_like(q)
        return jnp.concatenate([jnp.where(first, q, zero), jnp.where(first, zero, q)], axis=0)

    def tiles(qm, kbs, carry, masks):
        total = None
        for kb, mask in zip(kbs, masks):
            start = pl.multiple_of(kb * TK, TK)
            o, carry = _sb_tile(qm, k_ref[0, pl.ds(start, TK), :], v_ref[0, pl.ds(start, TK), :],
                                cum, carry, mask)
            total = o if total is None else total + o
        return total, carry

    zero_carry = jnp.zeros((2 * TQ, 1), jnp.float32)

    def first_block(g, state):
        u0 = g * SB_PAIR
        qt0 = step * SB_QT + u0

        def block(tile0_has_left):
            for w in range(SB_PAIR):
                qm = stacked(u0 + w)
                if w == 0 and not tile0_has_left:
                    o, carry = tiles(qm, [qt0], zero_carry, [diag_mask])
                else:
                    o, carry = tiles(qm, [qt0 + w, qt0 + w - 1], zero_carry, [diag_mask, None])
                acc[acc_rows(u0 + w), :] = o
                carry_s[acc_rows(u0 + w), :] = carry

        @pl.when(qt0 > 0)
        def _():
            block(True)

        @pl.when(qt0 == 0)
        def _():
            block(False)

        return state

    lax.fori_loop(0, SB_QT // SB_PAIR, first_block, 0)

    def alive(carry):
        return jnp.max(carry) > SB_DEAD_LOG2

    def sweep(u, qm, n_steps, kbs_of_step, carry):
        def cond(state):
            i, _, live = state
            return jnp.logical_and(i < n_steps, live)

        def body(state):
            i, carry, _ = state
            kbs = kbs_of_step(i)
            o, carry = tiles(qm, kbs, carry, [None] * len(kbs))
            acc[acc_rows(u), :] += o
            return i + 1, carry, alive(carry)

        return lax.while_loop(cond, body, (jnp.int32(0), carry, alive(carry)))[1]

    @pl.when(alive(carry_s[...]))
    def _():
        def sweeps_of(u, state):
            qm = stacked(u)
            left = jnp.maximum(step * SB_QT + u - 1, 0)
            rem = lax.rem(left, SB_GROUP)
            carry = sweep(u, qm, rem, lambda i: [left - 1 - i], carry_s[acc_rows(u), :])
            sweep(u, qm, left // SB_GROUP,
                  lambda i: [left - rem - i * SB_GROUP - 1 - t for t in range(SB_GROUP)], carry)
            return state

        lax.fori_loop(0, SB_QT, sweeps_of, 0)

    def finish(u, state):
        a = acc[acc_rows(u), :]
        out = jnp.where(first, a[:TQ, :], a[TQ:, :])
        g = g_ref[0, q_rows(u), :].astype(jnp.float32)
        o_ref[0, q_rows(u), :] = (out * _silu(g)).astype(o_ref.dtype)
        return state

    lax.fori_loop(0, SB_QT, finish, 0)


def _sb_attention(proj):
    bsz, s, _ = proj.shape
    rows = SB_QT * TQ
    return pl.pallas_call(
        _sb_kernel,
        out_shape=jax.ShapeDtypeStruct((bsz, s, SB_WIDTH), jnp.bfloat16),
        grid=(bsz, SB_WIDTH // LANES, s // rows),
        in_specs=[pl.BlockSpec((1, rows, LANES), lambda b, h, i: (b, i, CB_SB_Q + h)),
                  pl.BlockSpec((1, s, LANES), lambda b, h, i: (b, 0, CB_SB_K + h)),
                  pl.BlockSpec((1, s, LANES), lambda b, h, i: (b, 0, CB_SB_V + h)),
                  pl.BlockSpec((1, rows, LANES), lambda b, h, i: (b, i, CB_SB_G + h))],
        out_specs=pl.BlockSpec((1, rows, LANES), lambda b, h, i: (b, i, h)),
        scratch_shapes=[pltpu.VMEM((SB_QT * 2 * TQ, LANES), jnp.float32),
                        pltpu.VMEM((SB_QT * 2 * TQ, 1), jnp.float32)],
        compiler_params=pltpu.CompilerParams(vmem_limit_bytes=VMEM_LIMIT),
        name="sb_attn",
    )(proj, proj, proj, proj)


def _df_kernel(q_ref, k_ref, v_ref, g_ref, slope_ref, lq1_ref, lk1_ref, lq2_ref, lk2_ref, sg_ref,
               qg_ref, kg_ref, o_ref, m_s, l_s, acc, *, lam_init):
    qi = pl.program_id(2)
    q = q_ref[0]
    lane = lax.broadcasted_iota(jnp.int32, (1, LANES), 1)
    first = lane < HEAD_DIM
    zero = jnp.zeros_like(q)
    qs = jnp.concatenate([jnp.where(first, q, zero), jnp.where(first, zero, q)], axis=0)
    slope = slope_ref[0] * LOG2E
    row = lax.broadcasted_iota(jnp.int32, (DF_TQ, DF_TK), 0)
    col = lax.broadcasted_iota(jnp.int32, (DF_TQ, DF_TK), 1)
    causal = col <= row
    diag_mask = jnp.concatenate([causal, causal], axis=0)

    qk_bound = (jnp.max(jnp.abs(qg_ref[...])) * jnp.max(jnp.abs(kg_ref[...]))
                * (math.sqrt(HEAD_DIM) * LOG2E * DF_BOUND_MARGIN))
    slope_s = jnp.max(slope)

    def load(first_kb, width):
        start = pl.multiple_of(first_kb * DF_TK, DF_TK)
        rel = lax.broadcasted_iota(jnp.int32, (1, width), 1).astype(jnp.float32)
        off = ((first_kb - qi) * DF_TK).astype(jnp.float32)
        bias = jnp.concatenate([slope] * (width // LANES), axis=1) * (rel + off)
        return k_ref[0, pl.ds(start, width), :], v_ref[0, pl.ds(start, width), :], bias

    def last_rel(first_kb, width):
        return ((first_kb - qi) * DF_TK + (width - 1)).astype(jnp.float32)

    def sweep(n_steps, first_kb_of_step, width, tile, needed):
        def cond(state):
            i, live = state
            return jnp.logical_and(i < n_steps, live)

        def body(state):
            i, _ = state
            tile(first_kb_of_step(i), width, None)
            return i + 1, needed(first_kb_of_step(i + 1), width)

        lax.while_loop(cond, body, (jnp.int32(0), needed(first_kb_of_step(jnp.int32(0)), width)))

    def sweep_all(tile, needed):
        tile(qi, DF_TK, diag_mask)
        rem = lax.rem(qi, DF_GROUP)
        sweep(rem, lambda i: qi - 1 - i, DF_TK, tile, needed)
        sweep(qi // DF_GROUP, lambda i: qi - rem - (i + 1) * DF_GROUP, DF_GROUP * DF_TK, tile, needed)

    def fixed_shift():
        rowpos = lax.broadcasted_iota(jnp.int32, (DF_TQ, LANES), 0).astype(jnp.float32)
        shift = qk_bound + slope * jnp.concatenate([rowpos, rowpos], axis=0)
        l_s[...] = jnp.zeros((2 * DF_TQ, LANES), jnp.float32)
        acc[...] = jnp.zeros((2 * DF_TQ, LANES), jnp.float32)

        def tile(first_kb, width, mask):
            reps = width // LANES
            kblk, vblk, bias = load(first_kb, width)
            s = (lax.dot_general(qs, kblk, _NT, preferred_element_type=jnp.float32)
                 + bias - jnp.concatenate([shift] * reps, axis=1))
            if mask is not None:
                s = jnp.where(mask, s, NEG_BIG)
            p = jnp.exp2(s)
            part = p[:, :LANES]
            for r in range(1, reps):
                part = part + p[:, r * LANES:(r + 1) * LANES]
            l_s[...] += part
            acc[...] += jnp.dot(p.astype(jnp.bfloat16), vblk, preferred_element_type=jnp.float32)

        def needed(first_kb, width):
            return slope_s * last_rel(first_kb, width) > DF_DEAD_LOG2

        sweep_all(tile, needed)
        l = jnp.sum(l_s[...], axis=-1, keepdims=True)
        return l[:DF_TQ], l[DF_TQ:]

    def running_max():
        m_s[...] = jnp.full((2 * DF_TQ, LANES), NEG_BIG, jnp.float32)
        l_s[...] = jnp.zeros((2 * DF_TQ, LANES), jnp.float32)
        acc[...] = jnp.zeros((2 * DF_TQ, LANES), jnp.float32)

        def tile(first_kb, width, mask):
            reps = width // LANES
            kblk, vblk, bias = load(first_kb, width)
            s = lax.dot_general(qs, kblk, _NT, preferred_element_type=jnp.float32) + bias
            if mask is not None:
                s = jnp.where(mask, s, NEG_BIG)
            m_old = m_s[...]
            m_new = jnp.maximum(m_old, jnp.max(s, axis=-1, keepdims=True))
            p = jnp.exp2(s - jnp.concatenate([m_new] * reps, axis=1))
            alpha = jnp.exp2(m_old - m_new)
            l_s[...] = alpha * l_s[...] + jnp.sum(p, axis=-1, keepdims=True)
            acc[...] = alpha * acc[...] + jnp.dot(p.astype(jnp.bfloat16), vblk,
                                                  preferred_element_type=jnp.float32)
            m_s[...] = m_new

        def needed(first_kb, width):
            return (qk_bound + slope_s * last_rel(first_kb, width) - jnp.min(m_s[...])
                    > DF_DEAD_LOG2)

        sweep_all(tile, needed)
        return l_s[:DF_TQ, :1], l_s[DF_TQ:, :1]

    l1, l2 = lax.cond(qk_bound < DF_FIXED_SHIFT_MAX, fixed_shift, running_max)

    lam = (jnp.exp(jnp.sum(lq1_ref[...] * lk1_ref[...], keepdims=True))
           - jnp.exp(jnp.sum(lq2_ref[...] * lk2_ref[...], keepdims=True)) + lam_init)
    o = acc[:DF_TQ, :] / l1 - lam * (acc[DF_TQ:, :] / l2)
    ms = jnp.mean(o * o, axis=-1, keepdims=True)
    o = o * lax.rsqrt(ms + EPS) * sg_ref[...] * (1.0 - lam_init)
    g = g_ref[0].astype(jnp.float32)
    o_ref[0] = (o * _silu(g)).astype(o_ref.dtype)


def _df_attention(proj, slopes, lq1, lk1, lq2, lk2, subln_g, q_norm_g, k_norm_g, lam_init):
    bsz, s, _ = proj.shape
    nq = s // DF_TQ
    vec = pl.BlockSpec((1, HEAD_DIM), lambda b, h, i: (0, 0))
    stat = pltpu.VMEM((2 * DF_TQ, LANES), jnp.float32)
    return pl.pallas_call(
        functools.partial(_df_kernel, lam_init=lam_init),
        out_shape=jax.ShapeDtypeStruct((bsz, s, DF_WIDTH), jnp.bfloat16),
        grid=(bsz, DF_HEADS, nq),
        in_specs=[pl.BlockSpec((1, DF_TQ, LANES), lambda b, h, i: (b, i, CB_DF_Q + h)),
                  pl.BlockSpec((1, s, LANES), lambda b, h, i: (b, 0, CB_DF_K + h)),
                  pl.BlockSpec((1, s, LANES), lambda b, h, i: (b, 0, CB_DF_V + h)),
                  pl.BlockSpec((1, DF_TQ, LANES), lambda b, h, i: (b, i, CB_DF_G + h)),
                  pl.BlockSpec((1, 1, LANES), lambda b, h, i: (h, 0, 0)),
                  vec, vec, vec, vec,
                  pl.BlockSpec((1, LANES), lambda b, h, i: (0, 0)),
                  vec, vec],
        out_specs=pl.BlockSpec((1, DF_TQ, LANES), lambda b, h, i: (b, i, h)),
        scratch_shapes=[stat, stat, stat],
        compiler_params=pltpu.CompilerParams(vmem_limit_bytes=VMEM_LIMIT),
        name="df_attn",
    )(proj, proj, proj, proj, slopes, lq1, lk1, lq2, lk2, subln_g, q_norm_g, k_norm_g)


def _out_proj_kernel(x_ref, sb_ref, df_ref, w_ref, gate_ref, o_ref):
    y = (jnp.dot(sb_ref[0], w_ref[:SB_WIDTH, :], preferred_element_type=jnp.float32)
         + jnp.dot(df_ref[0], w_ref[SB_WIDTH:, :], preferred_element_type=jnp.float32))
    o_ref[0] = x_ref[0] + gate_ref[0] * y


def _out_proj(x, sb_o, df_o, w_out_bf16, gate):
    bsz, s, d = x.shape
    tm = ROWS_OUT
    return pl.pallas_call(
        _out_proj_kernel,
        out_shape=jax.ShapeDtypeStruct((bsz, s, d), jnp.float32),
        grid=(bsz, s // tm),
        in_specs=[pl.BlockSpec((1, tm, d), lambda b, i: (b, i, 0)),
                  pl.BlockSpec((1, tm, SB_WIDTH), lambda b, i: (b, i, 0)),
                  pl.BlockSpec((1, tm, DF_WIDTH), lambda b, i: (b, i, 0)),
                  pl.BlockSpec((SB_WIDTH + DF_WIDTH, d), lambda b, i: (0, 0)),
                  pl.BlockSpec((1, 1, d), lambda b, i: (b, 0, 0))],
        out_specs=pl.BlockSpec((1, tm, d), lambda b, i: (b, i, 0)),
        compiler_params=pltpu.CompilerParams(vmem_limit_bytes=VMEM_LIMIT),
        name="out_proj",
    )(x, sb_o, df_o, w_out_bf16, gate.reshape(bsz, 1, d))


def _layer(x, c, layer_idx, norm_g, w_ada, b_ada, w_in, q_norm_g, k_norm_g,
           lambda_q1, lambda_k1, lambda_q2, lambda_k2, subln_g, w_out):
    d = x.shape[-1]
    lam_init = 0.8 - 0.6 * math.exp(-0.3 * layer_idx)
    mod = _adaln_mod(c, w_ada, b_ada)
    shift, scale, gate = mod[:, :d], mod[:, d:2 * d], mod[:, 2 * d:]
    proj = _in_proj(x, norm_g, shift, scale, w_in.astype(jnp.bfloat16), q_norm_g, k_norm_g)
    sb_o = _sb_attention(proj)
    slopes = jnp.asarray([2.0 ** (-8.0 * (h + 1) / DF_HEADS) for h in range(DF_HEADS)], jnp.float32)
    slopes = jnp.broadcast_to(slopes[:, None, None], (DF_HEADS, 1, LANES))
    row = lambda v: v.reshape(1, -1)
    df_o = _df_attention(proj, slopes, row(lambda_q1), row(lambda_k1), row(lambda_q2),
                         row(lambda_k2), row(subln_g), row(q_norm_g), row(k_norm_g), lam_init)
    return _out_proj(x, sb_o, df_o, w_out.astype(jnp.bfloat16), gate)


@jax.jit
def kernel(x, c, norm_g, w_ada, b_ada, w_in, q_norm_g, k_norm_g, lambda_q1, lambda_k1,
           lambda_q2, lambda_k2, subln_g, w_out):
    for l in range(norm_g.shape[0]):
        x = _layer(x, c, l, norm_g[l], w_ada[l], b_ada[l], w_in[l], q_norm_g[l], k_norm_g[l],
                   lambda_q1[l], lambda_k1[l], lambda_q2[l], lambda_k2[l], subln_g[l], w_out[l])
    return x
```

```python
import functools
import math

import jax
import jax.numpy as jnp
from jax import lax
from jax.experimental import pallas as pl
from jax.experimental.pallas import tpu as pltpu

D_MODEL = 1024
SB_HEADS = 8
DF_HEADS = 4
HEAD_DIM = 64
LANES = 128
SB_WIDTH = SB_HEADS * HEAD_DIM
DF_WIDTH = DF_HEADS * 2 * HEAD_DIM
IN_COLS = 4 * SB_WIDTH + 4 * DF_WIDTH
EPS = 1e-6
NEG_BIG = -1e30

CB_SB_Q, CB_SB_K, CB_SB_V, CB_SB_G = 0, 4, 8, 12
CB_DF_Q, CB_DF_K, CB_DF_V, CB_DF_G = 16, 20, 24, 28

ROWS_PROJ = 512
ROWS_OUT = 1024
TQ = 256
TK = 256
DF_TQ = 512
DF_TK = 512
SB_QT = 8
SB_PAIR = 2
SB_GROUP = 4
DF_GROUP = 2
LOG2E = math.log2(math.e)
SB_DEAD_LOG2 = -150.0
DF_DEAD_LOG2 = -150.0
DF_FIXED_SHIFT_MAX = 48.0
DF_BOUND_MARGIN = 1.02
PROJ_CHUNK = 512
VMEM_LIMIT = 48 * 1024 * 1024

_NT = (((1,), (1,)), ((), ()))


def _silu(g):
    return g / (1.0 + jnp.exp(-g))


def _adaln_kernel(ct_ref, w_ref, b_ref, o_ref):
    w = w_ref[...]
    rows = []
    for b in range(o_ref.shape[0]):
        col = ct_ref[:, b:b + 1]
        rows.append(jnp.sum(col * w, axis=0, keepdims=True))
    o_ref[...] = jnp.concatenate(rows, axis=0) + b_ref[...]


def _adaln_mod(c, w_ada, b_ada):
    bsz, d = c.shape
    n = w_ada.shape[1]
    tn = 512
    return pl.pallas_call(
        _adaln_kernel,
        out_shape=jax.ShapeDtypeStruct((bsz, n), jnp.float32),
        grid=(n // tn,),
        in_specs=[pl.BlockSpec((d, bsz), lambda j: (0, 0)),
                  pl.BlockSpec((d, tn), lambda j: (0, j)),
                  pl.BlockSpec((1, tn), lambda j: (0, j))],
        out_specs=pl.BlockSpec((bsz, tn), lambda j: (0, j)),
        name="adaln_mod",
    )(c.T, w_ada, b_ada.reshape(1, n))


def _group_rmsnorm(y, gain):
    outs = []
    lane = lax.broadcasted_iota(jnp.int32, (1, LANES), 1)
    lo = lane < HEAD_DIM
    for j in range(y.shape[1] // LANES):
        blk = y[:, j * LANES:(j + 1) * LANES]
        sq = blk * blk
        s_lo = jnp.sum(jnp.where(lo, sq, 0.0), axis=-1, keepdims=True)
        s_hi = jnp.sum(jnp.where(lo, 0.0, sq), axis=-1, keepdims=True)
        ms = jnp.where(lo, s_lo, s_hi) * (1.0 / HEAD_DIM)
        outs.append(blk * lax.rsqrt(ms + EPS) * gain)
    return jnp.concatenate(outs, axis=1)


def _in_proj_kernel(x_ref, ng_ref, shift_ref, scale_ref, w_ref, qg_ref, kg_ref, o_ref):
    x = x_ref[0]
    ms = jnp.mean(x * x, axis=-1, keepdims=True)
    h = x * lax.rsqrt(ms + EPS) * ng_ref[...]
    h = (h * (1.0 + scale_ref[0]) + shift_ref[0]).astype(jnp.bfloat16)
    inv = LOG2E / math.sqrt(HEAD_DIM)
    for ci in range(IN_COLS // PROJ_CHUNK):
        c0 = ci * PROJ_CHUNK
        y = jnp.dot(h, w_ref[:, c0:c0 + PROJ_CHUNK], preferred_element_type=jnp.float32)
        cb = c0 // LANES
        if cb == CB_SB_Q:
            y = y * inv
        elif cb == CB_DF_Q:
            y = _group_rmsnorm(y, qg_ref[...]) * inv
        elif cb == CB_DF_K:
            y = _group_rmsnorm(y, kg_ref[...])
        for j in range(PROJ_CHUNK // LANES):
            o_ref[0, cb + j] = y[:, j * LANES:(j + 1) * LANES].astype(o_ref.dtype)


def _in_proj(x, norm_g, shift, scale, w_in_bf16, q_norm_g, k_norm_g):
    bsz, s, d = x.shape
    tm = ROWS_PROJ
    qg = jnp.tile(q_norm_g.reshape(1, HEAD_DIM), (1, 2))
    kg = jnp.tile(k_norm_g.reshape(1, HEAD_DIM), (1, 2))
    return pl.pallas_call(
        _in_proj_kernel,
        out_shape=jax.ShapeDtypeStruct((bsz, IN_COLS // LANES, s, LANES), jnp.bfloat16),
        grid=(bsz, s // tm),
        in_specs=[pl.BlockSpec((1, tm, d), lambda b, i: (b, i, 0)),
                  pl.BlockSpec((1, d), lambda b, i: (0, 0)),
                  pl.BlockSpec((1, 1, d), lambda b, i: (b, 0, 0)),
                  pl.BlockSpec((1, 1, d), lambda b, i: (b, 0, 0)),
                  pl.BlockSpec((d, IN_COLS), lambda b, i: (0, 0)),
                  pl.BlockSpec((1, LANES), lambda b, i: (0, 0)),
                  pl.BlockSpec((1, LANES), lambda b, i: (0, 0))],
        out_specs=pl.BlockSpec((1, IN_COLS // LANES, tm, LANES), lambda b, i: (b, 0, i, 0)),
        compiler_params=pltpu.CompilerParams(vmem_limit_bytes=VMEM_LIMIT),
        name="in_proj",
    )(x, norm_g.reshape(1, d), shift.reshape(bsz, 1, d), scale.reshape(bsz, 1, d),
      w_in_bf16, qg, kg)


def _sb_tile(qm, kblk, vblk, cum, carry, mask):
    z = lax.dot_general(qm, kblk, _NT, preferred_element_type=jnp.float32)
    nz = -z
    l1p = jnp.log(1.0 + jnp.exp2(jnp.minimum(z, nz))) * LOG2E
    lg = jnp.minimum(nz, 0.0) - l1p
    if mask is not None:
        lg = jnp.where(mask, lg, 0.0)
    rem = jnp.dot(lg.astype(jnp.bfloat16), cum, preferred_element_type=jnp.float32)
    a = jnp.exp2(lg + z + rem + carry)
    if mask is not None:
        a = jnp.where(mask, a, 0.0)
    o = jnp.dot(a.astype(jnp.bfloat16), vblk, preferred_element_type=jnp.float32)
    return o, carry + jnp.sum(lg, axis=-1, keepdims=True)


def _sb_kernel(q_ref, k_ref, v_ref, g_ref, o_ref, acc, carry_s):
    step = pl.program_id(2)
    lane = lax.broadcasted_iota(jnp.int32, (1, LANES), 1)
    first = lane < HEAD_DIM
    row = lax.broadcasted_iota(jnp.int32, (TK, TK), 0)
    col = lax.broadcasted_iota(jnp.int32, (TK, TK), 1)
    cum = (row > col).astype(jnp.bfloat16)
    strict = col < row
    diag_mask = jnp.concatenate([strict, strict], axis=0)

    def q_rows(u):
        return pl.ds(pl.multiple_of(u * TQ, TQ), TQ)

    def acc_rows(u):
        return pl.ds(pl.multiple_of(u * (2 * TQ), 2 * TQ), 2 * TQ)

    def stacked(u):
        q = q_ref[0, 0, q_rows(u), :]
        zero = jnp.zeros_like(q)
        return jnp.concatenate([jnp.where(first, q, zero), jnp.where(first, zero, q)], axis=0)

    def tiles(qm, kbs, carry, masks):
        total = None
        for kb, mask in zip(kbs, masks):
            start = pl.multiple_of(kb * TK, TK)
            o, carry = _sb_tile(qm, k_ref[0, 0, pl.ds(start, TK), :], v_ref[0, 0, pl.ds(start, TK), :],
                                cum, carry, mask)
            total = o if total is None else total + o
        return total, carry

    zero_carry = jnp.zeros((2 * TQ, 1), jnp.float32)

    def first_block(g, state):
        u0 = g * SB_PAIR
        qt0 = step * SB_QT + u0

        def block(tile0_has_left):
            for w in range(SB_PAIR):
                qm = stacked(u0 + w)
                if w == 0 and not tile0_has_left:
                    o, carry = tiles(qm, [qt0], zero_carry, [diag_mask])
                else:
                    o, carry = tiles(qm, [qt0 + w, qt0 + w - 1], zero_carry, [diag_mask, None])
                acc[acc_rows(u0 + w), :] = o
                carry_s[acc_rows(u0 + w), :] = carry

        @pl.when(qt0 > 0)
        def _():
            block(True)

        @pl.when(qt0 == 0)
        def _():
            block(False)

        return state

    lax.fori_loop(0, SB_QT // SB_PAIR, first_block, 0)

    def alive(carry):
        return jnp.max(carry) > SB_DEAD_LOG2

    def sweep(u, qm, n_steps, kbs_of_step, carry):
        def cond(state):
            i, _, live = state
            return jnp.logical_and(i < n_steps, live)

        def body(state):
            i, carry, _ = state
            kbs = kbs_of_step(i)
            o, carry = tiles(qm, kbs, carry, [None] * len(kbs))
            acc[acc_rows(u), :] += o
            return i + 1, carry, alive(carry)

        return lax.while_loop(cond, body, (jnp.int32(0), carry, alive(carry)))[1]

    @pl.when(alive(carry_s[...]))
    def _():
        def sweeps_of(u, state):
            qm = stacked(u)
            left = jnp.maximum(step * SB_QT + u - 1, 0)
            rem = lax.rem(left, SB_GROUP)
            carry = sweep(u, qm, rem, lambda i: [left - 1 - i], carry_s[acc_rows(u), :])
            sweep(u, qm, left // SB_GROUP,
                  lambda i: [left - rem - i * SB_GROUP - 1 - t for t in range(SB_GROUP)], carry)
            return state

        lax.fori_loop(0, SB_QT, sweeps_of, 0)

    def finish(u, state):
        a = acc[acc_rows(u), :]
        out = jnp.where(first, a[:TQ, :], a[TQ:, :])
        g = g_ref[0, 0, q_rows(u), :].astype(jnp.float32)
        o_ref[0, q_rows(u), :] = (out * _silu(g)).astype(o_ref.dtype)
        return state

    lax.fori_loop(0, SB_QT, finish, 0)


def _sb_attention(proj):
    bsz, _, s, _ = proj.shape
    rows = SB_QT * TQ
    return pl.pallas_call(
        _sb_kernel,
        out_shape=jax.ShapeDtypeStruct((bsz, s, SB_WIDTH), jnp.bfloat16),
        grid=(bsz, SB_WIDTH // LANES, s // rows),
        in_specs=[pl.BlockSpec((1, 1, rows, LANES), lambda b, h, i: (b, CB_SB_Q + h, i, 0)),
                  pl.BlockSpec((1, 1, s, LANES), lambda b, h, i: (b, CB_SB_K + h, 0, 0)),
                  pl.BlockSpec((1, 1, s, LANES), lambda b, h, i: (b, CB_SB_V + h, 0, 0)),
                  pl.BlockSpec((1, 1, rows, LANES), lambda b, h, i: (b, CB_SB_G + h, i, 0))],
        out_specs=pl.BlockSpec((1, rows, LANES), lambda b, h, i: (b, i, h)),
        scratch_shapes=[pltpu.VMEM((SB_QT * 2 * TQ, LANES), jnp.float32),
                        pltpu.VMEM((SB_QT * 2 * TQ, 1), jnp.float32)],
        compiler_params=pltpu.CompilerParams(vmem_limit_bytes=VMEM_LIMIT),
        name="sb_attn",
    )(proj, proj, proj, proj)


def _df_kernel(q_ref, k_ref, v_ref, g_ref, slope_ref, lq1_ref, lk1_ref, lq2_ref, lk2_ref, sg_ref,
               qg_ref, kg_ref, o_ref, m_s, l_s, acc, *, lam_init):
    qi = pl.program_id(2)
    q = q_ref[0, 0]
    lane = lax.broadcasted_iota(jnp.int32, (1, LANES), 1)
    first = lane < HEAD_DIM
    zero = jnp.zeros_like(q)
    qs = jnp.concatenate([jnp.where(first, q, zero), jnp.where(first, zero, q)], axis=0)
    slope = slope_ref[0] * LOG2E
    row = lax.broadcasted_iota(jnp.int32, (DF_TQ, DF_TK), 0)
    col = lax.broadcasted_iota(jnp.int32, (DF_TQ, DF_TK), 1)
    causal = col <= row
    diag_mask = jnp.concatenate([causal, causal], axis=0)

    qk_bound = (jnp.max(jnp.abs(qg_ref[...])) * jnp.max(jnp.abs(kg_ref[...]))
                * (math.sqrt(HEAD_DIM) * LOG2E * DF_BOUND_MARGIN))
    slope_s = jnp.max(slope)

    def load(first_kb, width):
        start = pl.multiple_of(first_kb * DF_TK, DF_TK)
        rel = lax.broadcasted_iota(jnp.int32, (1, width), 1).astype(jnp.float32)
        off = ((first_kb - qi) * DF_TK).astype(jnp.float32)
        bias = jnp.concatenate([slope] * (width // LANES), axis=1) * (rel + off)
        return k_ref[0, 0, pl.ds(start, width), :], v_ref[0, 0, pl.ds(start, width), :], bias

    def last_rel(first_kb, width):
        return ((first_kb - qi) * DF_TK + (width - 1)).astype(jnp.float32)

    def sweep(n_steps, first_kb_of_step, width, tile, needed):
        def cond(state):
            i, live = state
            return jnp.logical_and(i < n_steps, live)

        def body(state):
            i, _ = state
            tile(first_kb_of_step(i), width, None)
            return i + 1, needed(first_kb_of_step(i + 1), width)

        lax.while_loop(cond, body, (jnp.int32(0), needed(first_kb_of_step(jnp.int32(0)), width)))

    def sweep_all(tile, needed):
        tile(qi, DF_TK, diag_mask)
        rem = lax.rem(qi, DF_GROUP)
        sweep(rem, lambda i: qi - 1 - i, DF_TK, tile, needed)
        sweep(qi // DF_GROUP, lambda i: qi - rem - (i + 1) * DF_GROUP, DF_GROUP * DF_TK, tile, needed)

    def fixed_shift():
        rowpos = lax.broadcasted_iota(jnp.int32, (DF_TQ, LANES), 0).astype(jnp.float32)
        shift = qk_bound + slope * jnp.concatenate([rowpos, rowpos], axis=0)
        l_s[...] = jnp.zeros((2 * DF_TQ, LANES), jnp.float32)
        acc[...] = jnp.zeros((2 * DF_TQ, LANES), jnp.float32)

        def tile(first_kb, width, mask):
            reps = width // LANES
            kblk, vblk, bias = load(first_kb, width)
            s = (lax.dot_general(qs, kblk, _NT, preferred_element_type=jnp.float32)
                 + bias - jnp.concatenate([shift] * reps, axis=1))
            if mask is not None:
                s = jnp.where(mask, s, NEG_BIG)
            p = jnp.exp2(s)
            part = p[:, :LANES]
            for r in range(1, reps):
                part = part + p[:, r * LANES:(r + 1) * LANES]
            l_s[...] += part
            acc[...] += jnp.dot(p.astype(jnp.bfloat16), vblk, preferred_element_type=jnp.float32)

        def needed(first_kb, width):
            return slope_s * last_rel(first_kb, width) > DF_DEAD_LOG2

        sweep_all(tile, needed)
        l = jnp.sum(l_s[...], axis=-1, keepdims=True)
        return l[:DF_TQ], l[DF_TQ:]

    def running_max():
        m_s[...] = jnp.full((2 * DF_TQ, LANES), NEG_BIG, jnp.float32)
        l_s[...] = jnp.zeros((2 * DF_TQ, LANES), jnp.float32)
        acc[...] = jnp.zeros((2 * DF_TQ, LANES), jnp.float32)

        def tile(first_kb, width, mask):
            reps = width // LANES
            kblk, vblk, bias = load(first_kb, width)
            s = lax.dot_general(qs, kblk, _NT, preferred_element_type=jnp.float32) + bias
            if mask is not None:
                s = jnp.where(mask, s, NEG_BIG)
            m_old = m_s[...]
            m_new = jnp.maximum(m_old, jnp.max(s, axis=-1, keepdims=True))
            p = jnp.exp2(s - jnp.concatenate([m_new] * reps, axis=1))
            alpha = jnp.exp2(m_old - m_new)
            l_s[...] = alpha * l_s[...] + jnp.sum(p, axis=-1, keepdims=True)
            acc[...] = alpha * acc[...] + jnp.dot(p.astype(jnp.bfloat16), vblk,
                                                  preferred_element_type=jnp.float32)
            m_s[...] = m_new

        def needed(first_kb, width):
            return (qk_bound + slope_s * last_rel(first_kb, width) - jnp.min(m_s[...])
                    > DF_DEAD_LOG2)

        sweep_all(tile, needed)
        return l_s[:DF_TQ, :1], l_s[DF_TQ:, :1]

    l1, l2 = lax.cond(qk_bound < DF_FIXED_SHIFT_MAX, fixed_shift, running_max)

    lam = (jnp.exp(jnp.sum(lq1_ref[...] * lk1_ref[...], keepdims=True))
           - jnp.exp(jnp.sum(lq2_ref[...] * lk2_ref[...], keepdims=True)) + lam_init)
    o = acc[:DF_TQ, :] / l1 - lam * (acc[DF_TQ:, :] / l2)
    ms = jnp.mean(o * o, axis=-1, keepdims=True)
    o = o * lax.rsqrt(ms + EPS) * sg_ref[...] * (1.0 - lam_init)
    g = g_ref[0, 0].astype(jnp.float32)
    o_ref[0] = (o * _silu(g)).astype(o_ref.dtype)


def _df_attention(proj, slopes, lq1, lk1, lq2, lk2, subln_g, q_norm_g, k_norm_g, lam_init):
    bsz, _, s, _ = proj.shape
    nq = s // DF_TQ
    vec = pl.BlockSpec((1, HEAD_DIM), lambda b, h, i: (0, 0))
    stat = pltpu.VMEM((2 * DF_TQ, LANES), jnp.float32)
    return pl.pallas_call(
        functools.partial(_df_kernel, lam_init=lam_init),
        out_shape=jax.ShapeDtypeStruct((bsz, s, DF_WIDTH), jnp.bfloat16),
        grid=(bsz, DF_HEADS, nq),
        in_specs=[pl.BlockSpec((1, 1, DF_TQ, LANES), lambda b, h, i: (b, CB_DF_Q + h, i, 0)),
                  pl.BlockSpec((1, 1, s, LANES), lambda b, h, i: (b, CB_DF_K + h, 0, 0)),
                  pl.BlockSpec((1, 1, s, LANES), lambda b, h, i: (b, CB_DF_V + h, 0, 0)),
                  pl.BlockSpec((1, 1, DF_TQ, LANES), lambda b, h, i: (b, CB_DF_G + h, i, 0)),
                  pl.BlockSpec((1, 1, LANES), lambda b, h, i: (h, 0, 0)),
                  vec, vec, vec, vec,
                  pl.BlockSpec((1, LANES), lambda b, h, i: (0, 0)),
                  vec, vec],
        out_specs=pl.BlockSpec((1, DF_TQ, LANES), lambda b, h, i: (b, i, h)),
        scratch_shapes=[stat, stat, stat],
        compiler_params=pltpu.CompilerParams(vmem_limit_bytes=VMEM_LIMIT),
        name="df_attn",
    )(proj, proj, proj, proj, slopes, lq1, lk1, lq2, lk2, subln_g, q_norm_g, k_norm_g)


def _out_proj_kernel(x_ref, sb_ref, df_ref, w_ref, gate_ref, o_ref):
    y = (jnp.dot(sb_ref[0], w_ref[:SB_WIDTH, :], preferred_element_type=jnp.float32)
         + jnp.dot(df_ref[0], w_ref[SB_WIDTH:, :], preferred_element_type=jnp.float32))
    o_ref[0] = x_ref[0] + gate_ref[0] * y


def _out_proj(x, sb_o, df_o, w_out_bf16, gate):
    bsz, s, d = x.shape
    tm = ROWS_OUT
    return pl.pallas_call(
        _out_proj_kernel,
        out_shape=jax.ShapeDtypeStruct((bsz, s, d), jnp.float32),
        grid=(bsz, s // tm),
        in_specs=[pl.BlockSpec((1, tm, d), lambda b, i: (b, i, 0)),
                  pl.BlockSpec((1, tm, SB_WIDTH), lambda b, i: (b, i, 0)),
                  pl.BlockSpec((1, tm, DF_WIDTH), lambda b, i: (b, i, 0)),
                  pl.BlockSpec((SB_WIDTH + DF_WIDTH, d), lambda b, i: (0, 0)),
                  pl.BlockSpec((1, 1, d), lambda b, i: (b, 0, 0))],
        out_specs=pl.BlockSpec((1, tm, d), lambda b, i: (b, i, 0)),
        compiler_params=pltpu.CompilerParams(vmem_limit_bytes=VMEM_LIMIT),
        name="out_proj",
    )(x, sb_o, df_o, w_out_bf16, gate.reshape(bsz, 1, d))


def _layer(x, c, layer_idx, norm_g, w_ada, b_ada, w_in, q_norm_g, k_norm_g,
           lambda_q1, lambda_k1, lambda_q2, lambda_k2, subln_g, w_out):
    d = x.shape[-1]
    lam_init = 0.8 - 0.6 * math.exp(-0.3 * layer_idx)
    mod = _adaln_mod(c, w_ada, b_ada)
    shift, scale, gate = mod[:, :d], mod[:, d:2 * d], mod[:, 2 * d:]
    proj = _in_proj(x, norm_g, shift, scale, w_in.astype(jnp.bfloat16), q_norm_g, k_norm_g)
    sb_o = _sb_attention(proj)
    slopes = jnp.asarray([2.0 ** (-8.0 * (h + 1) / DF_HEADS) for h in range(DF_HEADS)], jnp.float32)
    slopes = jnp.broadcast_to(slopes[:, None, None], (DF_HEADS, 1, LANES))
    row = lambda v: v.reshape(1, -1)
    df_o = _df_attention(proj, slopes, row(lambda_q1), row(lambda_k1), row(lambda_q2),
                         row(lambda_k2), row(subln_g), row(q_norm_g), row(k_norm_g), lam_init)
    return _out_proj(x, sb_o, df_o, w_out.astype(jnp.bfloat16), gate)


@jax.jit
def kernel(x, c, norm_g, w_ada, b_ada, w_in, q_norm_g, k_norm_g, lambda_q1, lambda_k1,
           lambda_q2, lambda_k2, subln_g, w_out):
    for l in range(norm_g.shape[0]):
        x = _layer(x, c, l, norm_g[l], w_ada[l], b_ada[l], w_in[l], q_norm_g[l], k_norm_g[l],
                   lambda_q1[l], lambda_k1[l], lambda_q2[l], lambda_k2[l], subln_g[l], w_out[l])
    return x
```

```python
import functools
import math

import jax
import jax.numpy as jnp
from jax import lax
from jax.experimental import pallas as pl
from jax.experimental.pallas import tpu as pltpu

D_MODEL = 1024
SB_HEADS = 8
DF_HEADS = 4
HEAD_DIM = 64
LANES = 128
SB_WIDTH = SB_HEADS * HEAD_DIM
DF_WIDTH = DF_HEADS * 2 * HEAD_DIM
IN_COLS = 4 * SB_WIDTH + 4 * DF_WIDTH
EPS = 1e-6
NEG_BIG = -1e30

CB_SB_Q, CB_SB_K, CB_SB_V, CB_SB_G = 0, 4, 8, 12
CB_DF_Q, CB_DF_K, CB_DF_V, CB_DF_G = 16, 20, 24, 28

ROWS_PROJ = 512
ROWS_OUT = 1024
TQ = 256
TK = 256
DF_TQ = 512
DF_TK = 512
SB_QT = 4
SB_GROUP = 4
DF_GROUP = 2
LOG2E = math.log2(math.e)
SB_DEAD_LOG2 = -150.0
DF_DEAD_LOG2 = -150.0
DF_FIXED_SHIFT_MAX = 48.0
DF_BOUND_MARGIN = 1.02
PROJ_CHUNK = 512
VMEM_LIMIT = 48 * 1024 * 1024

_NT = (((1,), (1,)), ((), ()))


def _silu(g):
    return g / (1.0 + jnp.exp(-g))


def _adaln_kernel(ct_ref, w_ref, b_ref, o_ref):
    w = w_ref[...]
    rows = []
    for b in range(o_ref.shape[0]):
        col = ct_ref[:, b:b + 1]
        rows.append(jnp.sum(col * w, axis=0, keepdims=True))
    o_ref[...] = jnp.concatenate(rows, axis=0) + b_ref[...]


def _adaln_mod(c, w_ada, b_ada):
    bsz, d = c.shape
    n = w_ada.shape[1]
    tn = 512
    return pl.pallas_call(
        _adaln_kernel,
        out_shape=jax.ShapeDtypeStruct((bsz, n), jnp.float32),
        grid=(n // tn,),
        in_specs=[pl.BlockSpec((d, bsz), lambda j: (0, 0)),
                  pl.BlockSpec((d, tn), lambda j: (0, j)),
                  pl.BlockSpec((1, tn), lambda j: (0, j))],
        out_specs=pl.BlockSpec((bsz, tn), lambda j: (0, j)),
        name="adaln_mod",
    )(c.T, w_ada, b_ada.reshape(1, n))


def _group_rmsnorm(y, gain):
    outs = []
    lane = lax.broadcasted_iota(jnp.int32, (1, LANES), 1)
    lo = lane < HEAD_DIM
    for j in range(y.shape[1] // LANES):
        blk = y[:, j * LANES:(j + 1) * LANES]
        sq = blk * blk
        s_lo = jnp.sum(jnp.where(lo, sq, 0.0), axis=-1, keepdims=True)
        s_hi = jnp.sum(jnp.where(lo, 0.0, sq), axis=-1, keepdims=True)
        ms = jnp.where(lo, s_lo, s_hi) * (1.0 / HEAD_DIM)
        outs.append(blk * lax.rsqrt(ms + EPS) * gain)
    return jnp.concatenate(outs, axis=1)


def _in_proj_kernel(x_ref, ng_ref, shift_ref, scale_ref, w_ref, qg_ref, kg_ref, o_ref):
    x = x_ref[0]
    ms = jnp.mean(x * x, axis=-1, keepdims=True)
    h = x * lax.rsqrt(ms + EPS) * ng_ref[...]
    h = (h * (1.0 + scale_ref[0]) + shift_ref[0]).astype(jnp.bfloat16)
    inv = LOG2E / math.sqrt(HEAD_DIM)
    for ci in range(IN_COLS // PROJ_CHUNK):
        c0 = ci * PROJ_CHUNK
        y = jnp.dot(h, w_ref[:, c0:c0 + PROJ_CHUNK], preferred_element_type=jnp.float32)
        cb = c0 // LANES
        if cb == CB_SB_Q:
            y = y * inv
        elif cb == CB_DF_Q:
            y = _group_rmsnorm(y, qg_ref[...]) * inv
        elif cb == CB_DF_K:
            y = _group_rmsnorm(y, kg_ref[...])
        o_ref[0, :, c0:c0 + PROJ_CHUNK] = y.astype(o_ref.dtype)


def _in_proj(x, norm_g, shift, scale, w_in_bf16, q_norm_g, k_norm_g):
    bsz, s, d = x.shape
    tm = ROWS_PROJ
    qg = jnp.tile(q_norm_g.reshape(1, HEAD_DIM), (1, 2))
    kg = jnp.tile(k_norm_g.reshape(1, HEAD_DIM), (1, 2))
    return pl.pallas_call(
        _in_proj_kernel,
        out_shape=jax.ShapeDtypeStruct((bsz, s, IN_COLS), jnp.bfloat16),
        grid=(bsz, s // tm),
        in_specs=[pl.BlockSpec((1, tm, d), lambda b, i: (b, i, 0)),
                  pl.BlockSpec((1, d), lambda b, i: (0, 0)),
                  pl.BlockSpec((1, 1, d), lambda b, i: (b, 0, 0)),
                  pl.BlockSpec((1, 1, d), lambda b, i: (b, 0, 0)),
                  pl.BlockSpec((d, IN_COLS), lambda b, i: (0, 0)),
                  pl.BlockSpec((1, LANES), lambda b, i: (0, 0)),
                  pl.BlockSpec((1, LANES), lambda b, i: (0, 0))],
        out_specs=pl.BlockSpec((1, tm, IN_COLS), lambda b, i: (b, i, 0)),
        compiler_params=pltpu.CompilerParams(vmem_limit_bytes=VMEM_LIMIT),
        name="in_proj",
    )(x, norm_g.reshape(1, d), shift.reshape(bsz, 1, d), scale.reshape(bsz, 1, d),
      w_in_bf16, qg, kg)


def _sb_tile(qm, kblk, vblk, cum_ones, carry, mask):
    z = lax.dot_general(qm, kblk, _NT, preferred_element_type=jnp.float32)
    nz = -z
    l1p = jnp.log(1.0 + jnp.exp2(jnp.minimum(z, nz))) * LOG2E
    lg = jnp.minimum(nz, 0.0) - l1p
    if mask is not None:
        lg = jnp.where(mask, lg, 0.0)
    sums = jnp.dot(lg.astype(jnp.bfloat16), cum_ones, preferred_element_type=jnp.float32)
    rem, row_sum = sums[:, :TK], sums[:, TK:]
    wide_carry = jnp.concatenate([carry] * (TK // LANES), axis=1)
    a = jnp.exp2(lg + z + rem + wide_carry)
    if mask is not None:
        a = jnp.where(mask, a, 0.0)
    o = jnp.dot(a.astype(jnp.bfloat16), vblk, preferred_element_type=jnp.float32)
    return o, carry + row_sum


def _sb_kernel(q_ref, k_ref, v_ref, g_ref, o_ref, acc):
    step = pl.program_id(2)
    lane = lax.broadcasted_iota(jnp.int32, (1, LANES), 1)
    first = lane < HEAD_DIM
    row = lax.broadcasted_iota(jnp.int32, (TK, TK), 0)
    col = lax.broadcasted_iota(jnp.int32, (TK, TK), 1)
    cum = (row > col).astype(jnp.bfloat16)
    cum_ones = jnp.concatenate([cum, jnp.ones((TK, LANES), jnp.bfloat16)], axis=1)
    strict = col < row
    diag_mask = jnp.concatenate([strict, strict], axis=0)

    def stacked(u):
        q = q_ref[0, u * TQ:(u + 1) * TQ, :]
        zero = jnp.zeros_like(q)
        return jnp.concatenate([jnp.where(first, q, zero), jnp.where(first, zero, q)], axis=0)

    qs = [stacked(u) for u in range(SB_QT)]
    acc_of = [acc.at[u * 2 * TQ:(u + 1) * 2 * TQ, :] for u in range(SB_QT)]

    def tiles(qm, kbs, carry, masks):
        total = None
        for kb, mask in zip(kbs, masks):
            start = pl.multiple_of(kb * TK, TK)
            o, carry = _sb_tile(qm, k_ref[0, pl.ds(start, TK), :], v_ref[0, pl.ds(start, TK), :],
                                cum_ones, carry, mask)
            total = o if total is None else total + o
        return total, carry

    zero_carry = jnp.zeros((2 * TQ, LANES), jnp.float32)

    def first_block(tile0_has_left):
        carries = []
        for u in range(SB_QT):
            qt = step * SB_QT + u
            if u == 0 and not tile0_has_left:
                o, carry = tiles(qs[u], [qt], zero_carry, [diag_mask])
            else:
                o, carry = tiles(qs[u], [qt, qt - 1], zero_carry, [diag_mask, None])
            acc_of[u][...] = o
            carries.append(carry)
        return tuple(carries)

    carries = lax.cond(step > 0, lambda: first_block(True), lambda: first_block(False))

    def alive(carry):
        return jnp.max(carry) > SB_DEAD_LOG2

    def sweep(u, n_steps, kbs_of_step, carry):
        def cond(state):
            i, _, live = state
            return jnp.logical_and(i < n_steps, live)

        def body(state):
            i, carry, _ = state
            kbs = kbs_of_step(i)
            o, carry = tiles(qs[u], kbs, carry, [None] * len(kbs))
            acc_of[u][...] += o
            return i + 1, carry, alive(carry)

        return lax.while_loop(cond, body, (jnp.int32(0), carry, alive(carry)))[1]

    any_alive = alive(carries[0])
    for carry in carries[1:]:
        any_alive = jnp.logical_or(any_alive, alive(carry))

    @pl.when(any_alive)
    def _():
        for u in range(SB_QT):
            left = jnp.maximum(step * SB_QT + u - 1, 0)
            rem = lax.rem(left, SB_GROUP)
            carry = sweep(u, rem, lambda i, left=left: [left - 1 - i], carries[u])
            sweep(u, left // SB_GROUP,
                  lambda i, left=left, rem=rem: [left - rem - i * SB_GROUP - 1 - t
                                                 for t in range(SB_GROUP)], carry)

    for u in range(SB_QT):
        out = jnp.where(first, acc_of[u][:TQ, :], acc_of[u][TQ:, :])
        g = g_ref[0, u * TQ:(u + 1) * TQ, :].astype(jnp.float32)
        o_ref[0, u * TQ:(u + 1) * TQ, :] = (out * _silu(g)).astype(o_ref.dtype)


def _sb_attention(proj):
    bsz, s, _ = proj.shape
    rows = SB_QT * TQ
    return pl.pallas_call(
        _sb_kernel,
        out_shape=jax.ShapeDtypeStruct((bsz, s, SB_WIDTH), jnp.bfloat16),
        grid=(bsz, SB_WIDTH // LANES, s // rows),
        in_specs=[pl.BlockSpec((1, rows, LANES), lambda b, h, i: (b, i, CB_SB_Q + h)),
                  pl.BlockSpec((1, s, LANES), lambda b, h, i: (b, 0, CB_SB_K + h)),
                  pl.BlockSpec((1, s, LANES), lambda b, h, i: (b, 0, CB_SB_V + h)),
                  pl.BlockSpec((1, rows, LANES), lambda b, h, i: (b, i, CB_SB_G + h))],
        out_specs=pl.BlockSpec((1, rows, LANES), lambda b, h, i: (b, i, h)),
        scratch_shapes=[pltpu.VMEM((SB_QT * 2 * TQ, LANES), jnp.float32)],
        compiler_params=pltpu.CompilerParams(vmem_limit_bytes=VMEM_LIMIT),
        name="sb_attn",
    )(proj, proj, proj, proj)


def _df_kernel(q_ref, k_ref, v_ref, g_ref, slope_ref, lq1_ref, lk1_ref, lq2_ref, lk2_ref, sg_ref,
               qg_ref, kg_ref, o_ref, m_s, l_s, acc, *, lam_init):
    qi = pl.program_id(2)
    q = q_ref[0]
    lane = lax.broadcasted_iota(jnp.int32, (1, LANES), 1)
    first = lane < HEAD_DIM
    zero = jnp.zeros_like(q)
    qs = jnp.concatenate([jnp.where(first, q, zero), jnp.where(first, zero, q)], axis=0)
    slope = slope_ref[0] * LOG2E
    row = lax.broadcasted_iota(jnp.int32, (DF_TQ, DF_TK), 0)
    col = lax.broadcasted_iota(jnp.int32, (DF_TQ, DF_TK), 1)
    causal = col <= row
    diag_mask = jnp.concatenate([causal, causal], axis=0)

    qk_bound = (jnp.max(jnp.abs(qg_ref[...])) * jnp.max(jnp.abs(kg_ref[...]))
                * (math.sqrt(HEAD_DIM) * LOG2E * DF_BOUND_MARGIN))
    slope_s = jnp.max(slope)

    def load(first_kb, width):
        start = pl.multiple_of(first_kb * DF_TK, DF_TK)
        rel = lax.broadcasted_iota(jnp.int32, (1, width), 1).astype(jnp.float32)
        off = ((first_kb - qi) * DF_TK).astype(jnp.float32)
        bias = jnp.concatenate([slope] * (width // LANES), axis=1) * (rel + off)
        return k_ref[0, pl.ds(start, width), :], v_ref[0, pl.ds(start, width), :], bias

    def last_rel(first_kb, width):
        return ((first_kb - qi) * DF_TK + (width - 1)).astype(jnp.float32)

    def sweep(n_steps, first_kb_of_step, width, tile, needed):
        def cond(state):
            i, live = state
            return jnp.logical_and(i < n_steps, live)

        def body(state):
            i, _ = state
            tile(first_kb_of_step(i), width, None)
            return i + 1, needed(first_kb_of_step(i + 1), width)

        lax.while_loop(cond, body, (jnp.int32(0), needed(first_kb_of_step(jnp.int32(0)), width)))

    def sweep_all(tile, needed):
        tile(qi, DF_TK, diag_mask)
        rem = lax.rem(qi, DF_GROUP)
        sweep(rem, lambda i: qi - 1 - i, DF_TK, tile, needed)
        sweep(qi // DF_GROUP, lambda i: qi - rem - (i + 1) * DF_GROUP, DF_GROUP * DF_TK, tile, needed)

    def fixed_shift():
        rowpos = lax.broadcasted_iota(jnp.int32, (DF_TQ, LANES), 0).astype(jnp.float32)
        shift = qk_bound + slope * jnp.concatenate([rowpos, rowpos], axis=0)
        l_s[...] = jnp.zeros((2 * DF_TQ, LANES), jnp.float32)
        acc[...] = jnp.zeros((2 * DF_TQ, LANES), jnp.float32)

        def tile(first_kb, width, mask):
            reps = width // LANES
            kblk, vblk, bias = load(first_kb, width)
            s = (lax.dot_general(qs, kblk, _NT, preferred_element_type=jnp.float32)
                 + bias - jnp.concatenate([shift] * reps, axis=1))
            if mask is not None:
                s = jnp.where(mask, s, NEG_BIG)
            p = jnp.exp2(s)
            part = p[:, :LANES]
            for r in range(1, reps):
                part = part + p[:, r * LANES:(r + 1) * LANES]
            l_s[...] += part
            acc[...] += jnp.dot(p.astype(jnp.bfloat16), vblk, preferred_element_type=jnp.float32)

        def needed(first_kb, width):
            return slope_s * last_rel(first_kb, width) > DF_DEAD_LOG2

        sweep_all(tile, needed)
        l = jnp.sum(l_s[...], axis=-1, keepdims=True)
        return l[:DF_TQ], l[DF_TQ:]

    def running_max():
        m_s[...] = jnp.full((2 * DF_TQ, LANES), NEG_BIG, jnp.float32)
        l_s[...] = jnp.zeros((2 * DF_TQ, LANES), jnp.float32)
        acc[...] = jnp.zeros((2 * DF_TQ, LANES), jnp.float32)

        def tile(first_kb, width, mask):
            reps = width // LANES
            kblk, vblk, bias = load(first_kb, width)
            s = lax.dot_general(qs, kblk, _NT, preferred_element_type=jnp.float32) + bias
            if mask is not None:
                s = jnp.where(mask, s, NEG_BIG)
            m_old = m_s[...]
            m_new = jnp.maximum(m_old, jnp.max(s, axis=-1, keepdims=True))
            p = jnp.exp2(s - jnp.concatenate([m_new] * reps, axis=1))
            alpha = jnp.exp2(m_old - m_new)
            l_s[...] = alpha * l_s[...] + jnp.sum(p, axis=-1, keepdims=True)
            acc[...] = alpha * acc[...] + jnp.dot(p.astype(jnp.bfloat16), vblk,
                                                  preferred_element_type=jnp.float32)
            m_s[...] = m_new

        def needed(first_kb, width):
            return (qk_bound + slope_s * last_rel(first_kb, width) - jnp.min(m_s[...])
                    > DF_DEAD_LOG2)

        sweep_all(tile, needed)
        return l_s[:DF_TQ, :1], l_s[DF_TQ:, :1]

    l1, l2 = lax.cond(qk_bound < DF_FIXED_SHIFT_MAX, fixed_shift, running_max)

    lam = (jnp.exp(jnp.sum(lq1_ref[...] * lk1_ref[...], keepdims=True))
           - jnp.exp(jnp.sum(lq2_ref[...] * lk2_ref[...], keepdims=True)) + lam_init)
    o = acc[:DF_TQ, :] / l1 - lam * (acc[DF_TQ:, :] / l2)
    ms = jnp.mean(o * o, axis=-1, keepdims=True)
    o = o * lax.rsqrt(ms + EPS) * sg_ref[...] * (1.0 - lam_init)
    g = g_ref[0].astype(jnp.float32)
    o_ref[0] = (o * _silu(g)).astype(o_ref.dtype)


def _df_attention(proj, slopes, lq1, lk1, lq2, lk2, subln_g, q_norm_g, k_norm_g, lam_init):
    bsz, s, _ = proj.shape
    nq = s // DF_TQ
    vec = pl.BlockSpec((1, HEAD_DIM), lambda b, h, i: (0, 0))
    stat = pltpu.VMEM((2 * DF_TQ, LANES), jnp.float32)
    return pl.pallas_call(
        functools.partial(_df_kernel, lam_init=lam_init),
        out_shape=jax.ShapeDtypeStruct((bsz, s, DF_WIDTH), jnp.bfloat16),
        grid=(bsz, DF_HEADS, nq),
        in_specs=[pl.BlockSpec((1, DF_TQ, LANES), lambda b, h, i: (b, i, CB_DF_Q + h)),
                  pl.BlockSpec((1, s, LANES), lambda b, h, i: (b, 0, CB_DF_K + h)),
                  pl.BlockSpec((1, s, LANES), lambda b, h, i: (b, 0, CB_DF_V + h)),
                  pl.BlockSpec((1, DF_TQ, LANES), lambda b, h, i: (b, i, CB_DF_G + h)),
                  pl.BlockSpec((1, 1, LANES), lambda b, h, i: (h, 0, 0)),
                  vec, vec, vec, vec,
                  pl.BlockSpec((1, LANES), lambda b, h, i: (0, 0)),
                  vec, vec],
        out_specs=pl.BlockSpec((1, DF_TQ, LANES), lambda b, h, i: (b, i, h)),
        scratch_shapes=[stat, stat, stat],
        compiler_params=pltpu.CompilerParams(vmem_limit_bytes=VMEM_LIMIT),
        name="df_attn",
    )(proj, proj, proj, proj, slopes, lq1, lk1, lq2, lk2, subln_g, q_norm_g, k_norm_g)


def _out_proj_kernel(x_ref, sb_ref, df_ref, w_ref, gate_ref, o_ref):
    y = (jnp.dot(sb_ref[0], w_ref[:SB_WIDTH, :], preferred_element_type=jnp.float32)
         + jnp.dot(df_ref[0], w_ref[SB_WIDTH:, :], preferred_element_type=jnp.float32))
    o_ref[0] = x_ref[0] + gate_ref[0] * y


def _out_proj(x, sb_o, df_o, w_out_bf16, gate):
    bsz, s, d = x.shape
    tm = ROWS_OUT
    return pl.pallas_call(
        _out_proj_kernel,
        out_shape=jax.ShapeDtypeStruct((bsz, s, d), jnp.float32),
        grid=(bsz, s // tm),
        in_specs=[pl.BlockSpec((1, tm, d), lambda b, i: (b, i, 0)),
                  pl.BlockSpec((1, tm, SB_WIDTH), lambda b, i: (b, i, 0)),
                  pl.BlockSpec((1, tm, DF_WIDTH), lambda b, i: (b, i, 0)),
                  pl.BlockSpec((SB_WIDTH + DF_WIDTH, d), lambda b, i: (0, 0)),
                  pl.BlockSpec((1, 1, d), lambda b, i: (b, 0, 0))],
        out_specs=pl.BlockSpec((1, tm, d), lambda b, i: (b, i, 0)),
        compiler_params=pltpu.CompilerParams(vmem_limit_bytes=VMEM_LIMIT),
        name="out_proj",
    )(x, sb_o, df_o, w_out_bf16, gate.reshape(bsz, 1, d))


def _layer(x, c, layer_idx, norm_g, w_ada, b_ada, w_in, q_norm_g, k_norm_g,
           lambda_q1, lambda_k1, lambda_q2, lambda_k2, subln_g, w_out):
    d = x.shape[-1]
    lam_init = 0.8 - 0.6 * math.exp(-0.3 * layer_idx)
    mod = _adaln_mod(c, w_ada, b_ada)
    shift, scale, gate = mod[:, :d], mod[:, d:2 * d], mod[:, 2 * d:]
    proj = _in_proj(x, norm_g, shift, scale, w_in.astype(jnp.bfloat16), q_norm_g, k_norm_g)
    sb_o = _sb_attention(proj)
    slopes = jnp.asarray([2.0 ** (-8.0 * (h + 1) / DF_HEADS) for h in range(DF_HEADS)], jnp.float32)
    slopes = jnp.broadcast_to(slopes[:, None, None], (DF_HEADS, 1, LANES))
    row = lambda v: v.reshape(1, -1)
    df_o = _df_attention(proj, slopes, row(lambda_q1), row(lambda_k1), row(lambda_q2),
                         row(lambda_k2), row(subln_g), row(q_norm_g), row(k_norm_g), lam_init)
    return _out_proj(x, sb_o, df_o, w_out.astype(jnp.bfloat16), gate)


@jax.jit
def kernel(x, c, norm_g, w_ada, b_ada, w_in, q_norm_g, k_norm_g, lambda_q1, lambda_k1,
           lambda_q2, lambda_k2, subln_g, w_out):
    for l in range(norm_g.shape[0]):
        x = _layer(x, c, l, norm_g[l], w_ada[l], b_ada[l], w_in[l], q_norm_g[l], k_norm_g[l],
                   lambda_q1[l], lambda_k1[l], lambda_q2[l], lambda_k2[l], subln_g[l], w_out[l])
    return x
```

```python
import functools
import math

import jax
import jax.numpy as jnp
from jax import lax
from jax.experimental import pallas as pl
from jax.experimental.pallas import tpu as pltpu

D_MODEL = 1024
SB_HEADS = 8
DF_HEADS = 4
HEAD_DIM = 64
LANES = 128
SB_WIDTH = SB_HEADS * HEAD_DIM
DF_WIDTH = DF_HEADS * 2 * HEAD_DIM
IN_COLS = 4 * SB_WIDTH + 4 * DF_WIDTH
EPS = 1e-6
NEG_BIG = -1e30

CB_SB_Q, CB_SB_K, CB_SB_V, CB_SB_G = 0, 4, 8, 12
CB_DF_Q, CB_DF_K, CB_DF_V, CB_DF_G = 16, 20, 24, 28

ROWS_PROJ = 512
ROWS_OUT = 1024
TQ = 256
TK = 256
DF_TQ = 512
DF_TK = 512
SB_QT = 4
SB_FIRST = 3
SB_GROUP = 4
DF_GROUP = 2
LOG2E = math.log2(math.e)
SB_DEAD_LOG2 = -150.0
DF_DEAD_LOG2 = -150.0
DF_FIXED_SHIFT_MAX = 48.0
DF_BOUND_MARGIN = 1.02
PROJ_CHUNK = 512
VMEM_LIMIT = 48 * 1024 * 1024

_NT = (((1,), (1,)), ((), ()))


def _silu(g):
    return g / (1.0 + jnp.exp(-g))


def _adaln_kernel(ct_ref, w_ref, b_ref, o_ref):
    w = w_ref[...]
    rows = []
    for b in range(o_ref.shape[0]):
        col = ct_ref[:, b:b + 1]
        rows.append(jnp.sum(col * w, axis=0, keepdims=True))
    o_ref[...] = jnp.concatenate(rows, axis=0) + b_ref[...]


def _adaln_mod(c, w_ada, b_ada):
    bsz, d = c.shape
    n = w_ada.shape[1]
    tn = 512
    return pl.pallas_call(
        _adaln_kernel,
        out_shape=jax.ShapeDtypeStruct((bsz, n), jnp.float32),
        grid=(n // tn,),
        in_specs=[pl.BlockSpec((d, bsz), lambda j: (0, 0)),
                  pl.BlockSpec((d, tn), lambda j: (0, j)),
                  pl.BlockSpec((1, tn), lambda j: (0, j))],
        out_specs=pl.BlockSpec((bsz, tn), lambda j: (0, j)),
        name="adaln_mod",
    )(c.T, w_ada, b_ada.reshape(1, n))


def _group_rmsnorm(y, gain):
    outs = []
    lane = lax.broadcasted_iota(jnp.int32, (1, LANES), 1)
    lo = lane < HEAD_DIM
    for j in range(y.shape[1] // LANES):
        blk = y[:, j * LANES:(j + 1) * LANES]
        sq = blk * blk
        s_lo = jnp.sum(jnp.where(lo, sq, 0.0), axis=-1, keepdims=True)
        s_hi = jnp.sum(jnp.where(lo, 0.0, sq), axis=-1, keepdims=True)
        ms = jnp.where(lo, s_lo, s_hi) * (1.0 / HEAD_DIM)
        outs.append(blk * lax.rsqrt(ms + EPS) * gain)
    return jnp.concatenate(outs, axis=1)


def _in_proj_kernel(x_ref, ng_ref, shift_ref, scale_ref, w_ref, qg_ref, kg_ref, o_ref):
    x = x_ref[0]
    ms = jnp.mean(x * x, axis=-1, keepdims=True)
    h = x * lax.rsqrt(ms + EPS) * ng_ref[...]
    h = (h * (1.0 + scale_ref[0]) + shift_ref[0]).astype(jnp.bfloat16)
    inv = LOG2E / math.sqrt(HEAD_DIM)
    for ci in range(IN_COLS // PROJ_CHUNK):
        c0 = ci * PROJ_CHUNK
        y = jnp.dot(h, w_ref[:, c0:c0 + PROJ_CHUNK], preferred_element_type=jnp.float32)
        cb = c0 // LANES
        if cb == CB_SB_Q:
            y = y * inv
        elif cb == CB_DF_Q:
            y = _group_rmsnorm(y, qg_ref[...]) * inv
        elif cb == CB_DF_K:
            y = _group_rmsnorm(y, kg_ref[...])
        o_ref[0, :, c0:c0 + PROJ_CHUNK] = y.astype(o_ref.dtype)


def _in_proj(x, norm_g, shift, scale, w_in_bf16, q_norm_g, k_norm_g):
    bsz, s, d = x.shape
    tm = ROWS_PROJ
    qg = jnp.tile(q_norm_g.reshape(1, HEAD_DIM), (1, 2))
    kg = jnp.tile(k_norm_g.reshape(1, HEAD_DIM), (1, 2))
    return pl.pallas_call(
        _in_proj_kernel,
        out_shape=jax.ShapeDtypeStruct((bsz, s, IN_COLS), jnp.bfloat16),
        grid=(bsz, s // tm),
        in_specs=[pl.BlockSpec((1, tm, d), lambda b, i: (b, i, 0)),
                  pl.BlockSpec((1, d), lambda b, i: (0, 0)),
                  pl.BlockSpec((1, 1, d), lambda b, i: (b, 0, 0)),
                  pl.BlockSpec((1, 1, d), lambda b, i: (b, 0, 0)),
                  pl.BlockSpec((d, IN_COLS), lambda b, i: (0, 0)),
                  pl.BlockSpec((1, LANES), lambda b, i: (0, 0)),
                  pl.BlockSpec((1, LANES), lambda b, i: (0, 0))],
        out_specs=pl.BlockSpec((1, tm, IN_COLS), lambda b, i: (b, i, 0)),
        compiler_params=pltpu.CompilerParams(vmem_limit_bytes=VMEM_LIMIT),
        name="in_proj",
    )(x, norm_g.reshape(1, d), shift.reshape(bsz, 1, d), scale.reshape(bsz, 1, d),
      w_in_bf16, qg, kg)


def _sb_tile(qm, kblk, vblk, cum_ones, carry, mask):
    z = lax.dot_general(qm, kblk, _NT, preferred_element_type=jnp.float32)
    nz = -z
    l1p = jnp.log(1.0 + jnp.exp2(jnp.minimum(z, nz))) * LOG2E
    lg = jnp.minimum(nz, 0.0) - l1p
    if mask is not None:
        lg = jnp.where(mask, lg, 0.0)
    sums = jnp.dot(lg.astype(jnp.bfloat16), cum_ones, preferred_element_type=jnp.float32)
    rem, row_sum = sums[:, :TK], sums[:, TK:]
    wide_carry = jnp.concatenate([carry] * (TK // LANES), axis=1)
    a = jnp.exp2(lg + z + rem + wide_carry)
    if mask is not None:
        a = jnp.where(mask, a, 0.0)
    o = jnp.dot(a.astype(jnp.bfloat16), vblk, preferred_element_type=jnp.float32)
    return o, carry + row_sum


def _sb_kernel(q_ref, k_ref, v_ref, g_ref, o_ref, acc):
    step = pl.program_id(2)
    lane = lax.broadcasted_iota(jnp.int32, (1, LANES), 1)
    first = lane < HEAD_DIM
    row = lax.broadcasted_iota(jnp.int32, (TK, TK), 0)
    col = lax.broadcasted_iota(jnp.int32, (TK, TK), 1)
    cum = (row > col).astype(jnp.bfloat16)
    cum_ones = jnp.concatenate([cum, jnp.ones((TK, LANES), jnp.bfloat16)], axis=1)
    strict = col < row
    diag_mask = jnp.concatenate([strict, strict], axis=0)

    def stacked(u):
        q = q_ref[0, u * TQ:(u + 1) * TQ, :]
        zero = jnp.zeros_like(q)
        return jnp.concatenate([jnp.where(first, q, zero), jnp.where(first, zero, q)], axis=0)

    qs = [stacked(u) for u in range(SB_QT)]
    acc_of = [acc.at[u * 2 * TQ:(u + 1) * 2 * TQ, :] for u in range(SB_QT)]

    def tiles(qm, kbs, carry, masks):
        total = None
        for kb, mask in zip(kbs, masks):
            start = pl.multiple_of(kb * TK, TK)
            o, carry = _sb_tile(qm, k_ref[0, pl.ds(start, TK), :], v_ref[0, pl.ds(start, TK), :],
                                cum_ones, carry, mask)
            total = o if total is None else total + o
        return total, carry

    zero_carry = jnp.zeros((2 * TQ, LANES), jnp.float32)

    def first_block(first_step):
        carries = []
        for u in range(SB_QT):
            qt = step * SB_QT + u
            n = min(SB_FIRST, u + 1) if first_step else SB_FIRST
            o, carry = tiles(qs[u], [qt - t for t in range(n)], zero_carry,
                             [diag_mask] + [None] * (n - 1))
            acc_of[u][...] = o
            carries.append(carry)
        return tuple(carries)

    carries = lax.cond(step > 0, lambda: first_block(False), lambda: first_block(True))

    def alive(carry):
        return jnp.max(carry) > SB_DEAD_LOG2

    def sweep(u, n_steps, kbs_of_step, carry):
        def cond(state):
            i, _, live = state
            return jnp.logical_and(i < n_steps, live)

        def body(state):
            i, carry, _ = state
            kbs = kbs_of_step(i)
            o, carry = tiles(qs[u], kbs, carry, [None] * len(kbs))
            acc_of[u][...] += o
            return i + 1, carry, alive(carry)

        return lax.while_loop(cond, body, (jnp.int32(0), carry, alive(carry)))[1]

    any_alive = alive(carries[0])
    for carry in carries[1:]:
        any_alive = jnp.logical_or(any_alive, alive(carry))

    @pl.when(any_alive)
    def _():
        for u in range(SB_QT):
            left = jnp.maximum(step * SB_QT + u - (SB_FIRST - 1), 0)
            rem = lax.rem(left, SB_GROUP)
            carry = sweep(u, rem, lambda i, left=left: [left - 1 - i], carries[u])
            sweep(u, left // SB_GROUP,
                  lambda i, left=left, rem=rem: [left - rem - i * SB_GROUP - 1 - t
                                                 for t in range(SB_GROUP)], carry)

    for u in range(SB_QT):
        out = jnp.where(first, acc_of[u][:TQ, :], acc_of[u][TQ:, :])
        g = g_ref[0, u * TQ:(u + 1) * TQ, :].astype(jnp.float32)
        o_ref[0, u * TQ:(u + 1) * TQ, :] = (out * _silu(g)).astype(o_ref.dtype)


def _sb_attention(proj):
    bsz, s, _ = proj.shape
    rows = SB_QT * TQ
    return pl.pallas_call(
        _sb_kernel,
        out_shape=jax.ShapeDtypeStruct((bsz, s, SB_WIDTH), jnp.bfloat16),
        grid=(bsz, SB_WIDTH // LANES, s // rows),
        in_specs=[pl.BlockSpec((1, rows, LANES), lambda b, h, i: (b, i, CB_SB_Q + h)),
                  pl.BlockSpec((1, s, LANES), lambda b, h, i: (b, 0, CB_SB_K + h)),
                  pl.BlockSpec((1, s, LANES), lambda b, h, i: (b, 0, CB_SB_V + h)),
                  pl.BlockSpec((1, rows, LANES), lambda b, h, i: (b, i, CB_SB_G + h))],
        out_specs=pl.BlockSpec((1, rows, LANES), lambda b, h, i: (b, i, h)),
        scratch_shapes=[pltpu.VMEM((SB_QT * 2 * TQ, LANES), jnp.float32)],
        compiler_params=pltpu.CompilerParams(vmem_limit_bytes=VMEM_LIMIT),
        name="sb_attn",
    )(proj, proj, proj, proj)


def _df_kernel(q_ref, k_ref, v_ref, g_ref, slope_ref, lq1_ref, lk1_ref, lq2_ref, lk2_ref, sg_ref,
               qg_ref, kg_ref, o_ref, m_s, l_s, acc, *, lam_init):
    qi = pl.program_id(2)
    q = q_ref[0]
    lane = lax.broadcasted_iota(jnp.int32, (1, LANES), 1)
    first = lane < HEAD_DIM
    zero = jnp.zeros_like(q)
    qs = jnp.concatenate([jnp.where(first, q, zero), jnp.where(first, zero, q)], axis=0)
    slope = slope_ref[0] * LOG2E
    row = lax.broadcasted_iota(jnp.int32, (DF_TQ, DF_TK), 0)
    col = lax.broadcasted_iota(jnp.int32, (DF_TQ, DF_TK), 1)
    causal = col <= row
    diag_mask = jnp.concatenate([causal, causal], axis=0)

    qk_bound = (jnp.max(jnp.abs(qg_ref[...])) * jnp.max(jnp.abs(kg_ref[...]))
                * (math.sqrt(HEAD_DIM) * LOG2E * DF_BOUND_MARGIN))
    slope_s = jnp.max(slope)

    def load(first_kb, width):
        start = pl.multiple_of(first_kb * DF_TK, DF_TK)
        rel = lax.broadcasted_iota(jnp.int32, (1, width), 1).astype(jnp.float32)
        off = ((first_kb - qi) * DF_TK).astype(jnp.float32)
        bias = jnp.concatenate([slope] * (width // LANES), axis=1) * (rel + off)
        return k_ref[0, pl.ds(start, width), :], v_ref[0, pl.ds(start, width), :], bias

    def last_rel(first_kb, width):
        return ((first_kb - qi) * DF_TK + (width - 1)).astype(jnp.float32)

    def sweep(n_steps, first_kb_of_step, width, tile, needed):
        def cond(state):
            i, live = state
            return jnp.logical_and(i < n_steps, live)

        def body(state):
            i, _ = state
            tile(first_kb_of_step(i), width, None)
            return i + 1, needed(first_kb_of_step(i + 1), width)

        lax.while_loop(cond, body, (jnp.int32(0), needed(first_kb_of_step(jnp.int32(0)), width)))

    def sweep_all(tile, needed):
        tile(qi, DF_TK, diag_mask)
        rem = lax.rem(qi, DF_GROUP)
        sweep(rem, lambda i: qi - 1 - i, DF_TK, tile, needed)
        sweep(qi // DF_GROUP, lambda i: qi - rem - (i + 1) * DF_GROUP, DF_GROUP * DF_TK, tile, needed)

    def fixed_shift():
        rowpos = lax.broadcasted_iota(jnp.int32, (DF_TQ, LANES), 0).astype(jnp.float32)
        shift = qk_bound + slope * jnp.concatenate([rowpos, rowpos], axis=0)
        l_s[...] = jnp.zeros((2 * DF_TQ, LANES), jnp.float32)
        acc[...] = jnp.zeros((2 * DF_TQ, LANES), jnp.float32)

        def tile(first_kb, width, mask):
            reps = width // LANES
            kblk, vblk, bias = load(first_kb, width)
            s = (lax.dot_general(qs, kblk, _NT, preferred_element_type=jnp.float32)
                 + bias - jnp.concatenate([shift] * reps, axis=1))
            if mask is not None:
                s = jnp.where(mask, s, NEG_BIG)
            p = jnp.exp2(s)
            part = p[:, :LANES]
            for r in range(1, reps):
                part = part + p[:, r * LANES:(r + 1) * LANES]
            l_s[...] += part
            acc[...] += jnp.dot(p.astype(jnp.bfloat16), vblk, preferred_element_type=jnp.float32)

        def needed(first_kb, width):
            return slope_s * last_rel(first_kb, width) > DF_DEAD_LOG2

        sweep_all(tile, needed)
        l = jnp.sum(l_s[...], axis=-1, keepdims=True)
        return l[:DF_TQ], l[DF_TQ:]

    def running_max():
        m_s[...] = jnp.full((2 * DF_TQ, LANES), NEG_BIG, jnp.float32)
        l_s[...] = jnp.zeros((2 * DF_TQ, LANES), jnp.float32)
        acc[...] = jnp.zeros((2 * DF_TQ, LANES), jnp.float32)

        def tile(first_kb, width, mask):
            reps = width // LANES
            kblk, vblk, bias = load(first_kb, width)
            s = lax.dot_general(qs, kblk, _NT, preferred_element_type=jnp.float32) + bias
            if mask is not None:
                s = jnp.where(mask, s, NEG_BIG)
            m_old = m_s[...]
            m_new = jnp.maximum(m_old, jnp.max(s, axis=-1, keepdims=True))
            p = jnp.exp2(s - jnp.concatenate([m_new] * reps, axis=1))
            alpha = jnp.exp2(m_old - m_new)
            l_s[...] = alpha * l_s[...] + jnp.sum(p, axis=-1, keepdims=True)
            acc[...] = alpha * acc[...] + jnp.dot(p.astype(jnp.bfloat16), vblk,
                                                  preferred_element_type=jnp.float32)
            m_s[...] = m_new

        def needed(first_kb, width):
            return (qk_bound + slope_s * last_rel(first_kb, width) - jnp.min(m_s[...])
                    > DF_DEAD_LOG2)

        sweep_all(tile, needed)
        return l_s[:DF_TQ, :1], l_s[DF_TQ:, :1]

    l1, l2 = lax.cond(qk_bound < DF_FIXED_SHIFT_MAX, fixed_shift, running_max)

    lam = (jnp.exp(jnp.sum(lq1_ref[...] * lk1_ref[...], keepdims=True))
           - jnp.exp(jnp.sum(lq2_ref[...] * lk2_ref[...], keepdims=True)) + lam_init)
    o = acc[:DF_TQ, :] / l1 - lam * (acc[DF_TQ:, :] / l2)
    ms = jnp.mean(o * o, axis=-1, keepdims=True)
    o = o * lax.rsqrt(ms + EPS) * sg_ref[...] * (1.0 - lam_init)
    g = g_ref[0].astype(jnp.float32)
    o_ref[0] = (o * _silu(g)).astype(o_ref.dtype)


def _df_attention(proj, slopes, lq1, lk1, lq2, lk2, subln_g, q_norm_g, k_norm_g, lam_init):
    bsz, s, _ = proj.shape
    nq = s // DF_TQ
    vec = pl.BlockSpec((1, HEAD_DIM), lambda b, h, i: (0, 0))
    stat = pltpu.VMEM((2 * DF_TQ, LANES), jnp.float32)
    return pl.pallas_call(
        functools.partial(_df_kernel, lam_init=lam_init),
        out_shape=jax.ShapeDtypeStruct((bsz, s, DF_WIDTH), jnp.bfloat16),
        grid=(bsz, DF_HEADS, nq),
        in_specs=[pl.BlockSpec((1, DF_TQ, LANES), lambda b, h, i: (b, i, CB_DF_Q + h)),
                  pl.BlockSpec((1, s, LANES), lambda b, h, i: (b, 0, CB_DF_K + h)),
                  pl.BlockSpec((1, s, LANES), lambda b, h, i: (b, 0, CB_DF_V + h)),
                  pl.BlockSpec((1, DF_TQ, LANES), lambda b, h, i: (b, i, CB_DF_G + h)),
                  pl.BlockSpec((1, 1, LANES), lambda b, h, i: (h, 0, 0)),
                  vec, vec, vec, vec,
                  pl.BlockSpec((1, LANES), lambda b, h, i: (0, 0)),
                  vec, vec],
        out_specs=pl.BlockSpec((1, DF_TQ, LANES), lambda b, h, i: (b, i, h)),
        scratch_shapes=[stat, stat, stat],
        compiler_params=pltpu.CompilerParams(vmem_limit_bytes=VMEM_LIMIT),
        name="df_attn",
    )(proj, proj, proj, proj, slopes, lq1, lk1, lq2, lk2, subln_g, q_norm_g, k_norm_g)


def _out_proj_kernel(x_ref, sb_ref, df_ref, w_ref, gate_ref, o_ref):
    y = (jnp.dot(sb_ref[0], w_ref[:SB_WIDTH, :], preferred_element_type=jnp.float32)
         + jnp.dot(df_ref[0], w_ref[SB_WIDTH:, :], preferred_element_type=jnp.float32))
    o_ref[0] = x_ref[0] + gate_ref[0] * y


def _out_proj(x, sb_o, df_o, w_out_bf16, gate):
    bsz, s, d = x.shape
    tm = ROWS_OUT
    return pl.pallas_call(
        _out_proj_kernel,
        out_shape=jax.ShapeDtypeStruct((bsz, s, d), jnp.float32),
        grid=(bsz, s // tm),
        in_specs=[pl.BlockSpec((1, tm, d), lambda b, i: (b, i, 0)),
                  pl.BlockSpec((1, tm, SB_WIDTH), lambda b, i: (b, i, 0)),
                  pl.BlockSpec((1, tm, DF_WIDTH), lambda b, i: (b, i, 0)),
                  pl.BlockSpec((SB_WIDTH + DF_WIDTH, d), lambda b, i: (0, 0)),
                  pl.BlockSpec((1, 1, d), lambda b, i: (b, 0, 0))],
        out_specs=pl.BlockSpec((1, tm, d), lambda b, i: (b, i, 0)),
        compiler_params=pltpu.CompilerParams(vmem_limit_bytes=VMEM_LIMIT),
        name="out_proj",
    )(x, sb_o, df_o, w_out_bf16, gate.reshape(bsz, 1, d))


def _layer(x, c, layer_idx, norm_g, w_ada, b_ada, w_in, q_norm_g, k_norm_g,
           lambda_q1, lambda_k1, lambda_q2, lambda_k2, subln_g, w_out):
    d = x.shape[-1]
    lam_init = 0.8 - 0.6 * math.exp(-0.3 * layer_idx)
    mod = _adaln_mod(c, w_ada, b_ada)
    shift, scale, gate = mod[:, :d], mod[:, d:2 * d], mod[:, 2 * d:]
    proj = _in_proj(x, norm_g, shift, scale, w_in.astype(jnp.bfloat16), q_norm_g, k_norm_g)
    sb_o = _sb_attention(proj)
    slopes = jnp.asarray([2.0 ** (-8.0 * (h + 1) / DF_HEADS) for h in range(DF_HEADS)], jnp.float32)
    slopes = jnp.broadcast_to(slopes[:, None, None], (DF_HEADS, 1, LANES))
    row = lambda v: v.reshape(1, -1)
    df_o = _df_attention(proj, slopes, row(lambda_q1), row(lambda_k1), row(lambda_q2),
                         row(lambda_k2), row(subln_g), row(q_norm_g), row(k_norm_g), lam_init)
    return _out_proj(x, sb_o, df_o, w_out.astype(jnp.bfloat16), gate)


@jax.jit
def kernel(x, c, norm_g, w_ada, b_ada, w_in, q_norm_g, k_norm_g, lambda_q1, lambda_k1,
           lambda_q2, lambda_k2, subln_g, w_out):
    for l in range(norm_g.shape[0]):
        x = _layer(x, c, l, norm_g[l], w_ada[l], b_ada[l], w_in[l], q_norm_g[l], k_norm_g[l],
                   lambda_q1[l], lambda_k1[l], lambda_q2[l], lambda_k2[l], subln_g[l], w_out[l])
    return x
```

```python
import functools
import math

import jax
import jax.numpy as jnp
from jax import lax
from jax.experimental import pallas as pl
from jax.experimental.pallas import tpu as pltpu

D_MODEL = 1024
SB_HEADS = 8
DF_HEADS = 4
HEAD_DIM = 64
LANES = 128
SB_WIDTH = SB_HEADS * HEAD_DIM
DF_WIDTH = DF_HEADS * 2 * HEAD_DIM
IN_COLS = 4 * SB_WIDTH + 4 * DF_WIDTH
EPS = 1e-6
NEG_BIG = -1e30

CB_SB_Q, CB_SB_K, CB_SB_V, CB_SB_G = 0, 4, 8, 12
CB_DF_Q, CB_DF_K, CB_DF_V, CB_DF_G = 16, 20, 24, 28

ROWS_PROJ = 512
ROWS_OUT = 1024
TQ = 256
TK = 256
DF_TQ = 512
DF_TK = 512
SB_QT = 4
SB_FIRST = 3
SB_GROUP = 4
DF_GROUP = 2
LOG2E = math.log2(math.e)
SB_DEAD_LOG2 = -150.0
DF_DEAD_LOG2 = -150.0
DF_FIXED_SHIFT_MAX = 48.0
DF_BOUND_MARGIN = 1.02
PROJ_CHUNK = 512
VMEM_LIMIT = 48 * 1024 * 1024

_NT = (((1,), (1,)), ((), ()))


def _silu(g):
    return g / (1.0 + jnp.exp(-g))


def _adaln_kernel(ct_ref, w_ref, b_ref, o_ref):
    w = w_ref[...]
    rows = []
    for b in range(o_ref.shape[0]):
        col = ct_ref[:, b:b + 1]
        rows.append(jnp.sum(col * w, axis=0, keepdims=True))
    o_ref[...] = jnp.concatenate(rows, axis=0) + b_ref[...]


def _adaln_mod(c, w_ada, b_ada):
    bsz, d = c.shape
    n = w_ada.shape[1]
    tn = 512
    return pl.pallas_call(
        _adaln_kernel,
        out_shape=jax.ShapeDtypeStruct((bsz, n), jnp.float32),
        grid=(n // tn,),
        in_specs=[pl.BlockSpec((d, bsz), lambda j: (0, 0)),
                  pl.BlockSpec((d, tn), lambda j: (0, j)),
                  pl.BlockSpec((1, tn), lambda j: (0, j))],
        out_specs=pl.BlockSpec((bsz, tn), lambda j: (0, j)),
        name="adaln_mod",
    )(c.T, w_ada, b_ada.reshape(1, n))


def _group_rmsnorm(y, gain):
    outs = []
    lane = lax.broadcasted_iota(jnp.int32, (1, LANES), 1)
    lo = lane < HEAD_DIM
    for j in range(y.shape[1] // LANES):
        blk = y[:, j * LANES:(j + 1) * LANES]
        sq = blk * blk
        s_lo = jnp.sum(jnp.where(lo, sq, 0.0), axis=-1, keepdims=True)
        s_hi = jnp.sum(jnp.where(lo, 0.0, sq), axis=-1, keepdims=True)
        ms = jnp.where(lo, s_lo, s_hi) * (1.0 / HEAD_DIM)
        outs.append(blk * lax.rsqrt(ms + EPS) * gain)
    return jnp.concatenate(outs, axis=1)


def _in_proj_kernel(x_ref, ng_ref, shift_ref, scale_ref, w_ref, qg_ref, kg_ref, o_ref):
    x = x_ref[0]
    ms = jnp.mean(x * x, axis=-1, keepdims=True)
    h = x * lax.rsqrt(ms + EPS) * ng_ref[...]
    h = (h * (1.0 + scale_ref[0]) + shift_ref[0]).astype(jnp.bfloat16)
    inv = LOG2E / math.sqrt(HEAD_DIM)
    for ci in range(IN_COLS // PROJ_CHUNK):
        c0 = ci * PROJ_CHUNK
        y = jnp.dot(h, w_ref[:, c0:c0 + PROJ_CHUNK], preferred_element_type=jnp.float32)
        cb = c0 // LANES
        if cb == CB_SB_Q:
            y = y * inv
        elif cb == CB_DF_Q:
            y = _group_rmsnorm(y, qg_ref[...]) * inv
        elif cb == CB_DF_K:
            y = _group_rmsnorm(y, kg_ref[...])
        o_ref[0, :, c0:c0 + PROJ_CHUNK] = y.astype(o_ref.dtype)


def _in_proj(x, norm_g, shift, scale, w_in_bf16, q_norm_g, k_norm_g):
    bsz, s, d = x.shape
    tm = ROWS_PROJ
    qg = jnp.tile(q_norm_g.reshape(1, HEAD_DIM), (1, 2))
    kg = jnp.tile(k_norm_g.reshape(1, HEAD_DIM), (1, 2))
    return pl.pallas_call(
        _in_proj_kernel,
        out_shape=jax.ShapeDtypeStruct((bsz, s, IN_COLS), jnp.bfloat16),
        grid=(bsz, s // tm),
        in_specs=[pl.BlockSpec((1, tm, d), lambda b, i: (b, i, 0)),
                  pl.BlockSpec((1, d), lambda b, i: (0, 0)),
                  pl.BlockSpec((1, 1, d), lambda b, i: (b, 0, 0)),
                  pl.BlockSpec((1, 1, d), lambda b, i: (b, 0, 0)),
                  pl.BlockSpec((d, IN_COLS), lambda b, i: (0, 0)),
                  pl.BlockSpec((1, LANES), lambda b, i: (0, 0)),
                  pl.BlockSpec((1, LANES), lambda b, i: (0, 0))],
        out_specs=pl.BlockSpec((1, tm, IN_COLS), lambda b, i: (b, i, 0)),
        compiler_params=pltpu.CompilerParams(vmem_limit_bytes=VMEM_LIMIT),
        name="in_proj",
    )(x, norm_g.reshape(1, d), shift.reshape(bsz, 1, d), scale.reshape(bsz, 1, d),
      w_in_bf16, qg, kg)


def _sb_tile(qm, kblk, vblk, cum, carry, mask):
    z = lax.dot_general(qm, kblk, _NT, preferred_element_type=jnp.float32)
    nz = -z
    l1p = jnp.log(1.0 + jnp.exp2(jnp.minimum(z, nz))) * LOG2E
    lg = jnp.minimum(nz, 0.0) - l1p
    if mask is not None:
        lg = jnp.where(mask, lg, 0.0)
    rem = jnp.dot(lg.astype(jnp.bfloat16), cum, preferred_element_type=jnp.float32)
    a = jnp.exp2(lg + z + rem + carry)
    if mask is not None:
        a = jnp.where(mask, a, 0.0)
    o = jnp.dot(a.astype(jnp.bfloat16), vblk, preferred_element_type=jnp.float32)
    return o, carry + jnp.sum(lg, axis=-1, keepdims=True)


def _sb_kernel(q_ref, k_ref, v_ref, g_ref, o_ref, acc):
    step = pl.program_id(2)
    lane = lax.broadcasted_iota(jnp.int32, (1, LANES), 1)
    first = lane < HEAD_DIM
    row = lax.broadcasted_iota(jnp.int32, (TK, TK), 0)
    col = lax.broadcasted_iota(jnp.int32, (TK, TK), 1)
    cum = (row > col).astype(jnp.bfloat16)
    strict = col < row
    diag_mask = jnp.concatenate([strict, strict], axis=0)

    def stacked(u):
        q = q_ref[0, u * TQ:(u + 1) * TQ, :]
        zero = jnp.zeros_like(q)
        return jnp.concatenate([jnp.where(first, q, zero), jnp.where(first, zero, q)], axis=0)

    qs = [stacked(u) for u in range(SB_QT)]
    acc_of = [acc.at[u * 2 * TQ:(u + 1) * 2 * TQ, :] for u in range(SB_QT)]

    def tiles(qm, kbs, carry, masks):
        total = None
        for kb, mask in zip(kbs, masks):
            start = pl.multiple_of(kb * TK, TK)
            o, carry = _sb_tile(qm, k_ref[0, pl.ds(start, TK), :], v_ref[0, pl.ds(start, TK), :],
                                cum, carry, mask)
            total = o if total is None else total + o
        return total, carry

    zero_carry = jnp.zeros((2 * TQ, 1), jnp.float32)

    def first_block(first_step):
        carries = []
        for u in range(SB_QT):
            qt = step * SB_QT + u
            n = min(SB_FIRST, u + 1) if first_step else SB_FIRST
            o, carry = tiles(qs[u], [qt - t for t in range(n)], zero_carry,
                             [diag_mask] + [None] * (n - 1))
            acc_of[u][...] = o
            carries.append(carry)
        return tuple(carries)

    carries = lax.cond(step > 0, lambda: first_block(False), lambda: first_block(True))

    def alive(carry):
        return jnp.max(carry) > SB_DEAD_LOG2

    def sweep(u, n_steps, kbs_of_step, carry):
        def cond(state):
            i, _, live = state
            return jnp.logical_and(i < n_steps, live)

        def body(state):
            i, carry, _ = state
            kbs = kbs_of_step(i)
            o, carry = tiles(qs[u], kbs, carry, [None] * len(kbs))
            acc_of[u][...] += o
            return i + 1, carry, alive(carry)

        return lax.while_loop(cond, body, (jnp.int32(0), carry, alive(carry)))[1]

    any_alive = alive(carries[0])
    for carry in carries[1:]:
        any_alive = jnp.logical_or(any_alive, alive(carry))

    @pl.when(any_alive)
    def _():
        for u in range(SB_QT):
            left = jnp.maximum(step * SB_QT + u - (SB_FIRST - 1), 0)
            rem = lax.rem(left, SB_GROUP)
            carry = sweep(u, rem, lambda i, left=left: [left - 1 - i], carries[u])
            sweep(u, left // SB_GROUP,
                  lambda i, left=left, rem=rem: [left - rem - i * SB_GROUP - 1 - t
                                                 for t in range(SB_GROUP)], carry)

    for u in range(SB_QT):
        out = jnp.where(first, acc_of[u][:TQ, :], acc_of[u][TQ:, :])
        g = g_ref[0, u * TQ:(u + 1) * TQ, :].astype(jnp.float32)
        o_ref[0, u * TQ:(u + 1) * TQ, :] = (out * _silu(g)).astype(o_ref.dtype)


def _sb_attention(proj):
    bsz, s, _ = proj.shape
    rows = SB_QT * TQ
    return pl.pallas_call(
        _sb_kernel,
        out_shape=jax.ShapeDtypeStruct((bsz, s, SB_WIDTH), jnp.bfloat16),
        grid=(bsz, SB_WIDTH // LANES, s // rows),
        in_specs=[pl.BlockSpec((1, rows, LANES), lambda b, h, i: (b, i, CB_SB_Q + h)),
                  pl.BlockSpec((1, s, LANES), lambda b, h, i: (b, 0, CB_SB_K + h)),
                  pl.BlockSpec((1, s, LANES), lambda b, h, i: (b, 0, CB_SB_V + h)),
                  pl.BlockSpec((1, rows, LANES), lambda b, h, i: (b, i, CB_SB_G + h))],
        out_specs=pl.BlockSpec((1, rows, LANES), lambda b, h, i: (b, i, h)),
        scratch_shapes=[pltpu.VMEM((SB_QT * 2 * TQ, LANES), jnp.float32)],
        compiler_params=pltpu.CompilerParams(vmem_limit_bytes=VMEM_LIMIT),
        name="sb_attn",
    )(proj, proj, proj, proj)


def _df_kernel(q_ref, k_ref, v_ref, g_ref, slope_ref, lq1_ref, lk1_ref, lq2_ref, lk2_ref, sg_ref,
               qg_ref, kg_ref, o_ref, m_s, l_s, acc, *, lam_init):
    qi = pl.program_id(2)
    q = q_ref[0]
    lane = lax.broadcasted_iota(jnp.int32, (1, LANES), 1)
    first = lane < HEAD_DIM
    zero = jnp.zeros_like(q)
    qs = jnp.concatenate([jnp.where(first, q, zero), jnp.where(first, zero, q)], axis=0)
    slope = slope_ref[0] * LOG2E
    row = lax.broadcasted_iota(jnp.int32, (DF_TQ, DF_TK), 0)
    col = lax.broadcasted_iota(jnp.int32, (DF_TQ, DF_TK), 1)
    causal = col <= row
    diag_mask = jnp.concatenate([causal, causal], axis=0)

    qk_bound = (jnp.max(jnp.abs(qg_ref[...])) * jnp.max(jnp.abs(kg_ref[...]))
                * (math.sqrt(HEAD_DIM) * LOG2E * DF_BOUND_MARGIN))
    slope_s = jnp.max(slope)

    def load(first_kb, width):
        start = pl.multiple_of(first_kb * DF_TK, DF_TK)
        rel = lax.broadcasted_iota(jnp.int32, (1, width), 1).astype(jnp.float32)
        off = ((first_kb - qi) * DF_TK).astype(jnp.float32)
        bias = jnp.concatenate([slope] * (width // LANES), axis=1) * (rel + off)
        return k_ref[0, pl.ds(start, width), :], v_ref[0, pl.ds(start, width), :], bias

    def last_rel(first_kb, width):
        return ((first_kb - qi) * DF_TK + (width - 1)).astype(jnp.float32)

    def sweep(n_steps, first_kb_of_step, width, tile, needed):
        def cond(state):
            i, live = state
            return jnp.logical_and(i < n_steps, live)

        def body(state):
            i, _ = state
            tile(first_kb_of_step(i), width, None)
            return i + 1, needed(first_kb_of_step(i + 1), width)

        lax.while_loop(cond, body, (jnp.int32(0), needed(first_kb_of_step(jnp.int32(0)), width)))

    def sweep_all(tile, needed):
        tile(qi, DF_TK, diag_mask)
        rem = lax.rem(qi, DF_GROUP)
        sweep(rem, lambda i: qi - 1 - i, DF_TK, tile, needed)
        sweep(qi // DF_GROUP, lambda i: qi - rem - (i + 1) * DF_GROUP, DF_GROUP * DF_TK, tile, needed)

    def fixed_shift():
        rowpos = lax.broadcasted_iota(jnp.int32, (DF_TQ, LANES), 0).astype(jnp.float32)
        shift = qk_bound + slope * jnp.concatenate([rowpos, rowpos], axis=0)
        l_s[...] = jnp.zeros((2 * DF_TQ, LANES), jnp.float32)
        acc[...] = jnp.zeros((2 * DF_TQ, LANES), jnp.float32)

        def tile(first_kb, width, mask):
            reps = width // LANES
            kblk, vblk, bias = load(first_kb, width)
            s = (lax.dot_general(qs, kblk, _NT, preferred_element_type=jnp.float32)
                 + bias - jnp.concatenate([shift] * reps, axis=1))
            if mask is not None:
                s = jnp.where(mask, s, NEG_BIG)
            p = jnp.exp2(s)
            part = p[:, :LANES]
            for r in range(1, reps):
                part = part + p[:, r * LANES:(r + 1) * LANES]
            l_s[...] += part
            acc[...] += jnp.dot(p.astype(jnp.bfloat16), vblk, preferred_element_type=jnp.float32)

        def needed(first_kb, width):
            return slope_s * last_rel(first_kb, width) > DF_DEAD_LOG2

        sweep_all(tile, needed)
        l = jnp.sum(l_s[...], axis=-1, keepdims=True)
        return l[:DF_TQ], l[DF_TQ:]

    def running_max():
        m_s[...] = jnp.full((2 * DF_TQ, LANES), NEG_BIG, jnp.float32)
        l_s[...] = jnp.zeros((2 * DF_TQ, LANES), jnp.float32)
        acc[...] = jnp.zeros((2 * DF_TQ, LANES), jnp.float32)

        def tile(first_kb, width, mask):
            reps = width // LANES
            kblk, vblk, bias = load(first_kb, width)
            s = lax.dot_general(qs, kblk, _NT, preferred_element_type=jnp.float32) + bias
            if mask is not None:
                s = jnp.where(mask, s, NEG_BIG)
            m_old = m_s[...]
            m_new = jnp.maximum(m_old, jnp.max(s, axis=-1, keepdims=True))
            p = jnp.exp2(s - jnp.concatenate([m_new] * reps, axis=1))
            alpha = jnp.exp2(m_old - m_new)
            l_s[...] = alpha * l_s[...] + jnp.sum(p, axis=-1, keepdims=True)
            acc[...] = alpha * acc[...] + jnp.dot(p.astype(jnp.bfloat16), vblk,
                                                  preferred_element_type=jnp.float32)
            m_s[...] = m_new

        def needed(first_kb, width):
            return (qk_bound + slope_s * last_rel(first_kb, width) - jnp.min(m_s[...])
                    > DF_DEAD_LOG2)

        sweep_all(tile, needed)
        return l_s[:DF_TQ, :1], l_s[DF_TQ:, :1]

    l1, l2 = lax.cond(qk_bound < DF_FIXED_SHIFT_MAX, fixed_shift, running_max)

    lam = (jnp.exp(jnp.sum(lq1_ref[...] * lk1_ref[...], keepdims=True))
           - jnp.exp(jnp.sum(lq2_ref[...] * lk2_ref[...], keepdims=True)) + lam_init)
    o = acc[:DF_TQ, :] / l1 - lam * (acc[DF_TQ:, :] / l2)
    ms = jnp.mean(o * o, axis=-1, keepdims=True)
    o = o * lax.rsqrt(ms + EPS) * sg_ref[...] * (1.0 - lam_init)
    g = g_ref[0].astype(jnp.float32)
    o_ref[0] = (o * _silu(g)).astype(o_ref.dtype)


def _df_attention(proj, slopes, lq1, lk1, lq2, lk2, subln_g, q_norm_g, k_norm_g, lam_init):
    bsz, s, _ = proj.shape
    nq = s // DF_TQ
    vec = pl.BlockSpec((1, HEAD_DIM), lambda b, h, i: (0, 0))
    stat = pltpu.VMEM((2 * DF_TQ, LANES), jnp.float32)
    return pl.pallas_call(
        functools.partial(_df_kernel, lam_init=lam_init),
        out_shape=jax.ShapeDtypeStruct((bsz, s, DF_WIDTH), jnp.bfloat16),
        grid=(bsz, DF_HEADS, nq),
        in_specs=[pl.BlockSpec((1, DF_TQ, LANES), lambda b, h, i: (b, i, CB_DF_Q + h)),
                  pl.BlockSpec((1, s, LANES), lambda b, h, i: (b, 0, CB_DF_K + h)),
                  pl.BlockSpec((1, s, LANES), lambda b, h, i: (b, 0, CB_DF_V + h)),
                  pl.BlockSpec((1, DF_TQ, LANES), lambda b, h, i: (b, i, CB_DF_G + h)),
                  pl.BlockSpec((1, 1, LANES), lambda b, h, i: (h, 0, 0)),
                  vec, vec, vec, vec,
                  pl.BlockSpec((1, LANES), lambda b, h, i: (0, 0)),
                  vec, vec],
        out_specs=pl.BlockSpec((1, DF_TQ, LANES), lambda b, h, i: (b, i, h)),
        scratch_shapes=[stat, stat, stat],
        compiler_params=pltpu.CompilerParams(vmem_limit_bytes=VMEM_LIMIT),
        name="df_attn",
    )(proj, proj, proj, proj, slopes, lq1, lk1, lq2, lk2, subln_g, q_norm_g, k_norm_g)


def _out_proj_kernel(x_ref, sb_ref, df_ref, w_ref, gate_ref, o_ref):
    y = (jnp.dot(sb_ref[0], w_ref[:SB_WIDTH, :], preferred_element_type=jnp.float32)
         + jnp.dot(df_ref[0], w_ref[SB_WIDTH:, :], preferred_element_type=jnp.float32))
    o_ref[0] = x_ref[0] + gate_ref[0] * y


def _out_proj(x, sb_o, df_o, w_out_bf16, gate):
    bsz, s, d = x.shape
    tm = ROWS_OUT
    return pl.pallas_call(
        _out_proj_kernel,
        out_shape=jax.ShapeDtypeStruct((bsz, s, d), jnp.float32),
        grid=(bsz, s // tm),
        in_specs=[pl.BlockSpec((1, tm, d), lambda b, i: (b, i, 0)),
                  pl.BlockSpec((1, tm, SB_WIDTH), lambda b, i: (b, i, 0)),
                  pl.BlockSpec((1, tm, DF_WIDTH), lambda b, i: (b, i, 0)),
                  pl.BlockSpec((SB_WIDTH + DF_WIDTH, d), lambda b, i: (0, 0)),
                  pl.BlockSpec((1, 1, d), lambda b, i: (b, 0, 0))],
        out_specs=pl.BlockSpec((1, tm, d), lambda b, i: (b, i, 0)),
        compiler_params=pltpu.CompilerParams(vmem_limit_bytes=VMEM_LIMIT),
        name="out_proj",
    )(x, sb_o, df_o, w_out_bf16, gate.reshape(bsz, 1, d))


def _layer(x, c, layer_idx, norm_g, w_ada, b_ada, w_in, q_norm_g, k_norm_g,
           lambda_q1, lambda_k1, lambda_q2, lambda_k2, subln_g, w_out):
    d = x.shape[-1]
    lam_init = 0.8 - 0.6 * math.exp(-0.3 * layer_idx)
    mod = _adaln_mod(c, w_ada, b_ada)
    shift, scale, gate = mod[:, :d], mod[:, d:2 * d], mod[:, 2 * d:]
    proj = _in_proj(x, norm_g, shift, scale, w_in.astype(jnp.bfloat16), q_norm_g, k_norm_g)
    sb_o = _sb_attention(proj)
    slopes = jnp.asarray([2.0 ** (-8.0 * (h + 1) / DF_HEADS) for h in range(DF_HEADS)], jnp.float32)
    slopes = jnp.broadcast_to(slopes[:, None, None], (DF_HEADS, 1, LANES))
    row = lambda v: v.reshape(1, -1)
    df_o = _df_attention(proj, slopes, row(lambda_q1), row(lambda_k1), row(lambda_q2),
                         row(lambda_k2), row(subln_g), row(q_norm_g), row(k_norm_g), lam_init)
    return _out_proj(x, sb_o, df_o, w_out.astype(jnp.bfloat16), gate)


@jax.jit
def kernel(x, c, norm_g, w_ada, b_ada, w_in, q_norm_g, k_norm_g, lambda_q1, lambda_k1,
           lambda_q2, lambda_k2, subln_g, w_out):
    for l in range(norm_g.shape[0]):
        x = _layer(x, c, l, norm_g[l], w_ada[l], b_ada[l], w_in[l], q_norm_g[l], k_norm_g[l],
                   lambda_q1[l], lambda_k1[l], lambda_q2[l], lambda_k2[l], subln_g[l], w_out[l])
    return x
```

```python
import functools
import math

import jax
import jax.numpy as jnp
from jax import lax
from jax.experimental import pallas as pl
from jax.experimental.pallas import tpu as pltpu

D_MODEL = 1024
SB_HEADS = 8
DF_HEADS = 4
HEAD_DIM = 64
LANES = 128
SB_WIDTH = SB_HEADS * HEAD_DIM
DF_WIDTH = DF_HEADS * 2 * HEAD_DIM
IN_COLS = 4 * SB_WIDTH + 4 * DF_WIDTH
EPS = 1e-6
NEG_BIG = -1e30

CB_SB_Q, CB_SB_K, CB_SB_V, CB_SB_G = 0, 4, 8, 12
CB_DF_Q, CB_DF_K, CB_DF_V, CB_DF_G = 16, 20, 24, 28

ROWS_PROJ = 512
ROWS_OUT = 1024
TQ = 256
TK = 256
DF_TQ = 512
DF_TK = 512
SB_QT = 2
SB_FIRST = 3
SB_GROUP = 4
DF_GROUP = 2
LOG2E = math.log2(math.e)
SB_DEAD_LOG2 = -150.0
DF_DEAD_LOG2 = -150.0
DF_FIXED_SHIFT_MAX = 48.0
DF_BOUND_MARGIN = 1.02
PROJ_CHUNK = 512
VMEM_LIMIT = 48 * 1024 * 1024

_NT = (((1,), (1,)), ((), ()))


def _silu(g):
    return g / (1.0 + jnp.exp(-g))


def _adaln_kernel(ct_ref, w_ref, b_ref, o_ref):
    w = w_ref[...]
    rows = []
    for b in range(o_ref.shape[0]):
        col = ct_ref[:, b:b + 1]
        rows.append(jnp.sum(col * w, axis=0, keepdims=True))
    o_ref[...] = jnp.concatenate(rows, axis=0) + b_ref[...]


def _adaln_mod(c, w_ada, b_ada):
    bsz, d = c.shape
    n = w_ada.shape[1]
    tn = 512
    return pl.pallas_call(
        _adaln_kernel,
        out_shape=jax.ShapeDtypeStruct((bsz, n), jnp.float32),
        grid=(n // tn,),
        in_specs=[pl.BlockSpec((d, bsz), lambda j: (0, 0)),
                  pl.BlockSpec((d, tn), lambda j: (0, j)),
                  pl.BlockSpec((1, tn), lambda j: (0, j))],
        out_specs=pl.BlockSpec((bsz, tn), lambda j: (0, j)),
        name="adaln_mod",
    )(c.T, w_ada, b_ada.reshape(1, n))


def _group_rmsnorm(y, gain):
    outs = []
    lane = lax.broadcasted_iota(jnp.int32, (1, LANES), 1)
    lo = lane < HEAD_DIM
    for j in range(y.shape[1] // LANES):
        blk = y[:, j * LANES:(j + 1) * LANES]
        sq = blk * blk
        s_lo = jnp.sum(jnp.where(lo, sq, 0.0), axis=-1, keepdims=True)
        s_hi = jnp.sum(jnp.where(lo, 0.0, sq), axis=-1, keepdims=True)
        ms = jnp.where(lo, s_lo, s_hi) * (1.0 / HEAD_DIM)
        outs.append(blk * lax.rsqrt(ms + EPS) * gain)
    return jnp.concatenate(outs, axis=1)


def _in_proj_kernel(x_ref, ng_ref, shift_ref, scale_ref, w_ref, qg_ref, kg_ref, o_ref):
    x = x_ref[0]
    ms = jnp.mean(x * x, axis=-1, keepdims=True)
    h = x * lax.rsqrt(ms + EPS) * ng_ref[...]
    h = (h * (1.0 + scale_ref[0]) + shift_ref[0]).astype(jnp.bfloat16)
    inv = LOG2E / math.sqrt(HEAD_DIM)
    for ci in range(IN_COLS // PROJ_CHUNK):
        c0 = ci * PROJ_CHUNK
        y = jnp.dot(h, w_ref[:, c0:c0 + PROJ_CHUNK], preferred_element_type=jnp.float32)
        cb = c0 // LANES
        if cb == CB_SB_Q:
            y = y * inv
        elif cb == CB_DF_Q:
            y = _group_rmsnorm(y, qg_ref[...]) * inv
        elif cb == CB_DF_K:
            y = _group_rmsnorm(y, kg_ref[...])
        o_ref[0, :, c0:c0 + PROJ_CHUNK] = y.astype(o_ref.dtype)


def _in_proj(x, norm_g, shift, scale, w_in_bf16, q_norm_g, k_norm_g):
    bsz, s, d = x.shape
    tm = ROWS_PROJ
    qg = jnp.tile(q_norm_g.reshape(1, HEAD_DIM), (1, 2))
    kg = jnp.tile(k_norm_g.reshape(1, HEAD_DIM), (1, 2))
    return pl.pallas_call(
        _in_proj_kernel,
        out_shape=jax.ShapeDtypeStruct((bsz, s, IN_COLS), jnp.bfloat16),
        grid=(bsz, s // tm),
        in_specs=[pl.BlockSpec((1, tm, d), lambda b, i: (b, i, 0)),
                  pl.BlockSpec((1, d), lambda b, i: (0, 0)),
                  pl.BlockSpec((1, 1, d), lambda b, i: (b, 0, 0)),
                  pl.BlockSpec((1, 1, d), lambda b, i: (b, 0, 0)),
                  pl.BlockSpec((d, IN_COLS), lambda b, i: (0, 0)),
                  pl.BlockSpec((1, LANES), lambda b, i: (0, 0)),
                  pl.BlockSpec((1, LANES), lambda b, i: (0, 0))],
        out_specs=pl.BlockSpec((1, tm, IN_COLS), lambda b, i: (b, i, 0)),
        compiler_params=pltpu.CompilerParams(vmem_limit_bytes=VMEM_LIMIT),
        name="in_proj",
    )(x, norm_g.reshape(1, d), shift.reshape(bsz, 1, d), scale.reshape(bsz, 1, d),
      w_in_bf16, qg, kg)


def _sb_tile(qm, kblk, vblk, cum, carry, mask):
    z = lax.dot_general(qm, kblk, _NT, preferred_element_type=jnp.float32)
    nz = -z
    l1p = jnp.log(1.0 + jnp.exp2(jnp.minimum(z, nz))) * LOG2E
    lg = jnp.minimum(nz, 0.0) - l1p
    if mask is not None:
        lg = jnp.where(mask, lg, 0.0)
    rem = jnp.dot(lg.astype(jnp.bfloat16), cum, preferred_element_type=jnp.float32)
    a = jnp.exp2(lg + z + rem + carry)
    if mask is not None:
        a = jnp.where(mask, a, 0.0)
    o = jnp.dot(a.astype(jnp.bfloat16), vblk, preferred_element_type=jnp.float32)
    return o, carry + jnp.sum(lg, axis=-1, keepdims=True)


def _sb_kernel(q_ref, k_ref, v_ref, g_ref, o_ref, acc):
    step = pl.program_id(2)
    lane = lax.broadcasted_iota(jnp.int32, (1, LANES), 1)
    first = lane < HEAD_DIM
    row = lax.broadcasted_iota(jnp.int32, (TK, TK), 0)
    col = lax.broadcasted_iota(jnp.int32, (TK, TK), 1)
    cum = (row > col).astype(jnp.bfloat16)
    strict = col < row
    diag_mask = jnp.concatenate([strict, strict], axis=0)

    def stacked(u):
        q = q_ref[0, u * TQ:(u + 1) * TQ, :]
        zero = jnp.zeros_like(q)
        return jnp.concatenate([jnp.where(first, q, zero), jnp.where(first, zero, q)], axis=0)

    qs = [stacked(u) for u in range(SB_QT)]
    acc_of = [acc.at[u * 2 * TQ:(u + 1) * 2 * TQ, :] for u in range(SB_QT)]

    def tiles(qm, kbs, carry, masks):
        total = None
        for kb, mask in zip(kbs, masks):
            start = pl.multiple_of(kb * TK, TK)
            o, carry = _sb_tile(qm, k_ref[0, pl.ds(start, TK), :], v_ref[0, pl.ds(start, TK), :],
                                cum, carry, mask)
            total = o if total is None else total + o
        return total, carry

    zero_carry = jnp.zeros((2 * TQ, 1), jnp.float32)

    def first_block(first_step):
        carries = []
        for u in range(SB_QT):
            qt = step * SB_QT + u
            n = min(SB_FIRST, u + 1) if first_step else SB_FIRST
            o, carry = tiles(qs[u], [qt - t for t in range(n)], zero_carry,
                             [diag_mask] + [None] * (n - 1))
            acc_of[u][...] = o
            carries.append(carry)
        return tuple(carries)

    carries = lax.cond(step > 0, lambda: first_block(False), lambda: first_block(True))

    def alive(carry):
        return jnp.max(carry) > SB_DEAD_LOG2

    def sweep(u, n_steps, kbs_of_step, carry):
        def cond(state):
            i, _, live = state
            return jnp.logical_and(i < n_steps, live)

        def body(state):
            i, carry, _ = state
            kbs = kbs_of_step(i)
            o, carry = tiles(qs[u], kbs, carry, [None] * len(kbs))
            acc_of[u][...] += o
            return i + 1, carry, alive(carry)

        return lax.while_loop(cond, body, (jnp.int32(0), carry, alive(carry)))[1]

    any_alive = alive(carries[0])
    for carry in carries[1:]:
        any_alive = jnp.logical_or(any_alive, alive(carry))

    @pl.when(any_alive)
    def _():
        for u in range(SB_QT):
            left = jnp.maximum(step * SB_QT + u - (SB_FIRST - 1), 0)
            rem = lax.rem(left, SB_GROUP)
            carry = sweep(u, rem, lambda i, left=left: [left - 1 - i], carries[u])
            sweep(u, left // SB_GROUP,
                  lambda i, left=left, rem=rem: [left - rem - i * SB_GROUP - 1 - t
                                                 for t in range(SB_GROUP)], carry)

    for u in range(SB_QT):
        out = jnp.where(first, acc_of[u][:TQ, :], acc_of[u][TQ:, :])
        g = g_ref[0, u * TQ:(u + 1) * TQ, :].astype(jnp.float32)
        o_ref[0, u * TQ:(u + 1) * TQ, :] = (out * _silu(g)).astype(o_ref.dtype)


def _sb_attention(proj):
    bsz, s, _ = proj.shape
    rows = SB_QT * TQ
    return pl.pallas_call(
        _sb_kernel,
        out_shape=jax.ShapeDtypeStruct((bsz, s, SB_WIDTH), jnp.bfloat16),
        grid=(bsz, SB_WIDTH // LANES, s // rows),
        in_specs=[pl.BlockSpec((1, rows, LANES), lambda b, h, i: (b, i, CB_SB_Q + h)),
                  pl.BlockSpec((1, s, LANES), lambda b, h, i: (b, 0, CB_SB_K + h)),
                  pl.BlockSpec((1, s, LANES), lambda b, h, i: (b, 0, CB_SB_V + h)),
                  pl.BlockSpec((1, rows, LANES), lambda b, h, i: (b, i, CB_SB_G + h))],
        out_specs=pl.BlockSpec((1, rows, LANES), lambda b, h, i: (b, i, h)),
        scratch_shapes=[pltpu.VMEM((SB_QT * 2 * TQ, LANES), jnp.float32)],
        compiler_params=pltpu.CompilerParams(vmem_limit_bytes=VMEM_LIMIT),
        name="sb_attn",
    )(proj, proj, proj, proj)


def _df_kernel(q_ref, k_ref, v_ref, g_ref, slope_ref, lq1_ref, lk1_ref, lq2_ref, lk2_ref, sg_ref,
               qg_ref, kg_ref, o_ref, m_s, l_s, acc, *, lam_init):
    qi = pl.program_id(2)
    q = q_ref[0]
    lane = lax.broadcasted_iota(jnp.int32, (1, LANES), 1)
    first = lane < HEAD_DIM
    zero = jnp.zeros_like(q)
    qs = jnp.concatenate([jnp.where(first, q, zero), jnp.where(first, zero, q)], axis=0)
    slope = slope_ref[0] * LOG2E
    row = lax.broadcasted_iota(jnp.int32, (DF_TQ, DF_TK), 0)
    col = lax.broadcasted_iota(jnp.int32, (DF_TQ, DF_TK), 1)
    causal = col <= row
    diag_mask = jnp.concatenate([causal, causal], axis=0)

    qk_bound = (jnp.max(jnp.abs(qg_ref[...])) * jnp.max(jnp.abs(kg_ref[...]))
                * (math.sqrt(HEAD_DIM) * LOG2E * DF_BOUND_MARGIN))
    slope_s = jnp.max(slope)

    def load(first_kb, width):
        start = pl.multiple_of(first_kb * DF_TK, DF_TK)
        rel = lax.broadcasted_iota(jnp.int32, (1, width), 1).astype(jnp.float32)
        off = ((first_kb - qi) * DF_TK).astype(jnp.float32)
        bias = jnp.concatenate([slope] * (width // LANES), axis=1) * (rel + off)
        return k_ref[0, pl.ds(start, width), :], v_ref[0, pl.ds(start, width), :], bias

    def last_rel(first_kb, width):
        return ((first_kb - qi) * DF_TK + (width - 1)).astype(jnp.float32)

    def sweep(n_steps, first_kb_of_step, width, tile, needed):
        def cond(state):
            i, live = state
            return jnp.logical_and(i < n_steps, live)

        def body(state):
            i, _ = state
            tile(first_kb_of_step(i), width, None)
            return i + 1, needed(first_kb_of_step(i + 1), width)

        lax.while_loop(cond, body, (jnp.int32(0), needed(first_kb_of_step(jnp.int32(0)), width)))

    def sweep_all(tile, needed):
        tile(qi, DF_TK, diag_mask)
        rem = lax.rem(qi, DF_GROUP)
        sweep(rem, lambda i: qi - 1 - i, DF_TK, tile, needed)
        sweep(qi // DF_GROUP, lambda i: qi - rem - (i + 1) * DF_GROUP, DF_GROUP * DF_TK, tile, needed)

    def fixed_shift():
        rowpos = lax.broadcasted_iota(jnp.int32, (DF_TQ, LANES), 0).astype(jnp.float32)
        shift = qk_bound + slope * jnp.concatenate([rowpos, rowpos], axis=0)
        l_s[...] = jnp.zeros((2 * DF_TQ, LANES), jnp.float32)
        acc[...] = jnp.zeros((2 * DF_TQ, LANES), jnp.float32)

        def tile(first_kb, width, mask):
            reps = width // LANES
            kblk, vblk, bias = load(first_kb, width)
            s = (lax.dot_general(qs, kblk, _NT, preferred_element_type=jnp.float32)
                 + bias - jnp.concatenate([shift] * reps, axis=1))
            if mask is not None:
                s = jnp.where(mask, s, NEG_BIG)
            p = jnp.exp2(s)
            part = p[:, :LANES]
            for r in range(1, reps):
                part = part + p[:, r * LANES:(r + 1) * LANES]
            l_s[...] += part
            acc[...] += jnp.dot(p.astype(jnp.bfloat16), vblk, preferred_element_type=jnp.float32)

        def needed(first_kb, width):
            return slope_s * last_rel(first_kb, width) > DF_DEAD_LOG2

        sweep_all(tile, needed)
        l = jnp.sum(l_s[...], axis=-1, keepdims=True)
        return l[:DF_TQ], l[DF_TQ:]

    def running_max():
        m_s[...] = jnp.full((2 * DF_TQ, LANES), NEG_BIG, jnp.float32)
        l_s[...] = jnp.zeros((2 * DF_TQ, LANES), jnp.float32)
        acc[...] = jnp.zeros((2 * DF_TQ, LANES), jnp.float32)

        def tile(first_kb, width, mask):
            reps = width // LANES
            kblk, vblk, bias = load(first_kb, width)
            s = lax.dot_general(qs, kblk, _NT, preferred_element_type=jnp.float32) + bias
            if mask is not None:
                s = jnp.where(mask, s, NEG_BIG)
            m_old = m_s[...]
            m_new = jnp.maximum(m_old, jnp.max(s, axis=-1, keepdims=True))
            p = jnp.exp2(s - jnp.concatenate([m_new] * reps, axis=1))
            alpha = jnp.exp2(m_old - m_new)
            l_s[...] = alpha * l_s[...] + jnp.sum(p, axis=-1, keepdims=True)
            acc[...] = alpha * acc[...] + jnp.dot(p.astype(jnp.bfloat16), vblk,
                                                  preferred_element_type=jnp.float32)
            m_s[...] = m_new

        def needed(first_kb, width):
            return (qk_bound + slope_s * last_rel(first_kb, width) - jnp.min(m_s[...])
                    > DF_DEAD_LOG2)

        sweep_all(tile, needed)
        return l_s[:DF_TQ, :1], l_s[DF_TQ:, :1]

    l1, l2 = lax.cond(qk_bound < DF_FIXED_SHIFT_MAX, fixed_shift, running_max)

    lam = (jnp.exp(jnp.sum(lq1_ref[...] * lk1_ref[...], keepdims=True))
           - jnp.exp(jnp.sum(lq2_ref[...] * lk2_ref[...], keepdims=True)) + lam_init)
    o = acc[:DF_TQ, :] / l1 - lam * (acc[DF_TQ:, :] / l2)
    ms = jnp.mean(o * o, axis=-1, keepdims=True)
    o = o * lax.rsqrt(ms + EPS) * sg_ref[...] * (1.0 - lam_init)
    g = g_ref[0].astype(jnp.float32)
    o_ref[0] = (o * _silu(g)).astype(o_ref.dtype)


def _df_attention(proj, slopes, lq1, lk1, lq2, lk2, subln_g, q_norm_g, k_norm_g, lam_init):
    bsz, s, _ = proj.shape
    nq = s // DF_TQ
    vec = pl.BlockSpec((1, HEAD_DIM), lambda b, h, i: (0, 0))
    stat = pltpu.VMEM((2 * DF_TQ, LANES), jnp.float32)
    return pl.pallas_call(
        functools.partial(_df_kernel, lam_init=lam_init),
        out_shape=jax.ShapeDtypeStruct((bsz, s, DF_WIDTH), jnp.bfloat16),
        grid=(bsz, DF_HEADS, nq),
        in_specs=[pl.BlockSpec((1, DF_TQ, LANES), lambda b, h, i: (b, i, CB_DF_Q + h)),
                  pl.BlockSpec((1, s, LANES), lambda b, h, i: (b, 0, CB_DF_K + h)),
                  pl.BlockSpec((1, s, LANES), lambda b, h, i: (b, 0, CB_DF_V + h)),
                  pl.BlockSpec((1, DF_TQ, LANES), lambda b, h, i: (b, i, CB_DF_G + h)),
                  pl.BlockSpec((1, 1, LANES), lambda b, h, i: (h, 0, 0)),
                  vec, vec, vec, vec,
                  pl.BlockSpec((1, LANES), lambda b, h, i: (0, 0)),
                  vec, vec],
        out_specs=pl.BlockSpec((1, DF_TQ, LANES), lambda b, h, i: (b, i, h)),
        scratch_shapes=[stat, stat, stat],
        compiler_params=pltpu.CompilerParams(vmem_limit_bytes=VMEM_LIMIT),
        name="df_attn",
    )(proj, proj, proj, proj, slopes, lq1, lk1, lq2, lk2, subln_g, q_norm_g, k_norm_g)


def _out_proj_kernel(x_ref, sb_ref, df_ref, w_ref, gate_ref, o_ref):
    y = (jnp.dot(sb_ref[0], w_ref[:SB_WIDTH, :], preferred_element_type=jnp.float32)
         + jnp.dot(df_ref[0], w_ref[SB_WIDTH:, :], preferred_element_type=jnp.float32))
    o_ref[0] = x_ref[0] + gate_ref[0] * y


def _out_proj(x, sb_o, df_o, w_out_bf16, gate):
    bsz, s, d = x.shape
    tm = ROWS_OUT
    return pl.pallas_call(
        _out_proj_kernel,
        out_shape=jax.ShapeDtypeStruct((bsz, s, d), jnp.float32),
        grid=(bsz, s // tm),
        in_specs=[pl.BlockSpec((1, tm, d), lambda b, i: (b, i, 0)),
                  pl.BlockSpec((1, tm, SB_WIDTH), lambda b, i: (b, i, 0)),
                  pl.BlockSpec((1, tm, DF_WIDTH), lambda b, i: (b, i, 0)),
                  pl.BlockSpec((SB_WIDTH + DF_WIDTH, d), lambda b, i: (0, 0)),
                  pl.BlockSpec((1, 1, d), lambda b, i: (b, 0, 0))],
        out_specs=pl.BlockSpec((1, tm, d), lambda b, i: (b, i, 0)),
        compiler_params=pltpu.CompilerParams(vmem_limit_bytes=VMEM_LIMIT),
        name="out_proj",
    )(x, sb_o, df_o, w_out_bf16, gate.reshape(bsz, 1, d))


def _layer(x, c, layer_idx, norm_g, w_ada, b_ada, w_in, q_norm_g, k_norm_g,
           lambda_q1, lambda_k1, lambda_q2, lambda_k2, subln_g, w_out):
    d = x.shape[-1]
    lam_init = 0.8 - 0.6 * math.exp(-0.3 * layer_idx)
    mod = _adaln_mod(c, w_ada, b_ada)
    shift, scale, gate = mod[:, :d], mod[:, d:2 * d], mod[:, 2 * d:]
    proj = _in_proj(x, norm_g, shift, scale, w_in.astype(jnp.bfloat16), q_norm_g, k_norm_g)
    sb_o = _sb_attention(proj)
    slopes = jnp.asarray([2.0 ** (-8.0 * (h + 1) / DF_HEADS) for h in range(DF_HEADS)], jnp.float32)
    slopes = jnp.broadcast_to(slopes[:, None, None], (DF_HEADS, 1, LANES))
    row = lambda v: v.reshape(1, -1)
    df_o = _df_attention(proj, slopes, row(lambda_q1), row(lambda_k1), row(lambda_q2),
                         row(lambda_k2), row(subln_g), row(q_norm_g), row(k_norm_g), lam_init)
    return _out_proj(x, sb_o, df_o, w_out.astype(jnp.bfloat16), gate)


@jax.jit
def kernel(x, c, norm_g, w_ada, b_ada, w_in, q_norm_g, k_norm_g, lambda_q1, lambda_k1,
           lambda_q2, lambda_k2, subln_g, w_out):
    for l in range(norm_g.shape[0]):
        x = _layer(x, c, l, norm_g[l], w_ada[l], b_ada[l], w_in[l], q_norm_g[l], k_norm_g[l],
                   lambda_q1[l], lambda_k1[l], lambda_q2[l], lambda_k2[l], subln_g[l], w_out[l])
    return x
```

```python
import functools
import math

import jax
import jax.numpy as jnp
from jax import lax
from jax.experimental import pallas as pl
from jax.experimental.pallas import tpu as pltpu

D_MODEL = 1024
SB_HEADS = 8
DF_HEADS = 4
HEAD_DIM = 64
LANES = 128
SB_WIDTH = SB_HEADS * HEAD_DIM
DF_WIDTH = DF_HEADS * 2 * HEAD_DIM
IN_COLS = 4 * SB_WIDTH + 4 * DF_WIDTH
EPS = 1e-6
NEG_BIG = -1e30

CB_SB_Q, CB_SB_K, CB_SB_V, CB_SB_G = 0, 4, 8, 12
CB_DF_Q, CB_DF_K, CB_DF_V, CB_DF_G = 16, 20, 24, 28

ROWS_PROJ = 512
ROWS_OUT = 1024
TQ = 256
TK = 256
DF_TQ = 512
DF_TK = 512
SB_QT = 4
SB_HALF = TK // 2
SB_GROUP = 4
DF_GROUP = 2
LOG2E = math.log2(math.e)
SB_DEAD_LOG2 = -150.0
DF_DEAD_LOG2 = -150.0
DF_FIXED_SHIFT_MAX = 48.0
DF_BOUND_MARGIN = 1.02
PROJ_CHUNK = 512
VMEM_LIMIT = 48 * 1024 * 1024

_NT = (((1,), (1,)), ((), ()))


def _silu(g):
    return g / (1.0 + jnp.exp(-g))


def _adaln_kernel(ct_ref, w_ref, b_ref, o_ref):
    w = w_ref[...]
    rows = []
    for b in range(o_ref.shape[0]):
        col = ct_ref[:, b:b + 1]
        rows.append(jnp.sum(col * w, axis=0, keepdims=True))
    o_ref[...] = jnp.concatenate(rows, axis=0) + b_ref[...]


def _adaln_mod(c, w_ada, b_ada):
    bsz, d = c.shape
    n = w_ada.shape[1]
    tn = 512
    return pl.pallas_call(
        _adaln_kernel,
        out_shape=jax.ShapeDtypeStruct((bsz, n), jnp.float32),
        grid=(n // tn,),
        in_specs=[pl.BlockSpec((d, bsz), lambda j: (0, 0)),
                  pl.BlockSpec((d, tn), lambda j: (0, j)),
                  pl.BlockSpec((1, tn), lambda j: (0, j))],
        out_specs=pl.BlockSpec((bsz, tn), lambda j: (0, j)),
        name="adaln_mod",
    )(c.T, w_ada, b_ada.reshape(1, n))


def _group_rmsnorm(y, gain):
    outs = []
    lane = lax.broadcasted_iota(jnp.int32, (1, LANES), 1)
    lo = lane < HEAD_DIM
    for j in range(y.shape[1] // LANES):
        blk = y[:, j * LANES:(j + 1) * LANES]
        sq = blk * blk
        s_lo = jnp.sum(jnp.where(lo, sq, 0.0), axis=-1, keepdims=True)
        s_hi = jnp.sum(jnp.where(lo, 0.0, sq), axis=-1, keepdims=True)
        ms = jnp.where(lo, s_lo, s_hi) * (1.0 / HEAD_DIM)
        outs.append(blk * lax.rsqrt(ms + EPS) * gain)
    return jnp.concatenate(outs, axis=1)


def _in_proj_kernel(x_ref, ng_ref, shift_ref, scale_ref, w_ref, qg_ref, kg_ref, o_ref):
    x = x_ref[0]
    ms = jnp.mean(x * x, axis=-1, keepdims=True)
    h = x * lax.rsqrt(ms + EPS) * ng_ref[...]
    h = (h * (1.0 + scale_ref[0]) + shift_ref[0]).astype(jnp.bfloat16)
    inv = LOG2E / math.sqrt(HEAD_DIM)
    for ci in range(IN_COLS // PROJ_CHUNK):
        c0 = ci * PROJ_CHUNK
        y = jnp.dot(h, w_ref[:, c0:c0 + PROJ_CHUNK], preferred_element_type=jnp.float32)
        cb = c0 // LANES
        if cb == CB_SB_Q:
            y = y * inv
        elif cb == CB_DF_Q:
            y = _group_rmsnorm(y, qg_ref[...]) * inv
        elif cb == CB_DF_K:
            y = _group_rmsnorm(y, kg_ref[...])
        o_ref[0, :, c0:c0 + PROJ_CHUNK] = y.astype(o_ref.dtype)


def _in_proj(x, norm_g, shift, scale, w_in_bf16, q_norm_g, k_norm_g):
    bsz, s, d = x.shape
    tm = ROWS_PROJ
    qg = jnp.tile(q_norm_g.reshape(1, HEAD_DIM), (1, 2))
    kg = jnp.tile(k_norm_g.reshape(1, HEAD_DIM), (1, 2))
    return pl.pallas_call(
        _in_proj_kernel,
        out_shape=jax.ShapeDtypeStruct((bsz, s, IN_COLS), jnp.bfloat16),
        grid=(bsz, s // tm),
        in_specs=[pl.BlockSpec((1, tm, d), lambda b, i: (b, i, 0)),
                  pl.BlockSpec((1, d), lambda b, i: (0, 0)),
                  pl.BlockSpec((1, 1, d), lambda b, i: (b, 0, 0)),
                  pl.BlockSpec((1, 1, d), lambda b, i: (b, 0, 0)),
                  pl.BlockSpec((d, IN_COLS), lambda b, i: (0, 0)),
                  pl.BlockSpec((1, LANES), lambda b, i: (0, 0)),
                  pl.BlockSpec((1, LANES), lambda b, i: (0, 0))],
        out_specs=pl.BlockSpec((1, tm, IN_COLS), lambda b, i: (b, i, 0)),
        compiler_params=pltpu.CompilerParams(vmem_limit_bytes=VMEM_LIMIT),
        name="in_proj",
    )(x, norm_g.reshape(1, d), shift.reshape(bsz, 1, d), scale.reshape(bsz, 1, d),
      w_in_bf16, qg, kg)


def _sb_tile(qm, kblk, vblk, cum, carry, mask):
    z = lax.dot_general(qm, kblk, _NT, preferred_element_type=jnp.float32)
    nz = -z
    l1p = jnp.log(1.0 + jnp.exp2(jnp.minimum(z, nz))) * LOG2E
    lg = jnp.minimum(nz, 0.0) - l1p
    if mask is not None:
        lg = jnp.where(mask, lg, 0.0)
    rem = jnp.dot(lg.astype(jnp.bfloat16), cum, preferred_element_type=jnp.float32)
    a = jnp.exp2(lg + z + rem + carry)
    if mask is not None:
        a = jnp.where(mask, a, 0.0)
    o = jnp.dot(a.astype(jnp.bfloat16), vblk, preferred_element_type=jnp.float32)
    return o, carry + jnp.sum(lg, axis=-1, keepdims=True)


def _sb_kernel(q_ref, k_ref, v_ref, g_ref, o_ref, acc):
    step = pl.program_id(2)
    lane = lax.broadcasted_iota(jnp.int32, (1, LANES), 1)
    first = lane < HEAD_DIM
    row = lax.broadcasted_iota(jnp.int32, (TK, TK), 0)
    col = lax.broadcasted_iota(jnp.int32, (TK, TK), 1)
    cum = (row > col).astype(jnp.bfloat16)
    strict = col < row
    diag_mask = jnp.concatenate([strict, strict], axis=0)

    def stacked(u):
        q = q_ref[0, u * TQ:(u + 1) * TQ, :]
        zero = jnp.zeros_like(q)
        return jnp.concatenate([jnp.where(first, q, zero), jnp.where(first, zero, q)], axis=0)

    qs = [stacked(u) for u in range(SB_QT)]
    acc_of = [acc.at[u * 2 * TQ:(u + 1) * 2 * TQ, :] for u in range(SB_QT)]

    def tiles(qm, spans, carry):
        total = None
        for start, width, mask in spans:
            start = pl.multiple_of(start, LANES)
            o, carry = _sb_tile(qm, k_ref[0, pl.ds(start, width), :], v_ref[0, pl.ds(start, width), :],
                                cum[:width, :width], carry, mask)
            total = o if total is None else total + o
        return total, carry

    def full(kb):
        return (kb * TK, TK, None)

    zero_carry = jnp.zeros((2 * TQ, 1), jnp.float32)

    def first_block(first_step):
        carries = []
        for u in range(SB_QT):
            qt = step * SB_QT + u
            spans = [(qt * TK, TK, diag_mask)]
            if not first_step or u >= 1:
                spans.append(full(qt - 1))
            if not first_step or u >= 2:
                spans.append(((qt - 2) * TK + SB_HALF, SB_HALF, None))
            o, carry = tiles(qs[u], spans, zero_carry)
            acc_of[u][...] = o
            carries.append(carry)
        return tuple(carries)

    carries = lax.cond(step > 0, lambda: first_block(False), lambda: first_block(True))

    def alive(carry):
        return jnp.max(carry) > SB_DEAD_LOG2

    def sweep(u, n_steps, spans_of_step, carry):
        def cond(state):
            i, _, live = state
            return jnp.logical_and(i < n_steps, live)

        def body(state):
            i, carry, _ = state
            o, carry = tiles(qs[u], spans_of_step(i), carry)
            acc_of[u][...] += o
            return i + 1, carry, alive(carry)

        return lax.while_loop(cond, body, (jnp.int32(0), carry, alive(carry)))[1]

    any_alive = alive(carries[0])
    for carry in carries[1:]:
        any_alive = jnp.logical_or(any_alive, alive(carry))

    @pl.when(any_alive)
    def _():
        for u in range(SB_QT):
            qt = step * SB_QT + u
            left = jnp.maximum(qt - 2, 0)
            has_half = jnp.where(qt >= 2, 1, 0)
            carry = sweep(u, has_half, lambda i, qt=qt: [((qt - 2) * TK, SB_HALF, None)], carries[u])
            rem = lax.rem(left, SB_GROUP)
            carry = sweep(u, rem, lambda i, left=left: [full(left - 1 - i)], carry)
            sweep(u, left // SB_GROUP,
                  lambda i, left=left, rem=rem: [full(left - rem - i * SB_GROUP - 1 - t)
                                                 for t in range(SB_GROUP)], carry)

    for u in range(SB_QT):
        out = jnp.where(first, acc_of[u][:TQ, :], acc_of[u][TQ:, :])
        g = g_ref[0, u * TQ:(u + 1) * TQ, :].astype(jnp.float32)
        o_ref[0, u * TQ:(u + 1) * TQ, :] = (out * _silu(g)).astype(o_ref.dtype)


def _sb_attention(proj):
    bsz, s, _ = proj.shape
    rows = SB_QT * TQ
    return pl.pallas_call(
        _sb_kernel,
        out_shape=jax.ShapeDtypeStruct((bsz, s, SB_WIDTH), jnp.bfloat16),
        grid=(bsz, SB_WIDTH // LANES, s // rows),
        in_specs=[pl.BlockSpec((1, rows, LANES), lambda b, h, i: (b, i, CB_SB_Q + h)),
                  pl.BlockSpec((1, s, LANES), lambda b, h, i: (b, 0, CB_SB_K + h)),
                  pl.BlockSpec((1, s, LANES), lambda b, h, i: (b, 0, CB_SB_V + h)),
                  pl.BlockSpec((1, rows, LANES), lambda b, h, i: (b, i, CB_SB_G + h))],
        out_specs=pl.BlockSpec((1, rows, LANES), lambda b, h, i: (b, i, h)),
        scratch_shapes=[pltpu.VMEM((SB_QT * 2 * TQ, LANES), jnp.float32)],
        compiler_params=pltpu.CompilerParams(vmem_limit_bytes=VMEM_LIMIT),
        name="sb_attn",
    )(proj, proj, proj, proj)


def _df_kernel(q_ref, k_ref, v_ref, g_ref, slope_ref, lq1_ref, lk1_ref, lq2_ref, lk2_ref, sg_ref,
               qg_ref, kg_ref, o_ref, m_s, l_s, acc, *, lam_init):
    qi = pl.program_id(2)
    q = q_ref[0]
    lane = lax.broadcasted_iota(jnp.int32, (1, LANES), 1)
    first = lane < HEAD_DIM
    zero = jnp.zeros_like(q)
    qs = jnp.concatenate([jnp.where(first, q, zero), jnp.where(first, zero, q)], axis=0)
    slope = slope_ref[0] * LOG2E
    row = lax.broadcasted_iota(jnp.int32, (DF_TQ, DF_TK), 0)
    col = lax.broadcasted_iota(jnp.int32, (DF_TQ, DF_TK), 1)
    causal = col <= row
    diag_mask = jnp.concatenate([causal, causal], axis=0)

    qk_bound = (jnp.max(jnp.abs(qg_ref[...])) * jnp.max(jnp.abs(kg_ref[...]))
                * (math.sqrt(HEAD_DIM) * LOG2E * DF_BOUND_MARGIN))
    slope_s = jnp.max(slope)

    def load(first_kb, width):
        start = pl.multiple_of(first_kb * DF_TK, DF_TK)
        rel = lax.broadcasted_iota(jnp.int32, (1, width), 1).astype(jnp.float32)
        off = ((first_kb - qi) * DF_TK).astype(jnp.float32)
        bias = jnp.concatenate([slope] * (width // LANES), axis=1) * (rel + off)
        return k_ref[0, pl.ds(start, width), :], v_ref[0, pl.ds(start, width), :], bias

    def last_rel(first_kb, width):
        return ((first_kb - qi) * DF_TK + (width - 1)).astype(jnp.float32)

    def sweep(n_steps, first_kb_of_step, width, tile, needed):
        def cond(state):
            i, live = state
            return jnp.logical_and(i < n_steps, live)

        def body(state):
            i, _ = state
            tile(first_kb_of_step(i), width, None)
            return i + 1, needed(first_kb_of_step(i + 1), width)

        lax.while_loop(cond, body, (jnp.int32(0), needed(first_kb_of_step(jnp.int32(0)), width)))

    def sweep_all(tile, needed):
        tile(qi, DF_TK, diag_mask)
        rem = lax.rem(qi, DF_GROUP)
        sweep(rem, lambda i: qi - 1 - i, DF_TK, tile, needed)
        sweep(qi // DF_GROUP, lambda i: qi - rem - (i + 1) * DF_GROUP, DF_GROUP * DF_TK, tile, needed)

    def fixed_shift():
        rowpos = lax.broadcasted_iota(jnp.int32, (DF_TQ, LANES), 0).astype(jnp.float32)
        shift = qk_bound + slope * jnp.concatenate([rowpos, rowpos], axis=0)
        l_s[...] = jnp.zeros((2 * DF_TQ, LANES), jnp.float32)
        acc[...] = jnp.zeros((2 * DF_TQ, LANES), jnp.float32)

        def tile(first_kb, width, mask):
            reps = width // LANES
            kblk, vblk, bias = load(first_kb, width)
            s = (lax.dot_general(qs, kblk, _NT, preferred_element_type=jnp.float32)
                 + bias - jnp.concatenate([shift] * reps, axis=1))
            if mask is not None:
                s = jnp.where(mask, s, NEG_BIG)
            p = jnp.exp2(s)
            part = p[:, :LANES]
            for r in range(1, reps):
                part = part + p[:, r * LANES:(r + 1) * LANES]
            l_s[...] += part
            acc[...] += jnp.dot(p.astype(jnp.bfloat16), vblk, preferred_element_type=jnp.float32)

        def needed(first_kb, width):
            return slope_s * last_rel(first_kb, width) > DF_DEAD_LOG2

        sweep_all(tile, needed)
        l = jnp.sum(l_s[...], axis=-1, keepdims=True)
        return l[:DF_TQ], l[DF_TQ:]

    def running_max():
        m_s[...] = jnp.full((2 * DF_TQ, LANES), NEG_BIG, jnp.float32)
        l_s[...] = jnp.zeros((2 * DF_TQ, LANES), jnp.float32)
        acc[...] = jnp.zeros((2 * DF_TQ, LANES), jnp.float32)

        def tile(first_kb, width, mask):
            reps = width // LANES
            kblk, vblk, bias = load(first_kb, width)
            s = lax.dot_general(qs, kblk, _NT, preferred_element_type=jnp.float32) + bias
            if mask is not None:
                s = jnp.where(mask, s, NEG_BIG)
            m_old = m_s[...]
            m_new = jnp.maximum(m_old, jnp.max(s, axis=-1, keepdims=True))
            p = jnp.exp2(s - jnp.concatenate([m_new] * reps, axis=1))
            alpha = jnp.exp2(m_old - m_new)
            l_s[...] = alpha * l_s[...] + jnp.sum(p, axis=-1, keepdims=True)
            acc[...] = alpha * acc[...] + jnp.dot(p.astype(jnp.bfloat16), vblk,
                                                  preferred_element_type=jnp.float32)
            m_s[...] = m_new

        def needed(first_kb, width):
            return (qk_bound + slope_s * last_rel(first_kb, width) - jnp.min(m_s[...])
                    > DF_DEAD_LOG2)

        sweep_all(tile, needed)
        return l_s[:DF_TQ, :1], l_s[DF_TQ:, :1]

    l1, l2 = lax.cond(qk_bound < DF_FIXED_SHIFT_MAX, fixed_shift, running_max)

    lam = (jnp.exp(jnp.sum(lq1_ref[...] * lk1_ref[...], keepdims=True))
           - jnp.exp(jnp.sum(lq2_ref[...] * lk2_ref[...], keepdims=True)) + lam_init)
    o = acc[:DF_TQ, :] / l1 - lam * (acc[DF_TQ:, :] / l2)
    ms = jnp.mean(o * o, axis=-1, keepdims=True)
    o = o * lax.rsqrt(ms + EPS) * sg_ref[...] * (1.0 - lam_init)
    g = g_ref[0].astype(jnp.float32)
    o_ref[0] = (o * _silu(g)).astype(o_ref.dtype)


def _df_attention(proj, slopes, lq1, lk1, lq2, lk2, subln_g, q_norm_g, k_norm_g, lam_init):
    bsz, s, _ = proj.shape
    nq = s // DF_TQ
    vec = pl.BlockSpec((1, HEAD_DIM), lambda b, h, i: (0, 0))
    stat = pltpu.VMEM((2 * DF_TQ, LANES), jnp.float32)
    return pl.pallas_call(
        functools.partial(_df_kernel, lam_init=lam_init),
        out_shape=jax.ShapeDtypeStruct((bsz, s, DF_WIDTH), jnp.bfloat16),
        grid=(bsz, DF_HEADS, nq),
        in_specs=[pl.BlockSpec((1, DF_TQ, LANES), lambda b, h, i: (b, i, CB_DF_Q + h)),
                  pl.BlockSpec((1, s, LANES), lambda b, h, i: (b, 0, CB_DF_K + h)),
                  pl.BlockSpec((1, s, LANES), lambda b, h, i: (b, 0, CB_DF_V + h)),
                  pl.BlockSpec((1, DF_TQ, LANES), lambda b, h, i: (b, i, CB_DF_G + h)),
                  pl.BlockSpec((1, 1, LANES), lambda b, h, i: (h, 0, 0)),
                  vec, vec, vec, vec,
                  pl.BlockSpec((1, LANES), lambda b, h, i: (0, 0)),
                  vec, vec],
        out_specs=pl.BlockSpec((1, DF_TQ, LANES), lambda b, h, i: (b, i, h)),
        scratch_shapes=[stat, stat, stat],
        compiler_params=pltpu.CompilerParams(vmem_limit_bytes=VMEM_LIMIT),
        name="df_attn",
    )(proj, proj, proj, proj, slopes, lq1, lk1, lq2, lk2, subln_g, q_norm_g, k_norm_g)


def _out_proj_kernel(x_ref, sb_ref, df_ref, w_ref, gate_ref, o_ref):
    y = (jnp.dot(sb_ref[0], w_ref[:SB_WIDTH, :], preferred_element_type=jnp.float32)
         + jnp.dot(df_ref[0], w_ref[SB_WIDTH:, :], preferred_element_type=jnp.float32))
    o_ref[0] = x_ref[0] + gate_ref[0] * y


def _out_proj(x, sb_o, df_o, w_out_bf16, gate):
    bsz, s, d = x.shape
    tm = ROWS_OUT
    return pl.pallas_call(
        _out_proj_kernel,
        out_shape=jax.ShapeDtypeStruct((bsz, s, d), jnp.float32),
        grid=(bsz, s // tm),
        in_specs=[pl.BlockSpec((1, tm, d), lambda b, i: (b, i, 0)),
                  pl.BlockSpec((1, tm, SB_WIDTH), lambda b, i: (b, i, 0)),
                  pl.BlockSpec((1, tm, DF_WIDTH), lambda b, i: (b, i, 0)),
                  pl.BlockSpec((SB_WIDTH + DF_WIDTH, d), lambda b, i: (0, 0)),
                  pl.BlockSpec((1, 1, d), lambda b, i: (b, 0, 0))],
        out_specs=pl.BlockSpec((1, tm, d), lambda b, i: (b, i, 0)),
        compiler_params=pltpu.CompilerParams(vmem_limit_bytes=VMEM_LIMIT),
        name="out_proj",
    )(x, sb_o, df_o, w_out_bf16, gate.reshape(bsz, 1, d))


def _layer(x, c, layer_idx, norm_g, w_ada, b_ada, w_in, q_norm_g, k_norm_g,
           lambda_q1, lambda_k1, lambda_q2, lambda_k2, subln_g, w_out):
    d = x.shape[-1]
    lam_init = 0.8 - 0.6 * math.exp(-0.3 * layer_idx)
    mod = _adaln_mod(c, w_ada, b_ada)
    shift, scale, gate = mod[:, :d], mod[:, d:2 * d], mod[:, 2 * d:]
    proj = _in_proj(x, norm_g, shift, scale, w_in.astype(jnp.bfloat16), q_norm_g, k_norm_g)
    sb_o = _sb_attention(proj)
    slopes = jnp.asarray([2.0 ** (-8.0 * (h + 1) / DF_HEADS) for h in range(DF_HEADS)], jnp.float32)
    slopes = jnp.broadcast_to(slopes[:, None, None], (DF_HEADS, 1, LANES))
    row = lambda v: v.reshape(1, -1)
    df_o = _df_attention(proj, slopes, row(lambda_q1), row(lambda_k1), row(lambda_q2),
                         row(lambda_k2), row(subln_g), row(q_norm_g), row(k_norm_g), lam_init)
    return _out_proj(x, sb_o, df_o, w_out.astype(jnp.bfloat16), gate)


@jax.jit
def kernel(x, c, norm_g, w_ada, b_ada, w_in, q_norm_g, k_norm_g, lambda_q1, lambda_k1,
           lambda_q2, lambda_k2, subln_g, w_out):
    for l in range(norm_g.shape[0]):
        x = _layer(x, c, l, norm_g[l], w_ada[l], b_ada[l], w_in[l], q_norm_g[l], k_norm_g[l],
                   lambda_q1[l], lambda_k1[l], lambda_q2[l], lambda_k2[l], subln_g[l], w_out[l])
    return x
```

```python
import functools
import math

import jax
import jax.numpy as jnp
from jax import lax
from jax.experimental import pallas as pl
from jax.experimental.pallas import tpu as pltpu

D_MODEL = 1024
SB_HEADS = 8
DF_HEADS = 4
HEAD_DIM = 64
LANES = 128
SB_WIDTH = SB_HEADS * HEAD_DIM
DF_WIDTH = DF_HEADS * 2 * HEAD_DIM
IN_COLS = 4 * SB_WIDTH + 4 * DF_WIDTH
EPS = 1e-6
NEG_BIG = -1e30

CB_SB_Q, CB_SB_K, CB_SB_V, CB_SB_G = 0, 4, 8, 12
CB_DF_Q, CB_DF_K, CB_DF_V, CB_DF_G = 16, 20, 24, 28

ROWS_PROJ = 512
ROWS_OUT = 1024
TQ = 256
TK = 256
DF_TQ = 512
DF_TK = 512
SB_QT = 4
SB_FIRST = 3
SB_GROUP = 4
LOG2E = math.log2(math.e)
SB_DEAD_LOG2 = -150.0
DF_DEAD_LOG2 = -150.0
DF_FIXED_SHIFT_MAX = 48.0
DF_BOUND_MARGIN = 1.02
PROJ_CHUNK = 512
VMEM_LIMIT = 48 * 1024 * 1024

_NT = (((1,), (1,)), ((), ()))


def _silu(g):
    return g / (1.0 + jnp.exp(-g))


def _adaln_kernel(ct_ref, w_ref, b_ref, o_ref):
    w = w_ref[...]
    rows = []
    for b in range(o_ref.shape[0]):
        col = ct_ref[:, b:b + 1]
        rows.append(jnp.sum(col * w, axis=0, keepdims=True))
    o_ref[...] = jnp.concatenate(rows, axis=0) + b_ref[...]


def _adaln_mod(c, w_ada, b_ada):
    bsz, d = c.shape
    n = w_ada.shape[1]
    tn = 512
    return pl.pallas_call(
        _adaln_kernel,
        out_shape=jax.ShapeDtypeStruct((bsz, n), jnp.float32),
        grid=(n // tn,),
        in_specs=[pl.BlockSpec((d, bsz), lambda j: (0, 0)),
                  pl.BlockSpec((d, tn), lambda j: (0, j)),
                  pl.BlockSpec((1, tn), lambda j: (0, j))],
        out_specs=pl.BlockSpec((bsz, tn), lambda j: (0, j)),
        name="adaln_mod",
    )(c.T, w_ada, b_ada.reshape(1, n))


def _group_rmsnorm(y, gain):
    outs = []
    lane = lax.broadcasted_iota(jnp.int32, (1, LANES), 1)
    lo = lane < HEAD_DIM
    for j in range(y.shape[1] // LANES):
        blk = y[:, j * LANES:(j + 1) * LANES]
        sq = blk * blk
        s_lo = jnp.sum(jnp.where(lo, sq, 0.0), axis=-1, keepdims=True)
        s_hi = jnp.sum(jnp.where(lo, 0.0, sq), axis=-1, keepdims=True)
        ms = jnp.where(lo, s_lo, s_hi) * (1.0 / HEAD_DIM)
        outs.append(blk * lax.rsqrt(ms + EPS) * gain)
    return jnp.concatenate(outs, axis=1)


def _in_proj_kernel(x_ref, ng_ref, shift_ref, scale_ref, w_ref, qg_ref, kg_ref, o_ref):
    x = x_ref[0]
    ms = jnp.mean(x * x, axis=-1, keepdims=True)
    h = x * lax.rsqrt(ms + EPS) * ng_ref[...]
    h = (h * (1.0 + scale_ref[0]) + shift_ref[0]).astype(jnp.bfloat16)
    inv = LOG2E / math.sqrt(HEAD_DIM)
    for ci in range(IN_COLS // PROJ_CHUNK):
        c0 = ci * PROJ_CHUNK
        y = jnp.dot(h, w_ref[:, c0:c0 + PROJ_CHUNK], preferred_element_type=jnp.float32)
        cb = c0 // LANES
        if cb == CB_SB_Q:
            y = y * inv
        elif cb == CB_DF_Q:
            y = _group_rmsnorm(y, qg_ref[...]) * inv
        elif cb == CB_DF_K:
            y = _group_rmsnorm(y, kg_ref[...])
        o_ref[0, :, c0:c0 + PROJ_CHUNK] = y.astype(o_ref.dtype)


def _in_proj(x, norm_g, shift, scale, w_in_bf16, q_norm_g, k_norm_g):
    bsz, s, d = x.shape
    tm = ROWS_PROJ
    qg = jnp.tile(q_norm_g.reshape(1, HEAD_DIM), (1, 2))
    kg = jnp.tile(k_norm_g.reshape(1, HEAD_DIM), (1, 2))
    return pl.pallas_call(
        _in_proj_kernel,
        out_shape=jax.ShapeDtypeStruct((bsz, s, IN_COLS), jnp.bfloat16),
        grid=(bsz, s // tm),
        in_specs=[pl.BlockSpec((1, tm, d), lambda b, i: (b, i, 0)),
                  pl.BlockSpec((1, d), lambda b, i: (0, 0)),
                  pl.BlockSpec((1, 1, d), lambda b, i: (b, 0, 0)),
                  pl.BlockSpec((1, 1, d), lambda b, i: (b, 0, 0)),
                  pl.BlockSpec((d, IN_COLS), lambda b, i: (0, 0)),
                  pl.BlockSpec((1, LANES), lambda b, i: (0, 0)),
                  pl.BlockSpec((1, LANES), lambda b, i: (0, 0))],
        out_specs=pl.BlockSpec((1, tm, IN_COLS), lambda b, i: (b, i, 0)),
        compiler_params=pltpu.CompilerParams(vmem_limit_bytes=VMEM_LIMIT),
        name="in_proj",
    )(x, norm_g.reshape(1, d), shift.reshape(bsz, 1, d), scale.reshape(bsz, 1, d),
      w_in_bf16, qg, kg)


def _sb_tile(qm, kblk, vblk, cum, carry, mask):
    z = lax.dot_general(qm, kblk, _NT, preferred_element_type=jnp.float32)
    nz = -z
    l1p = jnp.log(1.0 + jnp.exp2(jnp.minimum(z, nz))) * LOG2E
    lg = jnp.minimum(nz, 0.0) - l1p
    if mask is not None:
        lg = jnp.where(mask, lg, 0.0)
    rem = jnp.dot(lg.astype(jnp.bfloat16), cum, preferred_element_type=jnp.float32)
    a = jnp.exp2(lg + z + rem + carry)
    if mask is not None:
        a = jnp.where(mask, a, 0.0)
    o = jnp.dot(a.astype(jnp.bfloat16), vblk, preferred_element_type=jnp.float32)
    return o, carry + jnp.sum(lg, axis=-1, keepdims=True)


def _sb_kernel(q_ref, k_ref, v_ref, g_ref, o_ref, acc):
    step = pl.program_id(2)
    lane = lax.broadcasted_iota(jnp.int32, (1, LANES), 1)
    first = lane < HEAD_DIM
    row = lax.broadcasted_iota(jnp.int32, (TK, TK), 0)
    col = lax.broadcasted_iota(jnp.int32, (TK, TK), 1)
    cum = (row > col).astype(jnp.bfloat16)
    strict = col < row
    diag_mask = jnp.concatenate([strict, strict], axis=0)

    def stacked(u):
        q = q_ref[0, u * TQ:(u + 1) * TQ, :]
        zero = jnp.zeros_like(q)
        return jnp.concatenate([jnp.where(first, q, zero), jnp.where(first, zero, q)], axis=0)

    qs = [stacked(u) for u in range(SB_QT)]
    acc_of = [acc.at[u * 2 * TQ:(u + 1) * 2 * TQ, :] for u in range(SB_QT)]

    def tiles(qm, kbs, carry, masks):
        total = None
        for kb, mask in zip(kbs, masks):
            start = pl.multiple_of(kb * TK, TK)
            o, carry = _sb_tile(qm, k_ref[0, pl.ds(start, TK), :], v_ref[0, pl.ds(start, TK), :],
                                cum, carry, mask)
            total = o if total is None else total + o
        return total, carry

    zero_carry = jnp.zeros((2 * TQ, 1), jnp.float32)

    def first_block(first_step):
        carries = []
        for u in range(SB_QT):
            qt = step * SB_QT + u
            n = min(SB_FIRST, u + 1) if first_step else SB_FIRST
            o, carry = tiles(qs[u], [qt - t for t in range(n)], zero_carry,
                             [diag_mask] + [None] * (n - 1))
            acc_of[u][...] = o
            carries.append(carry)
        return tuple(carries)

    carries = lax.cond(step > 0, lambda: first_block(False), lambda: first_block(True))

    def alive(carry):
        return jnp.max(carry) > SB_DEAD_LOG2

    def sweep(u, n_steps, kbs_of_step, carry):
        def cond(state):
            i, _, live = state
            return jnp.logical_and(i < n_steps, live)

        def body(state):
            i, carry, _ = state
            kbs = kbs_of_step(i)
            o, carry = tiles(qs[u], kbs, carry, [None] * len(kbs))
            acc_of[u][...] += o
            return i + 1, carry, alive(carry)

        return lax.while_loop(cond, body, (jnp.int32(0), carry, alive(carry)))[1]

    any_alive = alive(carries[0])
    for carry in carries[1:]:
        any_alive = jnp.logical_or(any_alive, alive(carry))

    @pl.when(any_alive)
    def _():
        for u in range(SB_QT):
            left = jnp.maximum(step * SB_QT + u - (SB_FIRST - 1), 0)
            rem = lax.rem(left, SB_GROUP)
            carry = sweep(u, rem, lambda i, left=left: [left - 1 - i], carries[u])
            sweep(u, left // SB_GROUP,
                  lambda i, left=left, rem=rem: [left - rem - i * SB_GROUP - 1 - t
                                                 for t in range(SB_GROUP)], carry)

    for u in range(SB_QT):
        out = jnp.where(first, acc_of[u][:TQ, :], acc_of[u][TQ:, :])
        g = g_ref[0, u * TQ:(u + 1) * TQ, :].astype(jnp.float32)
        o_ref[0, u * TQ:(u + 1) * TQ, :] = (out * _silu(g)).astype(o_ref.dtype)


def _sb_attention(proj):
    bsz, s, _ = proj.shape
    rows = SB_QT * TQ
    return pl.pallas_call(
        _sb_kernel,
        out_shape=jax.ShapeDtypeStruct((bsz, s, SB_WIDTH), jnp.bfloat16),
        grid=(bsz, SB_WIDTH // LANES, s // rows),
        in_specs=[pl.BlockSpec((1, rows, LANES), lambda b, h, i: (b, i, CB_SB_Q + h)),
                  pl.BlockSpec((1, s, LANES), lambda b, h, i: (b, 0, CB_SB_K + h)),
                  pl.BlockSpec((1, s, LANES), lambda b, h, i: (b, 0, CB_SB_V + h)),
                  pl.BlockSpec((1, rows, LANES), lambda b, h, i: (b, i, CB_SB_G + h))],
        out_specs=pl.BlockSpec((1, rows, LANES), lambda b, h, i: (b, i, h)),
        scratch_shapes=[pltpu.VMEM((SB_QT * 2 * TQ, LANES), jnp.float32)],
        compiler_params=pltpu.CompilerParams(vmem_limit_bytes=VMEM_LIMIT),
        name="sb_attn",
    )(proj, proj, proj, proj)


def _df_kernel(q_ref, k_ref, v_ref, g_ref, slope_ref, lq1_ref, lk1_ref, lq2_ref, lk2_ref, sg_ref,
               qg_ref, kg_ref, o_ref, m_s, l_s, acc, *, lam_init):
    qi = pl.program_id(2)
    q = q_ref[0]
    lane = lax.broadcasted_iota(jnp.int32, (1, LANES), 1)
    first = lane < HEAD_DIM
    zero = jnp.zeros_like(q)
    qs = jnp.concatenate([jnp.where(first, q, zero), jnp.where(first, zero, q)], axis=0)
    slope = slope_ref[0] * LOG2E
    row = lax.broadcasted_iota(jnp.int32, (DF_TQ, DF_TK), 0)
    col = lax.broadcasted_iota(jnp.int32, (DF_TQ, DF_TK), 1)
    causal = col <= row
    diag_mask = jnp.concatenate([causal, causal], axis=0)

    qk_bound = (jnp.max(jnp.abs(qg_ref[...])) * jnp.max(jnp.abs(kg_ref[...]))
                * (math.sqrt(HEAD_DIM) * LOG2E * DF_BOUND_MARGIN))
    slope_s = jnp.max(slope)

    def load(first_kb, width):
        start = pl.multiple_of(first_kb * DF_TK, DF_TK)
        rel = lax.broadcasted_iota(jnp.int32, (1, width), 1).astype(jnp.float32)
        off = ((first_kb - qi) * DF_TK).astype(jnp.float32)
        bias = jnp.concatenate([slope] * (width // LANES), axis=1) * (rel + off)
        return k_ref[0, pl.ds(start, width), :], v_ref[0, pl.ds(start, width), :], bias

    def last_rel(first_kb, width):
        return ((first_kb - qi) * DF_TK + (width - 1)).astype(jnp.float32)

    def sweep(n_steps, first_kb_of_step, width, tile, needed):
        def cond(state):
            i, live = state
            return jnp.logical_and(i < n_steps, live)

        def body(state):
            i, _ = state
            tile(first_kb_of_step(i), width, None)
            return i + 1, needed(first_kb_of_step(i + 1), width)

        lax.while_loop(cond, body, (jnp.int32(0), needed(first_kb_of_step(jnp.int32(0)), width)))

    def sweep_all(tile, needed):
        tile(qi, DF_TK, diag_mask)
        n1 = lax.rem(qi, 2)
        n2 = lax.rem(qi // 2, 2)
        sweep(n1, lambda i: qi - 1 - i, DF_TK, tile, needed)
        sweep(n2, lambda i: qi - n1 - 2 * (i + 1), 2 * DF_TK, tile, needed)
        sweep(qi // 4, lambda i: qi - n1 - 2 * n2 - 4 * (i + 1), 4 * DF_TK, tile, needed)

    def fixed_shift():
        rowpos = lax.broadcasted_iota(jnp.int32, (DF_TQ, LANES), 0).astype(jnp.float32)
        shift = qk_bound + slope * jnp.concatenate([rowpos, rowpos], axis=0)
        l_s[...] = jnp.zeros((2 * DF_TQ, LANES), jnp.float32)
        acc[...] = jnp.zeros((2 * DF_TQ, LANES), jnp.float32)

        def tile(first_kb, width, mask):
            reps = width // LANES
            kblk, vblk, bias = load(first_kb, width)
            s = (lax.dot_general(qs, kblk, _NT, preferred_element_type=jnp.float32)
                 + bias - jnp.concatenate([shift] * reps, axis=1))
            if mask is not None:
                s = jnp.where(mask, s, NEG_BIG)
            p = jnp.exp2(s)
            part = p[:, :LANES]
            for r in range(1, reps):
                part = part + p[:, r * LANES:(r + 1) * LANES]
            l_s[...] += part
            acc[...] += jnp.dot(p.astype(jnp.bfloat16), vblk, preferred_element_type=jnp.float32)

        def needed(first_kb, width):
            return slope_s * last_rel(first_kb, width) > DF_DEAD_LOG2

        sweep_all(tile, needed)
        l = jnp.sum(l_s[...], axis=-1, keepdims=True)
        return l[:DF_TQ], l[DF_TQ:]

    def running_max():
        m_s[...] = jnp.full((2 * DF_TQ, LANES), NEG_BIG, jnp.float32)
        l_s[...] = jnp.zeros((2 * DF_TQ, LANES), jnp.float32)
        acc[...] = jnp.zeros((2 * DF_TQ, LANES), jnp.float32)

        def tile(first_kb, width, mask):
            reps = width // LANES
            kblk, vblk, bias = load(first_kb, width)
            s = lax.dot_general(qs, kblk, _NT, preferred_element_type=jnp.float32) + bias
            if mask is not None:
                s = jnp.where(mask, s, NEG_BIG)
            m_old = m_s[...]
            m_new = jnp.maximum(m_old, jnp.max(s, axis=-1, keepdims=True))
            p = jnp.exp2(s - jnp.concatenate([m_new] * reps, axis=1))
            alpha = jnp.exp2(m_old - m_new)
            l_s[...] = alpha * l_s[...] + jnp.sum(p, axis=-1, keepdims=True)
            acc[...] = alpha * acc[...] + jnp.dot(p.astype(jnp.bfloat16), vblk,
                                                  preferred_element_type=jnp.float32)
            m_s[...] = m_new

        def needed(first_kb, width):
            return (qk_bound + slope_s * last_rel(first_kb, width) - jnp.min(m_s[...])
                    > DF_DEAD_LOG2)

        sweep_all(tile, needed)
        return l_s[:DF_TQ, :1], l_s[DF_TQ:, :1]

    l1, l2 = lax.cond(qk_bound < DF_FIXED_SHIFT_MAX, fixed_shift, running_max)

    lam = (jnp.exp(jnp.sum(lq1_ref[...] * lk1_ref[...], keepdims=True))
           - jnp.exp(jnp.sum(lq2_ref[...] * lk2_ref[...], keepdims=True)) + lam_init)
    o = acc[:DF_TQ, :] / l1 - lam * (acc[DF_TQ:, :] / l2)
    ms = jnp.mean(o * o, axis=-1, keepdims=True)
    o = o * lax.rsqrt(ms + EPS) * sg_ref[...] * (1.0 - lam_init)
    g = g_ref[0].astype(jnp.float32)
    o_ref[0] = (o * _silu(g)).astype(o_ref.dtype)


def _df_attention(proj, slopes, lq1, lk1, lq2, lk2, subln_g, q_norm_g, k_norm_g, lam_init):
    bsz, s, _ = proj.shape
    nq = s // DF_TQ
    vec = pl.BlockSpec((1, HEAD_DIM), lambda b, h, i: (0, 0))
    stat = pltpu.VMEM((2 * DF_TQ, LANES), jnp.float32)
    return pl.pallas_call(
        functools.partial(_df_kernel, lam_init=lam_init),
        out_shape=jax.ShapeDtypeStruct((bsz, s, DF_WIDTH), jnp.bfloat16),
        grid=(bsz, DF_HEADS, nq),
        in_specs=[pl.BlockSpec((1, DF_TQ, LANES), lambda b, h, i: (b, i, CB_DF_Q + h)),
                  pl.BlockSpec((1, s, LANES), lambda b, h, i: (b, 0, CB_DF_K + h)),
                  pl.BlockSpec((1, s, LANES), lambda b, h, i: (b, 0, CB_DF_V + h)),
                  pl.BlockSpec((1, DF_TQ, LANES), lambda b, h, i: (b, i, CB_DF_G + h)),
                  pl.BlockSpec((1, 1, LANES), lambda b, h, i: (h, 0, 0)),
                  vec, vec, vec, vec,
                  pl.BlockSpec((1, LANES), lambda b, h, i: (0, 0)),
                  vec, vec],
        out_specs=pl.BlockSpec((1, DF_TQ, LANES), lambda b, h, i: (b, i, h)),
        scratch_shapes=[stat, stat, stat],
        compiler_params=pltpu.CompilerParams(vmem_limit_bytes=VMEM_LIMIT),
        name="df_attn",
    )(proj, proj, proj, proj, slopes, lq1, lk1, lq2, lk2, subln_g, q_norm_g, k_norm_g)


def _out_proj_kernel(x_ref, sb_ref, df_ref, w_ref, gate_ref, o_ref):
    y = (jnp.dot(sb_ref[0], w_ref[:SB_WIDTH, :], preferred_element_type=jnp.float32)
         + jnp.dot(df_ref[0], w_ref[SB_WIDTH:, :], preferred_element_type=jnp.float32))
    o_ref[0] = x_ref[0] + gate_ref[0] * y


def _out_proj(x, sb_o, df_o, w_out_bf16, gate):
    bsz, s, d = x.shape
    tm = ROWS_OUT
    return pl.pallas_call(
        _out_proj_kernel,
        out_shape=jax.ShapeDtypeStruct((bsz, s, d), jnp.float32),
        grid=(bsz, s // tm),
        in_specs=[pl.BlockSpec((1, tm, d), lambda b, i: (b, i, 0)),
                  pl.BlockSpec((1, tm, SB_WIDTH), lambda b, i: (b, i, 0)),
                  pl.BlockSpec((1, tm, DF_WIDTH), lambda b, i: (b, i, 0)),
                  pl.BlockSpec((SB_WIDTH + DF_WIDTH, d), lambda b, i: (0, 0)),
                  pl.BlockSpec((1, 1, d), lambda b, i: (b, 0, 0))],
        out_specs=pl.BlockSpec((1, tm, d), lambda b, i: (b, i, 0)),
        compiler_params=pltpu.CompilerParams(vmem_limit_bytes=VMEM_LIMIT),
        name="out_proj",
    )(x, sb_o, df_o, w_out_bf16, gate.reshape(bsz, 1, d))


def _layer(x, c, layer_idx, norm_g, w_ada, b_ada, w_in, q_norm_g, k_norm_g,
           lambda_q1, lambda_k1, lambda_q2, lambda_k2, subln_g, w_out):
    d = x.shape[-1]
    lam_init = 0.8 - 0.6 * math.exp(-0.3 * layer_idx)
    mod = _adaln_mod(c, w_ada, b_ada)
    shift, scale, gate = mod[:, :d], mod[:, d:2 * d], mod[:, 2 * d:]
    proj = _in_proj(x, norm_g, shift, scale, w_in.astype(jnp.bfloat16), q_norm_g, k_norm_g)
    sb_o = _sb_attention(proj)
    slopes = jnp.asarray([2.0 ** (-8.0 * (h + 1) / DF_HEADS) for h in range(DF_HEADS)], jnp.float32)
    slopes = jnp.broadcast_to(slopes[:, None, None], (DF_HEADS, 1, LANES))
    row = lambda v: v.reshape(1, -1)
    df_o = _df_attention(proj, slopes, row(lambda_q1), row(lambda_k1), row(lambda_q2),
                         row(lambda_k2), row(subln_g), row(q_norm_g), row(k_norm_g), lam_init)
    return _out_proj(x, sb_o, df_o, w_out.astype(jnp.bfloat16), gate)


@jax.jit
def kernel(x, c, norm_g, w_ada, b_ada, w_in, q_norm_g, k_norm_g, lambda_q1, lambda_k1,
           lambda_q2, lambda_k2, subln_g, w_out):
    for l in range(norm_g.shape[0]):
        x = _layer(x, c, l, norm_g[l], w_ada[l], b_ada[l], w_in[l], q_norm_g[l], k_norm_g[l],
                   lambda_q1[l], lambda_k1[l], lambda_q2[l], lambda_k2[l], subln_g[l], w_out[l])
    return x
```

```python
import functools
import math

import jax
import jax.numpy as jnp
from jax import lax
from jax.experimental import pallas as pl
from jax.experimental.pallas import tpu as pltpu

D_MODEL = 1024
SB_HEADS = 8
DF_HEADS = 4
HEAD_DIM = 64
LANES = 128
SB_WIDTH = SB_HEADS * HEAD_DIM
DF_WIDTH = DF_HEADS * 2 * HEAD_DIM
IN_COLS = 4 * SB_WIDTH + 4 * DF_WIDTH
EPS = 1e-6
NEG_BIG = -1e30

CB_SB_Q, CB_SB_K, CB_SB_V, CB_SB_G = 0, 4, 8, 12
CB_DF_Q, CB_DF_K, CB_DF_V, CB_DF_G = 16, 20, 24, 28

ROWS_PROJ = 1024
ROWS_OUT = 1024
TQ = 256
TK = 256
DF_TQ = 512
DF_TK = 512
SB_QT = 4
SB_FIRST = 3
SB_GROUP = 4
DF_GROUP = 2
LOG2E = math.log2(math.e)
SB_DEAD_LOG2 = -150.0
DF_DEAD_LOG2 = -150.0
DF_FIXED_SHIFT_MAX = 48.0
DF_BOUND_MARGIN = 1.02
PROJ_CHUNK = 512
VMEM_LIMIT = 48 * 1024 * 1024

_NT = (((1,), (1,)), ((), ()))


def _silu(g):
    return g / (1.0 + jnp.exp(-g))


def _adaln_kernel(ct_ref, w_ref, b_ref, o_ref):
    w = w_ref[...]
    rows = []
    for b in range(o_ref.shape[0]):
        col = ct_ref[:, b:b + 1]
        rows.append(jnp.sum(col * w, axis=0, keepdims=True))
    o_ref[...] = jnp.concatenate(rows, axis=0) + b_ref[...]


def _adaln_mod(c, w_ada, b_ada):
    bsz, d = c.shape
    n = w_ada.shape[1]
    tn = 512
    return pl.pallas_call(
        _adaln_kernel,
        out_shape=jax.ShapeDtypeStruct((bsz, n), jnp.float32),
        grid=(n // tn,),
        in_specs=[pl.BlockSpec((d, bsz), lambda j: (0, 0)),
                  pl.BlockSpec((d, tn), lambda j: (0, j)),
                  pl.BlockSpec((1, tn), lambda j: (0, j))],
        out_specs=pl.BlockSpec((bsz, tn), lambda j: (0, j)),
        name="adaln_mod",
    )(c.T, w_ada, b_ada.reshape(1, n))


def _group_rmsnorm(y, gain):
    outs = []
    lane = lax.broadcasted_iota(jnp.int32, (1, LANES), 1)
    lo = lane < HEAD_DIM
    for j in range(y.shape[1] // LANES):
        blk = y[:, j * LANES:(j + 1) * LANES]
        sq = blk * blk
        s_lo = jnp.sum(jnp.where(lo, sq, 0.0), axis=-1, keepdims=True)
        s_hi = jnp.sum(jnp.where(lo, 0.0, sq), axis=-1, keepdims=True)
        ms = jnp.where(lo, s_lo, s_hi) * (1.0 / HEAD_DIM)
        outs.append(blk * lax.rsqrt(ms + EPS) * gain)
    return jnp.concatenate(outs, axis=1)


def _in_proj_kernel(x_ref, ng_ref, shift_ref, scale_ref, w_ref, qg_ref, kg_ref, o_ref):
    x = x_ref[0]
    ms = jnp.mean(x * x, axis=-1, keepdims=True)
    h = x * lax.rsqrt(ms + EPS) * ng_ref[...]
    h = (h * (1.0 + scale_ref[0]) + shift_ref[0]).astype(jnp.bfloat16)
    inv = LOG2E / math.sqrt(HEAD_DIM)
    for ci in range(IN_COLS // PROJ_CHUNK):
        c0 = ci * PROJ_CHUNK
        y = jnp.dot(h, w_ref[:, c0:c0 + PROJ_CHUNK], preferred_element_type=jnp.float32)
        cb = c0 // LANES
        if cb == CB_SB_Q:
            y = y * inv
        elif cb == CB_DF_Q:
            y = _group_rmsnorm(y, qg_ref[...]) * inv
        elif cb == CB_DF_K:
            y = _group_rmsnorm(y, kg_ref[...])
        o_ref[0, :, c0:c0 + PROJ_CHUNK] = y.astype(o_ref.dtype)


def _in_proj(x, norm_g, shift, scale, w_in_bf16, q_norm_g, k_norm_g):
    bsz, s, d = x.shape
    tm = ROWS_PROJ
    qg = jnp.tile(q_norm_g.reshape(1, HEAD_DIM), (1, 2))
    kg = jnp.tile(k_norm_g.reshape(1, HEAD_DIM), (1, 2))
    return pl.pallas_call(
        _in_proj_kernel,
        out_shape=jax.ShapeDtypeStruct((bsz, s, IN_COLS), jnp.bfloat16),
        grid=(bsz, s // tm),
        in_specs=[pl.BlockSpec((1, tm, d), lambda b, i: (b, i, 0)),
                  pl.BlockSpec((1, d), lambda b, i: (0, 0)),
                  pl.BlockSpec((1, 1, d), lambda b, i: (b, 0, 0)),
                  pl.BlockSpec((1, 1, d), lambda b, i: (b, 0, 0)),
                  pl.BlockSpec((d, IN_COLS), lambda b, i: (0, 0)),
                  pl.BlockSpec((1, LANES), lambda b, i: (0, 0)),
                  pl.BlockSpec((1, LANES), lambda b, i: (0, 0))],
        out_specs=pl.BlockSpec((1, tm, IN_COLS), lambda b, i: (b, i, 0)),
        compiler_params=pltpu.CompilerParams(vmem_limit_bytes=VMEM_LIMIT),
        name="in_proj",
    )(x, norm_g.reshape(1, d), shift.reshape(bsz, 1, d), scale.reshape(bsz, 1, d),
      w_in_bf16, qg, kg)


def _sb_tile(qm, kblk, vblk, cum, carry, mask):
    z = lax.dot_general(qm, kblk, _NT, preferred_element_type=jnp.float32)
    nz = -z
    l1p = jnp.log(1.0 + jnp.exp2(jnp.minimum(z, nz))) * LOG2E
    lg = jnp.minimum(nz, 0.0) - l1p
    if mask is not None:
        lg = jnp.where(mask, lg, 0.0)
    rem = jnp.dot(lg.astype(jnp.bfloat16), cum, preferred_element_type=jnp.float32)
    a = jnp.exp2(lg + z + rem + carry)
    if mask is not None:
        a = jnp.where(mask, a, 0.0)
    o = jnp.dot(a.astype(jnp.bfloat16), vblk, preferred_element_type=jnp.float32)
    return o, carry + jnp.sum(lg, axis=-1, keepdims=True)


def _sb_kernel(q_ref, k_ref, v_ref, g_ref, o_ref, acc):
    step = pl.program_id(2)
    lane = lax.broadcasted_iota(jnp.int32, (1, LANES), 1)
    first = lane < HEAD_DIM
    row = lax.broadcasted_iota(jnp.int32, (TK, TK), 0)
    col = lax.broadcasted_iota(jnp.int32, (TK, TK), 1)
    cum = (row > col).astype(jnp.bfloat16)
    strict = col < row
    diag_mask = jnp.concatenate([strict, strict], axis=0)

    def stacked(u):
        q = q_ref[0, u * TQ:(u + 1) * TQ, :]
        zero = jnp.zeros_like(q)
        return jnp.concatenate([jnp.where(first, q, zero), jnp.where(first, zero, q)], axis=0)

    qs = [stacked(u) for u in range(SB_QT)]
    acc_of = [acc.at[u * 2 * TQ:(u + 1) * 2 * TQ, :] for u in range(SB_QT)]

    def tiles(qm, kbs, carry, masks):
        total = None
        for kb, mask in zip(kbs, masks):
            start = pl.multiple_of(kb * TK, TK)
            o, carry = _sb_tile(qm, k_ref[0, pl.ds(start, TK), :], v_ref[0, pl.ds(start, TK), :],
                                cum, carry, mask)
            total = o if total is None else total + o
        return total, carry

    zero_carry = jnp.zeros((2 * TQ, 1), jnp.float32)

    def first_block(first_step):
        carries = []
        for u in range(SB_QT):
            qt = step * SB_QT + u
            n = min(SB_FIRST, u + 1) if first_step else SB_FIRST
            o, carry = tiles(qs[u], [qt - t for t in range(n)], zero_carry,
                             [diag_mask] + [None] * (n - 1))
            acc_of[u][...] = o
            carries.append(carry)
        return tuple(carries)

    carries = lax.cond(step > 0, lambda: first_block(False), lambda: first_block(True))

    def alive(carry):
        return jnp.max(carry) > SB_DEAD_LOG2

    def sweep(u, n_steps, kbs_of_step, carry):
        def cond(state):
            i, _, live = state
            return jnp.logical_and(i < n_steps, live)

        def body(state):
            i, carry, _ = state
            kbs = kbs_of_step(i)
            o, carry = tiles(qs[u], kbs, carry, [None] * len(kbs))
            acc_of[u][...] += o
            return i + 1, carry, alive(carry)

        return lax.while_loop(cond, body, (jnp.int32(0), carry, alive(carry)))[1]

    any_alive = alive(carries[0])
    for carry in carries[1:]:
        any_alive = jnp.logical_or(any_alive, alive(carry))

    @pl.when(any_alive)
    def _():
        for u in range(SB_QT):
            left = jnp.maximum(step * SB_QT + u - (SB_FIRST - 1), 0)
            rem = lax.rem(left, SB_GROUP)
            carry = sweep(u, rem, lambda i, left=left: [left - 1 - i], carries[u])
            sweep(u, left // SB_GROUP,
                  lambda i, left=left, rem=rem: [left - rem - i * SB_GROUP - 1 - t
                                                 for t in range(SB_GROUP)], carry)

    for u in range(SB_QT):
        out = jnp.where(first, acc_of[u][:TQ, :], acc_of[u][TQ:, :])
        g = g_ref[0, u * TQ:(u + 1) * TQ, :].astype(jnp.float32)
        o_ref[0, u * TQ:(u + 1) * TQ, :] = (out * _silu(g)).astype(o_ref.dtype)


def _sb_attention(proj):
    bsz, s, _ = proj.shape
    rows = SB_QT * TQ
    return pl.pallas_call(
        _sb_kernel,
        out_shape=jax.ShapeDtypeStruct((bsz, s, SB_WIDTH), jnp.bfloat16),
        grid=(bsz, SB_WIDTH // LANES, s // rows),
        in_specs=[pl.BlockSpec((1, rows, LANES), lambda b, h, i: (b, i, CB_SB_Q + h)),
                  pl.BlockSpec((1, s, LANES), lambda b, h, i: (b, 0, CB_SB_K + h)),
                  pl.BlockSpec((1, s, LANES), lambda b, h, i: (b, 0, CB_SB_V + h)),
                  pl.BlockSpec((1, rows, LANES), lambda b, h, i: (b, i, CB_SB_G + h))],
        out_specs=pl.BlockSpec((1, rows, LANES), lambda b, h, i: (b, i, h)),
        scratch_shapes=[pltpu.VMEM((SB_QT * 2 * TQ, LANES), jnp.float32)],
        compiler_params=pltpu.CompilerParams(vmem_limit_bytes=VMEM_LIMIT),
        name="sb_attn",
    )(proj, proj, proj, proj)


def _df_kernel(q_ref, k_ref, v_ref, g_ref, slope_ref, lq1_ref, lk1_ref, lq2_ref, lk2_ref, sg_ref,
               qg_ref, kg_ref, o_ref, m_s, l_s, acc, *, lam_init):
    qi = pl.program_id(2)
    q = q_ref[0]
    lane = lax.broadcasted_iota(jnp.int32, (1, LANES), 1)
    first = lane < HEAD_DIM
    zero = jnp.zeros_like(q)
    qs = jnp.concatenate([jnp.where(first, q, zero), jnp.where(first, zero, q)], axis=0)
    slope = slope_ref[0] * LOG2E
    row = lax.broadcasted_iota(jnp.int32, (DF_TQ, DF_TK), 0)
    col = lax.broadcasted_iota(jnp.int32, (DF_TQ, DF_TK), 1)
    causal = col <= row
    diag_mask = jnp.concatenate([causal, causal], axis=0)

    qk_bound = (jnp.max(jnp.abs(qg_ref[...])) * jnp.max(jnp.abs(kg_ref[...]))
                * (math.sqrt(HEAD_DIM) * LOG2E * DF_BOUND_MARGIN))
    slope_s = jnp.max(slope)

    def load(first_kb, width):
        start = pl.multiple_of(first_kb * DF_TK, DF_TK)
        rel = lax.broadcasted_iota(jnp.int32, (1, width), 1).astype(jnp.float32)
        off = ((first_kb - qi) * DF_TK).astype(jnp.float32)
        bias = jnp.concatenate([slope] * (width // LANES), axis=1) * (rel + off)
        return k_ref[0, pl.ds(start, width), :], v_ref[0, pl.ds(start, width), :], bias

    def last_rel(first_kb, width):
        return ((first_kb - qi) * DF_TK + (width - 1)).astype(jnp.float32)

    def sweep(n_steps, first_kb_of_step, width, tile, needed):
        def cond(state):
            i, live = state
            return jnp.logical_and(i < n_steps, live)

        def body(state):
            i, _ = state
            tile(first_kb_of_step(i), width, None)
            return i + 1, needed(first_kb_of_step(i + 1), width)

        lax.while_loop(cond, body, (jnp.int32(0), needed(first_kb_of_step(jnp.int32(0)), width)))

    def sweep_all(tile, needed):
        tile(qi, DF_TK, diag_mask)
        rem = lax.rem(qi, DF_GROUP)
        sweep(rem, lambda i: qi - 1 - i, DF_TK, tile, needed)
        sweep(qi // DF_GROUP, lambda i: qi - rem - (i + 1) * DF_GROUP, DF_GROUP * DF_TK, tile, needed)

    def fixed_shift():
        rowpos = lax.broadcasted_iota(jnp.int32, (DF_TQ, LANES), 0).astype(jnp.float32)
        shift = qk_bound + slope * jnp.concatenate([rowpos, rowpos], axis=0)
        l_s[...] = jnp.zeros((2 * DF_TQ, LANES), jnp.float32)
        acc[...] = jnp.zeros((2 * DF_TQ, LANES), jnp.float32)

        def tile(first_kb, width, mask):
            reps = width // LANES
            kblk, vblk, bias = load(first_kb, width)
            s = (lax.dot_general(qs, kblk, _NT, preferred_element_type=jnp.float32)
                 + bias - jnp.concatenate([shift] * reps, axis=1))
            if mask is not None:
                s = jnp.where(mask, s, NEG_BIG)
            p = jnp.exp2(s)
            part = p[:, :LANES]
            for r in range(1, reps):
                part = part + p[:, r * LANES:(r + 1) * LANES]
            l_s[...] += part
            acc[...] += jnp.dot(p.astype(jnp.bfloat16), vblk, preferred_element_type=jnp.float32)

        def needed(first_kb, width):
            return slope_s * last_rel(first_kb, width) > DF_DEAD_LOG2

        sweep_all(tile, needed)
        l = jnp.sum(l_s[...], axis=-1, keepdims=True)
        return l[:DF_TQ], l[DF_TQ:]

    def running_max():
        m_s[...] = jnp.full((2 * DF_TQ, LANES), NEG_BIG, jnp.float32)
        l_s[...] = jnp.zeros((2 * DF_TQ, LANES), jnp.float32)
        acc[...] = jnp.zeros((2 * DF_TQ, LANES), jnp.float32)

        def tile(first_kb, width, mask):
            reps = width // LANES
            kblk, vblk, bias = load(first_kb, width)
            s = lax.dot_general(qs, kblk, _NT, preferred_element_type=jnp.float32) + bias
            if mask is not None:
                s = jnp.where(mask, s, NEG_BIG)
            m_old = m_s[...]
            m_new = jnp.maximum(m_old, jnp.max(s, axis=-1, keepdims=True))
            p = jnp.exp2(s - jnp.concatenate([m_new] * reps, axis=1))
            alpha = jnp.exp2(m_old - m_new)
            l_s[...] = alpha * l_s[...] + jnp.sum(p, axis=-1, keepdims=True)
            acc[...] = alpha * acc[...] + jnp.dot(p.astype(jnp.bfloat16), vblk,
                                                  preferred_element_type=jnp.float32)
            m_s[...] = m_new

        def needed(first_kb, width):
            return (qk_bound + slope_s * last_rel(first_kb, width) - jnp.min(m_s[...])
                    > DF_DEAD_LOG2)

        sweep_all(tile, needed)
        return l_s[:DF_TQ, :1], l_s[DF_TQ:, :1]

    l1, l2 = lax.cond(qk_bound < DF_FIXED_SHIFT_MAX, fixed_shift, running_max)

    lam = (jnp.exp(jnp.sum(lq1_ref[...] * lk1_ref[...], keepdims=True))
           - jnp.exp(jnp.sum(lq2_ref[...] * lk2_ref[...], keepdims=True)) + lam_init)
    o = acc[:DF_TQ, :] / l1 - lam * (acc[DF_TQ:, :] / l2)
    ms = jnp.mean(o * o, axis=-1, keepdims=True)
    o = o * lax.rsqrt(ms + EPS) * sg_ref[...] * (1.0 - lam_init)
    g = g_ref[0].astype(jnp.float32)
    o_ref[0] = (o * _silu(g)).astype(o_ref.dtype)


def _df_attention(proj, slopes, lq1, lk1, lq2, lk2, subln_g, q_norm_g, k_norm_g, lam_init):
    bsz, s, _ = proj.shape
    nq = s // DF_TQ
    vec = pl.BlockSpec((1, HEAD_DIM), lambda b, h, i: (0, 0))
    stat = pltpu.VMEM((2 * DF_TQ, LANES), jnp.float32)
    return pl.pallas_call(
        functools.partial(_df_kernel, lam_init=lam_init),
        out_shape=jax.ShapeDtypeStruct((bsz, s, DF_WIDTH), jnp.bfloat16),
        grid=(bsz, DF_HEADS, nq),
        in_specs=[pl.BlockSpec((1, DF_TQ, LANES), lambda b, h, i: (b, i, CB_DF_Q + h)),
                  pl.BlockSpec((1, s, LANES), lambda b, h, i: (b, 0, CB_DF_K + h)),
                  pl.BlockSpec((1, s, LANES), lambda b, h, i: (b, 0, CB_DF_V + h)),
                  pl.BlockSpec((1, DF_TQ, LANES), lambda b, h, i: (b, i, CB_DF_G + h)),
                  pl.BlockSpec((1, 1, LANES), lambda b, h, i: (h, 0, 0)),
                  vec, vec, vec, vec,
                  pl.BlockSpec((1, LANES), lambda b, h, i: (0, 0)),
                  vec, vec],
        out_specs=pl.BlockSpec((1, DF_TQ, LANES), lambda b, h, i: (b, i, h)),
        scratch_shapes=[stat, stat, stat],
        compiler_params=pltpu.CompilerParams(vmem_limit_bytes=VMEM_LIMIT),
        name="df_attn",
    )(proj, proj, proj, proj, slopes, lq1, lk1, lq2, lk2, subln_g, q_norm_g, k_norm_g)


def _out_proj_kernel(x_ref, sb_ref, df_ref, w_ref, gate_ref, o_ref):
    y = (jnp.dot(sb_ref[0], w_ref[:SB_WIDTH, :], preferred_element_type=jnp.float32)
         + jnp.dot(df_ref[0], w_ref[SB_WIDTH:, :], preferred_element_type=jnp.float32))
    o_ref[0] = x_ref[0] + gate_ref[0] * y


def _out_proj(x, sb_o, df_o, w_out_bf16, gate):
    bsz, s, d = x.shape
    tm = ROWS_OUT
    return pl.pallas_call(
        _out_proj_kernel,
        out_shape=jax.ShapeDtypeStruct((bsz, s, d), jnp.float32),
        grid=(bsz, s // tm),
        in_specs=[pl.BlockSpec((1, tm, d), lambda b, i: (b, i, 0)),
                  pl.BlockSpec((1, tm, SB_WIDTH), lambda b, i: (b, i, 0)),
                  pl.BlockSpec((1, tm, DF_WIDTH), lambda b, i: (b, i, 0)),
                  pl.BlockSpec((SB_WIDTH + DF_WIDTH, d), lambda b, i: (0, 0)),
                  pl.BlockSpec((1, 1, d), lambda b, i: (b, 0, 0))],
        out_specs=pl.BlockSpec((1, tm, d), lambda b, i: (b, i, 0)),
        compiler_params=pltpu.CompilerParams(vmem_limit_bytes=VMEM_LIMIT),
        name="out_proj",
    )(x, sb_o, df_o, w_out_bf16, gate.reshape(bsz, 1, d))


def _layer(x, c, layer_idx, norm_g, w_ada, b_ada, w_in, q_norm_g, k_norm_g,
           lambda_q1, lambda_k1, lambda_q2, lambda_k2, subln_g, w_out):
    d = x.shape[-1]
    lam_init = 0.8 - 0.6 * math.exp(-0.3 * layer_idx)
    mod = _adaln_mod(c, w_ada, b_ada)
    shift, scale, gate = mod[:, :d], mod[:, d:2 * d], mod[:, 2 * d:]
    proj = _in_proj(x, norm_g, shift, scale, w_in.astype(jnp.bfloat16), q_norm_g, k_norm_g)
    sb_o = _sb_attention(proj)
    slopes = jnp.asarray([2.0 ** (-8.0 * (h + 1) / DF_HEADS) for h in range(DF_HEADS)], jnp.float32)
    slopes = jnp.broadcast_to(slopes[:, None, None], (DF_HEADS, 1, LANES))
    row = lambda v: v.reshape(1, -1)
    df_o = _df_attention(proj, slopes, row(lambda_q1), row(lambda_k1), row(lambda_q2),
                         row(lambda_k2), row(subln_g), row(q_norm_g), row(k_norm_g), lam_init)
    return _out_proj(x, sb_o, df_o, w_out.astype(jnp.bfloat16), gate)


@jax.jit
def kernel(x, c, norm_g, w_ada, b_ada, w_in, q_norm_g, k_norm_g, lambda_q1, lambda_k1,
           lambda_q2, lambda_k2, subln_g, w_out):
    for l in range(norm_g.shape[0]):
        x = _layer(x, c, l, norm_g[l], w_ada[l], b_ada[l], w_in[l], q_norm_g[l], k_norm_g[l],
                   lambda_q1[l], lambda_k1[l], lambda_q2[l], lambda_k2[l], subln_g[l], w_out[l])
    return x
```

```python
import functools
import math

import jax
import jax.numpy as jnp
from jax import lax
from jax.experimental import pallas as pl
from jax.experimental.pallas import tpu as pltpu

D_MODEL = 1024
SB_HEADS = 8
DF_HEADS = 4
HEAD_DIM = 64
LANES = 128
SB_WIDTH = SB_HEADS * HEAD_DIM
DF_WIDTH = DF_HEADS * 2 * HEAD_DIM
IN_COLS = 4 * SB_WIDTH + 4 * DF_WIDTH
EPS = 1e-6
NEG_BIG = -1e30

CB_SB_Q, CB_SB_K, CB_SB_V, CB_SB_G = 0, 4, 8, 12
CB_DF_Q, CB_DF_K, CB_DF_V, CB_DF_G = 16, 20, 24, 28

ROWS_PROJ = 512
ROWS_OUT = 1024
TQ = 256
TK = 256
DF_TQ = 512
DF_TK = 512
SB_QT = 4
SB_FIRST = 3
SB_GROUP = 4
DF_GROUP = 2
LOG2E = math.log2(math.e)
SB_DEAD_LOG2 = -150.0
DF_DEAD_LOG2 = -150.0
DF_FIXED_SHIFT_MAX = 48.0
DF_BOUND_MARGIN = 1.02
PROJ_CHUNK = 512
VMEM_LIMIT = 48 * 1024 * 1024

_NT = (((1,), (1,)), ((), ()))


def _silu(g):
    return g / (1.0 + jnp.exp(-g))


def _adaln_kernel(ct_ref, w_ref, b_ref, o_ref):
    w = w_ref[...]
    rows = []
    for b in range(o_ref.shape[0]):
        col = ct_ref[:, b:b + 1]
        rows.append(jnp.sum(col * w, axis=0, keepdims=True))
    o_ref[...] = jnp.concatenate(rows, axis=0) + b_ref[...]


def _adaln_mod(c, w_ada, b_ada):
    bsz, d = c.shape
    n = w_ada.shape[1]
    tn = 512
    return pl.pallas_call(
        _adaln_kernel,
        out_shape=jax.ShapeDtypeStruct((bsz, n), jnp.float32),
        grid=(n // tn,),
        in_specs=[pl.BlockSpec((d, bsz), lambda j: (0, 0)),
                  pl.BlockSpec((d, tn), lambda j: (0, j)),
                  pl.BlockSpec((1, tn), lambda j: (0, j))],
        out_specs=pl.BlockSpec((bsz, tn), lambda j: (0, j)),
        name="adaln_mod",
    )(c.T, w_ada, b_ada.reshape(1, n))


def _group_rmsnorm(y, gain):
    outs = []
    lane = lax.broadcasted_iota(jnp.int32, (1, LANES), 1)
    lo = lane < HEAD_DIM
    for j in range(y.shape[1] // LANES):
        blk = y[:, j * LANES:(j + 1) * LANES]
        sq = blk * blk
        s_lo = jnp.sum(jnp.where(lo, sq, 0.0), axis=-1, keepdims=True)
        s_hi = jnp.sum(jnp.where(lo, 0.0, sq), axis=-1, keepdims=True)
        ms = jnp.where(lo, s_lo, s_hi) * (1.0 / HEAD_DIM)
        outs.append(blk * lax.rsqrt(ms + EPS) * gain)
    return jnp.concatenate(outs, axis=1)


def _in_proj_kernel(x_ref, ng_ref, shift_ref, scale_ref, w_ref, qg_ref, kg_ref, o_ref):
    x = x_ref[0]
    ms = jnp.mean(x * x, axis=-1, keepdims=True)
    h = x * lax.rsqrt(ms + EPS) * ng_ref[...]
    h = (h * (1.0 + scale_ref[0]) + shift_ref[0]).astype(jnp.bfloat16)
    inv = LOG2E / math.sqrt(HEAD_DIM)
    for ci in range(IN_COLS // PROJ_CHUNK):
        c0 = ci * PROJ_CHUNK
        y = jnp.dot(h, w_ref[:, c0:c0 + PROJ_CHUNK], preferred_element_type=jnp.float32)
        cb = c0 // LANES
        if cb == CB_SB_Q:
            y = y * inv
        elif cb == CB_DF_Q:
            y = _group_rmsnorm(y, qg_ref[...]) * inv
        elif cb == CB_DF_K:
            y = _group_rmsnorm(y, kg_ref[...])
        o_ref[0, :, c0:c0 + PROJ_CHUNK] = y.astype(o_ref.dtype)


def _in_proj(x, norm_g, shift, scale, w_in_bf16, q_norm_g, k_norm_g):
    bsz, s, d = x.shape
    tm = ROWS_PROJ
    qg = jnp.tile(q_norm_g.reshape(1, HEAD_DIM), (1, 2))
    kg = jnp.tile(k_norm_g.reshape(1, HEAD_DIM), (1, 2))
    return pl.pallas_call(
        _in_proj_kernel,
        out_shape=jax.ShapeDtypeStruct((bsz, s, IN_COLS), jnp.bfloat16),
        grid=(bsz, s // tm),
        in_specs=[pl.BlockSpec((1, tm, d), lambda b, i: (b, i, 0)),
                  pl.BlockSpec((1, d), lambda b, i: (0, 0)),
                  pl.BlockSpec((1, 1, d), lambda b, i: (b, 0, 0)),
                  pl.BlockSpec((1, 1, d), lambda b, i: (b, 0, 0)),
                  pl.BlockSpec((d, IN_COLS), lambda b, i: (0, 0)),
                  pl.BlockSpec((1, LANES), lambda b, i: (0, 0)),
                  pl.BlockSpec((1, LANES), lambda b, i: (0, 0))],
        out_specs=pl.BlockSpec((1, tm, IN_COLS), lambda b, i: (b, i, 0)),
        compiler_params=pltpu.CompilerParams(vmem_limit_bytes=VMEM_LIMIT),
        name="in_proj",
    )(x, norm_g.reshape(1, d), shift.reshape(bsz, 1, d), scale.reshape(bsz, 1, d),
      w_in_bf16, qg, kg)


def _sb_tile(qm, kblk, vblk, cum, carry, mask):
    z = lax.dot_general(qm, kblk, _NT, preferred_element_type=jnp.float32)
    nz = -z
    l1p = jnp.log(1.0 + jnp.exp2(jnp.minimum(z, nz))) * LOG2E
    lg = jnp.minimum(nz, 0.0) - l1p
    if mask is not None:
        lg = jnp.where(mask, lg, 0.0)
    rem = jnp.dot(lg.astype(jnp.bfloat16), cum, preferred_element_type=jnp.float32)
    a = jnp.exp2(lg + z + rem + carry)
    if mask is not None:
        a = jnp.where(mask, a, 0.0)
    o = jnp.dot(a.astype(jnp.bfloat16), vblk, preferred_element_type=jnp.float32)
    return o, carry + jnp.sum(lg, axis=-1, keepdims=True)


def _sb_kernel(q_ref, k_ref, v_ref, g_ref, o_ref, acc):
    step = pl.program_id(2)
    lane = lax.broadcasted_iota(jnp.int32, (1, LANES), 1)
    first = lane < HEAD_DIM
    row = lax.broadcasted_iota(jnp.int32, (TK, TK), 0)
    col = lax.broadcasted_iota(jnp.int32, (TK, TK), 1)
    cum = (row > col).astype(jnp.bfloat16)
    strict = col < row
    diag_mask = jnp.concatenate([strict, strict], axis=0)

    def stacked(u):
        q = q_ref[0, u * TQ:(u + 1) * TQ, :]
        zero = jnp.zeros_like(q)
        return jnp.concatenate([jnp.where(first, q, zero), jnp.where(first, zero, q)], axis=0)

    qs = [stacked(u) for u in range(SB_QT)]
    acc_of = [acc.at[u * 2 * TQ:(u + 1) * 2 * TQ, :] for u in range(SB_QT)]

    def tiles(qm, kbs, carry, masks):
        total = None
        for kb, mask in zip(kbs, masks):
            start = pl.multiple_of(kb * TK, TK)
            o, carry = _sb_tile(qm, k_ref[0, pl.ds(start, TK), :], v_ref[0, pl.ds(start, TK), :],
                                cum, carry, mask)
            total = o if total is None else total + o
        return total, carry

    zero_carry = jnp.zeros((2 * TQ, 1), jnp.float32)

    def first_block(first_step):
        carries = []
        for u in range(SB_QT):
            qt = step * SB_QT + u
            n = min(SB_FIRST, u + 1) if first_step else SB_FIRST
            o, carry = tiles(qs[u], [qt - t for t in range(n)], zero_carry,
                             [diag_mask] + [None] * (n - 1))
            acc_of[u][...] = o
            carries.append(carry)
        return tuple(carries)

    carries = lax.cond(step > 0, lambda: first_block(False), lambda: first_block(True))

    def alive(carry):
        return jnp.max(carry) > SB_DEAD_LOG2

    def sweep(u, n_steps, kbs_of_step, carry):
        def cond(state):
            i, _, live = state
            return jnp.logical_and(i < n_steps, live)

        def body(state):
            i, carry, _ = state
            kbs = kbs_of_step(i)
            o, carry = tiles(qs[u], kbs, carry, [None] * len(kbs))
            acc_of[u][...] += o
            return i + 1, carry, alive(carry)

        return lax.while_loop(cond, body, (jnp.int32(0), carry, alive(carry)))[1]

    any_alive = alive(carries[0])
    for carry in carries[1:]:
        any_alive = jnp.logical_or(any_alive, alive(carry))

    @pl.when(any_alive)
    def _():
        for u in range(SB_QT):
            left = jnp.maximum(step * SB_QT + u - (SB_FIRST - 1), 0)
            rem = lax.rem(left, SB_GROUP)
            carry = sweep(u, rem, lambda i, left=left: [left - 1 - i], carries[u])
            sweep(u, left // SB_GROUP,
                  lambda i, left=left, rem=rem: [left - rem - i * SB_GROUP - 1 - t
                                                 for t in range(SB_GROUP)], carry)

    for u in range(SB_QT):
        out = jnp.where(first, acc_of[u][:TQ, :], acc_of[u][TQ:, :])
        g = g_ref[0, u * TQ:(u + 1) * TQ, :].astype(jnp.float32)
        o_ref[0, u * TQ:(u + 1) * TQ, :] = (out * _silu(g)).astype(o_ref.dtype)


def _sb_attention(proj):
    bsz, s, _ = proj.shape
    rows = SB_QT * TQ
    return pl.pallas_call(
        _sb_kernel,
        out_shape=jax.ShapeDtypeStruct((bsz, s, SB_WIDTH), jnp.bfloat16),
        grid=(bsz, SB_WIDTH // LANES, s // rows),
        in_specs=[pl.BlockSpec((1, rows, LANES), lambda b, h, i: (b, i, CB_SB_Q + h)),
                  pl.BlockSpec((1, s, LANES), lambda b, h, i: (b, 0, CB_SB_K + h)),
                  pl.BlockSpec((1, s, LANES), lambda b, h, i: (b, 0, CB_SB_V + h)),
                  pl.BlockSpec((1, rows, LANES), lambda b, h, i: (b, i, CB_SB_G + h))],
        out_specs=pl.BlockSpec((1, rows, LANES), lambda b, h, i: (b, i, h)),
        scratch_shapes=[pltpu.VMEM((SB_QT * 2 * TQ, LANES), jnp.float32)],
        compiler_params=pltpu.CompilerParams(vmem_limit_bytes=VMEM_LIMIT),
        name="sb_attn",
    )(proj, proj, proj, proj)


def _df_kernel(q_ref, k_ref, v_ref, g_ref, slope_ref, lq1_ref, lk1_ref, lq2_ref, lk2_ref, sg_ref,
               qg_ref, kg_ref, o_ref, m_s, l_s, acc, *, lam_init):
    qi = pl.program_id(2)
    q = q_ref[0]
    lane = lax.broadcasted_iota(jnp.int32, (1, LANES), 1)
    first = lane < HEAD_DIM
    zero = jnp.zeros_like(q)
    qs = jnp.concatenate([jnp.where(first, q, zero), jnp.where(first, zero, q)], axis=0)
    slope = slope_ref[0] * LOG2E
    row = lax.broadcasted_iota(jnp.int32, (DF_TQ, DF_TK), 0)
    col = lax.broadcasted_iota(jnp.int32, (DF_TQ, DF_TK), 1)
    causal = col <= row
    diag_mask = jnp.concatenate([causal, causal], axis=0)

    qk_bound = (jnp.max(jnp.abs(qg_ref[...])) * jnp.max(jnp.abs(kg_ref[...]))
                * (math.sqrt(HEAD_DIM) * LOG2E * DF_BOUND_MARGIN))
    slope_s = jnp.max(slope)

    def load(first_kb, width):
        start = pl.multiple_of(first_kb * DF_TK, DF_TK)
        rel = lax.broadcasted_iota(jnp.int32, (1, width), 1).astype(jnp.float32)
        off = ((first_kb - qi) * DF_TK).astype(jnp.float32)
        bias = jnp.concatenate([slope] * (width // LANES), axis=1) * (rel + off)
        return k_ref[0, pl.ds(start, width), :], v_ref[0, pl.ds(start, width), :], bias

    def last_rel(first_kb, width):
        return ((first_kb - qi) * DF_TK + (width - 1)).astype(jnp.float32)

    def sweep(n_steps, first_kb_of_step, width, tile, needed):
        def cond(state):
            i, live = state
            return jnp.logical_and(i < n_steps, live)

        def body(state):
            i, _ = state
            tile(first_kb_of_step(i), width, None)
            return i + 1, needed(first_kb_of_step(i + 1), width)

        lax.while_loop(cond, body, (jnp.int32(0), needed(first_kb_of_step(jnp.int32(0)), width)))

    def sweep_all(tile, needed):
        @pl.when(qi > 0)
        def _():
            tile(qi, DF_TK, diag_mask)
            tile(qi - 1, DF_TK, None)

        @pl.when(qi == 0)
        def _():
            tile(qi, DF_TK, diag_mask)

        left = jnp.maximum(qi - 1, 0)
        rem = lax.rem(left, DF_GROUP)
        sweep(rem, lambda i: left - 1 - i, DF_TK, tile, needed)
        sweep(left // DF_GROUP, lambda i: left - rem - (i + 1) * DF_GROUP, DF_GROUP * DF_TK,
              tile, needed)

    def fixed_shift():
        rowpos = lax.broadcasted_iota(jnp.int32, (DF_TQ, LANES), 0).astype(jnp.float32)
        shift = qk_bound + slope * jnp.concatenate([rowpos, rowpos], axis=0)
        l_s[...] = jnp.zeros((2 * DF_TQ, LANES), jnp.float32)
        acc[...] = jnp.zeros((2 * DF_TQ, LANES), jnp.float32)

        def tile(first_kb, width, mask):
            reps = width // LANES
            kblk, vblk, bias = load(first_kb, width)
            s = (lax.dot_general(qs, kblk, _NT, preferred_element_type=jnp.float32)
                 + bias - jnp.concatenate([shift] * reps, axis=1))
            if mask is not None:
                s = jnp.where(mask, s, NEG_BIG)
            p = jnp.exp2(s)
            part = p[:, :LANES]
            for r in range(1, reps):
                part = part + p[:, r * LANES:(r + 1) * LANES]
            l_s[...] += part
            acc[...] += jnp.dot(p.astype(jnp.bfloat16), vblk, preferred_element_type=jnp.float32)

        def needed(first_kb, width):
            return slope_s * last_rel(first_kb, width) > DF_DEAD_LOG2

        sweep_all(tile, needed)
        l = jnp.sum(l_s[...], axis=-1, keepdims=True)
        return l[:DF_TQ], l[DF_TQ:]

    def running_max():
        m_s[...] = jnp.full((2 * DF_TQ, LANES), NEG_BIG, jnp.float32)
        l_s[...] = jnp.zeros((2 * DF_TQ, LANES), jnp.float32)
        acc[...] = jnp.zeros((2 * DF_TQ, LANES), jnp.float32)

        def tile(first_kb, width, mask):
            reps = width // LANES
            kblk, vblk, bias = load(first_kb, width)
            s = lax.dot_general(qs, kblk, _NT, preferred_element_type=jnp.float32) + bias
            if mask is not None:
                s = jnp.where(mask, s, NEG_BIG)
            m_old = m_s[...]
            m_new = jnp.maximum(m_old, jnp.max(s, axis=-1, keepdims=True))
            p = jnp.exp2(s - jnp.concatenate([m_new] * reps, axis=1))
            alpha = jnp.exp2(m_old - m_new)
            l_s[...] = alpha * l_s[...] + jnp.sum(p, axis=-1, keepdims=True)
            acc[...] = alpha * acc[...] + jnp.dot(p.astype(jnp.bfloat16), vblk,
                                                  preferred_element_type=jnp.float32)
            m_s[...] = m_new

        def needed(first_kb, width):
            return (qk_bound + slope_s * last_rel(first_kb, width) - jnp.min(m_s[...])
                    > DF_DEAD_LOG2)

        sweep_all(tile, needed)
        return l_s[:DF_TQ, :1], l_s[DF_TQ:, :1]

    l1, l2 = lax.cond(qk_bound < DF_FIXED_SHIFT_MAX, fixed_shift, running_max)

    lam = (jnp.exp(jnp.sum(lq1_ref[...] * lk1_ref[...], keepdims=True))
           - jnp.exp(jnp.sum(lq2_ref[...] * lk2_ref[...], keepdims=True)) + lam_init)
    o = acc[:DF_TQ, :] / l1 - lam * (acc[DF_TQ:, :] / l2)
    ms = jnp.mean(o * o, axis=-1, keepdims=True)
    o = o * lax.rsqrt(ms + EPS) * sg_ref[...] * (1.0 - lam_init)
    g = g_ref[0].astype(jnp.float32)
    o_ref[0] = (o * _silu(g)).astype(o_ref.dtype)


def _df_attention(proj, slopes, lq1, lk1, lq2, lk2, subln_g, q_norm_g, k_norm_g, lam_init):
    bsz, s, _ = proj.shape
    nq = s // DF_TQ
    vec = pl.BlockSpec((1, HEAD_DIM), lambda b, h, i: (0, 0))
    stat = pltpu.VMEM((2 * DF_TQ, LANES), jnp.float32)
    return pl.pallas_call(
        functools.partial(_df_kernel, lam_init=lam_init),
        out_shape=jax.ShapeDtypeStruct((bsz, s, DF_WIDTH), jnp.bfloat16),
        grid=(bsz, DF_HEADS, nq),
        in_specs=[pl.BlockSpec((1, DF_TQ, LANES), lambda b, h, i: (b, i, CB_DF_Q + h)),
                  pl.BlockSpec((1, s, LANES), lambda b, h, i: (b, 0, CB_DF_K + h)),
                  pl.BlockSpec((1, s, LANES), lambda b, h, i: (b, 0, CB_DF_V + h)),
                  pl.BlockSpec((1, DF_TQ, LANES), lambda b, h, i: (b, i, CB_DF_G + h)),
                  pl.BlockSpec((1, 1, LANES), lambda b, h, i: (h, 0, 0)),
                  vec, vec, vec, vec,
                  pl.BlockSpec((1, LANES), lambda b, h, i: (0, 0)),
                  vec, vec],
        out_specs=pl.BlockSpec((1, DF_TQ, LANES), lambda b, h, i: (b, i, h)),
        scratch_shapes=[stat, stat, stat],
        compiler_params=pltpu.CompilerParams(vmem_limit_bytes=VMEM_LIMIT),
        name="df_attn",
    )(proj, proj, proj, proj, slopes, lq1, lk1, lq2, lk2, subln_g, q_norm_g, k_norm_g)


def _out_proj_kernel(x_ref, sb_ref, df_ref, w_ref, gate_ref, o_ref):
    y = (jnp.dot(sb_ref[0], w_ref[:SB_WIDTH, :], preferred_element_type=jnp.float32)
         + jnp.dot(df_ref[0], w_ref[SB_WIDTH:, :], preferred_element_type=jnp.float32))
    o_ref[0] = x_ref[0] + gate_ref[0] * y


def _out_proj(x, sb_o, df_o, w_out_bf16, gate):
    bsz, s, d = x.shape
    tm = ROWS_OUT
    return pl.pallas_call(
        _out_proj_kernel,
        out_shape=jax.ShapeDtypeStruct((bsz, s, d), jnp.float32),
        grid=(bsz, s // tm),
        in_specs=[pl.BlockSpec((1, tm, d), lambda b, i: (b, i, 0)),
                  pl.BlockSpec((1, tm, SB_WIDTH), lambda b, i: (b, i, 0)),
                  pl.BlockSpec((1, tm, DF_WIDTH), lambda b, i: (b, i, 0)),
                  pl.BlockSpec((SB_WIDTH + DF_WIDTH, d), lambda b, i: (0, 0)),
                  pl.BlockSpec((1, 1, d), lambda b, i: (b, 0, 0))],
        out_specs=pl.BlockSpec((1, tm, d), lambda b, i: (b, i, 0)),
        compiler_params=pltpu.CompilerParams(vmem_limit_bytes=VMEM_LIMIT),
        name="out_proj",
    )(x, sb_o, df_o, w_out_bf16, gate.reshape(bsz, 1, d))


def _layer(x, c, layer_idx, norm_g, w_ada, b_ada, w_in, q_norm_g, k_norm_g,
           lambda_q1, lambda_k1, lambda_q2, lambda_k2, subln_g, w_out):
    d = x.shape[-1]
    lam_init = 0.8 - 0.6 * math.exp(-0.3 * layer_idx)
    mod = _adaln_mod(c, w_ada, b_ada)
    shift, scale, gate = mod[:, :d], mod[:, d:2 * d], mod[:, 2 * d:]
    proj = _in_proj(x, norm_g, shift, scale, w_in.astype(jnp.bfloat16), q_norm_g, k_norm_g)
    sb_o = _sb_attention(proj)
    slopes = jnp.asarray([2.0 ** (-8.0 * (h + 1) / DF_HEADS) for h in range(DF_HEADS)], jnp.float32)
    slopes = jnp.broadcast_to(slopes[:, None, None], (DF_HEADS, 1, LANES))
    row = lambda v: v.reshape(1, -1)
    df_o = _df_attention(proj, slopes, row(lambda_q1), row(lambda_k1), row(lambda_q2),
                         row(lambda_k2), row(subln_g), row(q_norm_g), row(k_norm_g), lam_init)
    return _out_proj(x, sb_o, df_o, w_out.astype(jnp.bfloat16), gate)


@jax.jit
def kernel(x, c, norm_g, w_ada, b_ada, w_in, q_norm_g, k_norm_g, lambda_q1, lambda_k1,
           lambda_q2, lambda_k2, subln_g, w_out):
    for l in range(norm_g.shape[0]):
        x = _layer(x, c, l, norm_g[l], w_ada[l], b_ada[l], w_in[l], q_norm_g[l], k_norm_g[l],
                   lambda_q1[l], lambda_k1[l], lambda_q2[l], lambda_k2[l], subln_g[l], w_out[l])
    return x
```

```python
import functools
import math

import jax
import jax.numpy as jnp
from jax import lax
from jax.experimental import pallas as pl
from jax.experimental.pallas import tpu as pltpu

D_MODEL = 1024
SB_HEADS = 8
DF_HEADS = 4
HEAD_DIM = 64
LANES = 128
SB_WIDTH = SB_HEADS * HEAD_DIM
DF_WIDTH = DF_HEADS * 2 * HEAD_DIM
IN_COLS = 4 * SB_WIDTH + 4 * DF_WIDTH
EPS = 1e-6
NEG_BIG = -1e30

CB_SB_Q, CB_SB_K, CB_SB_V, CB_SB_G = 0, 4, 8, 12
CB_DF_Q, CB_DF_K, CB_DF_V, CB_DF_G = 16, 20, 24, 28

ROWS_PROJ = 512
ROWS_OUT = 1024
TQ = 256
TK = 256
DF_TQ = 512
DF_TK = 512
SB_QT = 4
SB_FIRST = 3
SB_GROUP = 4
DF_GROUP = 2
LOG2E = math.log2(math.e)
SB_DEAD_LOG2 = -150.0
DF_DEAD_LOG2 = -150.0
DF_FIXED_SHIFT_MAX = 48.0
DF_BOUND_MARGIN = 1.02
PROJ_CHUNK = 512
VMEM_LIMIT = 48 * 1024 * 1024

_NT = (((1,), (1,)), ((), ()))


def _silu(g):
    return g / (1.0 + jnp.exp(-g))


def _adaln_kernel(ct_ref, w_ref, b_ref, o_ref):
    w = w_ref[...]
    rows = []
    for b in range(o_ref.shape[0]):
        col = ct_ref[:, b:b + 1]
        rows.append(jnp.sum(col * w, axis=0, keepdims=True))
    o_ref[...] = jnp.concatenate(rows, axis=0) + b_ref[...]


def _adaln_mod(c, w_ada, b_ada):
    bsz, d = c.shape
    n = w_ada.shape[1]
    tn = 512
    return pl.pallas_call(
        _adaln_kernel,
        out_shape=jax.ShapeDtypeStruct((bsz, n), jnp.float32),
        grid=(n // tn,),
        in_specs=[pl.BlockSpec((d, bsz), lambda j: (0, 0)),
                  pl.BlockSpec((d, tn), lambda j: (0, j)),
                  pl.BlockSpec((1, tn), lambda j: (0, j))],
        out_specs=pl.BlockSpec((bsz, tn), lambda j: (0, j)),
        name="adaln_mod",
    )(c.T, w_ada, b_ada.reshape(1, n))


def _group_rmsnorm(y, gain):
    outs = []
    lane = lax.broadcasted_iota(jnp.int32, (1, LANES), 1)
    lo = lane < HEAD_DIM
    for j in range(y.shape[1] // LANES):
        blk = y[:, j * LANES:(j + 1) * LANES]
        sq = blk * blk
        s_lo = jnp.sum(jnp.where(lo, sq, 0.0), axis=-1, keepdims=True)
        s_hi = jnp.sum(jnp.where(lo, 0.0, sq), axis=-1, keepdims=True)
        ms = jnp.where(lo, s_lo, s_hi) * (1.0 / HEAD_DIM)
        outs.append(blk * lax.rsqrt(ms + EPS) * gain)
    return jnp.concatenate(outs, axis=1)


def _in_proj_kernel(x_ref, ng_ref, shift_ref, scale_ref, w_ref, qg_ref, kg_ref, o_ref):
    x = x_ref[0]
    ms = jnp.mean(x * x, axis=-1, keepdims=True)
    h = x * lax.rsqrt(ms + EPS) * ng_ref[...]
    h = (h * (1.0 + scale_ref[0]) + shift_ref[0]).astype(jnp.bfloat16)
    inv = LOG2E / math.sqrt(HEAD_DIM)
    for ci in range(IN_COLS // PROJ_CHUNK):
        c0 = ci * PROJ_CHUNK
        y = jnp.dot(h, w_ref[:, c0:c0 + PROJ_CHUNK], preferred_element_type=jnp.float32)
        cb = c0 // LANES
        if cb == CB_SB_Q:
            y = y * inv
        elif cb == CB_DF_Q:
            y = _group_rmsnorm(y, qg_ref[...]) * inv
        elif cb == CB_DF_K:
            y = _group_rmsnorm(y, kg_ref[...])
        o_ref[0, :, c0:c0 + PROJ_CHUNK] = y.astype(o_ref.dtype)


def _in_proj(x, norm_g, shift, scale, w_in_bf16, q_norm_g, k_norm_g):
    bsz, s, d = x.shape
    tm = ROWS_PROJ
    qg = jnp.tile(q_norm_g.reshape(1, HEAD_DIM), (1, 2))
    kg = jnp.tile(k_norm_g.reshape(1, HEAD_DIM), (1, 2))
    return pl.pallas_call(
        _in_proj_kernel,
        out_shape=jax.ShapeDtypeStruct((bsz, s, IN_COLS), jnp.bfloat16),
        grid=(bsz, s // tm),
        in_specs=[pl.BlockSpec((1, tm, d), lambda b, i: (b, i, 0)),
                  pl.BlockSpec((1, d), lambda b, i: (0, 0)),
                  pl.BlockSpec((1, 1, d), lambda b, i: (b, 0, 0)),
                  pl.BlockSpec((1, 1, d), lambda b, i: (b, 0, 0)),
                  pl.BlockSpec((d, IN_COLS), lambda b, i: (0, 0)),
                  pl.BlockSpec((1, LANES), lambda b, i: (0, 0)),
                  pl.BlockSpec((1, LANES), lambda b, i: (0, 0))],
        out_specs=pl.BlockSpec((1, tm, IN_COLS), lambda b, i: (b, i, 0)),
        compiler_params=pltpu.CompilerParams(vmem_limit_bytes=VMEM_LIMIT),
        name="in_proj",
    )(x, norm_g.reshape(1, d), shift.reshape(bsz, 1, d), scale.reshape(bsz, 1, d),
      w_in_bf16, qg, kg)


def _sb_tile(qm, kblk, vblk, cum, carry, mask):
    z = lax.dot_general(qm, kblk, _NT, preferred_element_type=jnp.float32)
    nz = -z
    l1p = jnp.log(1.0 + jnp.exp2(jnp.minimum(z, nz))) * LOG2E
    lg = jnp.minimum(nz, 0.0) - l1p
    if mask is not None:
        lg = jnp.where(mask, lg, 0.0)
    rem = jnp.dot(lg.astype(jnp.bfloat16), cum, preferred_element_type=jnp.float32)
    a = jnp.exp2(lg + z + rem + carry)
    if mask is not None:
        a = jnp.where(mask, a, 0.0)
    o = jnp.dot(a.astype(jnp.bfloat16), vblk, preferred_element_type=jnp.float32)
    return o, carry + jnp.sum(lg, axis=-1, keepdims=True)


def _sb_kernel(q_ref, k_ref, v_ref, g_ref, o_ref, acc):
    step = pl.program_id(2)
    lane = lax.broadcasted_iota(jnp.int32, (1, LANES), 1)
    first = lane < HEAD_DIM
    row = lax.broadcasted_iota(jnp.int32, (TK, TK), 0)
    col = lax.broadcasted_iota(jnp.int32, (TK, TK), 1)
    cum = (row > col).astype(jnp.bfloat16)
    strict = col < row
    diag_mask = jnp.concatenate([strict, strict], axis=0)

    def stacked(u):
        q = q_ref[0, u * TQ:(u + 1) * TQ, :]
        zero = jnp.zeros_like(q)
        return jnp.concatenate([jnp.where(first, q, zero), jnp.where(first, zero, q)], axis=0)

    qs = [stacked(u) for u in range(SB_QT)]
    acc_of = [acc.at[u * 2 * TQ:(u + 1) * 2 * TQ, :] for u in range(SB_QT)]

    def tiles(qm, kbs, carry, masks):
        total = None
        for kb, mask in zip(kbs, masks):
            start = pl.multiple_of(kb * TK, TK)
            o, carry = _sb_tile(qm, k_ref[0, pl.ds(start, TK), :], v_ref[0, pl.ds(start, TK), :],
                                cum, carry, mask)
            total = o if total is None else total + o
        return total, carry

    zero_carry = jnp.zeros((2 * TQ, 1), jnp.float32)

    def first_block(first_step):
        carries = []
        for u in range(SB_QT):
            qt = step * SB_QT + u
            n = min(SB_FIRST, u + 1) if first_step else SB_FIRST
            o, carry = tiles(qs[u], [qt - t for t in range(n)], zero_carry,
                             [diag_mask] + [None] * (n - 1))
            acc_of[u][...] = o
            carries.append(carry)
        return tuple(carries)

    carries = lax.cond(step > 0, lambda: first_block(False), lambda: first_block(True))

    def alive(carry):
        return jnp.max(carry) > SB_DEAD_LOG2

    def sweep(u, n_steps, kbs_of_step, carry):
        def cond(state):
            i, _, live = state
            return jnp.logical_and(i < n_steps, live)

        def body(state):
            i, carry, _ = state
            kbs = kbs_of_step(i)
            o, carry = tiles(qs[u], kbs, carry, [None] * len(kbs))
            acc_of[u][...] += o
            return i + 1, carry, alive(carry)

        return lax.while_loop(cond, body, (jnp.int32(0), carry, alive(carry)))[1]

    slowest = carries[0]
    for carry in carries[1:]:
        slowest = jnp.maximum(slowest, carry)

    @pl.when(alive(slowest))
    def _():
        for u in range(SB_QT):
            left = jnp.maximum(step * SB_QT + u - (SB_FIRST - 1), 0)
            rem = lax.rem(left, SB_GROUP)
            carry = sweep(u, rem, lambda i, left=left: [left - 1 - i], carries[u])
            sweep(u, left // SB_GROUP,
                  lambda i, left=left, rem=rem: [left - rem - i * SB_GROUP - 1 - t
                                                 for t in range(SB_GROUP)], carry)

    for u in range(SB_QT):
        out = jnp.where(first, acc_of[u][:TQ, :], acc_of[u][TQ:, :])
        g = g_ref[0, u * TQ:(u + 1) * TQ, :].astype(jnp.float32)
        o_ref[0, u * TQ:(u + 1) * TQ, :] = (out * _silu(g)).astype(o_ref.dtype)


def _sb_attention(proj):
    bsz, s, _ = proj.shape
    rows = SB_QT * TQ
    return pl.pallas_call(
        _sb_kernel,
        out_shape=jax.ShapeDtypeStruct((bsz, s, SB_WIDTH), jnp.bfloat16),
        grid=(bsz, SB_WIDTH // LANES, s // rows),
        in_specs=[pl.BlockSpec((1, rows, LANES), lambda b, h, i: (b, i, CB_SB_Q + h)),
                  pl.BlockSpec((1, s, LANES), lambda b, h, i: (b, 0, CB_SB_K + h)),
                  pl.BlockSpec((1, s, LANES), lambda b, h, i: (b, 0, CB_SB_V + h)),
                  pl.BlockSpec((1, rows, LANES), lambda b, h, i: (b, i, CB_SB_G + h))],
        out_specs=pl.BlockSpec((1, rows, LANES), lambda b, h, i: (b, i, h)),
        scratch_shapes=[pltpu.VMEM((SB_QT * 2 * TQ, LANES), jnp.float32)],
        compiler_params=pltpu.CompilerParams(vmem_limit_bytes=VMEM_LIMIT),
        name="sb_attn",
    )(proj, proj, proj, proj)


def _df_kernel(q_ref, k_ref, v_ref, g_ref, slope_ref, lq1_ref, lk1_ref, lq2_ref, lk2_ref, sg_ref,
               qg_ref, kg_ref, o_ref, m_s, l_s, acc, *, lam_init):
    qi = pl.program_id(2)
    q = q_ref[0]
    lane = lax.broadcasted_iota(jnp.int32, (1, LANES), 1)
    first = lane < HEAD_DIM
    zero = jnp.zeros_like(q)
    qs = jnp.concatenate([jnp.where(first, q, zero), jnp.where(first, zero, q)], axis=0)
    slope = slope_ref[0] * LOG2E
    row = lax.broadcasted_iota(jnp.int32, (DF_TQ, DF_TK), 0)
    col = lax.broadcasted_iota(jnp.int32, (DF_TQ, DF_TK), 1)
    causal = col <= row
    diag_mask = jnp.concatenate([causal, causal], axis=0)

    qk_bound = (jnp.max(jnp.abs(qg_ref[...])) * jnp.max(jnp.abs(kg_ref[...]))
                * (math.sqrt(HEAD_DIM) * LOG2E * DF_BOUND_MARGIN))
    slope_s = jnp.max(slope)

    def load(first_kb, width):
        start = pl.multiple_of(first_kb * DF_TK, DF_TK)
        rel = lax.broadcasted_iota(jnp.int32, (1, width), 1).astype(jnp.float32)
        off = ((first_kb - qi) * DF_TK).astype(jnp.float32)
        bias = jnp.concatenate([slope] * (width // LANES), axis=1) * (rel + off)
        return k_ref[0, pl.ds(start, width), :], v_ref[0, pl.ds(start, width), :], bias

    def last_rel(first_kb, width):
        return ((first_kb - qi) * DF_TK + (width - 1)).astype(jnp.float32)

    def sweep(n_steps, first_kb_of_step, width, tile, needed):
        def cond(state):
            i, live = state
            return jnp.logical_and(i < n_steps, live)

        def body(state):
            i, _ = state
            tile(first_kb_of_step(i), width, None)
            return i + 1, needed(first_kb_of_step(i + 1), width)

        lax.while_loop(cond, body, (jnp.int32(0), needed(first_kb_of_step(jnp.int32(0)), width)))

    def sweep_all(tile, needed):
        tile(qi, DF_TK, diag_mask)
        rem = lax.rem(qi, DF_GROUP)
        sweep(rem, lambda i: qi - 1 - i, DF_TK, tile, needed)
        sweep(qi // DF_GROUP, lambda i: qi - rem - (i + 1) * DF_GROUP, DF_GROUP * DF_TK, tile, needed)

    def fixed_shift():
        rowpos = lax.broadcasted_iota(jnp.int32, (DF_TQ, LANES), 0).astype(jnp.float32)
        shift = qk_bound + slope * jnp.concatenate([rowpos, rowpos], axis=0)
        l_s[...] = jnp.zeros((2 * DF_TQ, LANES), jnp.float32)
        acc[...] = jnp.zeros((2 * DF_TQ, LANES), jnp.float32)

        def tile(first_kb, width, mask):
            reps = width // LANES
            kblk, vblk, bias = load(first_kb, width)
            s = (lax.dot_general(qs, kblk, _NT, preferred_element_type=jnp.float32)
                 + bias - jnp.concatenate([shift] * reps, axis=1))
            if mask is not None:
                s = jnp.where(mask, s, NEG_BIG)
            p = jnp.exp2(s)
            part = p[:, :LANES]
            for r in range(1, reps):
                part = part + p[:, r * LANES:(r + 1) * LANES]
            l_s[...] += part
            acc[...] += jnp.dot(p.astype(jnp.bfloat16), vblk, preferred_element_type=jnp.float32)

        def needed(first_kb, width):
            return slope_s * last_rel(first_kb, width) > DF_DEAD_LOG2

        sweep_all(tile, needed)
        l = jnp.sum(l_s[...], axis=-1, keepdims=True)
        return l[:DF_TQ], l[DF_TQ:]

    def running_max():
        m_s[...] = jnp.full((2 * DF_TQ, LANES), NEG_BIG, jnp.float32)
        l_s[...] = jnp.zeros((2 * DF_TQ, LANES), jnp.float32)
        acc[...] = jnp.zeros((2 * DF_TQ, LANES), jnp.float32)

        def tile(first_kb, width, mask):
            reps = width // LANES
            kblk, vblk, bias = load(first_kb, width)
            s = lax.dot_general(qs, kblk, _NT, preferred_element_type=jnp.float32) + bias
            if mask is not None:
                s = jnp.where(mask, s, NEG_BIG)
            m_old = m_s[...]
            m_new = jnp.maximum(m_old, jnp.max(s, axis=-1, keepdims=True))
            p = jnp.exp2(s - jnp.concatenate([m_new] * reps, axis=1))
            alpha = jnp.exp2(m_old - m_new)
            l_s[...] = alpha * l_s[...] + jnp.sum(p, axis=-1, keepdims=True)
            acc[...] = alpha * acc[...] + jnp.dot(p.astype(jnp.bfloat16), vblk,
                                                  preferred_element_type=jnp.float32)
            m_s[...] = m_new

        def needed(first_kb, width):
            return (qk_bound + slope_s * last_rel(first_kb, width) - jnp.min(m_s[...])
                    > DF_DEAD_LOG2)

        sweep_all(tile, needed)
        return l_s[:DF_TQ, :1], l_s[DF_TQ:, :1]

    l1, l2 = lax.cond(qk_bound < DF_FIXED_SHIFT_MAX, fixed_shift, running_max)

    lam = (jnp.exp(jnp.sum(lq1_ref[...] * lk1_ref[...], keepdims=True))
           - jnp.exp(jnp.sum(lq2_ref[...] * lk2_ref[...], keepdims=True)) + lam_init)
    o = acc[:DF_TQ, :] / l1 - lam * (acc[DF_TQ:, :] / l2)
    ms = jnp.mean(o * o, axis=-1, keepdims=True)
    o = o * lax.rsqrt(ms + EPS) * sg_ref[...] * (1.0 - lam_init)
    g = g_ref[0].astype(jnp.float32)
    o_ref[0] = (o * _silu(g)).astype(o_ref.dtype)


def _df_attention(proj, slopes, lq1, lk1, lq2, lk2, subln_g, q_norm_g, k_norm_g, lam_init):
    bsz, s, _ = proj.shape
    nq = s // DF_TQ
    vec = pl.BlockSpec((1, HEAD_DIM), lambda b, h, i: (0, 0))
    stat = pltpu.VMEM((2 * DF_TQ, LANES), jnp.float32)
    return pl.pallas_call(
        functools.partial(_df_kernel, lam_init=lam_init),
        out_shape=jax.ShapeDtypeStruct((bsz, s, DF_WIDTH), jnp.bfloat16),
        grid=(bsz, DF_HEADS, nq),
        in_specs=[pl.BlockSpec((1, DF_TQ, LANES), lambda b, h, i: (b, i, CB_DF_Q + h)),
                  pl.BlockSpec((1, s, LANES), lambda b, h, i: (b, 0, CB_DF_K + h)),
                  pl.BlockSpec((1, s, LANES), lambda b, h, i: (b, 0, CB_DF_V + h)),
                  pl.BlockSpec((1, DF_TQ, LANES), lambda b, h, i: (b, i, CB_DF_G + h)),
                  pl.BlockSpec((1, 1, LANES), lambda b, h, i: (h, 0, 0)),
                  vec, vec, vec, vec,
                  pl.BlockSpec((1, LANES), lambda b, h, i: (0, 0)),
                  vec, vec],
        out_specs=pl.BlockSpec((1, DF_TQ, LANES), lambda b, h, i: (b, i, h)),
        scratch_shapes=[stat, stat, stat],
        compiler_params=pltpu.CompilerParams(vmem_limit_bytes=VMEM_LIMIT),
        name="df_attn",
    )(proj, proj, proj, proj, slopes, lq1, lk1, lq2, lk2, subln_g, q_norm_g, k_norm_g)


def _out_proj_kernel(x_ref, sb_ref, df_ref, w_ref, gate_ref, o_ref):
    y = (jnp.dot(sb_ref[0], w_ref[:SB_WIDTH, :], preferred_element_type=jnp.float32)
         + jnp.dot(df_ref[0], w_ref[SB_WIDTH:, :], preferred_element_type=jnp.float32))
    o_ref[0] = x_ref[0] + gate_ref[0] * y


def _out_proj(x, sb_o, df_o, w_out_bf16, gate):
    bsz, s, d = x.shape
    tm = ROWS_OUT
    return pl.pallas_call(
        _out_proj_kernel,
        out_shape=jax.ShapeDtypeStruct((bsz, s, d), jnp.float32),
        grid=(bsz, s // tm),
        in_specs=[pl.BlockSpec((1, tm, d), lambda b, i: (b, i, 0)),
                  pl.BlockSpec((1, tm, SB_WIDTH), lambda b, i: (b, i, 0)),
                  pl.BlockSpec((1, tm, DF_WIDTH), lambda b, i: (b, i, 0)),
                  pl.BlockSpec((SB_WIDTH + DF_WIDTH, d), lambda b, i: (0, 0)),
                  pl.BlockSpec((1, 1, d), lambda b, i: (b, 0, 0))],
        out_specs=pl.BlockSpec((1, tm, d), lambda b, i: (b, i, 0)),
        compiler_params=pltpu.CompilerParams(vmem_limit_bytes=VMEM_LIMIT),
        name="out_proj",
    )(x, sb_o, df_o, w_out_bf16, gate.reshape(bsz, 1, d))


def _layer(x, c, layer_idx, norm_g, w_ada, b_ada, w_in, q_norm_g, k_norm_g,
           lambda_q1, lambda_k1, lambda_q2, lambda_k2, subln_g, w_out):
    d = x.shape[-1]
    lam_init = 0.8 - 0.6 * math.exp(-0.3 * layer_idx)
    mod = _adaln_mod(c, w_ada, b_ada)
    shift, scale, gate = mod[:, :d], mod[:, d:2 * d], mod[:, 2 * d:]
    proj = _in_proj(x, norm_g, shift, scale, w_in.astype(jnp.bfloat16), q_norm_g, k_norm_g)
    sb_o = _sb_attention(proj)
    slopes = jnp.asarray([2.0 ** (-8.0 * (h + 1) / DF_HEADS) for h in range(DF_HEADS)], jnp.float32)
    slopes = jnp.broadcast_to(slopes[:, None, None], (DF_HEADS, 1, LANES))
    row = lambda v: v.reshape(1, -1)
    df_o = _df_attention(proj, slopes, row(lambda_q1), row(lambda_k1), row(lambda_q2),
                         row(lambda_k2), row(subln_g), row(q_norm_g), row(k_norm_g), lam_init)
    return _out_proj(x, sb_o, df_o, w_out.astype(jnp.bfloat16), gate)


@jax.jit
def kernel(x, c, norm_g, w_ada, b_ada, w_in, q_norm_g, k_norm_g, lambda_q1, lambda_k1,
           lambda_q2, lambda_k2, subln_g, w_out):
    for l in range(norm_g.shape[0]):
        x = _layer(x, c, l, norm_g[l], w_ada[l], b_ada[l], w_in[l], q_norm_g[l], k_norm_g[l],
                   lambda_q1[l], lambda_k1[l], lambda_q2[l], lambda_k2[l], subln_g[l], w_out[l])
    return x
```

```python
import functools
import math

import jax
import jax.numpy as jnp
from jax import lax
from jax.experimental import pallas as pl
from jax.experimental.pallas import tpu as pltpu

D_MODEL = 1024
SB_HEADS = 8
DF_HEADS = 4
HEAD_DIM = 64
LANES = 128
SB_WIDTH = SB_HEADS * HEAD_DIM
DF_WIDTH = DF_HEADS * 2 * HEAD_DIM
IN_COLS = 4 * SB_WIDTH + 4 * DF_WIDTH
EPS = 1e-6
NEG_BIG = -1e30

CB_SB_Q, CB_SB_K, CB_SB_V, CB_SB_G = 0, 4, 8, 12
CB_DF_Q, CB_DF_K, CB_DF_V, CB_DF_G = 16, 20, 24, 28

ROWS_PROJ = 512
ROWS_OUT = 1024
TQ = 256
TK = 256
DF_TQ = 512
DF_TK = 512
SB_QT = 4
SB_FIRST = 3
SB_GROUP = 4
DF_GROUP = 2
LOG2E = math.log2(math.e)
SB_DEAD_LOG2 = -150.0
DF_DEAD_LOG2 = -150.0
DF_FIXED_SHIFT_MAX = 48.0
DF_BOUND_MARGIN = 1.02
PROJ_CHUNK = 512
VMEM_LIMIT = 48 * 1024 * 1024

_NT = (((1,), (1,)), ((), ()))


def _silu(g):
    return g / (1.0 + jnp.exp(-g))


def _adaln_kernel(ct_ref, w_ref, b_ref, o_ref):
    w = w_ref[...]
    rows = []
    for b in range(o_ref.shape[0]):
        col = ct_ref[:, b:b + 1]
        rows.append(jnp.sum(col * w, axis=0, keepdims=True))
    o_ref[...] = jnp.concatenate(rows, axis=0) + b_ref[...]


def _adaln_mod(c, w_ada, b_ada):
    bsz, d = c.shape
    n = w_ada.shape[1]
    tn = 512
    return pl.pallas_call(
        _adaln_kernel,
        out_shape=jax.ShapeDtypeStruct((bsz, n), jnp.float32),
        grid=(n // tn,),
        in_specs=[pl.BlockSpec((d, bsz), lambda j: (0, 0)),
                  pl.BlockSpec((d, tn), lambda j: (0, j)),
                  pl.BlockSpec((1, tn), lambda j: (0, j))],
        out_specs=pl.BlockSpec((bsz, tn), lambda j: (0, j)),
        name="adaln_mod",
    )(c.T, w_ada, b_ada.reshape(1, n))


def _group_rmsnorm(y, gain):
    outs = []
    lane = lax.broadcasted_iota(jnp.int32, (1, LANES), 1)
    lo = lane < HEAD_DIM
    for j in range(y.shape[1] // LANES):
        blk = y[:, j * LANES:(j + 1) * LANES]
        sq = blk * blk
        s_lo = jnp.sum(jnp.where(lo, sq, 0.0), axis=-1, keepdims=True)
        s_hi = jnp.sum(jnp.where(lo, 0.0, sq), axis=-1, keepdims=True)
        ms = jnp.where(lo, s_lo, s_hi) * (1.0 / HEAD_DIM)
        outs.append(blk * lax.rsqrt(ms + EPS) * gain)
    return jnp.concatenate(outs, axis=1)


def _in_proj_kernel(x_ref, ng_ref, shift_ref, scale_ref, w_ref, qg_ref, kg_ref, o_ref):
    x = x_ref[0]
    ms = jnp.mean(x * x, axis=-1, keepdims=True)
    h = x * lax.rsqrt(ms + EPS) * ng_ref[...]
    h = (h * (1.0 + scale_ref[0]) + shift_ref[0]).astype(jnp.bfloat16)
    inv = LOG2E / math.sqrt(HEAD_DIM)
    for ci in range(IN_COLS // PROJ_CHUNK):
        c0 = ci * PROJ_CHUNK
        y = jnp.dot(h, w_ref[:, c0:c0 + PROJ_CHUNK], preferred_element_type=jnp.float32)
        cb = c0 // LANES
        if cb == CB_SB_Q:
            y = y * inv
        elif cb == CB_DF_Q:
            y = _group_rmsnorm(y, qg_ref[...]) * inv
        elif cb == CB_DF_K:
            y = _group_rmsnorm(y, kg_ref[...])
        o_ref[0, :, c0:c0 + PROJ_CHUNK] = y.astype(o_ref.dtype)


def _in_proj(x, norm_g, shift, scale, w_in_bf16, q_norm_g, k_norm_g):
    bsz, s, d = x.shape
    tm = ROWS_PROJ
    qg = jnp.tile(q_norm_g.reshape(1, HEAD_DIM), (1, 2))
    kg = jnp.tile(k_norm_g.reshape(1, HEAD_DIM), (1, 2))
    return pl.pallas_call(
        _in_proj_kernel,
        out_shape=jax.ShapeDtypeStruct((bsz, s, IN_COLS), jnp.bfloat16),
        grid=(bsz, s // tm),
        in_specs=[pl.BlockSpec((1, tm, d), lambda b, i: (b, i, 0)),
                  pl.BlockSpec((1, d), lambda b, i: (0, 0)),
                  pl.BlockSpec((1, 1, d), lambda b, i: (b, 0, 0)),
                  pl.BlockSpec((1, 1, d), lambda b, i: (b, 0, 0)),
                  pl.BlockSpec((d, IN_COLS), lambda b, i: (0, 0)),
                  pl.BlockSpec((1, LANES), lambda b, i: (0, 0)),
                  pl.BlockSpec((1, LANES), lambda b, i: (0, 0))],
        out_specs=pl.BlockSpec((1, tm, IN_COLS), lambda b, i: (b, i, 0)),
        compiler_params=pltpu.CompilerParams(vmem_limit_bytes=VMEM_LIMIT),
        name="in_proj",
    )(x, norm_g.reshape(1, d), shift.reshape(bsz, 1, d), scale.reshape(bsz, 1, d),
      w_in_bf16, qg, kg)


def _sb_tile(qm, kblk, vblk, cum, carry, mask):
    z = lax.dot_general(qm, kblk, _NT, preferred_element_type=jnp.float32)
    nz = -z
    l1p = jnp.log(1.0 + jnp.exp2(jnp.minimum(z, nz))) * LOG2E
    lg = jnp.minimum(nz, 0.0) - l1p
    if mask is not None:
        lg = jnp.where(mask, lg, 0.0)
    rem = jnp.dot(lg.astype(jnp.bfloat16), cum, preferred_element_type=jnp.float32)
    a = jnp.exp2(lg + z + rem)
    if mask is not None:
        a = jnp.where(mask, a, 0.0)
    o = jnp.dot(a.astype(jnp.bfloat16), vblk, preferred_element_type=jnp.float32)
    row_sum = jnp.sum(lg, axis=-1, keepdims=True)
    if carry is None:
        return o, row_sum
    return o * jnp.exp2(carry), carry + row_sum


def _sb_kernel(q_ref, k_ref, v_ref, g_ref, o_ref, acc):
    step = pl.program_id(2)
    lane = lax.broadcasted_iota(jnp.int32, (1, LANES), 1)
    first = lane < HEAD_DIM
    row = lax.broadcasted_iota(jnp.int32, (TK, TK), 0)
    col = lax.broadcasted_iota(jnp.int32, (TK, TK), 1)
    cum = (row > col).astype(jnp.bfloat16)
    strict = col < row
    diag_mask = jnp.concatenate([strict, strict], axis=0)

    def stacked(u):
        q = q_ref[0, u * TQ:(u + 1) * TQ, :]
        zero = jnp.zeros_like(q)
        return jnp.concatenate([jnp.where(first, q, zero), jnp.where(first, zero, q)], axis=0)

    qs = [stacked(u) for u in range(SB_QT)]
    acc_of = [acc.at[u * 2 * TQ:(u + 1) * 2 * TQ, :] for u in range(SB_QT)]

    def tiles(qm, kbs, carry, masks):
        total = None
        for kb, mask in zip(kbs, masks):
            start = pl.multiple_of(kb * TK, TK)
            o, carry = _sb_tile(qm, k_ref[0, pl.ds(start, TK), :], v_ref[0, pl.ds(start, TK), :],
                                cum, carry, mask)
            total = o if total is None else total + o
        return total, carry

    def first_block(first_step):
        carries = []
        for u in range(SB_QT):
            qt = step * SB_QT + u
            n = min(SB_FIRST, u + 1) if first_step else SB_FIRST
            o, carry = tiles(qs[u], [qt - t for t in range(n)], None,
                             [diag_mask] + [None] * (n - 1))
            acc_of[u][...] = o
            carries.append(carry)
        return tuple(carries)

    carries = lax.cond(step > 0, lambda: first_block(False), lambda: first_block(True))

    def alive(carry):
        return jnp.max(carry) > SB_DEAD_LOG2

    def sweep(u, n_steps, kbs_of_step, carry):
        def cond(state):
            i, _, live = state
            return jnp.logical_and(i < n_steps, live)

        def body(state):
            i, carry, _ = state
            kbs = kbs_of_step(i)
            o, carry = tiles(qs[u], kbs, carry, [None] * len(kbs))
            acc_of[u][...] += o
            return i + 1, carry, alive(carry)

        return lax.while_loop(cond, body, (jnp.int32(0), carry, alive(carry)))[1]

    any_alive = alive(carries[0])
    for carry in carries[1:]:
        any_alive = jnp.logical_or(any_alive, alive(carry))

    @pl.when(any_alive)
    def _():
        for u in range(SB_QT):
            left = jnp.maximum(step * SB_QT + u - (SB_FIRST - 1), 0)
            rem = lax.rem(left, SB_GROUP)
            carry = sweep(u, rem, lambda i, left=left: [left - 1 - i], carries[u])
            sweep(u, left // SB_GROUP,
                  lambda i, left=left, rem=rem: [left - rem - i * SB_GROUP - 1 - t
                                                 for t in range(SB_GROUP)], carry)

    for u in range(SB_QT):
        out = jnp.where(first, acc_of[u][:TQ, :], acc_of[u][TQ:, :])
        g = g_ref[0, u * TQ:(u + 1) * TQ, :].astype(jnp.float32)
        o_ref[0, u * TQ:(u + 1) * TQ, :] = (out * _silu(g)).astype(o_ref.dtype)


def _sb_attention(proj):
    bsz, s, _ = proj.shape
    rows = SB_QT * TQ
    return pl.pallas_call(
        _sb_kernel,
        out_shape=jax.ShapeDtypeStruct((bsz, s, SB_WIDTH), jnp.bfloat16),
        grid=(bsz, SB_WIDTH // LANES, s // rows),
        in_specs=[pl.BlockSpec((1, rows, LANES), lambda b, h, i: (b, i, CB_SB_Q + h)),
                  pl.BlockSpec((1, s, LANES), lambda b, h, i: (b, 0, CB_SB_K + h)),
                  pl.BlockSpec((1, s, LANES), lambda b, h, i: (b, 0, CB_SB_V + h)),
                  pl.BlockSpec((1, rows, LANES), lambda b, h, i: (b, i, CB_SB_G + h))],
        out_specs=pl.BlockSpec((1, rows, LANES), lambda b, h, i: (b, i, h)),
        scratch_shapes=[pltpu.VMEM((SB_QT * 2 * TQ, LANES), jnp.float32)],
        compiler_params=pltpu.CompilerParams(vmem_limit_bytes=VMEM_LIMIT),
        name="sb_attn",
    )(proj, proj, proj, proj)


def _df_kernel(q_ref, k_ref, v_ref, g_ref, slope_ref, lq1_ref, lk1_ref, lq2_ref, lk2_ref, sg_ref,
               qg_ref, kg_ref, o_ref, m_s, l_s, acc, *, lam_init):
    qi = pl.program_id(2)
    q = q_ref[0]
    lane = lax.broadcasted_iota(jnp.int32, (1, LANES), 1)
    first = lane < HEAD_DIM
    zero = jnp.zeros_like(q)
    qs = jnp.concatenate([jnp.where(first, q, zero), jnp.where(first, zero, q)], axis=0)
    slope = slope_ref[0] * LOG2E
    row = lax.broadcasted_iota(jnp.int32, (DF_TQ, DF_TK), 0)
    col = lax.broadcasted_iota(jnp.int32, (DF_TQ, DF_TK), 1)
    causal = col <= row
    diag_mask = jnp.concatenate([causal, causal], axis=0)

    qk_bound = (jnp.max(jnp.abs(qg_ref[...])) * jnp.max(jnp.abs(kg_ref[...]))
                * (math.sqrt(HEAD_DIM) * LOG2E * DF_BOUND_MARGIN))
    slope_s = jnp.max(slope)

    def load(first_kb, width):
        start = pl.multiple_of(first_kb * DF_TK, DF_TK)
        rel = lax.broadcasted_iota(jnp.int32, (1, width), 1).astype(jnp.float32)
        off = ((first_kb - qi) * DF_TK).astype(jnp.float32)
        bias = jnp.concatenate([slope] * (width // LANES), axis=1) * (rel + off)
        return k_ref[0, pl.ds(start, width), :], v_ref[0, pl.ds(start, width), :], bias

    def last_rel(first_kb, width):
        return ((first_kb - qi) * DF_TK + (width - 1)).astype(jnp.float32)

    def sweep(n_steps, first_kb_of_step, width, tile, needed):
        def cond(state):
            i, live = state
            return jnp.logical_and(i < n_steps, live)

        def body(state):
            i, _ = state
            tile(first_kb_of_step(i), width, None)
            return i + 1, needed(first_kb_of_step(i + 1), width)

        lax.while_loop(cond, body, (jnp.int32(0), needed(first_kb_of_step(jnp.int32(0)), width)))

    def sweep_all(tile, needed):
        tile(qi, DF_TK, diag_mask)
        rem = lax.rem(qi, DF_GROUP)
        sweep(rem, lambda i: qi - 1 - i, DF_TK, tile, needed)
        sweep(qi // DF_GROUP, lambda i: qi - rem - (i + 1) * DF_GROUP, DF_GROUP * DF_TK, tile, needed)

    def fixed_shift():
        rowpos = lax.broadcasted_iota(jnp.int32, (DF_TQ, LANES), 0).astype(jnp.float32)
        shift = qk_bound + slope * jnp.concatenate([rowpos, rowpos], axis=0)
        l_s[...] = jnp.zeros((2 * DF_TQ, LANES), jnp.float32)
        acc[...] = jnp.zeros((2 * DF_TQ, LANES), jnp.float32)

        def tile(first_kb, width, mask):
            reps = width // LANES
            kblk, vblk, bias = load(first_kb, width)
            s = (lax.dot_general(qs, kblk, _NT, preferred_element_type=jnp.float32)
                 + bias - jnp.concatenate([shift] * reps, axis=1))
            if mask is not None:
                s = jnp.where(mask, s, NEG_BIG)
            p = jnp.exp2(s)
            part = p[:, :LANES]
            for r in range(1, reps):
                part = part + p[:, r * LANES:(r + 1) * LANES]
            l_s[...] += part
            acc[...] += jnp.dot(p.astype(jnp.bfloat16), vblk, preferred_element_type=jnp.float32)

        def needed(first_kb, width):
            return slope_s * last_rel(first_kb, width) > DF_DEAD_LOG2

        sweep_all(tile, needed)
        l = jnp.sum(l_s[...], axis=-1, keepdims=True)
        return l[:DF_TQ], l[DF_TQ:]

    def running_max():
        m_s[...] = jnp.full((2 * DF_TQ, LANES), NEG_BIG, jnp.float32)
        l_s[...] = jnp.zeros((2 * DF_TQ, LANES), jnp.float32)
        acc[...] = jnp.zeros((2 * DF_TQ, LANES), jnp.float32)

        def tile(first_kb, width, mask):
            reps = width // LANES
            kblk, vblk, bias = load(first_kb, width)
            s = lax.dot_general(qs, kblk, _NT, preferred_element_type=jnp.float32) + bias
            if mask is not None:
                s = jnp.where(mask, s, NEG_BIG)
            m_old = m_s[...]
            m_new = jnp.maximum(m_old, jnp.max(s, axis=-1, keepdims=True))
            p = jnp.exp2(s - jnp.concatenate([m_new] * reps, axis=1))
            alpha = jnp.exp2(m_old - m_new)
            l_s[...] = alpha * l_s[...] + jnp.sum(p, axis=-1, keepdims=True)
            acc[...] = alpha * acc[...] + jnp.dot(p.astype(jnp.bfloat16), vblk,
                                                  preferred_element_type=jnp.float32)
            m_s[...] = m_new

        def needed(first_kb, width):
            return (qk_bound + slope_s * last_rel(first_kb, width) - jnp.min(m_s[...])
                    > DF_DEAD_LOG2)

        sweep_all(tile, needed)
        return l_s[:DF_TQ, :1], l_s[DF_TQ:, :1]

    l1, l2 = lax.cond(qk_bound < DF_FIXED_SHIFT_MAX, fixed_shift, running_max)

    lam = (jnp.exp(jnp.sum(lq1_ref[...] * lk1_ref[...], keepdims=True))
           - jnp.exp(jnp.sum(lq2_ref[...] * lk2_ref[...], keepdims=True)) + lam_init)
    o = acc[:DF_TQ, :] / l1 - lam * (acc[DF_TQ:, :] / l2)
    ms = jnp.mean(o * o, axis=-1, keepdims=True)
    o = o * lax.rsqrt(ms + EPS) * sg_ref[...] * (1.0 - lam_init)
    g = g_ref[0].astype(jnp.float32)
    o_ref[0] = (o * _silu(g)).astype(o_ref.dtype)


def _df_attention(proj, slopes, lq1, lk1, lq2, lk2, subln_g, q_norm_g, k_norm_g, lam_init):
    bsz, s, _ = proj.shape
    nq = s // DF_TQ
    vec = pl.BlockSpec((1, HEAD_DIM), lambda b, h, i: (0, 0))
    stat = pltpu.VMEM((2 * DF_TQ, LANES), jnp.float32)
    return pl.pallas_call(
        functools.partial(_df_kernel, lam_init=lam_init),
        out_shape=jax.ShapeDtypeStruct((bsz, s, DF_WIDTH), jnp.bfloat16),
        grid=(bsz, DF_HEADS, nq),
        in_specs=[pl.BlockSpec((1, DF_TQ, LANES), lambda b, h, i: (b, i, CB_DF_Q + h)),
                  pl.BlockSpec((1, s, LANES), lambda b, h, i: (b, 0, CB_DF_K + h)),
                  pl.BlockSpec((1, s, LANES), lambda b, h, i: (b, 0, CB_DF_V + h)),
                  pl.BlockSpec((1, DF_TQ, LANES), lambda b, h, i: (b, i, CB_DF_G + h)),
                  pl.BlockSpec((1, 1, LANES), lambda b, h, i: (h, 0, 0)),
                  vec, vec, vec, vec,
                  pl.BlockSpec((1, LANES), lambda b, h, i: (0, 0)),
                  vec, vec],
        out_specs=pl.BlockSpec((1, DF_TQ, LANES), lambda b, h, i: (b, i, h)),
        scratch_shapes=[stat, stat, stat],
        compiler_params=pltpu.CompilerParams(vmem_limit_bytes=VMEM_LIMIT),
        name="df_attn",
    )(proj, proj, proj, proj, slopes, lq1, lk1, lq2, lk2, subln_g, q_norm_g, k_norm_g)


def _out_proj_kernel(x_ref, sb_ref, df_ref, w_ref, gate_ref, o_ref):
    y = (jnp.dot(sb_ref[0], w_ref[:SB_WIDTH, :], preferred_element_type=jnp.float32)
         + jnp.dot(df_ref[0], w_ref[SB_WIDTH:, :], preferred_element_type=jnp.float32))
    o_ref[0] = x_ref[0] + gate_ref[0] * y


def _out_proj(x, sb_o, df_o, w_out_bf16, gate):
    bsz, s, d = x.shape
    tm = ROWS_OUT
    return pl.pallas_call(
        _out_proj_kernel,
        out_shape=jax.ShapeDtypeStruct((bsz, s, d), jnp.float32),
        grid=(bsz, s // tm),
        in_specs=[pl.BlockSpec((1, tm, d), lambda b, i: (b, i, 0)),
                  pl.BlockSpec((1, tm, SB_WIDTH), lambda b, i: (b, i, 0)),
                  pl.BlockSpec((1, tm, DF_WIDTH), lambda b, i: (b, i, 0)),
                  pl.BlockSpec((SB_WIDTH + DF_WIDTH, d), lambda b, i: (0, 0)),
                  pl.BlockSpec((1, 1, d), lambda b, i: (b, 0, 0))],
        out_specs=pl.BlockSpec((1, tm, d), lambda b, i: (b, i, 0)),
        compiler_params=pltpu.CompilerParams(vmem_limit_bytes=VMEM_LIMIT),
        name="out_proj",
    )(x, sb_o, df_o, w_out_bf16, gate.reshape(bsz, 1, d))


def _layer(x, c, layer_idx, norm_g, w_ada, b_ada, w_in, q_norm_g, k_norm_g,
           lambda_q1, lambda_k1, lambda_q2, lambda_k2, subln_g, w_out):
    d = x.shape[-1]
    lam_init = 0.8 - 0.6 * math.exp(-0.3 * layer_idx)
    mod = _adaln_mod(c, w_ada, b_ada)
    shift, scale, gate = mod[:, :d], mod[:, d:2 * d], mod[:, 2 * d:]
    proj = _in_proj(x, norm_g, shift, scale, w_in.astype(jnp.bfloat16), q_norm_g, k_norm_g)
    sb_o = _sb_attention(proj)
    slopes = jnp.asarray([2.0 ** (-8.0 * (h + 1) / DF_HEADS) for h in range(DF_HEADS)], jnp.float32)
    slopes = jnp.broadcast_to(slopes[:, None, None], (DF_HEADS, 1, LANES))
    row = lambda v: v.reshape(1, -1)
    df_o = _df_attention(proj, slopes, row(lambda_q1), row(lambda_k1), row(lambda_q2),
                         row(lambda_k2), row(subln_g), row(q_norm_g), row(k_norm_g), lam_init)
    return _out_proj(x, sb_o, df_o, w_out.astype(jnp.bfloat16), gate)


@jax.jit
def kernel(x, c, norm_g, w_ada, b_ada, w_in, q_norm_g, k_norm_g, lambda_q1, lambda_k1,
           lambda_q2, lambda_k2, subln_g, w_out):
    for l in range(norm_g.shape[0]):
        x = _layer(x, c, l, norm_g[l], w_ada[l], b_ada[l], w_in[l], q_norm_g[l], k_norm_g[l],
                   lambda_q1[l], lambda_k1[l], lambda_q2[l], lambda_k2[l], subln_g[l], w_out[l])
    return x
```

```python
import functools
import math

import jax
import jax.numpy as jnp
from jax import lax
from jax.experimental import pallas as pl
from jax.experimental.pallas import tpu as pltpu

D_MODEL = 1024
SB_HEADS = 8
DF_HEADS = 4
HEAD_DIM = 64
LANES = 128
SB_WIDTH = SB_HEADS * HEAD_DIM
DF_WIDTH = DF_HEADS * 2 * HEAD_DIM
IN_COLS = 4 * SB_WIDTH + 4 * DF_WIDTH
EPS = 1e-6
NEG_BIG = -1e30

CB_SB_Q, CB_SB_K, CB_SB_V, CB_SB_G = 0, 4, 8, 12
CB_DF_Q, CB_DF_K, CB_DF_V, CB_DF_G = 16, 20, 24, 28

ROWS_PROJ = 512
ROWS_OUT = 1024
TQ = 256
TK = 256
DF_TQ = 1024
DF_TK = 1024
SB_QT = 4
SB_FIRST = 3
SB_GROUP = 4
DF_GROUP = 1
LOG2E = math.log2(math.e)
SB_DEAD_LOG2 = -150.0
DF_DEAD_LOG2 = -150.0
DF_FIXED_SHIFT_MAX = 48.0
DF_BOUND_MARGIN = 1.02
PROJ_CHUNK = 512
VMEM_LIMIT = 48 * 1024 * 1024

_NT = (((1,), (1,)), ((), ()))


def _silu(g):
    return g / (1.0 + jnp.exp(-g))


def _adaln_kernel(ct_ref, w_ref, b_ref, o_ref):
    w = w_ref[...]
    rows = []
    for b in range(o_ref.shape[0]):
        col = ct_ref[:, b:b + 1]
        rows.append(jnp.sum(col * w, axis=0, keepdims=True))
    o_ref[...] = jnp.concatenate(rows, axis=0) + b_ref[...]


def _adaln_mod(c, w_ada, b_ada):
    bsz, d = c.shape
    n = w_ada.shape[1]
    tn = 512
    return pl.pallas_call(
        _adaln_kernel,
        out_shape=jax.ShapeDtypeStruct((bsz, n), jnp.float32),
        grid=(n // tn,),
        in_specs=[pl.BlockSpec((d, bsz), lambda j: (0, 0)),
                  pl.BlockSpec((d, tn), lambda j: (0, j)),
                  pl.BlockSpec((1, tn), lambda j: (0, j))],
        out_specs=pl.BlockSpec((bsz, tn), lambda j: (0, j)),
        name="adaln_mod",
    )(c.T, w_ada, b_ada.reshape(1, n))


def _group_rmsnorm(y, gain):
    outs = []
    lane = lax.broadcasted_iota(jnp.int32, (1, LANES), 1)
    lo = lane < HEAD_DIM
    for j in range(y.shape[1] // LANES):
        blk = y[:, j * LANES:(j + 1) * LANES]
        sq = blk * blk
        s_lo = jnp.sum(jnp.where(lo, sq, 0.0), axis=-1, keepdims=True)
        s_hi = jnp.sum(jnp.where(lo, 0.0, sq), axis=-1, keepdims=True)
        ms = jnp.where(lo, s_lo, s_hi) * (1.0 / HEAD_DIM)
        outs.append(blk * lax.rsqrt(ms + EPS) * gain)
    return jnp.concatenate(outs, axis=1)


def _in_proj_kernel(x_ref, ng_ref, shift_ref, scale_ref, w_ref, qg_ref, kg_ref, o_ref):
    x = x_ref[0]
    ms = jnp.mean(x * x, axis=-1, keepdims=True)
    h = x * lax.rsqrt(ms + EPS) * ng_ref[...]
    h = (h * (1.0 + scale_ref[0]) + shift_ref[0]).astype(jnp.bfloat16)
    inv = LOG2E / math.sqrt(HEAD_DIM)
    for ci in range(IN_COLS // PROJ_CHUNK):
        c0 = ci * PROJ_CHUNK
        y = jnp.dot(h, w_ref[:, c0:c0 + PROJ_CHUNK], preferred_element_type=jnp.float32)
        cb = c0 // LANES
        if cb == CB_SB_Q:
            y = y * inv
        elif cb == CB_DF_Q:
            y = _group_rmsnorm(y, qg_ref[...]) * inv
        elif cb == CB_DF_K:
            y = _group_rmsnorm(y, kg_ref[...])
        o_ref[0, :, c0:c0 + PROJ_CHUNK] = y.astype(o_ref.dtype)


def _in_proj(x, norm_g, shift, scale, w_in_bf16, q_norm_g, k_norm_g):
    bsz, s, d = x.shape
    tm = ROWS_PROJ
    qg = jnp.tile(q_norm_g.reshape(1, HEAD_DIM), (1, 2))
    kg = jnp.tile(k_norm_g.reshape(1, HEAD_DIM), (1, 2))
    return pl.pallas_call(
        _in_proj_kernel,
        out_shape=jax.ShapeDtypeStruct((bsz, s, IN_COLS), jnp.bfloat16),
        grid=(bsz, s // tm),
        in_specs=[pl.BlockSpec((1, tm, d), lambda b, i: (b, i, 0)),
                  pl.BlockSpec((1, d), lambda b, i: (0, 0)),
                  pl.BlockSpec((1, 1, d), lambda b, i: (b, 0, 0)),
                  pl.BlockSpec((1, 1, d), lambda b, i: (b, 0, 0)),
                  pl.BlockSpec((d, IN_COLS), lambda b, i: (0, 0)),
                  pl.BlockSpec((1, LANES), lambda b, i: (0, 0)),
                  pl.BlockSpec((1, LANES), lambda b, i: (0, 0))],
        out_specs=pl.BlockSpec((1, tm, IN_COLS), lambda b, i: (b, i, 0)),
        compiler_params=pltpu.CompilerParams(vmem_limit_bytes=VMEM_LIMIT),
        name="in_proj",
    )(x, norm_g.reshape(1, d), shift.reshape(bsz, 1, d), scale.reshape(bsz, 1, d),
      w_in_bf16, qg, kg)


def _sb_tile(qm, kblk, vblk, cum, carry, mask):
    z = lax.dot_general(qm, kblk, _NT, preferred_element_type=jnp.float32)
    nz = -z
    l1p = jnp.log(1.0 + jnp.exp2(jnp.minimum(z, nz))) * LOG2E
    lg = jnp.minimum(nz, 0.0) - l1p
    if mask is not None:
        lg = jnp.where(mask, lg, 0.0)
    rem = jnp.dot(lg.astype(jnp.bfloat16), cum, preferred_element_type=jnp.float32)
    a = jnp.exp2(lg + z + rem + carry)
    if mask is not None:
        a = jnp.where(mask, a, 0.0)
    o = jnp.dot(a.astype(jnp.bfloat16), vblk, preferred_element_type=jnp.float32)
    return o, carry + jnp.sum(lg, axis=-1, keepdims=True)


def _sb_kernel(q_ref, k_ref, v_ref, g_ref, o_ref, acc):
    step = pl.program_id(2)
    lane = lax.broadcasted_iota(jnp.int32, (1, LANES), 1)
    first = lane < HEAD_DIM
    row = lax.broadcasted_iota(jnp.int32, (TK, TK), 0)
    col = lax.broadcasted_iota(jnp.int32, (TK, TK), 1)
    cum = (row > col).astype(jnp.bfloat16)
    strict = col < row
    diag_mask = jnp.concatenate([strict, strict], axis=0)

    def stacked(u):
        q = q_ref[0, u * TQ:(u + 1) * TQ, :]
        zero = jnp.zeros_like(q)
        return jnp.concatenate([jnp.where(first, q, zero), jnp.where(first, zero, q)], axis=0)

    qs = [stacked(u) for u in range(SB_QT)]
    acc_of = [acc.at[u * 2 * TQ:(u + 1) * 2 * TQ, :] for u in range(SB_QT)]

    def tiles(qm, kbs, carry, masks):
        total = None
        for kb, mask in zip(kbs, masks):
            start = pl.multiple_of(kb * TK, TK)
            o, carry = _sb_tile(qm, k_ref[0, pl.ds(start, TK), :], v_ref[0, pl.ds(start, TK), :],
                                cum, carry, mask)
            total = o if total is None else total + o
        return total, carry

    zero_carry = jnp.zeros((2 * TQ, 1), jnp.float32)

    def first_block(first_step):
        carries = []
        for u in range(SB_QT):
            qt = step * SB_QT + u
            n = min(SB_FIRST, u + 1) if first_step else SB_FIRST
            o, carry = tiles(qs[u], [qt - t for t in range(n)], zero_carry,
                             [diag_mask] + [None] * (n - 1))
            acc_of[u][...] = o
            carries.append(carry)
        return tuple(carries)

    carries = lax.cond(step > 0, lambda: first_block(False), lambda: first_block(True))

    def alive(carry):
        return jnp.max(carry) > SB_DEAD_LOG2

    def sweep(u, n_steps, kbs_of_step, carry):
        def cond(state):
            i, _, live = state
            return jnp.logical_and(i < n_steps, live)

        def body(state):
            i, carry, _ = state
            kbs = kbs_of_step(i)
            o, carry = tiles(qs[u], kbs, carry, [None] * len(kbs))
            acc_of[u][...] += o
            return i + 1, carry, alive(carry)

        return lax.while_loop(cond, body, (jnp.int32(0), carry, alive(carry)))[1]

    any_alive = alive(carries[0])
    for carry in carries[1:]:
        any_alive = jnp.logical_or(any_alive, alive(carry))

    @pl.when(any_alive)
    def _():
        for u in range(SB_QT):
            left = jnp.maximum(step * SB_QT + u - (SB_FIRST - 1), 0)
            rem = lax.rem(left, SB_GROUP)
            carry = sweep(u, rem, lambda i, left=left: [left - 1 - i], carries[u])
            sweep(u, left // SB_GROUP,
                  lambda i, left=left, rem=rem: [left - rem - i * SB_GROUP - 1 - t
                                                 for t in range(SB_GROUP)], carry)

    for u in range(SB_QT):
        out = jnp.where(first, acc_of[u][:TQ, :], acc_of[u][TQ:, :])
        g = g_ref[0, u * TQ:(u + 1) * TQ, :].astype(jnp.float32)
        o_ref[0, u * TQ:(u + 1) * TQ, :] = (out * _silu(g)).astype(o_ref.dtype)


def _sb_attention(proj):
    bsz, s, _ = proj.shape
    rows = SB_QT * TQ
    return pl.pallas_call(
        _sb_kernel,
        out_shape=jax.ShapeDtypeStruct((bsz, s, SB_WIDTH), jnp.bfloat16),
        grid=(bsz, SB_WIDTH // LANES, s // rows),
        in_specs=[pl.BlockSpec((1, rows, LANES), lambda b, h, i: (b, i, CB_SB_Q + h)),
                  pl.BlockSpec((1, s, LANES), lambda b, h, i: (b, 0, CB_SB_K + h)),
                  pl.BlockSpec((1, s, LANES), lambda b, h, i: (b, 0, CB_SB_V + h)),
                  pl.BlockSpec((1, rows, LANES), lambda b, h, i: (b, i, CB_SB_G + h))],
        out_specs=pl.BlockSpec((1, rows, LANES), lambda b, h, i: (b, i, h)),
        scratch_shapes=[pltpu.VMEM((SB_QT * 2 * TQ, LANES), jnp.float32)],
        compiler_params=pltpu.CompilerParams(vmem_limit_bytes=VMEM_LIMIT),
        name="sb_attn",
    )(proj, proj, proj, proj)


def _df_kernel(q_ref, k_ref, v_ref, g_ref, slope_ref, lq1_ref, lk1_ref, lq2_ref, lk2_ref, sg_ref,
               qg_ref, kg_ref, o_ref, m_s, l_s, acc, *, lam_init):
    qi = pl.program_id(2)
    q = q_ref[0]
    lane = lax.broadcasted_iota(jnp.int32, (1, LANES), 1)
    first = lane < HEAD_DIM
    zero = jnp.zeros_like(q)
    qs = jnp.concatenate([jnp.where(first, q, zero), jnp.where(first, zero, q)], axis=0)
    slope = slope_ref[0] * LOG2E
    row = lax.broadcasted_iota(jnp.int32, (DF_TQ, DF_TK), 0)
    col = lax.broadcasted_iota(jnp.int32, (DF_TQ, DF_TK), 1)
    causal = col <= row
    diag_mask = jnp.concatenate([causal, causal], axis=0)

    qk_bound = (jnp.max(jnp.abs(qg_ref[...])) * jnp.max(jnp.abs(kg_ref[...]))
                * (math.sqrt(HEAD_DIM) * LOG2E * DF_BOUND_MARGIN))
    slope_s = jnp.max(slope)

    def load(first_kb, width):
        start = pl.multiple_of(first_kb * DF_TK, DF_TK)
        rel = lax.broadcasted_iota(jnp.int32, (1, width), 1).astype(jnp.float32)
        off = ((first_kb - qi) * DF_TK).astype(jnp.float32)
        bias = jnp.concatenate([slope] * (width // LANES), axis=1) * (rel + off)
        return k_ref[0, pl.ds(start, width), :], v_ref[0, pl.ds(start, width), :], bias

    def last_rel(first_kb, width):
        return ((first_kb - qi) * DF_TK + (width - 1)).astype(jnp.float32)

    def sweep(n_steps, first_kb_of_step, width, tile, needed):
        def cond(state):
            i, live = state
            return jnp.logical_and(i < n_steps, live)

        def body(state):
            i, _ = state
            tile(first_kb_of_step(i), width, None)
            return i + 1, needed(first_kb_of_step(i + 1), width)

        lax.while_loop(cond, body, (jnp.int32(0), needed(first_kb_of_step(jnp.int32(0)), width)))

    def sweep_all(tile, needed):
        tile(qi, DF_TK, diag_mask)
        rem = lax.rem(qi, DF_GROUP)
        sweep(rem, lambda i: qi - 1 - i, DF_TK, tile, needed)
        sweep(qi // DF_GROUP, lambda i: qi - rem - (i + 1) * DF_GROUP, DF_GROUP * DF_TK, tile, needed)

    def fixed_shift():
        rowpos = lax.broadcasted_iota(jnp.int32, (DF_TQ, LANES), 0).astype(jnp.float32)
        shift = qk_bound + slope * jnp.concatenate([rowpos, rowpos], axis=0)
        l_s[...] = jnp.zeros((2 * DF_TQ, LANES), jnp.float32)
        acc[...] = jnp.zeros((2 * DF_TQ, LANES), jnp.float32)

        def tile(first_kb, width, mask):
            reps = width // LANES
            kblk, vblk, bias = load(first_kb, width)
            s = (lax.dot_general(qs, kblk, _NT, preferred_element_type=jnp.float32)
                 + bias - jnp.concatenate([shift] * reps, axis=1))
            if mask is not None:
                s = jnp.where(mask, s, NEG_BIG)
            p = jnp.exp2(s)
            part = p[:, :LANES]
            for r in range(1, reps):
                part = part + p[:, r * LANES:(r + 1) * LANES]
            l_s[...] += part
            acc[...] += jnp.dot(p.astype(jnp.bfloat16), vblk, preferred_element_type=jnp.float32)

        def needed(first_kb, width):
            return slope_s * last_rel(first_kb, width) > DF_DEAD_LOG2

        sweep_all(tile, needed)
        l = jnp.sum(l_s[...], axis=-1, keepdims=True)
        return l[:DF_TQ], l[DF_TQ:]

    def running_max():
        m_s[...] = jnp.full((2 * DF_TQ, LANES), NEG_BIG, jnp.float32)
        l_s[...] = jnp.zeros((2 * DF_TQ, LANES), jnp.float32)
        acc[...] = jnp.zeros((2 * DF_TQ, LANES), jnp.float32)

        def tile(first_kb, width, mask):
            reps = width // LANES
            kblk, vblk, bias = load(first_kb, width)
            s = lax.dot_general(qs, kblk, _NT, preferred_element_type=jnp.float32) + bias
            if mask is not None:
                s = jnp.where(mask, s, NEG_BIG)
            m_old = m_s[...]
            m_new = jnp.maximum(m_old, jnp.max(s, axis=-1, keepdims=True))
            p = jnp.exp2(s - jnp.concatenate([m_new] * reps, axis=1))
            alpha = jnp.exp2(m_old - m_new)
            l_s[...] = alpha * l_s[...] + jnp.sum(p, axis=-1, keepdims=True)
            acc[...] = alpha * acc[...] + jnp.dot(p.astype(jnp.bfloat16), vblk,
                                                  preferred_element_type=jnp.float32)
            m_s[...] = m_new

        def needed(first_kb, width):
            return (qk_bound + slope_s * last_rel(first_kb, width) - jnp.min(m_s[...])
                    > DF_DEAD_LOG2)

        sweep_all(tile, needed)
        return l_s[:DF_TQ, :1], l_s[DF_TQ:, :1]

    l1, l2 = lax.cond(qk_bound < DF_FIXED_SHIFT_MAX, fixed_shift, running_max)

    lam = (jnp.exp(jnp.sum(lq1_ref[...] * lk1_ref[...], keepdims=True))
           - jnp.exp(jnp.sum(lq2_ref[...] * lk2_ref[...], keepdims=True)) + lam_init)
    o = acc[:DF_TQ, :] / l1 - lam * (acc[DF_TQ:, :] / l2)
    ms = jnp.mean(o * o, axis=-1, keepdims=True)
    o = o * lax.rsqrt(ms + EPS) * sg_ref[...] * (1.0 - lam_init)
    g = g_ref[0].astype(jnp.float32)
    o_ref[0] = (o * _silu(g)).astype(o_ref.dtype)


def _df_attention(proj, slopes, lq1, lk1, lq2, lk2, subln_g, q_norm_g, k_norm_g, lam_init):
    bsz, s, _ = proj.shape
    nq = s // DF_TQ
    vec = pl.BlockSpec((1, HEAD_DIM), lambda b, h, i: (0, 0))
    stat = pltpu.VMEM((2 * DF_TQ, LANES), jnp.float32)
    return pl.pallas_call(
        functools.partial(_df_kernel, lam_init=lam_init),
        out_shape=jax.ShapeDtypeStruct((bsz, s, DF_WIDTH), jnp.bfloat16),
        grid=(bsz, DF_HEADS, nq),
        in_specs=[pl.BlockSpec((1, DF_TQ, LANES), lambda b, h, i: (b, i, CB_DF_Q + h)),
                  pl.BlockSpec((1, s, LANES), lambda b, h, i: (b, 0, CB_DF_K + h)),
                  pl.BlockSpec((1, s, LANES), lambda b, h, i: (b, 0, CB_DF_V + h)),
                  pl.BlockSpec((1, DF_TQ, LANES), lambda b, h, i: (b, i, CB_DF_G + h)),
                  pl.BlockSpec((1, 1, LANES), lambda b, h, i: (h, 0, 0)),
                  vec, vec, vec, vec,
                  pl.BlockSpec((1, LANES), lambda b, h, i: (0, 0)),
                  vec, vec],
        out_specs=pl.BlockSpec((1, DF_TQ, LANES), lambda b, h, i: (b, i, h)),
        scratch_shapes=[stat, stat, stat],
        compiler_params=pltpu.CompilerParams(vmem_limit_bytes=VMEM_LIMIT),
        name="df_attn",
    )(proj, proj, proj, proj, slopes, lq1, lk1, lq2, lk2, subln_g, q_norm_g, k_norm_g)


def _out_proj_kernel(x_ref, sb_ref, df_ref, w_ref, gate_ref, o_ref):
    y = (jnp.dot(sb_ref[0], w_ref[:SB_WIDTH, :], preferred_element_type=jnp.float32)
         + jnp.dot(df_ref[0], w_ref[SB_WIDTH:, :], preferred_element_type=jnp.float32))
    o_ref[0] = x_ref[0] + gate_ref[0] * y


def _out_proj(x, sb_o, df_o, w_out_bf16, gate):
    bsz, s, d = x.shape
    tm = ROWS_OUT
    return pl.pallas_call(
        _out_proj_kernel,
        out_shape=jax.ShapeDtypeStruct((bsz, s, d), jnp.float32),
        grid=(bsz, s // tm),
        in_specs=[pl.BlockSpec((1, tm, d), lambda b, i: (b, i, 0)),
                  pl.BlockSpec((1, tm, SB_WIDTH), lambda b, i: (b, i, 0)),
                  pl.BlockSpec((1, tm, DF_WIDTH), lambda b, i: (b, i, 0)),
                  pl.BlockSpec((SB_WIDTH + DF_WIDTH, d), lambda b, i: (0, 0)),
                  pl.BlockSpec((1, 1, d), lambda b, i: (b, 0, 0))],
        out_specs=pl.BlockSpec((1, tm, d), lambda b, i: (b, i, 0)),
        compiler_params=pltpu.CompilerParams(vmem_limit_bytes=VMEM_LIMIT),
        name="out_proj",
    )(x, sb_o, df_o, w_out_bf16, gate.reshape(bsz, 1, d))


def _layer(x, c, layer_idx, norm_g, w_ada, b_ada, w_in, q_norm_g, k_norm_g,
           lambda_q1, lambda_k1, lambda_q2, lambda_k2, subln_g, w_out):
    d = x.shape[-1]
    lam_init = 0.8 - 0.6 * math.exp(-0.3 * layer_idx)
    mod = _adaln_mod(c, w_ada, b_ada)
    shift, scale, gate = mod[:, :d], mod[:, d:2 * d], mod[:, 2 * d:]
    proj = _in_proj(x, norm_g, shift, scale, w_in.astype(jnp.bfloat16), q_norm_g, k_norm_g)
    sb_o = _sb_attention(proj)
    slopes = jnp.asarray([2.0 ** (-8.0 * (h + 1) / DF_HEADS) for h in range(DF_HEADS)], jnp.float32)
    slopes = jnp.broadcast_to(slopes[:, None, None], (DF_HEADS, 1, LANES))
    row = lambda v: v.reshape(1, -1)
    df_o = _df_attention(proj, slopes, row(lambda_q1), row(lambda_k1), row(lambda_q2),
                         row(lambda_k2), row(subln_g), row(q_norm_g), row(k_norm_g), lam_init)
    return _out_proj(x, sb_o, df_o, w_out.astype(jnp.bfloat16), gate)


@jax.jit
def kernel(x, c, norm_g, w_ada, b_ada, w_in, q_norm_g, k_norm_g, lambda_q1, lambda_k1,
           lambda_q2, lambda_k2, subln_g, w_out):
    for l in range(norm_g.shape[0]):
        x = _layer(x, c, l, norm_g[l], w_ada[l], b_ada[l], w_in[l], q_norm_g[l], k_norm_g[l],
                   lambda_q1[l], lambda_k1[l], lambda_q2[l], lambda_k2[l], subln_g[l], w_out[l])
    return x
```

```python
import functools
import math

import jax
import jax.numpy as jnp
from jax import lax
from jax.experimental import pallas as pl
from jax.experimental.pallas import tpu as pltpu

D_MODEL = 1024
SB_HEADS = 8
DF_HEADS = 4
HEAD_DIM = 64
LANES = 128
SB_WIDTH = SB_HEADS * HEAD_DIM
DF_WIDTH = DF_HEADS * 2 * HEAD_DIM
IN_COLS = 4 * SB_WIDTH + 4 * DF_WIDTH
EPS = 1e-6
NEG_BIG = -1e30

CB_SB_Q, CB_SB_K, CB_SB_V, CB_SB_G = 0, 4, 8, 12
CB_DF_Q, CB_DF_K, CB_DF_V, CB_DF_G = 16, 20, 24, 28

ROWS_PROJ = 512
ROWS_OUT = 1024
TQ = 256
TK = 256
DF_TQ = 1024
DF_TK = 1024
SB_QT = 4
SB_FIRST = 3
SB_GROUP = 4
DF_GROUP = 2
LOG2E = math.log2(math.e)
SB_DEAD_LOG2 = -150.0
DF_DEAD_LOG2 = -150.0
DF_FIXED_SHIFT_MAX = 48.0
DF_BOUND_MARGIN = 1.02
PROJ_CHUNK = 512
VMEM_LIMIT = 48 * 1024 * 1024

_NT = (((1,), (1,)), ((), ()))


def _silu(g):
    return g / (1.0 + jnp.exp(-g))


def _adaln_kernel(ct_ref, w_ref, b_ref, o_ref):
    w = w_ref[...]
    rows = []
    for b in range(o_ref.shape[0]):
        col = ct_ref[:, b:b + 1]
        rows.append(jnp.sum(col * w, axis=0, keepdims=True))
    o_ref[...] = jnp.concatenate(rows, axis=0) + b_ref[...]


def _adaln_mod(c, w_ada, b_ada):
    bsz, d = c.shape
    n = w_ada.shape[1]
    tn = 512
    return pl.pallas_call(
        _adaln_kernel,
        out_shape=jax.ShapeDtypeStruct((bsz, n), jnp.float32),
        grid=(n // tn,),
        in_specs=[pl.BlockSpec((d, bsz), lambda j: (0, 0)),
                  pl.BlockSpec((d, tn), lambda j: (0, j)),
                  pl.BlockSpec((1, tn), lambda j: (0, j))],
        out_specs=pl.BlockSpec((bsz, tn), lambda j: (0, j)),
        name="adaln_mod",
    )(c.T, w_ada, b_ada.reshape(1, n))


def _group_rmsnorm(y, gain):
    outs = []
    lane = lax.broadcasted_iota(jnp.int32, (1, LANES), 1)
    lo = lane < HEAD_DIM
    for j in range(y.shape[1] // LANES):
        blk = y[:, j * LANES:(j + 1) * LANES]
        sq = blk * blk
        s_lo = jnp.sum(jnp.where(lo, sq, 0.0), axis=-1, keepdims=True)
        s_hi = jnp.sum(jnp.where(lo, 0.0, sq), axis=-1, keepdims=True)
        ms = jnp.where(lo, s_lo, s_hi) * (1.0 / HEAD_DIM)
        outs.append(blk * lax.rsqrt(ms + EPS) * gain)
    return jnp.concatenate(outs, axis=1)


def _in_proj_kernel(x_ref, ng_ref, shift_ref, scale_ref, w_ref, qg_ref, kg_ref, o_ref):
    x = x_ref[0]
    ms = jnp.mean(x * x, axis=-1, keepdims=True)
    h = x * lax.rsqrt(ms + EPS) * ng_ref[...]
    h = (h * (1.0 + scale_ref[0]) + shift_ref[0]).astype(jnp.bfloat16)
    inv = LOG2E / math.sqrt(HEAD_DIM)
    for ci in range(IN_COLS // PROJ_CHUNK):
        c0 = ci * PROJ_CHUNK
        y = jnp.dot(h, w_ref[:, c0:c0 + PROJ_CHUNK], preferred_element_type=jnp.float32)
        cb = c0 // LANES
        if cb == CB_SB_Q:
            y = y * inv
        elif cb == CB_DF_Q:
            y = _group_rmsnorm(y, qg_ref[...]) * inv
        elif cb == CB_DF_K:
            y = _group_rmsnorm(y, kg_ref[...])
        o_ref[0, :, c0:c0 + PROJ_CHUNK] = y.astype(o_ref.dtype)


def _in_proj(x, norm_g, shift, scale, w_in_bf16, q_norm_g, k_norm_g):
    bsz, s, d = x.shape
    tm = ROWS_PROJ
    qg = jnp.tile(q_norm_g.reshape(1, HEAD_DIM), (1, 2))
    kg = jnp.tile(k_norm_g.reshape(1, HEAD_DIM), (1, 2))
    return pl.pallas_call(
        _in_proj_kernel,
        out_shape=jax.ShapeDtypeStruct((bsz, s, IN_COLS), jnp.bfloat16),
        grid=(bsz, s // tm),
        in_specs=[pl.BlockSpec((1, tm, d), lambda b, i: (b, i, 0)),
                  pl.BlockSpec((1, d), lambda b, i: (0, 0)),
                  pl.BlockSpec((1, 1, d), lambda b, i: (b, 0, 0)),
                  pl.BlockSpec((1, 1, d), lambda b, i: (b, 0, 0)),
                  pl.BlockSpec((d, IN_COLS), lambda b, i: (0, 0)),
                  pl.BlockSpec((1, LANES), lambda b, i: (0, 0)),
                  pl.BlockSpec((1, LANES), lambda b, i: (0, 0))],
        out_specs=pl.BlockSpec((1, tm, IN_COLS), lambda b, i: (b, i, 0)),
        compiler_params=pltpu.CompilerParams(vmem_limit_bytes=VMEM_LIMIT),
        name="in_proj",
    )(x, norm_g.reshape(1, d), shift.reshape(bsz, 1, d), scale.reshape(bsz, 1, d),
      w_in_bf16, qg, kg)


def _sb_tile(qm, kblk, vblk, cum, carry, mask):
    z = lax.dot_general(qm, kblk, _NT, preferred_element_type=jnp.float32)
    nz = -z
    l1p = jnp.log(1.0 + jnp.exp2(jnp.minimum(z, nz))) * LOG2E
    lg = jnp.minimum(nz, 0.0) - l1p
    if mask is not None:
        lg = jnp.where(mask, lg, 0.0)
    rem = jnp.dot(lg.astype(jnp.bfloat16), cum, preferred_element_type=jnp.float32)
    a = jnp.exp2(lg + z + rem + carry)
    if mask is not None:
        a = jnp.where(mask, a, 0.0)
    o = jnp.dot(a.astype(jnp.bfloat16), vblk, preferred_element_type=jnp.float32)
    return o, carry + jnp.sum(lg, axis=-1, keepdims=True)


def _sb_kernel(q_ref, k_ref, v_ref, g_ref, o_ref, acc):
    step = pl.program_id(2)
    lane = lax.broadcasted_iota(jnp.int32, (1, LANES), 1)
    first = lane < HEAD_DIM
    row = lax.broadcasted_iota(jnp.int32, (TK, TK), 0)
    col = lax.broadcasted_iota(jnp.int32, (TK, TK), 1)
    cum = (row > col).astype(jnp.bfloat16)
    strict = col < row
    diag_mask = jnp.concatenate([strict, strict], axis=0)

    def stacked(u):
        q = q_ref[0, u * TQ:(u + 1) * TQ, :]
        zero = jnp.zeros_like(q)
        return jnp.concatenate([jnp.where(first, q, zero), jnp.where(first, zero, q)], axis=0)

    qs = [stacked(u) for u in range(SB_QT)]
    acc_of = [acc.at[u * 2 * TQ:(u + 1) * 2 * TQ, :] for u in range(SB_QT)]

    def tiles(qm, kbs, carry, masks):
        total = None
        for kb, mask in zip(kbs, masks):
            start = pl.multiple_of(kb * TK, TK)
            o, carry = _sb_tile(qm, k_ref[0, pl.ds(start, TK), :], v_ref[0, pl.ds(start, TK), :],
                                cum, carry, mask)
            total = o if total is None else total + o
        return total, carry

    zero_carry = jnp.zeros((2 * TQ, 1), jnp.float32)

    def first_block(first_step):
        carries = []
        for u in range(SB_QT):
            qt = step * SB_QT + u
            n = min(SB_FIRST, u + 1) if first_step else SB_FIRST
            o, carry = tiles(qs[u], [qt - t for t in range(n)], zero_carry,
                             [diag_mask] + [None] * (n - 1))
            acc_of[u][...] = o
            carries.append(carry)
        return tuple(carries)

    carries = lax.cond(step > 0, lambda: first_block(False), lambda: first_block(True))

    def alive(carry):
        return jnp.max(carry) > SB_DEAD_LOG2

    def sweep(u, n_steps, kbs_of_step, carry):
        def cond(state):
            i, _, live = state
            return jnp.logical_and(i < n_steps, live)

        def body(state):
            i, carry, _ = state
            kbs = kbs_of_step(i)
            o, carry = tiles(qs[u], kbs, carry, [None] * len(kbs))
            acc_of[u][...] += o
            return i + 1, carry, alive(carry)

        return lax.while_loop(cond, body, (jnp.int32(0), carry, alive(carry)))[1]

    any_alive = alive(carries[0])
    for carry in carries[1:]:
        any_alive = jnp.logical_or(any_alive, alive(carry))

    @pl.when(any_alive)
    def _():
        for u in range(SB_QT):
            left = jnp.maximum(step * SB_QT + u - (SB_FIRST - 1), 0)
            rem = lax.rem(left, SB_GROUP)
            carry = sweep(u, rem, lambda i, left=left: [left - 1 - i], carries[u])
            sweep(u, left // SB_GROUP,
                  lambda i, left=left, rem=rem: [left - rem - i * SB_GROUP - 1 - t
                                                 for t in range(SB_GROUP)], carry)

    for u in range(SB_QT):
        out = jnp.where(first, acc_of[u][:TQ, :], acc_of[u][TQ:, :])
        g = g_ref[0, u * TQ:(u + 1) * TQ, :].astype(jnp.float32)
        o_ref[0, u * TQ:(u + 1) * TQ, :] = (out * _silu(g)).astype(o_ref.dtype)


def _sb_attention(proj):
    bsz, s, _ = proj.shape
    rows = SB_QT * TQ
    return pl.pallas_call(
        _sb_kernel,
        out_shape=jax.ShapeDtypeStruct((bsz, s, SB_WIDTH), jnp.bfloat16),
        grid=(bsz, SB_WIDTH // LANES, s // rows),
        in_specs=[pl.BlockSpec((1, rows, LANES), lambda b, h, i: (b, i, CB_SB_Q + h)),
                  pl.BlockSpec((1, s, LANES), lambda b, h, i: (b, 0, CB_SB_K + h)),
                  pl.BlockSpec((1, s, LANES), lambda b, h, i: (b, 0, CB_SB_V + h)),
                  pl.BlockSpec((1, rows, LANES), lambda b, h, i: (b, i, CB_SB_G + h))],
        out_specs=pl.BlockSpec((1, rows, LANES), lambda b, h, i: (b, i, h)),
        scratch_shapes=[pltpu.VMEM((SB_QT * 2 * TQ, LANES), jnp.float32)],
        compiler_params=pltpu.CompilerParams(vmem_limit_bytes=VMEM_LIMIT),
        name="sb_attn",
    )(proj, proj, proj, proj)


def _df_kernel(q_ref, k_ref, v_ref, g_ref, slope_ref, lq1_ref, lk1_ref, lq2_ref, lk2_ref, sg_ref,
               qg_ref, kg_ref, o_ref, m_s, l_s, acc, *, lam_init):
    qi = pl.program_id(2)
    q = q_ref[0]
    lane = lax.broadcasted_iota(jnp.int32, (1, LANES), 1)
    first = lane < HEAD_DIM
    zero = jnp.zeros_like(q)
    qs = jnp.concatenate([jnp.where(first, q, zero), jnp.where(first, zero, q)], axis=0)
    slope = slope_ref[0] * LOG2E
    row = lax.broadcasted_iota(jnp.int32, (DF_TQ, DF_TK), 0)
    col = lax.broadcasted_iota(jnp.int32, (DF_TQ, DF_TK), 1)
    causal = col <= row
    diag_mask = jnp.concatenate([causal, causal], axis=0)

    qk_bound = (jnp.max(jnp.abs(qg_ref[...])) * jnp.max(jnp.abs(kg_ref[...]))
                * (math.sqrt(HEAD_DIM) * LOG2E * DF_BOUND_MARGIN))
    slope_s = jnp.max(slope)

    def load(first_kb, width):
        start = pl.multiple_of(first_kb * DF_TK, DF_TK)
        rel = lax.broadcasted_iota(jnp.int32, (1, width), 1).astype(jnp.float32)
        off = ((first_kb - qi) * DF_TK).astype(jnp.float32)
        bias = jnp.concatenate([slope] * (width // LANES), axis=1) * (rel + off)
        return k_ref[0, pl.ds(start, width), :], v_ref[0, pl.ds(start, width), :], bias

    def last_rel(first_kb, width):
        return ((first_kb - qi) * DF_TK + (width - 1)).astype(jnp.float32)

    def sweep(n_steps, first_kb_of_step, width, tile, needed):
        def cond(state):
            i, live = state
            return jnp.logical_and(i < n_steps, live)

        def body(state):
            i, _ = state
            tile(first_kb_of_step(i), width, None)
            return i + 1, needed(first_kb_of_step(i + 1), width)

        lax.while_loop(cond, body, (jnp.int32(0), needed(first_kb_of_step(jnp.int32(0)), width)))

    def sweep_all(tile, needed):
        tile(qi, DF_TK, diag_mask)
        rem = lax.rem(qi, DF_GROUP)
        sweep(rem, lambda i: qi - 1 - i, DF_TK, tile, needed)
        sweep(qi // DF_GROUP, lambda i: qi - rem - (i + 1) * DF_GROUP, DF_GROUP * DF_TK, tile, needed)

    def fixed_shift():
        rowpos = lax.broadcasted_iota(jnp.int32, (DF_TQ, LANES), 0).astype(jnp.float32)
        shift = qk_bound + slope * jnp.concatenate([rowpos, rowpos], axis=0)
        l_s[...] = jnp.zeros((2 * DF_TQ, LANES), jnp.float32)
        acc[...] = jnp.zeros((2 * DF_TQ, LANES), jnp.float32)

        def tile(first_kb, width, mask):
            reps = width // LANES
            kblk, vblk, bias = load(first_kb, width)
            s = (lax.dot_general(qs, kblk, _NT, preferred_element_type=jnp.float32)
                 + bias - jnp.concatenate([shift] * reps, axis=1))
            if mask is not None:
                s = jnp.where(mask, s, NEG_BIG)
            p = jnp.exp2(s)
            part = p[:, :LANES]
            for r in range(1, reps):
                part = part + p[:, r * LANES:(r + 1) * LANES]
            l_s[...] += part
            acc[...] += jnp.dot(p.astype(jnp.bfloat16), vblk, preferred_element_type=jnp.float32)

        def needed(first_kb, width):
            return slope_s * last_rel(first_kb, width) > DF_DEAD_LOG2

        sweep_all(tile, needed)
        l = jnp.sum(l_s[...], axis=-1, keepdims=True)
        return l[:DF_TQ], l[DF_TQ:]

    def running_max():
        m_s[...] = jnp.full((2 * DF_TQ, LANES), NEG_BIG, jnp.float32)
        l_s[...] = jnp.zeros((2 * DF_TQ, LANES), jnp.float32)
        acc[...] = jnp.zeros((2 * DF_TQ, LANES), jnp.float32)

        def tile(first_kb, width, mask):
            reps = width // LANES
            kblk, vblk, bias = load(first_kb, width)
            s = lax.dot_general(qs, kblk, _NT, preferred_element_type=jnp.float32) + bias
            if mask is not None:
                s = jnp.where(mask, s, NEG_BIG)
            m_old = m_s[...]
            m_new = jnp.maximum(m_old, jnp.max(s, axis=-1, keepdims=True))
            p = jnp.exp2(s - jnp.concatenate([m_new] * reps, axis=1))
            alpha = jnp.exp2(m_old - m_new)
            l_s[...] = alpha * l_s[...] + jnp.sum(p, axis=-1, keepdims=True)
            acc[...] = alpha * acc[...] + jnp.dot(p.astype(jnp.bfloat16), vblk,
                                                  preferred_element_type=jnp.float32)
            m_s[...] = m_new

        def needed(first_kb, width):
            return (qk_bound + slope_s * last_rel(first_kb, width) - jnp.min(m_s[...])
                    > DF_DEAD_LOG2)

        sweep_all(tile, needed)
        return l_s[:DF_TQ, :1], l_s[DF_TQ:, :1]

    l1, l2 = lax.cond(qk_bound < DF_FIXED_SHIFT_MAX, fixed_shift, running_max)

    lam = (jnp.exp(jnp.sum(lq1_ref[...] * lk1_ref[...], keepdims=True))
           - jnp.exp(jnp.sum(lq2_ref[...] * lk2_ref[...], keepdims=True)) + lam_init)
    o = acc[:DF_TQ, :] / l1 - lam * (acc[DF_TQ:, :] / l2)
    ms = jnp.mean(o * o, axis=-1, keepdims=True)
    o = o * lax.rsqrt(ms + EPS) * sg_ref[...] * (1.0 - lam_init)
    g = g_ref[0].astype(jnp.float32)
    o_ref[0] = (o * _silu(g)).astype(o_ref.dtype)


def _df_attention(proj, slopes, lq1, lk1, lq2, lk2, subln_g, q_norm_g, k_norm_g, lam_init):
    bsz, s, _ = proj.shape
    nq = s // DF_TQ
    vec = pl.BlockSpec((1, HEAD_DIM), lambda b, h, i: (0, 0))
    stat = pltpu.VMEM((2 * DF_TQ, LANES), jnp.float32)
    return pl.pallas_call(
        functools.partial(_df_kernel, lam_init=lam_init),
        out_shape=jax.ShapeDtypeStruct((bsz, s, DF_WIDTH), jnp.bfloat16),
        grid=(bsz, DF_HEADS, nq),
        in_specs=[pl.BlockSpec((1, DF_TQ, LANES), lambda b, h, i: (b, i, CB_DF_Q + h)),
                  pl.BlockSpec((1, s, LANES), lambda b, h, i: (b, 0, CB_DF_K + h)),
                  pl.BlockSpec((1, s, LANES), lambda b, h, i: (b, 0, CB_DF_V + h)),
                  pl.BlockSpec((1, DF_TQ, LANES), lambda b, h, i: (b, i, CB_DF_G + h)),
                  pl.BlockSpec((1, 1, LANES), lambda b, h, i: (h, 0, 0)),
                  vec, vec, vec, vec,
                  pl.BlockSpec((1, LANES), lambda b, h, i: (0, 0)),
                  vec, vec],
        out_specs=pl.BlockSpec((1, DF_TQ, LANES), lambda b, h, i: (b, i, h)),
        scratch_shapes=[stat, stat, stat],
        compiler_params=pltpu.CompilerParams(vmem_limit_bytes=VMEM_LIMIT),
        name="df_attn",
    )(proj, proj, proj, proj, slopes, lq1, lk1, lq2, lk2, subln_g, q_norm_g, k_norm_g)


def _out_proj_kernel(x_ref, sb_ref, df_ref, w_ref, gate_ref, o_ref):
    y = (jnp.dot(sb_ref[0], w_ref[:SB_WIDTH, :], preferred_element_type=jnp.float32)
         + jnp.dot(df_ref[0], w_ref[SB_WIDTH:, :], preferred_element_type=jnp.float32))
    o_ref[0] = x_ref[0] + gate_ref[0] * y


def _out_proj(x, sb_o, df_o, w_out_bf16, gate):
    bsz, s, d = x.shape
    tm = ROWS_OUT
    return pl.pallas_call(
        _out_proj_kernel,
        out_shape=jax.ShapeDtypeStruct((bsz, s, d), jnp.float32),
        grid=(bsz, s // tm),
        in_specs=[pl.BlockSpec((1, tm, d), lambda b, i: (b, i, 0)),
                  pl.BlockSpec((1, tm, SB_WIDTH), lambda b, i: (b, i, 0)),
                  pl.BlockSpec((1, tm, DF_WIDTH), lambda b, i: (b, i, 0)),
                  pl.BlockSpec((SB_WIDTH + DF_WIDTH, d), lambda b, i: (0, 0)),
                  pl.BlockSpec((1, 1, d), lambda b, i: (b, 0, 0))],
        out_specs=pl.BlockSpec((1, tm, d), lambda b, i: (b, i, 0)),
        compiler_params=pltpu.CompilerParams(vmem_limit_bytes=VMEM_LIMIT),
        name="out_proj",
    )(x, sb_o, df_o, w_out_bf16, gate.reshape(bsz, 1, d))


def _layer(x, c, layer_idx, norm_g, w_ada, b_ada, w_in, q_norm_g, k_norm_g,
           lambda_q1, lambda_k1, lambda_q2, lambda_k2, subln_g, w_out):
    d = x.shape[-1]
    lam_init = 0.8 - 0.6 * math.exp(-0.3 * layer_idx)
    mod = _adaln_mod(c, w_ada, b_ada)
    shift, scale, gate = mod[:, :d], mod[:, d:2 * d], mod[:, 2 * d:]
    proj = _in_proj(x, norm_g, shift, scale, w_in.astype(jnp.bfloat16), q_norm_g, k_norm_g)
    sb_o = _sb_attention(proj)
    slopes = jnp.asarray([2.0 ** (-8.0 * (h + 1) / DF_HEADS) for h in range(DF_HEADS)], jnp.float32)
    slopes = jnp.broadcast_to(slopes[:, None, None], (DF_HEADS, 1, LANES))
    row = lambda v: v.reshape(1, -1)
    df_o = _df_attention(proj, slopes, row(lambda_q1), row(lambda_k1), row(lambda_q2),
                         row(lambda_k2), row(subln_g), row(q_norm_g), row(k_norm_g), lam_init)
    return _out_proj(x, sb_o, df_o, w_out.astype(jnp.bfloat16), gate)


@jax.jit
def kernel(x, c, norm_g, w_ada, b_ada, w_in, q_norm_g, k_norm_g, lambda_q1, lambda_k1,
           lambda_q2, lambda_k2, subln_g, w_out):
    for l in range(norm_g.shape[0]):
        x = _layer(x, c, l, norm_g[l], w_ada[l], b_ada[l], w_in[l], q_norm_g[l], k_norm_g[l],
                   lambda_q1[l], lambda_k1[l], lambda_q2[l], lambda_k2[l], subln_g[l], w_out[l])
    return x
```

```python
import functools
import math

import jax
import jax.numpy as jnp
from jax import lax
from jax.experimental import pallas as pl
from jax.experimental.pallas import tpu as pltpu

D_MODEL = 1024
SB_HEADS = 8
DF_HEADS = 4
HEAD_DIM = 64
LANES = 128
SB_WIDTH = SB_HEADS * HEAD_DIM
DF_WIDTH = DF_HEADS * 2 * HEAD_DIM
IN_COLS = 4 * SB_WIDTH + 4 * DF_WIDTH
EPS = 1e-6
NEG_BIG = -1e30

CB_SB_Q, CB_SB_K, CB_SB_V, CB_SB_G = 0, 4, 8, 12
CB_DF_Q, CB_DF_K, CB_DF_V, CB_DF_G = 16, 20, 24, 28

ROWS_PROJ = 512
ROWS_OUT = 1024
TQ = 512
TK = 512
DF_TQ = 1024
DF_TK = 1024
SB_QT = 2
SB_FIRST = 2
SB_GROUP = 4
DF_GROUP = 1
LOG2E = math.log2(math.e)
SB_DEAD_LOG2 = -150.0
DF_DEAD_LOG2 = -150.0
DF_FIXED_SHIFT_MAX = 48.0
DF_BOUND_MARGIN = 1.02
PROJ_CHUNK = 512
VMEM_LIMIT = 48 * 1024 * 1024

_NT = (((1,), (1,)), ((), ()))


def _silu(g):
    return g / (1.0 + jnp.exp(-g))


def _adaln_kernel(ct_ref, w_ref, b_ref, o_ref):
    w = w_ref[...]
    rows = []
    for b in range(o_ref.shape[0]):
        col = ct_ref[:, b:b + 1]
        rows.append(jnp.sum(col * w, axis=0, keepdims=True))
    o_ref[...] = jnp.concatenate(rows, axis=0) + b_ref[...]


def _adaln_mod(c, w_ada, b_ada):
    bsz, d = c.shape
    n = w_ada.shape[1]
    tn = 512
    return pl.pallas_call(
        _adaln_kernel,
        out_shape=jax.ShapeDtypeStruct((bsz, n), jnp.float32),
        grid=(n // tn,),
        in_specs=[pl.BlockSpec((d, bsz), lambda j: (0, 0)),
                  pl.BlockSpec((d, tn), lambda j: (0, j)),
                  pl.BlockSpec((1, tn), lambda j: (0, j))],
        out_specs=pl.BlockSpec((bsz, tn), lambda j: (0, j)),
        name="adaln_mod",
    )(c.T, w_ada, b_ada.reshape(1, n))


def _group_rmsnorm(y, gain):
    outs = []
    lane = lax.broadcasted_iota(jnp.int32, (1, LANES), 1)
    lo = lane < HEAD_DIM
    for j in range(y.shape[1] // LANES):
        blk = y[:, j * LANES:(j + 1) * LANES]
        sq = blk * blk
        s_lo = jnp.sum(jnp.where(lo, sq, 0.0), axis=-1, keepdims=True)
        s_hi = jnp.sum(jnp.where(lo, 0.0, sq), axis=-1, keepdims=True)
        ms = jnp.where(lo, s_lo, s_hi) * (1.0 / HEAD_DIM)
        outs.append(blk * lax.rsqrt(ms + EPS) * gain)
    return jnp.concatenate(outs, axis=1)


def _in_proj_kernel(x_ref, ng_ref, shift_ref, scale_ref, w_ref, qg_ref, kg_ref, o_ref):
    x = x_ref[0]
    ms = jnp.mean(x * x, axis=-1, keepdims=True)
    h = x * lax.rsqrt(ms + EPS) * ng_ref[...]
    h = (h * (1.0 + scale_ref[0]) + shift_ref[0]).astype(jnp.bfloat16)
    inv = LOG2E / math.sqrt(HEAD_DIM)
    for ci in range(IN_COLS // PROJ_CHUNK):
        c0 = ci * PROJ_CHUNK
        y = jnp.dot(h, w_ref[:, c0:c0 + PROJ_CHUNK], preferred_element_type=jnp.float32)
        cb = c0 // LANES
        if cb == CB_SB_Q:
            y = y * inv
        elif cb == CB_DF_Q:
            y = _group_rmsnorm(y, qg_ref[...]) * inv
        elif cb == CB_DF_K:
            y = _group_rmsnorm(y, kg_ref[...])
        o_ref[0, :, c0:c0 + PROJ_CHUNK] = y.astype(o_ref.dtype)


def _in_proj(x, norm_g, shift, scale, w_in_bf16, q_norm_g, k_norm_g):
    bsz, s, d = x.shape
    tm = ROWS_PROJ
    qg = jnp.tile(q_norm_g.reshape(1, HEAD_DIM), (1, 2))
    kg = jnp.tile(k_norm_g.reshape(1, HEAD_DIM), (1, 2))
    return pl.pallas_call(
        _in_proj_kernel,
        out_shape=jax.ShapeDtypeStruct((bsz, s, IN_COLS), jnp.bfloat16),
        grid=(bsz, s // tm),
        in_specs=[pl.BlockSpec((1, tm, d), lambda b, i: (b, i, 0)),
                  pl.BlockSpec((1, d), lambda b, i: (0, 0)),
                  pl.BlockSpec((1, 1, d), lambda b, i: (b, 0, 0)),
                  pl.BlockSpec((1, 1, d), lambda b, i: (b, 0, 0)),
                  pl.BlockSpec((d, IN_COLS), lambda b, i: (0, 0)),
                  pl.BlockSpec((1, LANES), lambda b, i: (0, 0)),
                  pl.BlockSpec((1, LANES), lambda b, i: (0, 0))],
        out_specs=pl.BlockSpec((1, tm, IN_COLS), lambda b, i: (b, i, 0)),
        compiler_params=pltpu.CompilerParams(vmem_limit_bytes=VMEM_LIMIT),
        name="in_proj",
    )(x, norm_g.reshape(1, d), shift.reshape(bsz, 1, d), scale.reshape(bsz, 1, d),
      w_in_bf16, qg, kg)


def _sb_tile(qm, kblk, vblk, cum, carry, mask):
    z = lax.dot_general(qm, kblk, _NT, preferred_element_type=jnp.float32)
    nz = -z
    l1p = jnp.log(1.0 + jnp.exp2(jnp.minimum(z, nz))) * LOG2E
    lg = jnp.minimum(nz, 0.0) - l1p
    if mask is not None:
        lg = jnp.where(mask, lg, 0.0)
    rem = jnp.dot(lg.astype(jnp.bfloat16), cum, preferred_element_type=jnp.float32)
    a = jnp.exp2(lg + z + rem + carry)
    if mask is not None:
        a = jnp.where(mask, a, 0.0)
    o = jnp.dot(a.astype(jnp.bfloat16), vblk, preferred_element_type=jnp.float32)
    return o, carry + jnp.sum(lg, axis=-1, keepdims=True)


def _sb_kernel(q_ref, k_ref, v_ref, g_ref, o_ref, acc):
    step = pl.program_id(2)
    lane = lax.broadcasted_iota(jnp.int32, (1, LANES), 1)
    first = lane < HEAD_DIM
    row = lax.broadcasted_iota(jnp.int32, (TK, TK), 0)
    col = lax.broadcasted_iota(jnp.int32, (TK, TK), 1)
    cum = (row > col).astype(jnp.bfloat16)
    strict = col < row
    diag_mask = jnp.concatenate([strict, strict], axis=0)

    def stacked(u):
        q = q_ref[0, u * TQ:(u + 1) * TQ, :]
        zero = jnp.zeros_like(q)
        return jnp.concatenate([jnp.where(first, q, zero), jnp.where(first, zero, q)], axis=0)

    qs = [stacked(u) for u in range(SB_QT)]
    acc_of = [acc.at[u * 2 * TQ:(u + 1) * 2 * TQ, :] for u in range(SB_QT)]

    def tiles(qm, kbs, carry, masks):
        total = None
        for kb, mask in zip(kbs, masks):
            start = pl.multiple_of(kb * TK, TK)
            o, carry = _sb_tile(qm, k_ref[0, pl.ds(start, TK), :], v_ref[0, pl.ds(start, TK), :],
                                cum, carry, mask)
            total = o if total is None else total + o
        return total, carry

    zero_carry = jnp.zeros((2 * TQ, 1), jnp.float32)

    def first_block(first_step):
        carries = []
        for u in range(SB_QT):
            qt = step * SB_QT + u
            n = min(SB_FIRST, u + 1) if first_step else SB_FIRST
            o, carry = tiles(qs[u], [qt - t for t in range(n)], zero_carry,
                             [diag_mask] + [None] * (n - 1))
            acc_of[u][...] = o
            carries.append(carry)
        return tuple(carries)

    carries = lax.cond(step > 0, lambda: first_block(False), lambda: first_block(True))

    def alive(carry):
        return jnp.max(carry) > SB_DEAD_LOG2

    def sweep(u, n_steps, kbs_of_step, carry):
        def cond(state):
            i, _, live = state
            return jnp.logical_and(i < n_steps, live)

        def body(state):
            i, carry, _ = state
            kbs = kbs_of_step(i)
            o, carry = tiles(qs[u], kbs, carry, [None] * len(kbs))
            acc_of[u][...] += o
            return i + 1, carry, alive(carry)

        return lax.while_loop(cond, body, (jnp.int32(0), carry, alive(carry)))[1]

    any_alive = alive(carries[0])
    for carry in carries[1:]:
        any_alive = jnp.logical_or(any_alive, alive(carry))

    @pl.when(any_alive)
    def _():
        for u in range(SB_QT):
            left = jnp.maximum(step * SB_QT + u - (SB_FIRST - 1), 0)
            rem = lax.rem(left, SB_GROUP)
            carry = sweep(u, rem, lambda i, left=left: [left - 1 - i], carries[u])
            sweep(u, left // SB_GROUP,
                  lambda i, left=left, rem=rem: [left - rem - i * SB_GROUP - 1 - t
                                                 for t in range(SB_GROUP)], carry)

    for u in range(SB_QT):
        out = jnp.where(first, acc_of[u][:TQ, :], acc_of[u][TQ:, :])
        g = g_ref[0, u * TQ:(u + 1) * TQ, :].astype(jnp.float32)
        o_ref[0, u * TQ:(u + 1) * TQ, :] = (out * _silu(g)).astype(o_ref.dtype)


def _sb_attention(proj):
    bsz, s, _ = proj.shape
    rows = SB_QT * TQ
    return pl.pallas_call(
        _sb_kernel,
        out_shape=jax.ShapeDtypeStruct((bsz, s, SB_WIDTH), jnp.bfloat16),
        grid=(bsz, SB_WIDTH // LANES, s // rows),
        in_specs=[pl.BlockSpec((1, rows, LANES), lambda b, h, i: (b, i, CB_SB_Q + h)),
                  pl.BlockSpec((1, s, LANES), lambda b, h, i: (b, 0, CB_SB_K + h)),
                  pl.BlockSpec((1, s, LANES), lambda b, h, i: (b, 0, CB_SB_V + h)),
                  pl.BlockSpec((1, rows, LANES), lambda b, h, i: (b, i, CB_SB_G + h))],
        out_specs=pl.BlockSpec((1, rows, LANES), lambda b, h, i: (b, i, h)),
        scratch_shapes=[pltpu.VMEM((SB_QT * 2 * TQ, LANES), jnp.float32)],
        compiler_params=pltpu.CompilerParams(vmem_limit_bytes=VMEM_LIMIT),
        name="sb_attn",
    )(proj, proj, proj, proj)


def _df_kernel(q_ref, k_ref, v_ref, g_ref, slope_ref, lq1_ref, lk1_ref, lq2_ref, lk2_ref, sg_ref,
               qg_ref, kg_ref, o_ref, m_s, l_s, acc, *, lam_init):
    qi = pl.program_id(2)
    q = q_ref[0]
    lane = lax.broadcasted_iota(jnp.int32, (1, LANES), 1)
    first = lane < HEAD_DIM
    zero = jnp.zeros_like(q)
    qs = jnp.concatenate([jnp.where(first, q, zero), jnp.where(first, zero, q)], axis=0)
    slope = slope_ref[0] * LOG2E
    row = lax.broadcasted_iota(jnp.int32, (DF_TQ, DF_TK), 0)
    col = lax.broadcasted_iota(jnp.int32, (DF_TQ, DF_TK), 1)
    causal = col <= row
    diag_mask = jnp.concatenate([causal, causal], axis=0)

    qk_bound = (jnp.max(jnp.abs(qg_ref[...])) * jnp.max(jnp.abs(kg_ref[...]))
                * (math.sqrt(HEAD_DIM) * LOG2E * DF_BOUND_MARGIN))
    slope_s = jnp.max(slope)

    def load(first_kb, width):
        start = pl.multiple_of(first_kb * DF_TK, DF_TK)
        rel = lax.broadcasted_iota(jnp.int32, (1, width), 1).astype(jnp.float32)
        off = ((first_kb - qi) * DF_TK).astype(jnp.float32)
        bias = jnp.concatenate([slope] * (width // LANES), axis=1) * (rel + off)
        return k_ref[0, pl.ds(start, width), :], v_ref[0, pl.ds(start, width), :], bias

    def last_rel(first_kb, width):
        return ((first_kb - qi) * DF_TK + (width - 1)).astype(jnp.float32)

    def sweep(n_steps, first_kb_of_step, width, tile, needed):
        def cond(state):
            i, live = state
            return jnp.logical_and(i < n_steps, live)

        def body(state):
            i, _ = state
            tile(first_kb_of_step(i), width, None)
            return i + 1, needed(first_kb_of_step(i + 1), width)

        lax.while_loop(cond, body, (jnp.int32(0), needed(first_kb_of_step(jnp.int32(0)), width)))

    def sweep_all(tile, needed):
        tile(qi, DF_TK, diag_mask)
        rem = lax.rem(qi, DF_GROUP)
        sweep(rem, lambda i: qi - 1 - i, DF_TK, tile, needed)
        sweep(qi // DF_GROUP, lambda i: qi - rem - (i + 1) * DF_GROUP, DF_GROUP * DF_TK, tile, needed)

    def fixed_shift():
        rowpos = lax.broadcasted_iota(jnp.int32, (DF_TQ, LANES), 0).astype(jnp.float32)
        shift = qk_bound + slope * jnp.concatenate([rowpos, rowpos], axis=0)
        l_s[...] = jnp.zeros((2 * DF_TQ, LANES), jnp.float32)
        acc[...] = jnp.zeros((2 * DF_TQ, LANES), jnp.float32)

        def tile(first_kb, width, mask):
            reps = width // LANES
            kblk, vblk, bias = load(first_kb, width)
            s = (lax.dot_general(qs, kblk, _NT, preferred_element_type=jnp.float32)
                 + bias - jnp.concatenate([shift] * reps, axis=1))
            if mask is not None:
                s = jnp.where(mask, s, NEG_BIG)
            p = jnp.exp2(s)
            part = p[:, :LANES]
            for r in range(1, reps):
                part = part + p[:, r * LANES:(r + 1) * LANES]
            l_s[...] += part
            acc[...] += jnp.dot(p.astype(jnp.bfloat16), vblk, preferred_element_type=jnp.float32)

        def needed(first_kb, width):
            return slope_s * last_rel(first_kb, width) > DF_DEAD_LOG2

        sweep_all(tile, needed)
        l = jnp.sum(l_s[...], axis=-1, keepdims=True)
        return l[:DF_TQ], l[DF_TQ:]

    def running_max():
        m_s[...] = jnp.full((2 * DF_TQ, LANES), NEG_BIG, jnp.float32)
        l_s[...] = jnp.zeros((2 * DF_TQ, LANES), jnp.float32)
        acc[...] = jnp.zeros((2 * DF_TQ, LANES), jnp.float32)

        def tile(first_kb, width, mask):
            reps = width // LANES
            kblk, vblk, bias = load(first_kb, width)
            s = lax.dot_general(qs, kblk, _NT, preferred_element_type=jnp.float32) + bias
            if mask is not None:
                s = jnp.where(mask, s, NEG_BIG)
            m_old = m_s[...]
            m_new = jnp.maximum(m_old, jnp.max(s, axis=-1, keepdims=True))
            p = jnp.exp2(s - jnp.concatenate([m_new] * reps, axis=1))
            alpha = jnp.exp2(m_old - m_new)
            l_s[...] = alpha * l_s[...] + jnp.sum(p, axis=-1, keepdims=True)
            acc[...] = alpha * acc[...] + jnp.dot(p.astype(jnp.bfloat16), vblk,
                                                  preferred_element_type=jnp.float32)
            m_s[...] = m_new

        def needed(first_kb, width):
            return (qk_bound + slope_s * last_rel(first_kb, width) - jnp.min(m_s[...])
                    > DF_DEAD_LOG2)

        sweep_all(tile, needed)
        return l_s[:DF_TQ, :1], l_s[DF_TQ:, :1]

    l1, l2 = lax.cond(qk_bound < DF_FIXED_SHIFT_MAX, fixed_shift, running_max)

    lam = (jnp.exp(jnp.sum(lq1_ref[...] * lk1_ref[...], keepdims=True))
           - jnp.exp(jnp.sum(lq2_ref[...] * lk2_ref[...], keepdims=True)) + lam_init)
    o = acc[:DF_TQ, :] / l1 - lam * (acc[DF_TQ:, :] / l2)
    ms = jnp.mean(o * o, axis=-1, keepdims=True)
    o = o * lax.rsqrt(ms + EPS) * sg_ref[...] * (1.0 - lam_init)
    g = g_ref[0].astype(jnp.float32)
    o_ref[0] = (o * _silu(g)).astype(o_ref.dtype)


def _df_attention(proj, slopes, lq1, lk1, lq2, lk2, subln_g, q_norm_g, k_norm_g, lam_init):
    bsz, s, _ = proj.shape
    nq = s // DF_TQ
    vec = pl.BlockSpec((1, HEAD_DIM), lambda b, h, i: (0, 0))
    stat = pltpu.VMEM((2 * DF_TQ, LANES), jnp.float32)
    return pl.pallas_call(
        functools.partial(_df_kernel, lam_init=lam_init),
        out_shape=jax.ShapeDtypeStruct((bsz, s, DF_WIDTH), jnp.bfloat16),
        grid=(bsz, DF_HEADS, nq),
        in_specs=[pl.BlockSpec((1, DF_TQ, LANES), lambda b, h, i: (b, i, CB_DF_Q + h)),
                  pl.BlockSpec((1, s, LANES), lambda b, h, i: (b, 0, CB_DF_K + h)),
                  pl.BlockSpec((1, s, LANES), lambda b, h, i: (b, 0, CB_DF_V + h)),
                  pl.BlockSpec((1, DF_TQ, LANES), lambda b, h, i: (b, i, CB_DF_G + h)),
                  pl.BlockSpec((1, 1, LANES), lambda b, h, i: (h, 0, 0)),
                  vec, vec, vec, vec,
                  pl.BlockSpec((1, LANES), lambda b, h, i: (0, 0)),
                  vec, vec],
        out_specs=pl.BlockSpec((1, DF_TQ, LANES), lambda b, h, i: (b, i, h)),
        scratch_shapes=[stat, stat, stat],
        compiler_params=pltpu.CompilerParams(vmem_limit_bytes=VMEM_LIMIT),
        name="df_attn",
    )(proj, proj, proj, proj, slopes, lq1, lk1, lq2, lk2, subln_g, q_norm_g, k_norm_g)


def _out_proj_kernel(x_ref, sb_ref, df_ref, w_ref, gate_ref, o_ref):
    y = (jnp.dot(sb_ref[0], w_ref[:SB_WIDTH, :], preferred_element_type=jnp.float32)
         + jnp.dot(df_ref[0], w_ref[SB_WIDTH:, :], preferred_element_type=jnp.float32))
    o_ref[0] = x_ref[0] + gate_ref[0] * y


def _out_proj(x, sb_o, df_o, w_out_bf16, gate):
    bsz, s, d = x.shape
    tm = ROWS_OUT
    return pl.pallas_call(
        _out_proj_kernel,
        out_shape=jax.ShapeDtypeStruct((bsz, s, d), jnp.float32),
        grid=(bsz, s // tm),
        in_specs=[pl.BlockSpec((1, tm, d), lambda b, i: (b, i, 0)),
                  pl.BlockSpec((1, tm, SB_WIDTH), lambda b, i: (b, i, 0)),
                  pl.BlockSpec((1, tm, DF_WIDTH), lambda b, i: (b, i, 0)),
                  pl.BlockSpec((SB_WIDTH + DF_WIDTH, d), lambda b, i: (0, 0)),
                  pl.BlockSpec((1, 1, d), lambda b, i: (b, 0, 0))],
        out_specs=pl.BlockSpec((1, tm, d), lambda b, i: (b, i, 0)),
        compiler_params=pltpu.CompilerParams(vmem_limit_bytes=VMEM_LIMIT),
        name="out_proj",
    )(x, sb_o, df_o, w_out_bf16, gate.reshape(bsz, 1, d))


def _layer(x, c, layer_idx, norm_g, w_ada, b_ada, w_in, q_norm_g, k_norm_g,
           lambda_q1, lambda_k1, lambda_q2, lambda_k2, subln_g, w_out):
    d = x.shape[-1]
    lam_init = 0.8 - 0.6 * math.exp(-0.3 * layer_idx)
    mod = _adaln_mod(c, w_ada, b_ada)
    shift, scale, gate = mod[:, :d], mod[:, d:2 * d], mod[:, 2 * d:]
    proj = _in_proj(x, norm_g, shift, scale, w_in.astype(jnp.bfloat16), q_norm_g, k_norm_g)
    sb_o = _sb_attention(proj)
    slopes = jnp.asarray([2.0 ** (-8.0 * (h + 1) / DF_HEADS) for h in range(DF_HEADS)], jnp.float32)
    slopes = jnp.broadcast_to(slopes[:, None, None], (DF_HEADS, 1, LANES))
    row = lambda v: v.reshape(1, -1)
    df_o = _df_attention(proj, slopes, row(lambda_q1), row(lambda_k1), row(lambda_q2),
                         row(lambda_k2), row(subln_g), row(q_norm_g), row(k_norm_g), lam_init)
    return _out_proj(x, sb_o, df_o, w_out.astype(jnp.bfloat16), gate)


@jax.jit
def kernel(x, c, norm_g, w_ada, b_ada, w_in, q_norm_g, k_norm_g, lambda_q1, lambda_k1,
           lambda_q2, lambda_k2, subln_g, w_out):
    for l in range(norm_g.shape[0]):
        x = _layer(x, c, l, norm_g[l], w_ada[l], b_ada[l], w_in[l], q_norm_g[l], k_norm_g[l],
                   lambda_q1[l], lambda_k1[l], lambda_q2[l], lambda_k2[l], subln_g[l], w_out[l])
    return x
```

```python
import functools
import math

import jax
import jax.numpy as jnp
from jax import lax
from jax.experimental import pallas as pl
from jax.experimental.pallas import tpu as pltpu

D_MODEL = 1024
SB_HEADS = 8
DF_HEADS = 4
HEAD_DIM = 64
LANES = 128
SB_WIDTH = SB_HEADS * HEAD_DIM
DF_WIDTH = DF_HEADS * 2 * HEAD_DIM
IN_COLS = 4 * SB_WIDTH + 4 * DF_WIDTH
EPS = 1e-6
NEG_BIG = -1e30

CB_SB_Q, CB_SB_K, CB_SB_V, CB_SB_G = 0, 4, 8, 12
CB_DF_Q, CB_DF_K, CB_DF_V, CB_DF_G = 16, 20, 24, 28

ROWS_PROJ = 512
ROWS_OUT = 1024
TQ = 256
TK = 256
DF_TQ = 1024
DF_TK = 1024
SB_QT = 4
SB_FIRST = 3
SB_GROUP = 4
DF_GROUP = 1
LOG2E = math.log2(math.e)
SB_DEAD_LOG2 = -150.0
DF_DEAD_LOG2 = -150.0
DF_FIXED_SHIFT_MAX = 48.0
DF_BOUND_MARGIN = 1.02
PROJ_CHUNK = 512
VMEM_LIMIT = 48 * 1024 * 1024

_NT = (((1,), (1,)), ((), ()))


def _silu(g):
    return g / (1.0 + jnp.exp(-g))


def _adaln_kernel(ct_ref, w_ref, b_ref, o_ref):
    w = w_ref[...]
    rows = []
    for b in range(o_ref.shape[0]):
        col = ct_ref[:, b:b + 1]
        rows.append(jnp.sum(col * w, axis=0, keepdims=True))
    o_ref[...] = jnp.concatenate(rows, axis=0) + b_ref[...]


def _adaln_mod(c, w_ada, b_ada):
    bsz, d = c.shape
    n = w_ada.shape[1]
    tn = 512
    return pl.pallas_call(
        _adaln_kernel,
        out_shape=jax.ShapeDtypeStruct((bsz, n), jnp.float32),
        grid=(n // tn,),
        in_specs=[pl.BlockSpec((d, bsz), lambda j: (0, 0)),
                  pl.BlockSpec((d, tn), lambda j: (0, j)),
                  pl.BlockSpec((1, tn), lambda j: (0, j))],
        out_specs=pl.BlockSpec((bsz, tn), lambda j: (0, j)),
        name="adaln_mod",
    )(c.T, w_ada, b_ada.reshape(1, n))


def _group_rmsnorm(y, gain):
    outs = []
    lane = lax.broadcasted_iota(jnp.int32, (1, LANES), 1)
    lo = lane < HEAD_DIM
    for j in range(y.shape[1] // LANES):
        blk = y[:, j * LANES:(j + 1) * LANES]
        sq = blk * blk
        s_lo = jnp.sum(jnp.where(lo, sq, 0.0), axis=-1, keepdims=True)
        s_hi = jnp.sum(jnp.where(lo, 0.0, sq), axis=-1, keepdims=True)
        ms = jnp.where(lo, s_lo, s_hi) * (1.0 / HEAD_DIM)
        outs.append(blk * lax.rsqrt(ms + EPS) * gain)
    return jnp.concatenate(outs, axis=1)


def _in_proj_kernel(x_ref, ng_ref, shift_ref, scale_ref, w_ref, qg_ref, kg_ref, o_ref):
    x = x_ref[0]
    ms = jnp.mean(x * x, axis=-1, keepdims=True)
    h = x * lax.rsqrt(ms + EPS) * ng_ref[...]
    h = (h * (1.0 + scale_ref[0]) + shift_ref[0]).astype(jnp.bfloat16)
    inv = LOG2E / math.sqrt(HEAD_DIM)
    for ci in range(IN_COLS // PROJ_CHUNK):
        c0 = ci * PROJ_CHUNK
        y = jnp.dot(h, w_ref[:, c0:c0 + PROJ_CHUNK], preferred_element_type=jnp.float32)
        cb = c0 // LANES
        if cb == CB_SB_Q:
            y = y * inv
        elif cb == CB_DF_Q:
            y = _group_rmsnorm(y, qg_ref[...]) * inv
        elif cb == CB_DF_K:
            y = _group_rmsnorm(y, kg_ref[...])
        o_ref[0, :, c0:c0 + PROJ_CHUNK] = y.astype(o_ref.dtype)


def _in_proj(x, norm_g, shift, scale, w_in_bf16, q_norm_g, k_norm_g):
    bsz, s, d = x.shape
    tm = ROWS_PROJ
    assert SB_WIDTH % PROJ_CHUNK == 0 and DF_WIDTH % PROJ_CHUNK == 0 and s % tm == 0
    qg = jnp.tile(q_norm_g.reshape(1, HEAD_DIM), (1, 2))
    kg = jnp.tile(k_norm_g.reshape(1, HEAD_DIM), (1, 2))
    return pl.pallas_call(
        _in_proj_kernel,
        out_shape=jax.ShapeDtypeStruct((bsz, s, IN_COLS), jnp.bfloat16),
        grid=(bsz, s // tm),
        in_specs=[pl.BlockSpec((1, tm, d), lambda b, i: (b, i, 0)),
                  pl.BlockSpec((1, d), lambda b, i: (0, 0)),
                  pl.BlockSpec((1, 1, d), lambda b, i: (b, 0, 0)),
                  pl.BlockSpec((1, 1, d), lambda b, i: (b, 0, 0)),
                  pl.BlockSpec((d, IN_COLS), lambda b, i: (0, 0)),
                  pl.BlockSpec((1, LANES), lambda b, i: (0, 0)),
                  pl.BlockSpec((1, LANES), lambda b, i: (0, 0))],
        out_specs=pl.BlockSpec((1, tm, IN_COLS), lambda b, i: (b, i, 0)),
        compiler_params=pltpu.CompilerParams(vmem_limit_bytes=VMEM_LIMIT),
        name="in_proj",
    )(x, norm_g.reshape(1, d), shift.reshape(bsz, 1, d), scale.reshape(bsz, 1, d),
      w_in_bf16, qg, kg)


def _sb_tile(qm, kblk, vblk, cum, carry, mask):
    z = lax.dot_general(qm, kblk, _NT, preferred_element_type=jnp.float32)
    nz = -z
    l1p = jnp.log(1.0 + jnp.exp2(jnp.minimum(z, nz))) * LOG2E
    lg = jnp.minimum(nz, 0.0) - l1p
    if mask is not None:
        lg = jnp.where(mask, lg, 0.0)
    rem = jnp.dot(lg.astype(jnp.bfloat16), cum, preferred_element_type=jnp.float32)
    a = jnp.exp2(lg + z + rem + carry)
    if mask is not None:
        a = jnp.where(mask, a, 0.0)
    o = jnp.dot(a.astype(jnp.bfloat16), vblk, preferred_element_type=jnp.float32)
    return o, carry + jnp.sum(lg, axis=-1, keepdims=True)


def _sb_kernel(q_ref, k_ref, v_ref, g_ref, o_ref, acc):
    step = pl.program_id(2)
    lane = lax.broadcasted_iota(jnp.int32, (1, LANES), 1)
    first = lane < HEAD_DIM
    row = lax.broadcasted_iota(jnp.int32, (TK, TK), 0)
    col = lax.broadcasted_iota(jnp.int32, (TK, TK), 1)
    cum = (row > col).astype(jnp.bfloat16)
    strict = col < row
    diag_mask = jnp.concatenate([strict, strict], axis=0)

    def stacked(u):
        q = q_ref[0, u * TQ:(u + 1) * TQ, :]
        zero = jnp.zeros_like(q)
        return jnp.concatenate([jnp.where(first, q, zero), jnp.where(first, zero, q)], axis=0)

    qs = [stacked(u) for u in range(SB_QT)]
    acc_of = [acc.at[u * 2 * TQ:(u + 1) * 2 * TQ, :] for u in range(SB_QT)]

    def tiles(qm, kbs, carry, masks):
        total = None
        for kb, mask in zip(kbs, masks):
            start = pl.multiple_of(kb * TK, TK)
            o, carry = _sb_tile(qm, k_ref[0, pl.ds(start, TK), :], v_ref[0, pl.ds(start, TK), :],
                                cum, carry, mask)
            total = o if total is None else total + o
        return total, carry

    zero_carry = jnp.zeros((2 * TQ, 1), jnp.float32)

    def first_block(first_step):
        carries = []
        for u in range(SB_QT):
            qt = step * SB_QT + u
            n = min(SB_FIRST, u + 1) if first_step else SB_FIRST
            o, carry = tiles(qs[u], [qt - t for t in range(n)], zero_carry,
                             [diag_mask] + [None] * (n - 1))
            acc_of[u][...] = o
            carries.append(carry)
        return tuple(carries)

    carries = lax.cond(step > 0, lambda: first_block(False), lambda: first_block(True))

    def alive(carry):
        return jnp.max(carry) > SB_DEAD_LOG2

    def sweep(u, n_steps, kbs_of_step, carry):
        def cond(state):
            i, _, live = state
            return jnp.logical_and(i < n_steps, live)

        def body(state):
            i, carry, _ = state
            kbs = kbs_of_step(i)
            o, carry = tiles(qs[u], kbs, carry, [None] * len(kbs))
            acc_of[u][...] += o
            return i + 1, carry, alive(carry)

        return lax.while_loop(cond, body, (jnp.int32(0), carry, alive(carry)))[1]

    any_alive = alive(carries[0])
    for carry in carries[1:]:
        any_alive = jnp.logical_or(any_alive, alive(carry))

    @pl.when(any_alive)
    def _():
        for u in range(SB_QT):
            left = jnp.maximum(step * SB_QT + u - (SB_FIRST - 1), 0)
            rem = lax.rem(left, SB_GROUP)
            carry = sweep(u, rem, lambda i, left=left: [left - 1 - i], carries[u])
            sweep(u, left // SB_GROUP,
                  lambda i, left=left, rem=rem: [left - rem - i * SB_GROUP - 1 - t
                                                 for t in range(SB_GROUP)], carry)

    for u in range(SB_QT):
        out = jnp.where(first, acc_of[u][:TQ, :], acc_of[u][TQ:, :])
        g = g_ref[0, u * TQ:(u + 1) * TQ, :].astype(jnp.float32)
        o_ref[0, u * TQ:(u + 1) * TQ, :] = (out * _silu(g)).astype(o_ref.dtype)


def _sb_attention(proj):
    bsz, s, _ = proj.shape
    rows = SB_QT * TQ
    assert TQ == TK and s % rows == 0 and SB_QT >= SB_FIRST - 1
    return pl.pallas_call(
        _sb_kernel,
        out_shape=jax.ShapeDtypeStruct((bsz, s, SB_WIDTH), jnp.bfloat16),
        grid=(bsz, SB_WIDTH // LANES, s // rows),
        in_specs=[pl.BlockSpec((1, rows, LANES), lambda b, h, i: (b, i, CB_SB_Q + h)),
                  pl.BlockSpec((1, s, LANES), lambda b, h, i: (b, 0, CB_SB_K + h)),
                  pl.BlockSpec((1, s, LANES), lambda b, h, i: (b, 0, CB_SB_V + h)),
                  pl.BlockSpec((1, rows, LANES), lambda b, h, i: (b, i, CB_SB_G + h))],
        out_specs=pl.BlockSpec((1, rows, LANES), lambda b, h, i: (b, i, h)),
        scratch_shapes=[pltpu.VMEM((SB_QT * 2 * TQ, LANES), jnp.float32)],
        compiler_params=pltpu.CompilerParams(vmem_limit_bytes=VMEM_LIMIT),
        name="sb_attn",
    )(proj, proj, proj, proj)


def _df_kernel(q_ref, k_ref, v_ref, g_ref, slope_ref, lq1_ref, lk1_ref, lq2_ref, lk2_ref, sg_ref,
               qg_ref, kg_ref, o_ref, m_s, l_s, acc, *, lam_init):
    qi = pl.program_id(2)
    q = q_ref[0]
    lane = lax.broadcasted_iota(jnp.int32, (1, LANES), 1)
    first = lane < HEAD_DIM
    zero = jnp.zeros_like(q)
    qs = jnp.concatenate([jnp.where(first, q, zero), jnp.where(first, zero, q)], axis=0)
    slope = slope_ref[0] * LOG2E
    row = lax.broadcasted_iota(jnp.int32, (DF_TQ, DF_TK), 0)
    col = lax.broadcasted_iota(jnp.int32, (DF_TQ, DF_TK), 1)
    causal = col <= row
    diag_mask = jnp.concatenate([causal, causal], axis=0)

    qk_bound = (jnp.max(jnp.abs(qg_ref[...])) * jnp.max(jnp.abs(kg_ref[...]))
                * (math.sqrt(HEAD_DIM) * LOG2E * DF_BOUND_MARGIN))
    slope_s = jnp.max(slope)

    def load(first_kb, width):
        start = pl.multiple_of(first_kb * DF_TK, DF_TK)
        rel = lax.broadcasted_iota(jnp.int32, (1, width), 1).astype(jnp.float32)
        off = ((first_kb - qi) * DF_TK).astype(jnp.float32)
        bias = jnp.concatenate([slope] * (width // LANES), axis=1) * (rel + off)
        return k_ref[0, pl.ds(start, width), :], v_ref[0, pl.ds(start, width), :], bias

    def last_rel(first_kb, width):
        return ((first_kb - qi) * DF_TK + (width - 1)).astype(jnp.float32)

    def sweep(n_steps, first_kb_of_step, width, tile, needed):
        def cond(state):
            i, live = state
            return jnp.logical_and(i < n_steps, live)

        def body(state):
            i, _ = state
            tile(first_kb_of_step(i), width, None)
            return i + 1, needed(first_kb_of_step(i + 1), width)

        lax.while_loop(cond, body, (jnp.int32(0), needed(first_kb_of_step(jnp.int32(0)), width)))

    def sweep_all(tile, needed):
        tile(qi, DF_TK, diag_mask)
        rem = lax.rem(qi, DF_GROUP)
        sweep(rem, lambda i: qi - 1 - i, DF_TK, tile, needed)
        sweep(qi // DF_GROUP, lambda i: qi - rem - (i + 1) * DF_GROUP, DF_GROUP * DF_TK, tile, needed)

    def fixed_shift():
        rowpos = lax.broadcasted_iota(jnp.int32, (DF_TQ, LANES), 0).astype(jnp.float32)
        shift = qk_bound + slope * jnp.concatenate([rowpos, rowpos], axis=0)
        l_s[...] = jnp.zeros((2 * DF_TQ, LANES), jnp.float32)
        acc[...] = jnp.zeros((2 * DF_TQ, LANES), jnp.float32)

        def tile(first_kb, width, mask):
            reps = width // LANES
            kblk, vblk, bias = load(first_kb, width)
            s = (lax.dot_general(qs, kblk, _NT, preferred_element_type=jnp.float32)
                 + bias - jnp.concatenate([shift] * reps, axis=1))
            if mask is not None:
                s = jnp.where(mask, s, NEG_BIG)
            p = jnp.exp2(s)
            part = p[:, :LANES]
            for r in range(1, reps):
                part = part + p[:, r * LANES:(r + 1) * LANES]
            l_s[...] += part
            acc[...] += jnp.dot(p.astype(jnp.bfloat16), vblk, preferred_element_type=jnp.float32)

        def needed(first_kb, width):
            return slope_s * last_rel(first_kb, width) > DF_DEAD_LOG2

        sweep_all(tile, needed)
        l = jnp.sum(l_s[...], axis=-1, keepdims=True)
        return l[:DF_TQ], l[DF_TQ:]

    def running_max():
        m_s[...] = jnp.full((2 * DF_TQ, LANES), NEG_BIG, jnp.float32)
        l_s[...] = jnp.zeros((2 * DF_TQ, LANES), jnp.float32)
        acc[...] = jnp.zeros((2 * DF_TQ, LANES), jnp.float32)

        def tile(first_kb, width, mask):
            reps = width // LANES
            kblk, vblk, bias = load(first_kb, width)
            s = lax.dot_general(qs, kblk, _NT, preferred_element_type=jnp.float32) + bias
            if mask is not None:
                s = jnp.where(mask, s, NEG_BIG)
            m_old = m_s[...]
            m_new = jnp.maximum(m_old, jnp.max(s, axis=-1, keepdims=True))
            p = jnp.exp2(s - jnp.concatenate([m_new] * reps, axis=1))
            alpha = jnp.exp2(m_old - m_new)
            l_s[...] = alpha * l_s[...] + jnp.sum(p, axis=-1, keepdims=True)
            acc[...] = alpha * acc[...] + jnp.dot(p.astype(jnp.bfloat16), vblk,
                                                  preferred_element_type=jnp.float32)
            m_s[...] = m_new

        def needed(first_kb, width):
            return (qk_bound + slope_s * last_rel(first_kb, width) - jnp.min(m_s[...])
                    > DF_DEAD_LOG2)

        sweep_all(tile, needed)
        return l_s[:DF_TQ, :1], l_s[DF_TQ:, :1]

    l1, l2 = lax.cond(qk_bound < DF_FIXED_SHIFT_MAX, fixed_shift, running_max)

    lam = (jnp.exp(jnp.sum(lq1_ref[...] * lk1_ref[...], keepdims=True))
           - jnp.exp(jnp.sum(lq2_ref[...] * lk2_ref[...], keepdims=True)) + lam_init)
    o = acc[:DF_TQ, :] / l1 - lam * (acc[DF_TQ:, :] / l2)
    ms = jnp.mean(o * o, axis=-1, keepdims=True)
    o = o * lax.rsqrt(ms + EPS) * sg_ref[...] * (1.0 - lam_init)
    g = g_ref[0].astype(jnp.float32)
    o_ref[0] = (o * _silu(g)).astype(o_ref.dtype)


def _df_attention(proj, slopes, lq1, lk1, lq2, lk2, subln_g, q_norm_g, k_norm_g, lam_init):
    bsz, s, _ = proj.shape
    assert DF_TQ == DF_TK and s % DF_TQ == 0
    nq = s // DF_TQ
    vec = pl.BlockSpec((1, HEAD_DIM), lambda b, h, i: (0, 0))
    stat = pltpu.VMEM((2 * DF_TQ, LANES), jnp.float32)
    return pl.pallas_call(
        functools.partial(_df_kernel, lam_init=lam_init),
        out_shape=jax.ShapeDtypeStruct((bsz, s, DF_WIDTH), jnp.bfloat16),
        grid=(bsz, DF_HEADS, nq),
        in_specs=[pl.BlockSpec((1, DF_TQ, LANES), lambda b, h, i: (b, i, CB_DF_Q + h)),
                  pl.BlockSpec((1, s, LANES), lambda b, h, i: (b, 0, CB_DF_K + h)),
                  pl.BlockSpec((1, s, LANES), lambda b, h, i: (b, 0, CB_DF_V + h)),
                  pl.BlockSpec((1, DF_TQ, LANES), lambda b, h, i: (b, i, CB_DF_G + h)),
                  pl.BlockSpec((1, 1, LANES), lambda b, h, i: (h, 0, 0)),
                  vec, vec, vec, vec,
                  pl.BlockSpec((1, LANES), lambda b, h, i: (0, 0)),
                  vec, vec],
        out_specs=pl.BlockSpec((1, DF_TQ, LANES), lambda b, h, i: (b, i, h)),
        scratch_shapes=[stat, stat, stat],
        compiler_params=pltpu.CompilerParams(vmem_limit_bytes=VMEM_LIMIT),
        name="df_attn",
    )(proj, proj, proj, proj, slopes, lq1, lk1, lq2, lk2, subln_g, q_norm_g, k_norm_g)


def _out_proj_kernel(x_ref, sb_ref, df_ref, w_ref, gate_ref, o_ref):
    y = (jnp.dot(sb_ref[0], w_ref[:SB_WIDTH, :], preferred_element_type=jnp.float32)
         + jnp.dot(df_ref[0], w_ref[SB_WIDTH:, :], preferred_element_type=jnp.float32))
    o_ref[0] = x_ref[0] + gate_ref[0] * y


def _out_proj(x, sb_o, df_o, w_out_bf16, gate):
    bsz, s, d = x.shape
    tm = ROWS_OUT
    return pl.pallas_call(
        _out_proj_kernel,
        out_shape=jax.ShapeDtypeStruct((bsz, s, d), jnp.float32),
        grid=(bsz, s // tm),
        in_specs=[pl.BlockSpec((1, tm, d), lambda b, i: (b, i, 0)),
                  pl.BlockSpec((1, tm, SB_WIDTH), lambda b, i: (b, i, 0)),
                  pl.BlockSpec((1, tm, DF_WIDTH), lambda b, i: (b, i, 0)),
                  pl.BlockSpec((SB_WIDTH + DF_WIDTH, d), lambda b, i: (0, 0)),
                  pl.BlockSpec((1, 1, d), lambda b, i: (b, 0, 0))],
        out_specs=pl.BlockSpec((1, tm, d), lambda b, i: (b, i, 0)),
        compiler_params=pltpu.CompilerParams(vmem_limit_bytes=VMEM_LIMIT),
        name="out_proj",
    )(x, sb_o, df_o, w_out_bf16, gate.reshape(bsz, 1, d))


def _layer(x, c, layer_idx, norm_g, w_ada, b_ada, w_in, q_norm_g, k_norm_g,
           lambda_q1, lambda_k1, lambda_q2, lambda_k2, subln_g, w_out):
    d = x.shape[-1]
    lam_init = 0.8 - 0.6 * math.exp(-0.3 * layer_idx)
    mod = _adaln_mod(c, w_ada, b_ada)
    shift, scale, gate = mod[:, :d], mod[:, d:2 * d], mod[:, 2 * d:]
    proj = _in_proj(x, norm_g, shift, scale, w_in.astype(jnp.bfloat16), q_norm_g, k_norm_g)
    sb_o = _sb_attention(proj)
    slopes = jnp.asarray([2.0 ** (-8.0 * (h + 1) / DF_HEADS) for h in range(DF_HEADS)], jnp.float32)
    slopes = jnp.broadcast_to(slopes[:, None, None], (DF_HEADS, 1, LANES))
    row = lambda v: v.reshape(1, -1)
    df_o = _df_attention(proj, slopes, row(lambda_q1), row(lambda_k1), row(lambda_q2),
                         row(lambda_k2), row(subln_g), row(q_norm_g), row(k_norm_g), lam_init)
    return _out_proj(x, sb_o, df_o, w_out.astype(jnp.bfloat16), gate)


@jax.jit
def kernel(x, c, norm_g, w_ada, b_ada, w_in, q_norm_g, k_norm_g, lambda_q1, lambda_k1,
           lambda_q2, lambda_k2, subln_g, w_out):
    for l in range(norm_g.shape[0]):
        x = _layer(x, c, l, norm_g[l], w_ada[l], b_ada[l], w_in[l], q_norm_g[l], k_norm_g[l],
                   lambda_q1[l], lambda_k1[l], lambda_q2[l], lambda_k2[l], subln_g[l], w_out[l])
    return x
```

```python
import functools
import math

import jax
import jax.numpy as jnp
from jax import lax
from jax.experimental import pallas as pl
from jax.experimental.pallas import tpu as pltpu

D_MODEL = 1024
SB_HEADS = 8
DF_HEADS = 4
HEAD_DIM = 64
LANES = 128
SB_WIDTH = SB_HEADS * HEAD_DIM
DF_WIDTH = DF_HEADS * 2 * HEAD_DIM
IN_COLS = 4 * SB_WIDTH + 4 * DF_WIDTH
EPS = 1e-6
NEG_BIG = -1e30

CB_SB_Q, CB_SB_K, CB_SB_V, CB_SB_G = 0, 4, 8, 12
CB_DF_Q, CB_DF_K, CB_DF_V, CB_DF_G = 16, 20, 24, 28

ROWS_PROJ = 512
ROWS_OUT = 1024
TQ = 256
TK = 256
DF_TQ = 1024
DF_TK = 1024
SB_QT = 4
SB_FIRST = 3
SB_GROUP = 4
DF_GROUP = 1
LOG2E = math.log2(math.e)
SB_DEAD_LOG2 = -150.0
DF_DEAD_LOG2 = -150.0
DF_FIXED_SHIFT_MAX = 48.0
DF_BOUND_MARGIN = 1.02
PROJ_CHUNK = 512
VMEM_LIMIT = 48 * 1024 * 1024

_NT = (((1,), (1,)), ((), ()))


def _silu(g):
    return g / (1.0 + jnp.exp(-g))


def _adaln_kernel(ct_ref, w_ref, b_ref, o_ref):
    w = w_ref[...]
    rows = []
    for b in range(o_ref.shape[0]):
        col = ct_ref[:, b:b + 1]
        rows.append(jnp.sum(col * w, axis=0, keepdims=True))
    o_ref[...] = jnp.concatenate(rows, axis=0) + b_ref[...]


def _adaln_mod(c, w_ada, b_ada):
    bsz, d = c.shape
    n = w_ada.shape[1]
    tn = 512
    return pl.pallas_call(
        _adaln_kernel,
        out_shape=jax.ShapeDtypeStruct((bsz, n), jnp.float32),
        grid=(n // tn,),
        in_specs=[pl.BlockSpec((d, bsz), lambda j: (0, 0)),
                  pl.BlockSpec((d, tn), lambda j: (0, j)),
                  pl.BlockSpec((1, tn), lambda j: (0, j))],
        out_specs=pl.BlockSpec((bsz, tn), lambda j: (0, j)),
        name="adaln_mod",
    )(c.T, w_ada, b_ada.reshape(1, n))


def _group_rmsnorm(y, gain):
    outs = []
    lane = lax.broadcasted_iota(jnp.int32, (1, LANES), 1)
    lo = lane < HEAD_DIM
    for j in range(y.shape[1] // LANES):
        blk = y[:, j * LANES:(j + 1) * LANES]
        sq = blk * blk
        s_lo = jnp.sum(jnp.where(lo, sq, 0.0), axis=-1, keepdims=True)
        s_hi = jnp.sum(jnp.where(lo, 0.0, sq), axis=-1, keepdims=True)
        ms = jnp.where(lo, s_lo, s_hi) * (1.0 / HEAD_DIM)
        outs.append(blk * lax.rsqrt(ms + EPS) * gain)
    return jnp.concatenate(outs, axis=1)


def _in_proj_kernel(x_ref, ng_ref, shift_ref, scale_ref, w_ref, qg_ref, kg_ref, o_ref):
    x = x_ref[0]
    ms = jnp.mean(x * x, axis=-1, keepdims=True)
    h = x * lax.rsqrt(ms + EPS) * ng_ref[...]
    h = (h * (1.0 + scale_ref[0]) + shift_ref[0]).astype(jnp.bfloat16)
    inv = LOG2E / math.sqrt(HEAD_DIM)
    for ci in range(IN_COLS // PROJ_CHUNK):
        c0 = ci * PROJ_CHUNK
        y = jnp.dot(h, w_ref[:, c0:c0 + PROJ_CHUNK], preferred_element_type=jnp.float32)
        cb = c0 // LANES
        if cb == CB_SB_Q:
            y = y * inv
        elif cb == CB_DF_Q:
            y = _group_rmsnorm(y, qg_ref[...]) * inv
        elif cb == CB_DF_K:
            y = _group_rmsnorm(y, kg_ref[...])
        o_ref[0, :, c0:c0 + PROJ_CHUNK] = y.astype(o_ref.dtype)


def _in_proj(x, norm_g, shift, scale, w_in_bf16, q_norm_g, k_norm_g):
    bsz, s, d = x.shape
    tm = ROWS_PROJ
    assert SB_WIDTH % PROJ_CHUNK == 0 and DF_WIDTH % PROJ_CHUNK == 0 and s % tm == 0
    qg = jnp.tile(q_norm_g.reshape(1, HEAD_DIM), (1, 2))
    kg = jnp.tile(k_norm_g.reshape(1, HEAD_DIM), (1, 2))
    return pl.pallas_call(
        _in_proj_kernel,
        out_shape=jax.ShapeDtypeStruct((bsz, s, IN_COLS), jnp.bfloat16),
        grid=(bsz, s // tm),
        in_specs=[pl.BlockSpec((1, tm, d), lambda b, i: (b, i, 0)),
                  pl.BlockSpec((1, d), lambda b, i: (0, 0)),
                  pl.BlockSpec((1, 1, d), lambda b, i: (b, 0, 0)),
                  pl.BlockSpec((1, 1, d), lambda b, i: (b, 0, 0)),
                  pl.BlockSpec((d, IN_COLS), lambda b, i: (0, 0)),
                  pl.BlockSpec((1, LANES), lambda b, i: (0, 0)),
                  pl.BlockSpec((1, LANES), lambda b, i: (0, 0))],
        out_specs=pl.BlockSpec((1, tm, IN_COLS), lambda b, i: (b, i, 0)),
        compiler_params=pltpu.CompilerParams(vmem_limit_bytes=VMEM_LIMIT),
        name="in_proj",
    )(x, norm_g.reshape(1, d), shift.reshape(bsz, 1, d), scale.reshape(bsz, 1, d),
      w_in_bf16, qg, kg)


def _sb_tile(qm, kblk, vblk, cum, carry, mask):
    z = lax.dot_general(qm, kblk, _NT, preferred_element_type=jnp.float32)
    nz = -z
    l1p = jnp.log(1.0 + jnp.exp2(jnp.minimum(z, nz))) * LOG2E
    lg = jnp.minimum(nz, 0.0) - l1p
    if mask is not None:
        lg = jnp.where(mask, lg, 0.0)
    rem = jnp.dot(lg.astype(jnp.bfloat16), cum, preferred_element_type=jnp.float32)
    a = jnp.exp2(lg + z + rem + carry)
    if mask is not None:
        a = jnp.where(mask, a, 0.0)
    o = jnp.dot(a.astype(jnp.bfloat16), vblk, preferred_element_type=jnp.float32)
    return o, carry + jnp.sum(lg, axis=-1, keepdims=True)


def _sb_kernel(q_ref, k_ref, v_ref, g_ref, o_ref, acc):
    step = pl.program_id(2)
    lane = lax.broadcasted_iota(jnp.int32, (1, LANES), 1)
    first = lane < HEAD_DIM
    row = lax.broadcasted_iota(jnp.int32, (TK, TK), 0)
    col = lax.broadcasted_iota(jnp.int32, (TK, TK), 1)
    cum = (row > col).astype(jnp.bfloat16)
    strict = col < row
    diag_mask = jnp.concatenate([strict, strict], axis=0)

    def stacked(u):
        q = q_ref[0, u * TQ:(u + 1) * TQ, :]
        zero = jnp.zeros_like(q)
        return jnp.concatenate([jnp.where(first, q, zero), jnp.where(first, zero, q)], axis=0)

    qs = [stacked(u) for u in range(SB_QT)]
    acc_of = [acc.at[u * 2 * TQ:(u + 1) * 2 * TQ, :] for u in range(SB_QT)]

    def tiles(qm, kbs, carry, masks):
        total = None
        for kb, mask in zip(kbs, masks):
            start = pl.multiple_of(kb * TK, TK)
            o, carry = _sb_tile(qm, k_ref[0, pl.ds(start, TK), :], v_ref[0, pl.ds(start, TK), :],
                                cum, carry, mask)
            total = o if total is None else total + o
        return total, carry

    zero_carry = jnp.zeros((2 * TQ, 1), jnp.float32)

    def first_block(first_step):
        carries = []
        for u in range(SB_QT):
            qt = step * SB_QT + u
            n = min(SB_FIRST, u + 1) if first_step else SB_FIRST
            o, carry = tiles(qs[u], [qt - t for t in range(n)], zero_carry,
                             [diag_mask] + [None] * (n - 1))
            acc_of[u][...] = o
            carries.append(carry)
        return tuple(carries)

    carries = lax.cond(step > 0, lambda: first_block(False), lambda: first_block(True))

    def alive(carry):
        return jnp.max(carry) > SB_DEAD_LOG2

    def sweep(u, n_steps, kbs_of_step, carry):
        def cond(state):
            i, _, live = state
            return jnp.logical_and(i < n_steps, live)

        def body(state):
            i, carry, _ = state
            kbs = kbs_of_step(i)
            o, carry = tiles(qs[u], kbs, carry, [None] * len(kbs))
            acc_of[u][...] += o
            return i + 1, carry, alive(carry)

        return lax.while_loop(cond, body, (jnp.int32(0), carry, alive(carry)))[1]

    any_alive = alive(carries[0])
    for carry in carries[1:]:
        any_alive = jnp.logical_or(any_alive, alive(carry))

    @pl.when(any_alive)
    def _():
        for u in range(SB_QT):
            left = jnp.maximum(step * SB_QT + u - (SB_FIRST - 1), 0)
            rem = lax.rem(left, SB_GROUP)
            carry = sweep(u, rem, lambda i, left=left: [left - 1 - i], carries[u])
            sweep(u, left // SB_GROUP,
                  lambda i, left=left, rem=rem: [left - rem - i * SB_GROUP - 1 - t
                                                 for t in range(SB_GROUP)], carry)

    for u in range(SB_QT):
        out = jnp.where(first, acc_of[u][:TQ, :], acc_of[u][TQ:, :])
        g = g_ref[0, u * TQ:(u + 1) * TQ, :].astype(jnp.float32)
        o_ref[0, u * TQ:(u + 1) * TQ, :] = (out * _silu(g)).astype(o_ref.dtype)


def _sb_attention(proj):
    bsz, s, _ = proj.shape
    rows = SB_QT * TQ
    assert TQ == TK and s % rows == 0 and SB_QT >= SB_FIRST - 1
    return pl.pallas_call(
        _sb_kernel,
        out_shape=jax.ShapeDtypeStruct((bsz, s, SB_WIDTH), jnp.bfloat16),
        grid=(bsz, SB_WIDTH // LANES, s // rows),
        in_specs=[pl.BlockSpec((1, rows, LANES), lambda b, h, i: (b, i, CB_SB_Q + h)),
                  pl.BlockSpec((1, s, LANES), lambda b, h, i: (b, 0, CB_SB_K + h)),
                  pl.BlockSpec((1, s, LANES), lambda b, h, i: (b, 0, CB_SB_V + h)),
                  pl.BlockSpec((1, rows, LANES), lambda b, h, i: (b, i, CB_SB_G + h))],
        out_specs=pl.BlockSpec((1, rows, LANES), lambda b, h, i: (b, i, h)),
        scratch_shapes=[pltpu.VMEM((SB_QT * 2 * TQ, LANES), jnp.float32)],
        compiler_params=pltpu.CompilerParams(vmem_limit_bytes=VMEM_LIMIT),
        name="sb_attn",
    )(proj, proj, proj, proj)


def _df_kernel(q_ref, k_ref, v_ref, g_ref, slope_ref, lq1_ref, lk1_ref, lq2_ref, lk2_ref, sg_ref,
               qg_ref, kg_ref, o_ref, m_s, l_s, acc, *, lam_init):
    qi = pl.program_id(2)
    q = q_ref[0]
    lane = lax.broadcasted_iota(jnp.int32, (1, LANES), 1)
    first = lane < HEAD_DIM
    zero = jnp.zeros_like(q)
    qs = jnp.concatenate([jnp.where(first, q, zero), jnp.where(first, zero, q)], axis=0)
    slope = slope_ref[0] * LOG2E
    row = lax.broadcasted_iota(jnp.int32, (DF_TQ, DF_TK), 0)
    col = lax.broadcasted_iota(jnp.int32, (DF_TQ, DF_TK), 1)
    causal = col <= row
    diag_mask = jnp.concatenate([causal, causal], axis=0)

    qk_bound = (jnp.max(jnp.abs(qg_ref[...])) * jnp.max(jnp.abs(kg_ref[...]))
                * (math.sqrt(HEAD_DIM) * LOG2E * DF_BOUND_MARGIN))
    slope_s = jnp.max(slope)

    def load(first_kb, width):
        start = pl.multiple_of(first_kb * DF_TK, DF_TK)
        rel = lax.broadcasted_iota(jnp.int32, (1, width), 1).astype(jnp.float32)
        off = ((first_kb - qi) * DF_TK).astype(jnp.float32)
        bias = jnp.concatenate([slope] * (width // LANES), axis=1) * (rel + off)
        return k_ref[0, pl.ds(start, width), :], v_ref[0, pl.ds(start, width), :], bias

    def last_rel(first_kb, width):
        return ((first_kb - qi) * DF_TK + (width - 1)).astype(jnp.float32)

    def sweep(n_steps, first_kb_of_step, width, tile, needed):
        def cond(state):
            i, live = state
            return jnp.logical_and(i < n_steps, live)

        def body(state):
            i, _ = state
            tile(first_kb_of_step(i), width, None)
            return i + 1, needed(first_kb_of_step(i + 1), width)

        lax.while_loop(cond, body, (jnp.int32(0), needed(first_kb_of_step(jnp.int32(0)), width)))

    def sweep_all(tile, needed, diagonal=None):
        if diagonal is None:
            tile(qi, DF_TK, diag_mask)
        else:
            diagonal()
        rem = lax.rem(qi, DF_GROUP)
        sweep(rem, lambda i: qi - 1 - i, DF_TK, tile, needed)
        sweep(qi // DF_GROUP, lambda i: qi - rem - (i + 1) * DF_GROUP, DF_GROUP * DF_TK, tile, needed)

    def fixed_shift():
        rowpos = lax.broadcasted_iota(jnp.int32, (DF_TQ, LANES), 0).astype(jnp.float32)
        shift = qk_bound + slope * jnp.concatenate([rowpos, rowpos], axis=0)
        l_s[...] = jnp.zeros((2 * DF_TQ, LANES), jnp.float32)
        acc[...] = jnp.zeros((2 * DF_TQ, LANES), jnp.float32)

        def tile(first_kb, width, mask):
            reps = width // LANES
            kblk, vblk, bias = load(first_kb, width)
            s = (lax.dot_general(qs, kblk, _NT, preferred_element_type=jnp.float32)
                 + bias - jnp.concatenate([shift] * reps, axis=1))
            if mask is not None:
                s = jnp.where(mask, s, NEG_BIG)
            p = jnp.exp2(s)
            part = p[:, :LANES]
            for r in range(1, reps):
                part = part + p[:, r * LANES:(r + 1) * LANES]
            l_s[...] += part
            acc[...] += jnp.dot(p.astype(jnp.bfloat16), vblk, preferred_element_type=jnp.float32)

        def needed(first_kb, width):
            return slope_s * last_rel(first_kb, width) > DF_DEAD_LOG2

        half = DF_TQ // 2
        hrow = lax.broadcasted_iota(jnp.int32, (half, half), 0)
        hcol = lax.broadcasted_iota(jnp.int32, (half, half), 1)
        tri = hcol <= hrow

        def diagonal_rows(r0, width, mask):
            both = lambda x: jnp.concatenate([x[r0:r0 + half], x[DF_TQ + r0:DF_TQ + r0 + half]], axis=0)
            reps = width // LANES
            start = pl.multiple_of(qi * DF_TK, DF_TK)
            kblk = k_ref[0, pl.ds(start, width), :]
            vblk = v_ref[0, pl.ds(start, width), :]
            rel = lax.broadcasted_iota(jnp.int32, (1, width), 1).astype(jnp.float32)
            bias = jnp.concatenate([slope] * reps, axis=1) * rel
            s = (lax.dot_general(both(qs), kblk, _NT, preferred_element_type=jnp.float32)
                 + bias - jnp.concatenate([both(shift)] * reps, axis=1))
            p = jnp.exp2(jnp.where(jnp.concatenate([mask, mask], axis=0), s, NEG_BIG))
            part = p[:, :LANES]
            for r in range(1, reps):
                part = part + p[:, r * LANES:(r + 1) * LANES]
            pv = jnp.dot(p.astype(jnp.bfloat16), vblk, preferred_element_type=jnp.float32)
            for m in range(2):
                rows = slice(m * DF_TQ + r0, m * DF_TQ + r0 + half)
                l_s[rows, :] += part[m * half:(m + 1) * half]
                acc[rows, :] += pv[m * half:(m + 1) * half]

        def diagonal():
            diagonal_rows(0, half, tri)
            diagonal_rows(half, DF_TK, jnp.concatenate([jnp.ones_like(tri), tri], axis=1))

        sweep_all(tile, needed, diagonal)
        l = jnp.sum(l_s[...], axis=-1, keepdims=True)
        return l[:DF_TQ], l[DF_TQ:]

    def running_max():
        m_s[...] = jnp.full((2 * DF_TQ, LANES), NEG_BIG, jnp.float32)
        l_s[...] = jnp.zeros((2 * DF_TQ, LANES), jnp.float32)
        acc[...] = jnp.zeros((2 * DF_TQ, LANES), jnp.float32)

        def tile(first_kb, width, mask):
            reps = width // LANES
            kblk, vblk, bias = load(first_kb, width)
            s = lax.dot_general(qs, kblk, _NT, preferred_element_type=jnp.float32) + bias
            if mask is not None:
                s = jnp.where(mask, s, NEG_BIG)
            m_old = m_s[...]
            m_new = jnp.maximum(m_old, jnp.max(s, axis=-1, keepdims=True))
            p = jnp.exp2(s - jnp.concatenate([m_new] * reps, axis=1))
            alpha = jnp.exp2(m_old - m_new)
            l_s[...] = alpha * l_s[...] + jnp.sum(p, axis=-1, keepdims=True)
            acc[...] = alpha * acc[...] + jnp.dot(p.astype(jnp.bfloat16), vblk,
                                                  preferred_element_type=jnp.float32)
            m_s[...] = m_new

        def needed(first_kb, width):
            return (qk_bound + slope_s * last_rel(first_kb, width) - jnp.min(m_s[...])
                    > DF_DEAD_LOG2)

        sweep_all(tile, needed)
        return l_s[:DF_TQ, :1], l_s[DF_TQ:, :1]

    l1, l2 = lax.cond(qk_bound < DF_FIXED_SHIFT_MAX, fixed_shift, running_max)

    lam = (jnp.exp(jnp.sum(lq1_ref[...] * lk1_ref[...], keepdims=True))
           - jnp.exp(jnp.sum(lq2_ref[...] * lk2_ref[...], keepdims=True)) + lam_init)
    o = acc[:DF_TQ, :] / l1 - lam * (acc[DF_TQ:, :] / l2)
    ms = jnp.mean(o * o, axis=-1, keepdims=True)
    o = o * lax.rsqrt(ms + EPS) * sg_ref[...] * (1.0 - lam_init)
    g = g_ref[0].astype(jnp.float32)
    o_ref[0] = (o * _silu(g)).astype(o_ref.dtype)


def _df_attention(proj, slopes, lq1, lk1, lq2, lk2, subln_g, q_norm_g, k_norm_g, lam_init):
    bsz, s, _ = proj.shape
    assert DF_TQ == DF_TK and s % DF_TQ == 0
    nq = s // DF_TQ
    vec = pl.BlockSpec((1, HEAD_DIM), lambda b, h, i: (0, 0))
    stat = pltpu.VMEM((2 * DF_TQ, LANES), jnp.float32)
    return pl.pallas_call(
        functools.partial(_df_kernel, lam_init=lam_init),
        out_shape=jax.ShapeDtypeStruct((bsz, s, DF_WIDTH), jnp.bfloat16),
        grid=(bsz, DF_HEADS, nq),
        in_specs=[pl.BlockSpec((1, DF_TQ, LANES), lambda b, h, i: (b, i, CB_DF_Q + h)),
                  pl.BlockSpec((1, s, LANES), lambda b, h, i: (b, 0, CB_DF_K + h)),
                  pl.BlockSpec((1, s, LANES), lambda b, h, i: (b, 0, CB_DF_V + h)),
                  pl.BlockSpec((1, DF_TQ, LANES), lambda b, h, i: (b, i, CB_DF_G + h)),
                  pl.BlockSpec((1, 1, LANES), lambda b, h, i: (h, 0, 0)),
                  vec, vec, vec, vec,
                  pl.BlockSpec((1, LANES), lambda b, h, i: (0, 0)),
                  vec, vec],
        out_specs=pl.BlockSpec((1, DF_TQ, LANES), lambda b, h, i: (b, i, h)),
        scratch_shapes=[stat, stat, stat],
        compiler_params=pltpu.CompilerParams(vmem_limit_bytes=VMEM_LIMIT),
        name="df_attn",
    )(proj, proj, proj, proj, slopes, lq1, lk1, lq2, lk2, subln_g, q_norm_g, k_norm_g)


def _out_proj_kernel(x_ref, sb_ref, df_ref, w_ref, gate_ref, o_ref):
    y = (jnp.dot(sb_ref[0], w_ref[:SB_WIDTH, :], preferred_element_type=jnp.float32)
         + jnp.dot(df_ref[0], w_ref[SB_WIDTH:, :], preferred_element_type=jnp.float32))
    o_ref[0] = x_ref[0] + gate_ref[0] * y


def _out_proj(x, sb_o, df_o, w_out_bf16, gate):
    bsz, s, d = x.shape
    tm = ROWS_OUT
    return pl.pallas_call(
        _out_proj_kernel,
        out_shape=jax.ShapeDtypeStruct((bsz, s, d), jnp.float32),
        grid=(bsz, s // tm),
        in_specs=[pl.BlockSpec((1, tm, d), lambda b, i: (b, i, 0)),
                  pl.BlockSpec((1, tm, SB_WIDTH), lambda b, i: (b, i, 0)),
                  pl.BlockSpec((1, tm, DF_WIDTH), lambda b, i: (b, i, 0)),
                  pl.BlockSpec((SB_WIDTH + DF_WIDTH, d), lambda b, i: (0, 0)),
                  pl.BlockSpec((1, 1, d), lambda b, i: (b, 0, 0))],
        out_specs=pl.BlockSpec((1, tm, d), lambda b, i: (b, i, 0)),
        compiler_params=pltpu.CompilerParams(vmem_limit_bytes=VMEM_LIMIT),
        name="out_proj",
    )(x, sb_o, df_o, w_out_bf16, gate.reshape(bsz, 1, d))


def _layer(x, c, layer_idx, norm_g, w_ada, b_ada, w_in, q_norm_g, k_norm_g,
           lambda_q1, lambda_k1, lambda_q2, lambda_k2, subln_g, w_out):
    d = x.shape[-1]
    lam_init = 0.8 - 0.6 * math.exp(-0.3 * layer_idx)
    mod = _adaln_mod(c, w_ada, b_ada)
    shift, scale, gate = mod[:, :d], mod[:, d:2 * d], mod[:, 2 * d:]
    proj = _in_proj(x, norm_g, shift, scale, w_in.astype(jnp.bfloat16), q_norm_g, k_norm_g)
    sb_o = _sb_attention(proj)
    slopes = jnp.asarray([2.0 ** (-8.0 * (h + 1) / DF_HEADS) for h in range(DF_HEADS)], jnp.float32)
    slopes = jnp.broadcast_to(slopes[:, None, None], (DF_HEADS, 1, LANES))
    row = lambda v: v.reshape(1, -1)
    df_o = _df_attention(proj, slopes, row(lambda_q1), row(lambda_k1), row(lambda_q2),
                         row(lambda_k2), row(subln_g), row(q_norm_g), row(k_norm_g), lam_init)
    return _out_proj(x, sb_o, df_o, w_out.astype(jnp.bfloat16), gate)


@jax.jit
def kernel(x, c, norm_g, w_ada, b_ada, w_in, q_norm_g, k_norm_g, lambda_q1, lambda_k1,
           lambda_q2, lambda_k2, subln_g, w_out):
    for l in range(norm_g.shape[0]):
        x = _layer(x, c, l, norm_g[l], w_ada[l], b_ada[l], w_in[l], q_norm_g[l], k_norm_g[l],
                   lambda_q1[l], lambda_k1[l], lambda_q2[l], lambda_k2[l], subln_g[l], w_out[l])
    return x
```

```python
import functools
import math

import jax
import jax.numpy as jnp
from jax import lax
from jax.experimental import pallas as pl
from jax.experimental.pallas import tpu as pltpu

D_MODEL = 1024
SB_HEADS = 8
DF_HEADS = 4
HEAD_DIM = 64
LANES = 128
SB_WIDTH = SB_HEADS * HEAD_DIM
DF_WIDTH = DF_HEADS * 2 * HEAD_DIM
IN_COLS = 4 * SB_WIDTH + 4 * DF_WIDTH
EPS = 1e-6
NEG_BIG = -1e30

CB_SB_Q, CB_SB_K, CB_SB_V, CB_SB_G = 0, 4, 8, 12
CB_DF_Q, CB_DF_K, CB_DF_V, CB_DF_G = 16, 20, 24, 28

ROWS_PROJ = 512
ROWS_OUT = 1024
TQ = 256
TK = 256
DF_TQ = 1024
DF_TK = 1024
SB_QT = 4
SB_FIRST = 3
SB_GROUP = 4
DF_GROUP = 1
DF_DIAG_PARTS = 4
LOG2E = math.log2(math.e)
SB_DEAD_LOG2 = -150.0
DF_DEAD_LOG2 = -150.0
DF_FIXED_SHIFT_MAX = 48.0
DF_BOUND_MARGIN = 1.02
PROJ_CHUNK = 512
VMEM_LIMIT = 48 * 1024 * 1024

_NT = (((1,), (1,)), ((), ()))


def _silu(g):
    return g / (1.0 + jnp.exp(-g))


def _adaln_kernel(ct_ref, w_ref, b_ref, o_ref):
    w = w_ref[...]
    rows = []
    for b in range(o_ref.shape[0]):
        col = ct_ref[:, b:b + 1]
        rows.append(jnp.sum(col * w, axis=0, keepdims=True))
    o_ref[...] = jnp.concatenate(rows, axis=0) + b_ref[...]


def _adaln_mod(c, w_ada, b_ada):
    bsz, d = c.shape
    n = w_ada.shape[1]
    tn = 512
    return pl.pallas_call(
        _adaln_kernel,
        out_shape=jax.ShapeDtypeStruct((bsz, n), jnp.float32),
        grid=(n // tn,),
        in_specs=[pl.BlockSpec((d, bsz), lambda j: (0, 0)),
                  pl.BlockSpec((d, tn), lambda j: (0, j)),
                  pl.BlockSpec((1, tn), lambda j: (0, j))],
        out_specs=pl.BlockSpec((bsz, tn), lambda j: (0, j)),
        name="adaln_mod",
    )(c.T, w_ada, b_ada.reshape(1, n))


def _group_rmsnorm(y, gain):
    outs = []
    lane = lax.broadcasted_iota(jnp.int32, (1, LANES), 1)
    lo = lane < HEAD_DIM
    for j in range(y.shape[1] // LANES):
        blk = y[:, j * LANES:(j + 1) * LANES]
        sq = blk * blk
        s_lo = jnp.sum(jnp.where(lo, sq, 0.0), axis=-1, keepdims=True)
        s_hi = jnp.sum(jnp.where(lo, 0.0, sq), axis=-1, keepdims=True)
        ms = jnp.where(lo, s_lo, s_hi) * (1.0 / HEAD_DIM)
        outs.append(blk * lax.rsqrt(ms + EPS) * gain)
    return jnp.concatenate(outs, axis=1)


def _in_proj_kernel(x_ref, ng_ref, shift_ref, scale_ref, w_ref, qg_ref, kg_ref, o_ref):
    x = x_ref[0]
    ms = jnp.mean(x * x, axis=-1, keepdims=True)
    h = x * lax.rsqrt(ms + EPS) * ng_ref[...]
    h = (h * (1.0 + scale_ref[0]) + shift_ref[0]).astype(jnp.bfloat16)
    inv = LOG2E / math.sqrt(HEAD_DIM)
    for ci in range(IN_COLS // PROJ_CHUNK):
        c0 = ci * PROJ_CHUNK
        y = jnp.dot(h, w_ref[:, c0:c0 + PROJ_CHUNK], preferred_element_type=jnp.float32)
        cb = c0 // LANES
        if cb == CB_SB_Q:
            y = y * inv
        elif cb == CB_DF_Q:
            y = _group_rmsnorm(y, qg_ref[...]) * inv
        elif cb == CB_DF_K:
            y = _group_rmsnorm(y, kg_ref[...])
        o_ref[0, :, c0:c0 + PROJ_CHUNK] = y.astype(o_ref.dtype)


def _in_proj(x, norm_g, shift, scale, w_in_bf16, q_norm_g, k_norm_g):
    bsz, s, d = x.shape
    tm = ROWS_PROJ
    assert SB_WIDTH % PROJ_CHUNK == 0 and DF_WIDTH % PROJ_CHUNK == 0 and s % tm == 0
    qg = jnp.tile(q_norm_g.reshape(1, HEAD_DIM), (1, 2))
    kg = jnp.tile(k_norm_g.reshape(1, HEAD_DIM), (1, 2))
    return pl.pallas_call(
        _in_proj_kernel,
        out_shape=jax.ShapeDtypeStruct((bsz, s, IN_COLS), jnp.bfloat16),
        grid=(bsz, s // tm),
        in_specs=[pl.BlockSpec((1, tm, d), lambda b, i: (b, i, 0)),
                  pl.BlockSpec((1, d), lambda b, i: (0, 0)),
                  pl.BlockSpec((1, 1, d), lambda b, i: (b, 0, 0)),
                  pl.BlockSpec((1, 1, d), lambda b, i: (b, 0, 0)),
                  pl.BlockSpec((d, IN_COLS), lambda b, i: (0, 0)),
                  pl.BlockSpec((1, LANES), lambda b, i: (0, 0)),
                  pl.BlockSpec((1, LANES), lambda b, i: (0, 0))],
        out_specs=pl.BlockSpec((1, tm, IN_COLS), lambda b, i: (b, i, 0)),
        compiler_params=pltpu.CompilerParams(vmem_limit_bytes=VMEM_LIMIT),
        name="in_proj",
    )(x, norm_g.reshape(1, d), shift.reshape(bsz, 1, d), scale.reshape(bsz, 1, d),
      w_in_bf16, qg, kg)


def _sb_tile(qm, kblk, vblk, cum, carry, mask):
    z = lax.dot_general(qm, kblk, _NT, preferred_element_type=jnp.float32)
    nz = -z
    l1p = jnp.log(1.0 + jnp.exp2(jnp.minimum(z, nz))) * LOG2E
    lg = jnp.minimum(nz, 0.0) - l1p
    if mask is not None:
        lg = jnp.where(mask, lg, 0.0)
    rem = jnp.dot(lg.astype(jnp.bfloat16), cum, preferred_element_type=jnp.float32)
    a = jnp.exp2(lg + z + rem + carry)
    if mask is not None:
        a = jnp.where(mask, a, 0.0)
    o = jnp.dot(a.astype(jnp.bfloat16), vblk, preferred_element_type=jnp.float32)
    return o, carry + jnp.sum(lg, axis=-1, keepdims=True)


def _sb_kernel(q_ref, k_ref, v_ref, g_ref, o_ref, acc):
    step = pl.program_id(2)
    lane = lax.broadcasted_iota(jnp.int32, (1, LANES), 1)
    first = lane < HEAD_DIM
    row = lax.broadcasted_iota(jnp.int32, (TK, TK), 0)
    col = lax.broadcasted_iota(jnp.int32, (TK, TK), 1)
    cum = (row > col).astype(jnp.bfloat16)
    strict = col < row
    diag_mask = jnp.concatenate([strict, strict], axis=0)

    def stacked(u):
        q = q_ref[0, u * TQ:(u + 1) * TQ, :]
        zero = jnp.zeros_like(q)
        return jnp.concatenate([jnp.where(first, q, zero), jnp.where(first, zero, q)], axis=0)

    qs = [stacked(u) for u in range(SB_QT)]
    acc_of = [acc.at[u * 2 * TQ:(u + 1) * 2 * TQ, :] for u in range(SB_QT)]

    def tiles(qm, kbs, carry, masks):
        total = None
        for kb, mask in zip(kbs, masks):
            start = pl.multiple_of(kb * TK, TK)
            o, carry = _sb_tile(qm, k_ref[0, pl.ds(start, TK), :], v_ref[0, pl.ds(start, TK), :],
                                cum, carry, mask)
            total = o if total is None else total + o
        return total, carry

    zero_carry = jnp.zeros((2 * TQ, 1), jnp.float32)

    def first_block(first_step):
        carries = []
        for u in range(SB_QT):
            qt = step * SB_QT + u
            n = min(SB_FIRST, u + 1) if first_step else SB_FIRST
            o, carry = tiles(qs[u], [qt - t for t in range(n)], zero_carry,
                             [diag_mask] + [None] * (n - 1))
            acc_of[u][...] = o
            carries.append(carry)
        return tuple(carries)

    carries = lax.cond(step > 0, lambda: first_block(False), lambda: first_block(True))

    def alive(carry):
        return jnp.max(carry) > SB_DEAD_LOG2

    def sweep(u, n_steps, kbs_of_step, carry):
        def cond(state):
            i, _, live = state
            return jnp.logical_and(i < n_steps, live)

        def body(state):
            i, carry, _ = state
            kbs = kbs_of_step(i)
            o, carry = tiles(qs[u], kbs, carry, [None] * len(kbs))
            acc_of[u][...] += o
            return i + 1, carry, alive(carry)

        return lax.while_loop(cond, body, (jnp.int32(0), carry, alive(carry)))[1]

    any_alive = alive(carries[0])
    for carry in carries[1:]:
        any_alive = jnp.logical_or(any_alive, alive(carry))

    @pl.when(any_alive)
    def _():
        for u in range(SB_QT):
            left = jnp.maximum(step * SB_QT + u - (SB_FIRST - 1), 0)
            rem = lax.rem(left, SB_GROUP)
            carry = sweep(u, rem, lambda i, left=left: [left - 1 - i], carries[u])
            sweep(u, left // SB_GROUP,
                  lambda i, left=left, rem=rem: [left - rem - i * SB_GROUP - 1 - t
                                                 for t in range(SB_GROUP)], carry)

    for u in range(SB_QT):
        out = jnp.where(first, acc_of[u][:TQ, :], acc_of[u][TQ:, :])
        g = g_ref[0, u * TQ:(u + 1) * TQ, :].astype(jnp.float32)
        o_ref[0, u * TQ:(u + 1) * TQ, :] = (out * _silu(g)).astype(o_ref.dtype)


def _sb_attention(proj):
    bsz, s, _ = proj.shape
    rows = SB_QT * TQ
    assert TQ == TK and s % rows == 0 and SB_QT >= SB_FIRST - 1
    return pl.pallas_call(
        _sb_kernel,
        out_shape=jax.ShapeDtypeStruct((bsz, s, SB_WIDTH), jnp.bfloat16),
        grid=(bsz, SB_WIDTH // LANES, s // rows),
        in_specs=[pl.BlockSpec((1, rows, LANES), lambda b, h, i: (b, i, CB_SB_Q + h)),
                  pl.BlockSpec((1, s, LANES), lambda b, h, i: (b, 0, CB_SB_K + h)),
                  pl.BlockSpec((1, s, LANES), lambda b, h, i: (b, 0, CB_SB_V + h)),
                  pl.BlockSpec((1, rows, LANES), lambda b, h, i: (b, i, CB_SB_G + h))],
        out_specs=pl.BlockSpec((1, rows, LANES), lambda b, h, i: (b, i, h)),
        scratch_shapes=[pltpu.VMEM((SB_QT * 2 * TQ, LANES), jnp.float32)],
        compiler_params=pltpu.CompilerParams(vmem_limit_bytes=VMEM_LIMIT),
        name="sb_attn",
    )(proj, proj, proj, proj)


def _df_kernel(q_ref, k_ref, v_ref, g_ref, slope_ref, lq1_ref, lk1_ref, lq2_ref, lk2_ref, sg_ref,
               qg_ref, kg_ref, o_ref, m_s, l_s, acc, *, lam_init):
    qi = pl.program_id(2)
    q = q_ref[0]
    lane = lax.broadcasted_iota(jnp.int32, (1, LANES), 1)
    first = lane < HEAD_DIM
    zero = jnp.zeros_like(q)
    qs = jnp.concatenate([jnp.where(first, q, zero), jnp.where(first, zero, q)], axis=0)
    slope = slope_ref[0] * LOG2E
    row = lax.broadcasted_iota(jnp.int32, (DF_TQ, DF_TK), 0)
    col = lax.broadcasted_iota(jnp.int32, (DF_TQ, DF_TK), 1)
    causal = col <= row
    diag_mask = jnp.concatenate([causal, causal], axis=0)

    qk_bound = (jnp.max(jnp.abs(qg_ref[...])) * jnp.max(jnp.abs(kg_ref[...]))
                * (math.sqrt(HEAD_DIM) * LOG2E * DF_BOUND_MARGIN))
    slope_s = jnp.max(slope)

    def load(first_kb, width):
        start = pl.multiple_of(first_kb * DF_TK, DF_TK)
        rel = lax.broadcasted_iota(jnp.int32, (1, width), 1).astype(jnp.float32)
        off = ((first_kb - qi) * DF_TK).astype(jnp.float32)
        bias = jnp.concatenate([slope] * (width // LANES), axis=1) * (rel + off)
        return k_ref[0, pl.ds(start, width), :], v_ref[0, pl.ds(start, width), :], bias

    def last_rel(first_kb, width):
        return ((first_kb - qi) * DF_TK + (width - 1)).astype(jnp.float32)

    def sweep(n_steps, first_kb_of_step, width, tile, needed):
        def cond(state):
            i, live = state
            return jnp.logical_and(i < n_steps, live)

        def body(state):
            i, _ = state
            tile(first_kb_of_step(i), width, None)
            return i + 1, needed(first_kb_of_step(i + 1), width)

        lax.while_loop(cond, body, (jnp.int32(0), needed(first_kb_of_step(jnp.int32(0)), width)))

    def sweep_all(tile, needed, diagonal=None):
        if diagonal is None:
            tile(qi, DF_TK, diag_mask)
        else:
            diagonal()
        rem = lax.rem(qi, DF_GROUP)
        sweep(rem, lambda i: qi - 1 - i, DF_TK, tile, needed)
        sweep(qi // DF_GROUP, lambda i: qi - rem - (i + 1) * DF_GROUP, DF_GROUP * DF_TK, tile, needed)

    def fixed_shift():
        rowpos = lax.broadcasted_iota(jnp.int32, (DF_TQ, LANES), 0).astype(jnp.float32)
        shift = qk_bound + slope * jnp.concatenate([rowpos, rowpos], axis=0)
        l_s[...] = jnp.zeros((2 * DF_TQ, LANES), jnp.float32)
        acc[...] = jnp.zeros((2 * DF_TQ, LANES), jnp.float32)

        def tile(first_kb, width, mask):
            reps = width // LANES
            kblk, vblk, bias = load(first_kb, width)
            s = (lax.dot_general(qs, kblk, _NT, preferred_element_type=jnp.float32)
                 + bias - jnp.concatenate([shift] * reps, axis=1))
            if mask is not None:
                s = jnp.where(mask, s, NEG_BIG)
            p = jnp.exp2(s)
            part = p[:, :LANES]
            for r in range(1, reps):
                part = part + p[:, r * LANES:(r + 1) * LANES]
            l_s[...] += part
            acc[...] += jnp.dot(p.astype(jnp.bfloat16), vblk, preferred_element_type=jnp.float32)

        def needed(first_kb, width):
            return slope_s * last_rel(first_kb, width) > DF_DEAD_LOG2

        part_rows = DF_TQ // DF_DIAG_PARTS
        hrow = lax.broadcasted_iota(jnp.int32, (part_rows, part_rows), 0)
        hcol = lax.broadcasted_iota(jnp.int32, (part_rows, part_rows), 1)
        tri = hcol <= hrow

        def diagonal_rows(r0, width, mask):
            half = part_rows
            both = lambda x: jnp.concatenate([x[r0:r0 + half], x[DF_TQ + r0:DF_TQ + r0 + half]], axis=0)
            reps = width // LANES
            start = pl.multiple_of(qi * DF_TK, DF_TK)
            kblk = k_ref[0, pl.ds(start, width), :]
            vblk = v_ref[0, pl.ds(start, width), :]
            rel = lax.broadcasted_iota(jnp.int32, (1, width), 1).astype(jnp.float32)
            bias = jnp.concatenate([slope] * reps, axis=1) * rel
            s = (lax.dot_general(both(qs), kblk, _NT, preferred_element_type=jnp.float32)
                 + bias - jnp.concatenate([both(shift)] * reps, axis=1))
            p = jnp.exp2(jnp.where(jnp.concatenate([mask, mask], axis=0), s, NEG_BIG))
            part = p[:, :LANES]
            for r in range(1, reps):
                part = part + p[:, r * LANES:(r + 1) * LANES]
            pv = jnp.dot(p.astype(jnp.bfloat16), vblk, preferred_element_type=jnp.float32)
            for m in range(2):
                rows = slice(m * DF_TQ + r0, m * DF_TQ + r0 + half)
                l_s[rows, :] += part[m * half:(m + 1) * half]
                acc[rows, :] += pv[m * half:(m + 1) * half]

        def diagonal():
            for i in range(DF_DIAG_PARTS):
                mask = jnp.concatenate([jnp.ones_like(tri)] * i + [tri], axis=1)
                diagonal_rows(i * part_rows, (i + 1) * part_rows, mask)

        sweep_all(tile, needed, diagonal)
        l = jnp.sum(l_s[...], axis=-1, keepdims=True)
        return l[:DF_TQ], l[DF_TQ:]

    def running_max():
        m_s[...] = jnp.full((2 * DF_TQ, LANES), NEG_BIG, jnp.float32)
        l_s[...] = jnp.zeros((2 * DF_TQ, LANES), jnp.float32)
        acc[...] = jnp.zeros((2 * DF_TQ, LANES), jnp.float32)

        def tile(first_kb, width, mask):
            reps = width // LANES
            kblk, vblk, bias = load(first_kb, width)
            s = lax.dot_general(qs, kblk, _NT, preferred_element_type=jnp.float32) + bias
            if mask is not None:
                s = jnp.where(mask, s, NEG_BIG)
            m_old = m_s[...]
            m_new = jnp.maximum(m_old, jnp.max(s, axis=-1, keepdims=True))
            p = jnp.exp2(s - jnp.concatenate([m_new] * reps, axis=1))
            alpha = jnp.exp2(m_old - m_new)
            l_s[...] = alpha * l_s[...] + jnp.sum(p, axis=-1, keepdims=True)
            acc[...] = alpha * acc[...] + jnp.dot(p.astype(jnp.bfloat16), vblk,
                                                  preferred_element_type=jnp.float32)
            m_s[...] = m_new

        def needed(first_kb, width):
            return (qk_bound + slope_s * last_rel(first_kb, width) - jnp.min(m_s[...])
                    > DF_DEAD_LOG2)

        sweep_all(tile, needed)
        return l_s[:DF_TQ, :1], l_s[DF_TQ:, :1]

    l1, l2 = lax.cond(qk_bound < DF_FIXED_SHIFT_MAX, fixed_shift, running_max)

    lam = (jnp.exp(jnp.sum(lq1_ref[...] * lk1_ref[...], keepdims=True))
           - jnp.exp(jnp.sum(lq2_ref[...] * lk2_ref[...], keepdims=True)) + lam_init)
    o = acc[:DF_TQ, :] / l1 - lam * (acc[DF_TQ:, :] / l2)
    ms = jnp.mean(o * o, axis=-1, keepdims=True)
    o = o * lax.rsqrt(ms + EPS) * sg_ref[...] * (1.0 - lam_init)
    g = g_ref[0].astype(jnp.float32)
    o_ref[0] = (o * _silu(g)).astype(o_ref.dtype)


def _df_attention(proj, slopes, lq1, lk1, lq2, lk2, subln_g, q_norm_g, k_norm_g, lam_init):
    bsz, s, _ = proj.shape
    assert DF_TQ == DF_TK and s % DF_TQ == 0
    nq = s // DF_TQ
    vec = pl.BlockSpec((1, HEAD_DIM), lambda b, h, i: (0, 0))
    stat = pltpu.VMEM((2 * DF_TQ, LANES), jnp.float32)
    return pl.pallas_call(
        functools.partial(_df_kernel, lam_init=lam_init),
        out_shape=jax.ShapeDtypeStruct((bsz, s, DF_WIDTH), jnp.bfloat16),
        grid=(bsz, DF_HEADS, nq),
        in_specs=[pl.BlockSpec((1, DF_TQ, LANES), lambda b, h, i: (b, i, CB_DF_Q + h)),
                  pl.BlockSpec((1, s, LANES), lambda b, h, i: (b, 0, CB_DF_K + h)),
                  pl.BlockSpec((1, s, LANES), lambda b, h, i: (b, 0, CB_DF_V + h)),
                  pl.BlockSpec((1, DF_TQ, LANES), lambda b, h, i: (b, i, CB_DF_G + h)),
                  pl.BlockSpec((1, 1, LANES), lambda b, h, i: (h, 0, 0)),
                  vec, vec, vec, vec,
                  pl.BlockSpec((1, LANES), lambda b, h, i: (0, 0)),
                  vec, vec],
        out_specs=pl.BlockSpec((1, DF_TQ, LANES), lambda b, h, i: (b, i, h)),
        scratch_shapes=[stat, stat, stat],
        compiler_params=pltpu.CompilerParams(vmem_limit_bytes=VMEM_LIMIT),
        name="df_attn",
    )(proj, proj, proj, proj, slopes, lq1, lk1, lq2, lk2, subln_g, q_norm_g, k_norm_g)


def _out_proj_kernel(x_ref, sb_ref, df_ref, w_ref, gate_ref, o_ref):
    y = (jnp.dot(sb_ref[0], w_ref[:SB_WIDTH, :], preferred_element_type=jnp.float32)
         + jnp.dot(df_ref[0], w_ref[SB_WIDTH:, :], preferred_element_type=jnp.float32))
    o_ref[0] = x_ref[0] + gate_ref[0] * y


def _out_proj(x, sb_o, df_o, w_out_bf16, gate):
    bsz, s, d = x.shape
    tm = ROWS_OUT
    return pl.pallas_call(
        _out_proj_kernel,
        out_shape=jax.ShapeDtypeStruct((bsz, s, d), jnp.float32),
        grid=(bsz, s // tm),
        in_specs=[pl.BlockSpec((1, tm, d), lambda b, i: (b, i, 0)),
                  pl.BlockSpec((1, tm, SB_WIDTH), lambda b, i: (b, i, 0)),
                  pl.BlockSpec((1, tm, DF_WIDTH), lambda b, i: (b, i, 0)),
                  pl.BlockSpec((SB_WIDTH + DF_WIDTH, d), lambda b, i: (0, 0)),
                  pl.BlockSpec((1, 1, d), lambda b, i: (b, 0, 0))],
        out_specs=pl.BlockSpec((1, tm, d), lambda b, i: (b, i, 0)),
        compiler_params=pltpu.CompilerParams(vmem_limit_bytes=VMEM_LIMIT),
        name="out_proj",
    )(x, sb_o, df_o, w_out_bf16, gate.reshape(bsz, 1, d))


def _layer(x, c, layer_idx, norm_g, w_ada, b_ada, w_in, q_norm_g, k_norm_g,
           lambda_q1, lambda_k1, lambda_q2, lambda_k2, subln_g, w_out):
    d = x.shape[-1]
    lam_init = 0.8 - 0.6 * math.exp(-0.3 * layer_idx)
    mod = _adaln_mod(c, w_ada, b_ada)
    shift, scale, gate = mod[:, :d], mod[:, d:2 * d], mod[:, 2 * d:]
    proj = _in_proj(x, norm_g, shift, scale, w_in.astype(jnp.bfloat16), q_norm_g, k_norm_g)
    sb_o = _sb_attention(proj)
    slopes = jnp.asarray([2.0 ** (-8.0 * (h + 1) / DF_HEADS) for h in range(DF_HEADS)], jnp.float32)
    slopes = jnp.broadcast_to(slopes[:, None, None], (DF_HEADS, 1, LANES))
    row = lambda v: v.reshape(1, -1)
    df_o = _df_attention(proj, slopes, row(lambda_q1), row(lambda_k1), row(lambda_q2),
                         row(lambda_k2), row(subln_g), row(q_norm_g), row(k_norm_g), lam_init)
    return _out_proj(x, sb_o, df_o, w_out.astype(jnp.bfloat16), gate)


@jax.jit
def kernel(x, c, norm_g, w_ada, b_ada, w_in, q_norm_g, k_norm_g, lambda_q1, lambda_k1,
           lambda_q2, lambda_k2, subln_g, w_out):
    for l in range(norm_g.shape[0]):
        x = _layer(x, c, l, norm_g[l], w_ada[l], b_ada[l], w_in[l], q_norm_g[l], k_norm_g[l],
                   lambda_q1[l], lambda_k1[l], lambda_q2[l], lambda_k2[l], subln_g[l], w_out[l])
    return x
```

```python
import functools
import math

import jax
import jax.numpy as jnp
from jax import lax
from jax.experimental import pallas as pl
from jax.experimental.pallas import tpu as pltpu

D_MODEL = 1024
SB_HEADS = 8
DF_HEADS = 4
HEAD_DIM = 64
LANES = 128
SB_WIDTH = SB_HEADS * HEAD_DIM
DF_WIDTH = DF_HEADS * 2 * HEAD_DIM
IN_COLS = 4 * SB_WIDTH + 4 * DF_WIDTH
EPS = 1e-6
NEG_BIG = -1e30

CB_SB_Q, CB_SB_K, CB_SB_V, CB_SB_G = 0, 4, 8, 12
CB_DF_Q, CB_DF_K, CB_DF_V, CB_DF_G = 16, 20, 24, 28

ROWS_PROJ = 512
ROWS_OUT = 1024
TQ = 256
TK = 256
DF_TQ = 1024
DF_TK = 1024
SB_QT = 4
SB_FIRST = 3
SB_GROUP = 4
DF_GROUP = 1
DF_DIAG_PARTS = 8
LOG2E = math.log2(math.e)
SB_DEAD_LOG2 = -150.0
DF_DEAD_LOG2 = -150.0
DF_FIXED_SHIFT_MAX = 48.0
DF_BOUND_MARGIN = 1.02
PROJ_CHUNK = 512
VMEM_LIMIT = 48 * 1024 * 1024

_NT = (((1,), (1,)), ((), ()))


def _silu(g):
    return g / (1.0 + jnp.exp(-g))


def _adaln_kernel(ct_ref, w_ref, b_ref, o_ref):
    w = w_ref[...]
    rows = []
    for b in range(o_ref.shape[0]):
        col = ct_ref[:, b:b + 1]
        rows.append(jnp.sum(col * w, axis=0, keepdims=True))
    o_ref[...] = jnp.concatenate(rows, axis=0) + b_ref[...]


def _adaln_mod(c, w_ada, b_ada):
    bsz, d = c.shape
    n = w_ada.shape[1]
    tn = 512
    return pl.pallas_call(
        _adaln_kernel,
        out_shape=jax.ShapeDtypeStruct((bsz, n), jnp.float32),
        grid=(n // tn,),
        in_specs=[pl.BlockSpec((d, bsz), lambda j: (0, 0)),
                  pl.BlockSpec((d, tn), lambda j: (0, j)),
                  pl.BlockSpec((1, tn), lambda j: (0, j))],
        out_specs=pl.BlockSpec((bsz, tn), lambda j: (0, j)),
        name="adaln_mod",
    )(c.T, w_ada, b_ada.reshape(1, n))


def _group_rmsnorm(y, gain):
    outs = []
    lane = lax.broadcasted_iota(jnp.int32, (1, LANES), 1)
    lo = lane < HEAD_DIM
    for j in range(y.shape[1] // LANES):
        blk = y[:, j * LANES:(j + 1) * LANES]
        sq = blk * blk
        s_lo = jnp.sum(jnp.where(lo, sq, 0.0), axis=-1, keepdims=True)
        s_hi = jnp.sum(jnp.where(lo, 0.0, sq), axis=-1, keepdims=True)
        ms = jnp.where(lo, s_lo, s_hi) * (1.0 / HEAD_DIM)
        outs.append(blk * lax.rsqrt(ms + EPS) * gain)
    return jnp.concatenate(outs, axis=1)


def _in_proj_kernel(x_ref, ng_ref, shift_ref, scale_ref, w_ref, qg_ref, kg_ref, o_ref):
    x = x_ref[0]
    ms = jnp.mean(x * x, axis=-1, keepdims=True)
    h = x * lax.rsqrt(ms + EPS) * ng_ref[...]
    h = (h * (1.0 + scale_ref[0]) + shift_ref[0]).astype(jnp.bfloat16)
    inv = LOG2E / math.sqrt(HEAD_DIM)
    for ci in range(IN_COLS // PROJ_CHUNK):
        c0 = ci * PROJ_CHUNK
        y = jnp.dot(h, w_ref[:, c0:c0 + PROJ_CHUNK], preferred_element_type=jnp.float32)
        cb = c0 // LANES
        if cb == CB_SB_Q:
            y = y * inv
        elif cb == CB_DF_Q:
            y = _group_rmsnorm(y, qg_ref[...]) * inv
        elif cb == CB_DF_K:
            y = _group_rmsnorm(y, kg_ref[...])
        o_ref[0, :, c0:c0 + PROJ_CHUNK] = y.astype(o_ref.dtype)


def _in_proj(x, norm_g, shift, scale, w_in_bf16, q_norm_g, k_norm_g):
    bsz, s, d = x.shape
    tm = ROWS_PROJ
    assert SB_WIDTH % PROJ_CHUNK == 0 and DF_WIDTH % PROJ_CHUNK == 0 and s % tm == 0
    qg = jnp.tile(q_norm_g.reshape(1, HEAD_DIM), (1, 2))
    kg = jnp.tile(k_norm_g.reshape(1, HEAD_DIM), (1, 2))
    return pl.pallas_call(
        _in_proj_kernel,
        out_shape=jax.ShapeDtypeStruct((bsz, s, IN_COLS), jnp.bfloat16),
        grid=(bsz, s // tm),
        in_specs=[pl.BlockSpec((1, tm, d), lambda b, i: (b, i, 0)),
                  pl.BlockSpec((1, d), lambda b, i: (0, 0)),
                  pl.BlockSpec((1, 1, d), lambda b, i: (b, 0, 0)),
                  pl.BlockSpec((1, 1, d), lambda b, i: (b, 0, 0)),
                  pl.BlockSpec((d, IN_COLS), lambda b, i: (0, 0)),
                  pl.BlockSpec((1, LANES), lambda b, i: (0, 0)),
                  pl.BlockSpec((1, LANES), lambda b, i: (0, 0))],
        out_specs=pl.BlockSpec((1, tm, IN_COLS), lambda b, i: (b, i, 0)),
        compiler_params=pltpu.CompilerParams(vmem_limit_bytes=VMEM_LIMIT),
        name="in_proj",
    )(x, norm_g.reshape(1, d), shift.reshape(bsz, 1, d), scale.reshape(bsz, 1, d),
      w_in_bf16, qg, kg)


def _sb_tile(qm, kblk, vblk, cum, carry, mask):
    z = lax.dot_general(qm, kblk, _NT, preferred_element_type=jnp.float32)
    nz = -z
    l1p = jnp.log(1.0 + jnp.exp2(jnp.minimum(z, nz))) * LOG2E
    lg = jnp.minimum(nz, 0.0) - l1p
    if mask is not None:
        lg = jnp.where(mask, lg, 0.0)
    rem = jnp.dot(lg.astype(jnp.bfloat16), cum, preferred_element_type=jnp.float32)
    a = jnp.exp2(lg + z + rem + carry)
    if mask is not None:
        a = jnp.where(mask, a, 0.0)
    o = jnp.dot(a.astype(jnp.bfloat16), vblk, preferred_element_type=jnp.float32)
    return o, carry + jnp.sum(lg, axis=-1, keepdims=True)


def _sb_kernel(q_ref, k_ref, v_ref, g_ref, o_ref, acc):
    step = pl.program_id(2)
    lane = lax.broadcasted_iota(jnp.int32, (1, LANES), 1)
    first = lane < HEAD_DIM
    row = lax.broadcasted_iota(jnp.int32, (TK, TK), 0)
    col = lax.broadcasted_iota(jnp.int32, (TK, TK), 1)
    cum = (row > col).astype(jnp.bfloat16)
    strict = col < row
    diag_mask = jnp.concatenate([strict, strict], axis=0)

    def stacked(u):
        q = q_ref[0, u * TQ:(u + 1) * TQ, :]
        zero = jnp.zeros_like(q)
        return jnp.concatenate([jnp.where(first, q, zero), jnp.where(first, zero, q)], axis=0)

    qs = [stacked(u) for u in range(SB_QT)]
    acc_of = [acc.at[u * 2 * TQ:(u + 1) * 2 * TQ, :] for u in range(SB_QT)]

    def tiles(qm, kbs, carry, masks):
        total = None
        for kb, mask in zip(kbs, masks):
            start = pl.multiple_of(kb * TK, TK)
            o, carry = _sb_tile(qm, k_ref[0, pl.ds(start, TK), :], v_ref[0, pl.ds(start, TK), :],
                                cum, carry, mask)
            total = o if total is None else total + o
        return total, carry

    zero_carry = jnp.zeros((2 * TQ, 1), jnp.float32)

    def first_block(first_step):
        carries = []
        for u in range(SB_QT):
            qt = step * SB_QT + u
            n = min(SB_FIRST, u + 1) if first_step else SB_FIRST
            o, carry = tiles(qs[u], [qt - t for t in range(n)], zero_carry,
                             [diag_mask] + [None] * (n - 1))
            acc_of[u][...] = o
            carries.append(carry)
        return tuple(carries)

    carries = lax.cond(step > 0, lambda: first_block(False), lambda: first_block(True))

    def alive(carry):
        return jnp.max(carry) > SB_DEAD_LOG2

    def sweep(u, n_steps, kbs_of_step, carry):
        def cond(state):
            i, _, live = state
            return jnp.logical_and(i < n_steps, live)

        def body(state):
            i, carry, _ = state
            kbs = kbs_of_step(i)
            o, carry = tiles(qs[u], kbs, carry, [None] * len(kbs))
            acc_of[u][...] += o
            return i + 1, carry, alive(carry)

        return lax.while_loop(cond, body, (jnp.int32(0), carry, alive(carry)))[1]

    any_alive = alive(carries[0])
    for carry in carries[1:]:
        any_alive = jnp.logical_or(any_alive, alive(carry))

    @pl.when(any_alive)
    def _():
        for u in range(SB_QT):
            left = jnp.maximum(step * SB_QT + u - (SB_FIRST - 1), 0)
            rem = lax.rem(left, SB_GROUP)
            carry = sweep(u, rem, lambda i, left=left: [left - 1 - i], carries[u])
            sweep(u, left // SB_GROUP,
                  lambda i, left=left, rem=rem: [left - rem - i * SB_GROUP - 1 - t
                                                 for t in range(SB_GROUP)], carry)

    for u in range(SB_QT):
        out = jnp.where(first, acc_of[u][:TQ, :], acc_of[u][TQ:, :])
        g = g_ref[0, u * TQ:(u + 1) * TQ, :].astype(jnp.float32)
        o_ref[0, u * TQ:(u + 1) * TQ, :] = (out * _silu(g)).astype(o_ref.dtype)


def _sb_attention(proj):
    bsz, s, _ = proj.shape
    rows = SB_QT * TQ
    assert TQ == TK and s % rows == 0 and SB_QT >= SB_FIRST - 1
    return pl.pallas_call(
        _sb_kernel,
        out_shape=jax.ShapeDtypeStruct((bsz, s, SB_WIDTH), jnp.bfloat16),
        grid=(bsz, SB_WIDTH // LANES, s // rows),
        in_specs=[pl.BlockSpec((1, rows, LANES), lambda b, h, i: (b, i, CB_SB_Q + h)),
                  pl.BlockSpec((1, s, LANES), lambda b, h, i: (b, 0, CB_SB_K + h)),
                  pl.BlockSpec((1, s, LANES), lambda b, h, i: (b, 0, CB_SB_V + h)),
                  pl.BlockSpec((1, rows, LANES), lambda b, h, i: (b, i, CB_SB_G + h))],
        out_specs=pl.BlockSpec((1, rows, LANES), lambda b, h, i: (b, i, h)),
        scratch_shapes=[pltpu.VMEM((SB_QT * 2 * TQ, LANES), jnp.float32)],
        compiler_params=pltpu.CompilerParams(vmem_limit_bytes=VMEM_LIMIT),
        name="sb_attn",
    )(proj, proj, proj, proj)


def _df_kernel(q_ref, k_ref, v_ref, g_ref, slope_ref, lq1_ref, lk1_ref, lq2_ref, lk2_ref, sg_ref,
               qg_ref, kg_ref, o_ref, m_s, l_s, acc, *, lam_init):
    qi = pl.program_id(2)
    q = q_ref[0]
    lane = lax.broadcasted_iota(jnp.int32, (1, LANES), 1)
    first = lane < HEAD_DIM
    zero = jnp.zeros_like(q)
    qs = jnp.concatenate([jnp.where(first, q, zero), jnp.where(first, zero, q)], axis=0)
    slope = slope_ref[0] * LOG2E
    row = lax.broadcasted_iota(jnp.int32, (DF_TQ, DF_TK), 0)
    col = lax.broadcasted_iota(jnp.int32, (DF_TQ, DF_TK), 1)
    causal = col <= row
    diag_mask = jnp.concatenate([causal, causal], axis=0)

    qk_bound = (jnp.max(jnp.abs(qg_ref[...])) * jnp.max(jnp.abs(kg_ref[...]))
                * (math.sqrt(HEAD_DIM) * LOG2E * DF_BOUND_MARGIN))
    slope_s = jnp.max(slope)

    def load(first_kb, width):
        start = pl.multiple_of(first_kb * DF_TK, DF_TK)
        rel = lax.broadcasted_iota(jnp.int32, (1, width), 1).astype(jnp.float32)
        off = ((first_kb - qi) * DF_TK).astype(jnp.float32)
        bias = jnp.concatenate([slope] * (width // LANES), axis=1) * (rel + off)
        return k_ref[0, pl.ds(start, width), :], v_ref[0, pl.ds(start, width), :], bias

    def last_rel(first_kb, width):
        return ((first_kb - qi) * DF_TK + (width - 1)).astype(jnp.float32)

    def sweep(n_steps, first_kb_of_step, width, tile, needed):
        def cond(state):
            i, live = state
            return jnp.logical_and(i < n_steps, live)

        def body(state):
            i, _ = state
            tile(first_kb_of_step(i), width, None)
            return i + 1, needed(first_kb_of_step(i + 1), width)

        lax.while_loop(cond, body, (jnp.int32(0), needed(first_kb_of_step(jnp.int32(0)), width)))

    def sweep_all(tile, needed, diagonal=None):
        if diagonal is None:
            tile(qi, DF_TK, diag_mask)
        else:
            diagonal()
        rem = lax.rem(qi, DF_GROUP)
        sweep(rem, lambda i: qi - 1 - i, DF_TK, tile, needed)
        sweep(qi // DF_GROUP, lambda i: qi - rem - (i + 1) * DF_GROUP, DF_GROUP * DF_TK, tile, needed)

    def fixed_shift():
        rowpos = lax.broadcasted_iota(jnp.int32, (DF_TQ, LANES), 0).astype(jnp.float32)
        shift = qk_bound + slope * jnp.concatenate([rowpos, rowpos], axis=0)
        l_s[...] = jnp.zeros((2 * DF_TQ, LANES), jnp.float32)
        acc[...] = jnp.zeros((2 * DF_TQ, LANES), jnp.float32)

        def tile(first_kb, width, mask):
            reps = width // LANES
            kblk, vblk, bias = load(first_kb, width)
            s = (lax.dot_general(qs, kblk, _NT, preferred_element_type=jnp.float32)
                 + bias - jnp.concatenate([shift] * reps, axis=1))
            if mask is not None:
                s = jnp.where(mask, s, NEG_BIG)
            p = jnp.exp2(s)
            part = p[:, :LANES]
            for r in range(1, reps):
                part = part + p[:, r * LANES:(r + 1) * LANES]
            l_s[...] += part
            acc[...] += jnp.dot(p.astype(jnp.bfloat16), vblk, preferred_element_type=jnp.float32)

        def needed(first_kb, width):
            return slope_s * last_rel(first_kb, width) > DF_DEAD_LOG2

        part_rows = DF_TQ // DF_DIAG_PARTS
        hrow = lax.broadcasted_iota(jnp.int32, (part_rows, part_rows), 0)
        hcol = lax.broadcasted_iota(jnp.int32, (part_rows, part_rows), 1)
        tri = hcol <= hrow

        def diagonal_rows(r0, width, mask):
            half = part_rows
            both = lambda x: jnp.concatenate([x[r0:r0 + half], x[DF_TQ + r0:DF_TQ + r0 + half]], axis=0)
            reps = width // LANES
            start = pl.multiple_of(qi * DF_TK, DF_TK)
            kblk = k_ref[0, pl.ds(start, width), :]
            vblk = v_ref[0, pl.ds(start, width), :]
            rel = lax.broadcasted_iota(jnp.int32, (1, width), 1).astype(jnp.float32)
            bias = jnp.concatenate([slope] * reps, axis=1) * rel
            s = (lax.dot_general(both(qs), kblk, _NT, preferred_element_type=jnp.float32)
                 + bias - jnp.concatenate([both(shift)] * reps, axis=1))
            p = jnp.exp2(jnp.where(jnp.concatenate([mask, mask], axis=0), s, NEG_BIG))
            part = p[:, :LANES]
            for r in range(1, reps):
                part = part + p[:, r * LANES:(r + 1) * LANES]
            pv = jnp.dot(p.astype(jnp.bfloat16), vblk, preferred_element_type=jnp.float32)
            for m in range(2):
                rows = slice(m * DF_TQ + r0, m * DF_TQ + r0 + half)
                l_s[rows, :] += part[m * half:(m + 1) * half]
                acc[rows, :] += pv[m * half:(m + 1) * half]

        def diagonal():
            for i in range(DF_DIAG_PARTS):
                mask = jnp.concatenate([jnp.ones_like(tri)] * i + [tri], axis=1)
                diagonal_rows(i * part_rows, (i + 1) * part_rows, mask)

        sweep_all(tile, needed, diagonal)
        l = jnp.sum(l_s[...], axis=-1, keepdims=True)
        return l[:DF_TQ], l[DF_TQ:]

    def running_max():
        m_s[...] = jnp.full((2 * DF_TQ, LANES), NEG_BIG, jnp.float32)
        l_s[...] = jnp.zeros((2 * DF_TQ, LANES), jnp.float32)
        acc[...] = jnp.zeros((2 * DF_TQ, LANES), jnp.float32)

        def tile(first_kb, width, mask):
            reps = width // LANES
            kblk, vblk, bias = load(first_kb, width)
            s = lax.dot_general(qs, kblk, _NT, preferred_element_type=jnp.float32) + bias
            if mask is not None:
                s = jnp.where(mask, s, NEG_BIG)
            m_old = m_s[...]
            m_new = jnp.maximum(m_old, jnp.max(s, axis=-1, keepdims=True))
            p = jnp.exp2(s - jnp.concatenate([m_new] * reps, axis=1))
            alpha = jnp.exp2(m_old - m_new)
            l_s[...] = alpha * l_s[...] + jnp.sum(p, axis=-1, keepdims=True)
            acc[...] = alpha * acc[...] + jnp.dot(p.astype(jnp.bfloat16), vblk,
                                                  preferred_element_type=jnp.float32)
            m_s[...] = m_new

        def needed(first_kb, width):
            return (qk_bound + slope_s * last_rel(first_kb, width) - jnp.min(m_s[...])
                    > DF_DEAD_LOG2)

        sweep_all(tile, needed)
        return l_s[:DF_TQ, :1], l_s[DF_TQ:, :1]

    l1, l2 = lax.cond(qk_bound < DF_FIXED_SHIFT_MAX, fixed_shift, running_max)

    lam = (jnp.exp(jnp.sum(lq1_ref[...] * lk1_ref[...], keepdims=True))
           - jnp.exp(jnp.sum(lq2_ref[...] * lk2_ref[...], keepdims=True)) + lam_init)
    o = acc[:DF_TQ, :] / l1 - lam * (acc[DF_TQ:, :] / l2)
    ms = jnp.mean(o * o, axis=-1, keepdims=True)
    o = o * lax.rsqrt(ms + EPS) * sg_ref[...] * (1.0 - lam_init)
    g = g_ref[0].astype(jnp.float32)
    o_ref[0] = (o * _silu(g)).astype(o_ref.dtype)


def _df_attention(proj, slopes, lq1, lk1, lq2, lk2, subln_g, q_norm_g, k_norm_g, lam_init):
    bsz, s, _ = proj.shape
    assert DF_TQ == DF_TK and s % DF_TQ == 0
    nq = s // DF_TQ
    vec = pl.BlockSpec((1, HEAD_DIM), lambda b, h, i: (0, 0))
    stat = pltpu.VMEM((2 * DF_TQ, LANES), jnp.float32)
    return pl.pallas_call(
        functools.partial(_df_kernel, lam_init=lam_init),
        out_shape=jax.ShapeDtypeStruct((bsz, s, DF_WIDTH), jnp.bfloat16),
        grid=(bsz, DF_HEADS, nq),
        in_specs=[pl.BlockSpec((1, DF_TQ, LANES), lambda b, h, i: (b, i, CB_DF_Q + h)),
                  pl.BlockSpec((1, s, LANES), lambda b, h, i: (b, 0, CB_DF_K + h)),
                  pl.BlockSpec((1, s, LANES), lambda b, h, i: (b, 0, CB_DF_V + h)),
                  pl.BlockSpec((1, DF_TQ, LANES), lambda b, h, i: (b, i, CB_DF_G + h)),
                  pl.BlockSpec((1, 1, LANES), lambda b, h, i: (h, 0, 0)),
                  vec, vec, vec, vec,
                  pl.BlockSpec((1, LANES), lambda b, h, i: (0, 0)),
                  vec, vec],
        out_specs=pl.BlockSpec((1, DF_TQ, LANES), lambda b, h, i: (b, i, h)),
        scratch_shapes=[stat, stat, stat],
        compiler_params=pltpu.CompilerParams(vmem_limit_bytes=VMEM_LIMIT),
        name="df_attn",
    )(proj, proj, proj, proj, slopes, lq1, lk1, lq2, lk2, subln_g, q_norm_g, k_norm_g)


def _out_proj_kernel(x_ref, sb_ref, df_ref, w_ref, gate_ref, o_ref):
    y = (jnp.dot(sb_ref[0], w_ref[:SB_WIDTH, :], preferred_element_type=jnp.float32)
         + jnp.dot(df_ref[0], w_ref[SB_WIDTH:, :], preferred_element_type=jnp.float32))
    o_ref[0] = x_ref[0] + gate_ref[0] * y


def _out_proj(x, sb_o, df_o, w_out_bf16, gate):
    bsz, s, d = x.shape
    tm = ROWS_OUT
    return pl.pallas_call(
        _out_proj_kernel,
        out_shape=jax.ShapeDtypeStruct((bsz, s, d), jnp.float32),
        grid=(bsz, s // tm),
        in_specs=[pl.BlockSpec((1, tm, d), lambda b, i: (b, i, 0)),
                  pl.BlockSpec((1, tm, SB_WIDTH), lambda b, i: (b, i, 0)),
                  pl.BlockSpec((1, tm, DF_WIDTH), lambda b, i: (b, i, 0)),
                  pl.BlockSpec((SB_WIDTH + DF_WIDTH, d), lambda b, i: (0, 0)),
                  pl.BlockSpec((1, 1, d), lambda b, i: (b, 0, 0))],
        out_specs=pl.BlockSpec((1, tm, d), lambda b, i: (b, i, 0)),
        compiler_params=pltpu.CompilerParams(vmem_limit_bytes=VMEM_LIMIT),
        name="out_proj",
    )(x, sb_o, df_o, w_out_bf16, gate.reshape(bsz, 1, d))


def _layer(x, c, layer_idx, norm_g, w_ada, b_ada, w_in, q_norm_g, k_norm_g,
           lambda_q1, lambda_k1, lambda_q2, lambda_k2, subln_g, w_out):
    d = x.shape[-1]
    lam_init = 0.8 - 0.6 * math.exp(-0.3 * layer_idx)
    mod = _adaln_mod(c, w_ada, b_ada)
    shift, scale, gate = mod[:, :d], mod[:, d:2 * d], mod[:, 2 * d:]
    proj = _in_proj(x, norm_g, shift, scale, w_in.astype(jnp.bfloat16), q_norm_g, k_norm_g)
    sb_o = _sb_attention(proj)
    slopes = jnp.asarray([2.0 ** (-8.0 * (h + 1) / DF_HEADS) for h in range(DF_HEADS)], jnp.float32)
    slopes = jnp.broadcast_to(slopes[:, None, None], (DF_HEADS, 1, LANES))
    row = lambda v: v.reshape(1, -1)
    df_o = _df_attention(proj, slopes, row(lambda_q1), row(lambda_k1), row(lambda_q2),
                         row(lambda_k2), row(subln_g), row(q_norm_g), row(k_norm_g), lam_init)
    return _out_proj(x, sb_o, df_o, w_out.astype(jnp.bfloat16), gate)


@jax.jit
def kernel(x, c, norm_g, w_ada, b_ada, w_in, q_norm_g, k_norm_g, lambda_q1, lambda_k1,
           lambda_q2, lambda_k2, subln_g, w_out):
    for l in range(norm_g.shape[0]):
        x = _layer(x, c, l, norm_g[l], w_ada[l], b_ada[l], w_in[l], q_norm_g[l], k_norm_g[l],
                   lambda_q1[l], lambda_k1[l], lambda_q2[l], lambda_k2[l], subln_g[l], w_out[l])
    return x
```

```python
import functools
import math

import jax
import jax.numpy as jnp
from jax import lax
from jax.experimental import pallas as pl
from jax.experimental.pallas import tpu as pltpu

D_MODEL = 1024
SB_HEADS = 8
DF_HEADS = 4
HEAD_DIM = 64
LANES = 128
SB_WIDTH = SB_HEADS * HEAD_DIM
DF_WIDTH = DF_HEADS * 2 * HEAD_DIM
IN_COLS = 4 * SB_WIDTH + 4 * DF_WIDTH
EPS = 1e-6
NEG_BIG = -1e30

CB_SB_Q, CB_SB_K, CB_SB_V, CB_SB_G = 0, 4, 8, 12
CB_DF_Q, CB_DF_K, CB_DF_V, CB_DF_G = 16, 20, 24, 28

ROWS_PROJ = 512
ROWS_OUT = 1024
TQ = 256
TK = 256
DF_TQ = 1024
DF_TK = 1024
SB_QT = 4
SB_FIRST = 3
SB_GROUP = 4
DF_GROUP = 1
DF_DIAG_PARTS = 4
LOG2E = math.log2(math.e)
SB_DEAD_LOG2 = -150.0
DF_DEAD_LOG2 = -150.0
DF_FIXED_SHIFT_MAX = 48.0
DF_BOUND_MARGIN = 1.02
PROJ_CHUNK = 512
VMEM_LIMIT = 48 * 1024 * 1024

_NT = (((1,), (1,)), ((), ()))


def _silu(g):
    return g / (1.0 + jnp.exp(-g))


def _adaln_kernel(ct_ref, w_ref, b_ref, o_ref):
    w = w_ref[...]
    rows = []
    for b in range(o_ref.shape[0]):
        col = ct_ref[:, b:b + 1]
        rows.append(jnp.sum(col * w, axis=0, keepdims=True))
    o_ref[...] = jnp.concatenate(rows, axis=0) + b_ref[...]


def _adaln_mod(c, w_ada, b_ada):
    bsz, d = c.shape
    n = w_ada.shape[1]
    tn = 512
    return pl.pallas_call(
        _adaln_kernel,
        out_shape=jax.ShapeDtypeStruct((bsz, n), jnp.float32),
        grid=(n // tn,),
        in_specs=[pl.BlockSpec((d, bsz), lambda j: (0, 0)),
                  pl.BlockSpec((d, tn), lambda j: (0, j)),
                  pl.BlockSpec((1, tn), lambda j: (0, j))],
        out_specs=pl.BlockSpec((bsz, tn), lambda j: (0, j)),
        name="adaln_mod",
    )(c.T, w_ada, b_ada.reshape(1, n))


def _group_rmsnorm(y, gain):
    outs = []
    lane = lax.broadcasted_iota(jnp.int32, (1, LANES), 1)
    lo = lane < HEAD_DIM
    for j in range(y.shape[1] // LANES):
        blk = y[:, j * LANES:(j + 1) * LANES]
        sq = blk * blk
        s_lo = jnp.sum(jnp.where(lo, sq, 0.0), axis=-1, keepdims=True)
        s_hi = jnp.sum(jnp.where(lo, 0.0, sq), axis=-1, keepdims=True)
        ms = jnp.where(lo, s_lo, s_hi) * (1.0 / HEAD_DIM)
        outs.append(blk * lax.rsqrt(ms + EPS) * gain)
    return jnp.concatenate(outs, axis=1)


def _in_proj_kernel(x_ref, ng_ref, shift_ref, scale_ref, w_ref, qg_ref, kg_ref, o_ref):
    x = x_ref[0]
    ms = jnp.mean(x * x, axis=-1, keepdims=True)
    h = x * lax.rsqrt(ms + EPS) * ng_ref[...]
    h = (h * (1.0 + scale_ref[0]) + shift_ref[0]).astype(jnp.bfloat16)
    inv = LOG2E / math.sqrt(HEAD_DIM)
    for ci in range(IN_COLS // PROJ_CHUNK):
        c0 = ci * PROJ_CHUNK
        y = jnp.dot(h, w_ref[:, c0:c0 + PROJ_CHUNK], preferred_element_type=jnp.float32)
        cb = c0 // LANES
        if cb == CB_SB_Q:
            y = y * inv
        elif cb == CB_DF_Q:
            y = _group_rmsnorm(y, qg_ref[...]) * inv
        elif cb == CB_DF_K:
            y = _group_rmsnorm(y, kg_ref[...])
        o_ref[0, :, c0:c0 + PROJ_CHUNK] = y.astype(o_ref.dtype)


def _in_proj(x, norm_g, shift, scale, w_in_bf16, q_norm_g, k_norm_g):
    bsz, s, d = x.shape
    tm = ROWS_PROJ
    assert SB_WIDTH % PROJ_CHUNK == 0 and DF_WIDTH % PROJ_CHUNK == 0 and s % tm == 0
    qg = jnp.tile(q_norm_g.reshape(1, HEAD_DIM), (1, 2))
    kg = jnp.tile(k_norm_g.reshape(1, HEAD_DIM), (1, 2))
    return pl.pallas_call(
        _in_proj_kernel,
        out_shape=jax.ShapeDtypeStruct((bsz, s, IN_COLS), jnp.bfloat16),
        grid=(bsz, s // tm),
        in_specs=[pl.BlockSpec((1, tm, d), lambda b, i: (b, i, 0)),
                  pl.BlockSpec((1, d), lambda b, i: (0, 0)),
                  pl.BlockSpec((1, 1, d), lambda b, i: (b, 0, 0)),
                  pl.BlockSpec((1, 1, d), lambda b, i: (b, 0, 0)),
                  pl.BlockSpec((d, IN_COLS), lambda b, i: (0, 0)),
                  pl.BlockSpec((1, LANES), lambda b, i: (0, 0)),
                  pl.BlockSpec((1, LANES), lambda b, i: (0, 0))],
        out_specs=pl.BlockSpec((1, tm, IN_COLS), lambda b, i: (b, i, 0)),
        compiler_params=pltpu.CompilerParams(vmem_limit_bytes=VMEM_LIMIT),
        name="in_proj",
    )(x, norm_g.reshape(1, d), shift.reshape(bsz, 1, d), scale.reshape(bsz, 1, d),
      w_in_bf16, qg, kg)


def _sb_tile(qm, kblk, vblk, cum, carry, mask):
    z = lax.dot_general(qm, kblk, _NT, preferred_element_type=jnp.float32)
    nz = -z
    l1p = jnp.log(1.0 + jnp.exp2(jnp.minimum(z, nz))) * LOG2E
    lg = jnp.minimum(nz, 0.0) - l1p
    if mask is not None:
        lg = jnp.where(mask, lg, 0.0)
    rem = jnp.dot(lg.astype(jnp.bfloat16), cum, preferred_element_type=jnp.float32)
    a = jnp.exp2(lg + z + rem + carry)
    if mask is not None:
        a = jnp.where(mask, a, 0.0)
    o = jnp.dot(a.astype(jnp.bfloat16), vblk, preferred_element_type=jnp.float32)
    return o, carry + jnp.sum(lg, axis=-1, keepdims=True)


def _sb_kernel(q_ref, k_ref, v_ref, g_ref, o_ref, acc):
    step = pl.program_id(2)
    lane = lax.broadcasted_iota(jnp.int32, (1, LANES), 1)
    first = lane < HEAD_DIM
    row = lax.broadcasted_iota(jnp.int32, (TK, TK), 0)
    col = lax.broadcasted_iota(jnp.int32, (TK, TK), 1)
    cum = (row > col).astype(jnp.bfloat16)
    strict = col < row
    diag_mask = jnp.concatenate([strict, strict], axis=0)

    def stacked(u):
        q = q_ref[0, u * TQ:(u + 1) * TQ, :]
        zero = jnp.zeros_like(q)
        return jnp.concatenate([jnp.where(first, q, zero), jnp.where(first, zero, q)], axis=0)

    qs = [stacked(u) for u in range(SB_QT)]
    acc_of = [acc.at[u * 2 * TQ:(u + 1) * 2 * TQ, :] for u in range(SB_QT)]

    half = TQ // 2
    hrow = lax.broadcasted_iota(jnp.int32, (half, half), 0)
    hcol = lax.broadcasted_iota(jnp.int32, (half, half), 1)
    tri = hcol < hrow
    band_masks = [jnp.concatenate([tri, tri], axis=0),
                  jnp.concatenate([jnp.concatenate([jnp.ones_like(tri), tri], axis=1)] * 2, axis=0)]

    def diagonal_tile(qm, kblk, vblk):
        outs, carries = [], []
        for band in range(2):
            r0 = band * half
            qb = jnp.concatenate([qm[r0:r0 + half], qm[TQ + r0:TQ + r0 + half]], axis=0)
            width = (band + 1) * half
            o, c = _sb_tile(qb, kblk[:width], vblk[:width], cum[:width, :width],
                            jnp.zeros((2 * half, 1), jnp.float32), band_masks[band])
            outs.append(o)
            carries.append(c)
        order = [(0, 0), (1, 0), (0, 1), (1, 1)]
        pick = lambda xs: jnp.concatenate([xs[b][hd * half:(hd + 1) * half] for b, hd in order], axis=0)
        return pick(outs), pick(carries)

    def tiles(qm, kbs, carry, masks):
        total = None
        for kb, mask in zip(kbs, masks):
            start = pl.multiple_of(kb * TK, TK)
            kblk, vblk = k_ref[0, pl.ds(start, TK), :], v_ref[0, pl.ds(start, TK), :]
            if mask is not None:
                o, carry = diagonal_tile(qm, kblk, vblk)
            else:
                o, carry = _sb_tile(qm, kblk, vblk, cum, carry, None)
            total = o if total is None else total + o
        return total, carry

    zero_carry = jnp.zeros((2 * TQ, 1), jnp.float32)

    def first_block(first_step):
        carries = []
        for u in range(SB_QT):
            qt = step * SB_QT + u
            n = min(SB_FIRST, u + 1) if first_step else SB_FIRST
            o, carry = tiles(qs[u], [qt - t for t in range(n)], zero_carry,
                             [diag_mask] + [None] * (n - 1))
            acc_of[u][...] = o
            carries.append(carry)
        return tuple(carries)

    carries = lax.cond(step > 0, lambda: first_block(False), lambda: first_block(True))

    def alive(carry):
        return jnp.max(carry) > SB_DEAD_LOG2

    def sweep(u, n_steps, kbs_of_step, carry):
        def cond(state):
            i, _, live = state
            return jnp.logical_and(i < n_steps, live)

        def body(state):
            i, carry, _ = state
            kbs = kbs_of_step(i)
            o, carry = tiles(qs[u], kbs, carry, [None] * len(kbs))
            acc_of[u][...] += o
            return i + 1, carry, alive(carry)

        return lax.while_loop(cond, body, (jnp.int32(0), carry, alive(carry)))[1]

    any_alive = alive(carries[0])
    for carry in carries[1:]:
        any_alive = jnp.logical_or(any_alive, alive(carry))

    @pl.when(any_alive)
    def _():
        for u in range(SB_QT):
            left = jnp.maximum(step * SB_QT + u - (SB_FIRST - 1), 0)
            rem = lax.rem(left, SB_GROUP)
            carry = sweep(u, rem, lambda i, left=left: [left - 1 - i], carries[u])
            sweep(u, left // SB_GROUP,
                  lambda i, left=left, rem=rem: [left - rem - i * SB_GROUP - 1 - t
                                                 for t in range(SB_GROUP)], carry)

    for u in range(SB_QT):
        out = jnp.where(first, acc_of[u][:TQ, :], acc_of[u][TQ:, :])
        g = g_ref[0, u * TQ:(u + 1) * TQ, :].astype(jnp.float32)
        o_ref[0, u * TQ:(u + 1) * TQ, :] = (out * _silu(g)).astype(o_ref.dtype)


def _sb_attention(proj):
    bsz, s, _ = proj.shape
    rows = SB_QT * TQ
    assert TQ == TK and s % rows == 0 and SB_QT >= SB_FIRST - 1
    return pl.pallas_call(
        _sb_kernel,
        out_shape=jax.ShapeDtypeStruct((bsz, s, SB_WIDTH), jnp.bfloat16),
        grid=(bsz, SB_WIDTH // LANES, s // rows),
        in_specs=[pl.BlockSpec((1, rows, LANES), lambda b, h, i: (b, i, CB_SB_Q + h)),
                  pl.BlockSpec((1, s, LANES), lambda b, h, i: (b, 0, CB_SB_K + h)),
                  pl.BlockSpec((1, s, LANES), lambda b, h, i: (b, 0, CB_SB_V + h)),
                  pl.BlockSpec((1, rows, LANES), lambda b, h, i: (b, i, CB_SB_G + h))],
        out_specs=pl.BlockSpec((1, rows, LANES), lambda b, h, i: (b, i, h)),
        scratch_shapes=[pltpu.VMEM((SB_QT * 2 * TQ, LANES), jnp.float32)],
        compiler_params=pltpu.CompilerParams(vmem_limit_bytes=VMEM_LIMIT),
        name="sb_attn",
    )(proj, proj, proj, proj)


def _df_kernel(q_ref, k_ref, v_ref, g_ref, slope_ref, lq1_ref, lk1_ref, lq2_ref, lk2_ref, sg_ref,
               qg_ref, kg_ref, o_ref, m_s, l_s, acc, *, lam_init):
    qi = pl.program_id(2)
    q = q_ref[0]
    lane = lax.broadcasted_iota(jnp.int32, (1, LANES), 1)
    first = lane < HEAD_DIM
    zero = jnp.zeros_like(q)
    qs = jnp.concatenate([jnp.where(first, q, zero), jnp.where(first, zero, q)], axis=0)
    slope = slope_ref[0] * LOG2E
    row = lax.broadcasted_iota(jnp.int32, (DF_TQ, DF_TK), 0)
    col = lax.broadcasted_iota(jnp.int32, (DF_TQ, DF_TK), 1)
    causal = col <= row
    diag_mask = jnp.concatenate([causal, causal], axis=0)

    qk_bound = (jnp.max(jnp.abs(qg_ref[...])) * jnp.max(jnp.abs(kg_ref[...]))
                * (math.sqrt(HEAD_DIM) * LOG2E * DF_BOUND_MARGIN))
    slope_s = jnp.max(slope)

    def load(first_kb, width):
        start = pl.multiple_of(first_kb * DF_TK, DF_TK)
        rel = lax.broadcasted_iota(jnp.int32, (1, width), 1).astype(jnp.float32)
        off = ((first_kb - qi) * DF_TK).astype(jnp.float32)
        bias = jnp.concatenate([slope] * (width // LANES), axis=1) * (rel + off)
        return k_ref[0, pl.ds(start, width), :], v_ref[0, pl.ds(start, width), :], bias

    def last_rel(first_kb, width):
        return ((first_kb - qi) * DF_TK + (width - 1)).astype(jnp.float32)

    def sweep(n_steps, first_kb_of_step, width, tile, needed):
        def cond(state):
            i, live = state
            return jnp.logical_and(i < n_steps, live)

        def body(state):
            i, _ = state
            tile(first_kb_of_step(i), width, None)
            return i + 1, needed(first_kb_of_step(i + 1), width)

        lax.while_loop(cond, body, (jnp.int32(0), needed(first_kb_of_step(jnp.int32(0)), width)))

    def sweep_all(tile, needed, diagonal=None):
        if diagonal is None:
            tile(qi, DF_TK, diag_mask)
        else:
            diagonal()
        rem = lax.rem(qi, DF_GROUP)
        sweep(rem, lambda i: qi - 1 - i, DF_TK, tile, needed)
        sweep(qi // DF_GROUP, lambda i: qi - rem - (i + 1) * DF_GROUP, DF_GROUP * DF_TK, tile, needed)

    def fixed_shift():
        rowpos = lax.broadcasted_iota(jnp.int32, (DF_TQ, LANES), 0).astype(jnp.float32)
        shift = qk_bound + slope * jnp.concatenate([rowpos, rowpos], axis=0)
        l_s[...] = jnp.zeros((2 * DF_TQ, LANES), jnp.float32)
        acc[...] = jnp.zeros((2 * DF_TQ, LANES), jnp.float32)

        def tile(first_kb, width, mask):
            reps = width // LANES
            kblk, vblk, bias = load(first_kb, width)
            s = (lax.dot_general(qs, kblk, _NT, preferred_element_type=jnp.float32)
                 + bias - jnp.concatenate([shift] * reps, axis=1))
            if mask is not None:
                s = jnp.where(mask, s, NEG_BIG)
            p = jnp.exp2(s)
            part = p[:, :LANES]
            for r in range(1, reps):
                part = part + p[:, r * LANES:(r + 1) * LANES]
            l_s[...] += part
            acc[...] += jnp.dot(p.astype(jnp.bfloat16), vblk, preferred_element_type=jnp.float32)

        def needed(first_kb, width):
            return slope_s * last_rel(first_kb, width) > DF_DEAD_LOG2

        part_rows = DF_TQ // DF_DIAG_PARTS
        hrow = lax.broadcasted_iota(jnp.int32, (part_rows, part_rows), 0)
        hcol = lax.broadcasted_iota(jnp.int32, (part_rows, part_rows), 1)
        tri = hcol <= hrow

        def diagonal_rows(r0, width, mask):
            half = part_rows
            both = lambda x: jnp.concatenate([x[r0:r0 + half], x[DF_TQ + r0:DF_TQ + r0 + half]], axis=0)
            reps = width // LANES
            start = pl.multiple_of(qi * DF_TK, DF_TK)
            kblk = k_ref[0, pl.ds(start, width), :]
            vblk = v_ref[0, pl.ds(start, width), :]
            rel = lax.broadcasted_iota(jnp.int32, (1, width), 1).astype(jnp.float32)
            bias = jnp.concatenate([slope] * reps, axis=1) * rel
            s = (lax.dot_general(both(qs), kblk, _NT, preferred_element_type=jnp.float32)
                 + bias - jnp.concatenate([both(shift)] * reps, axis=1))
            p = jnp.exp2(jnp.where(jnp.concatenate([mask, mask], axis=0), s, NEG_BIG))
            part = p[:, :LANES]
            for r in range(1, reps):
                part = part + p[:, r * LANES:(r + 1) * LANES]
            pv = jnp.dot(p.astype(jnp.bfloat16), vblk, preferred_element_type=jnp.float32)
            for m in range(2):
                rows = slice(m * DF_TQ + r0, m * DF_TQ + r0 + half)
                l_s[rows, :] += part[m * half:(m + 1) * half]
                acc[rows, :] += pv[m * half:(m + 1) * half]

        def diagonal():
            for i in range(DF_DIAG_PARTS):
                mask = jnp.concatenate([jnp.ones_like(tri)] * i + [tri], axis=1)
                diagonal_rows(i * part_rows, (i + 1) * part_rows, mask)

        sweep_all(tile, needed, diagonal)
        l = jnp.sum(l_s[...], axis=-1, keepdims=True)
        return l[:DF_TQ], l[DF_TQ:]

    def running_max():
        m_s[...] = jnp.full((2 * DF_TQ, LANES), NEG_BIG, jnp.float32)
        l_s[...] = jnp.zeros((2 * DF_TQ, LANES), jnp.float32)
        acc[...] = jnp.zeros((2 * DF_TQ, LANES), jnp.float32)

        def tile(first_kb, width, mask):
            reps = width // LANES
            kblk, vblk, bias = load(first_kb, width)
            s = lax.dot_general(qs, kblk, _NT, preferred_element_type=jnp.float32) + bias
            if mask is not None:
                s = jnp.where(mask, s, NEG_BIG)
            m_old = m_s[...]
            m_new = jnp.maximum(m_old, jnp.max(s, axis=-1, keepdims=True))
            p = jnp.exp2(s - jnp.concatenate([m_new] * reps, axis=1))
            alpha = jnp.exp2(m_old - m_new)
            l_s[...] = alpha * l_s[...] + jnp.sum(p, axis=-1, keepdims=True)
            acc[...] = alpha * acc[...] + jnp.dot(p.astype(jnp.bfloat16), vblk,
                                                  preferred_element_type=jnp.float32)
            m_s[...] = m_new

        def needed(first_kb, width):
            return (qk_bound + slope_s * last_rel(first_kb, width) - jnp.min(m_s[...])
                    > DF_DEAD_LOG2)

        sweep_all(tile, needed)
        return l_s[:DF_TQ, :1], l_s[DF_TQ:, :1]

    l1, l2 = lax.cond(qk_bound < DF_FIXED_SHIFT_MAX, fixed_shift, running_max)

    lam = (jnp.exp(jnp.sum(lq1_ref[...] * lk1_ref[...], keepdims=True))
           - jnp.exp(jnp.sum(lq2_ref[...] * lk2_ref[...], keepdims=True)) + lam_init)
    o = acc[:DF_TQ, :] / l1 - lam * (acc[DF_TQ:, :] / l2)
    ms = jnp.mean(o * o, axis=-1, keepdims=True)
    o = o * lax.rsqrt(ms + EPS) * sg_ref[...] * (1.0 - lam_init)
    g = g_ref[0].astype(jnp.float32)
    o_ref[0] = (o * _silu(g)).astype(o_ref.dtype)


def _df_attention(proj, slopes, lq1, lk1, lq2, lk2, subln_g, q_norm_g, k_norm_g, lam_init):
    bsz, s, _ = proj.shape
    assert DF_TQ == DF_TK and s % DF_TQ == 0
    nq = s // DF_TQ
    vec = pl.BlockSpec((1, HEAD_DIM), lambda b, h, i: (0, 0))
    stat = pltpu.VMEM((2 * DF_TQ, LANES), jnp.float32)
    return pl.pallas_call(
        functools.partial(_df_kernel, lam_init=lam_init),
        out_shape=jax.ShapeDtypeStruct((bsz, s, DF_WIDTH), jnp.bfloat16),
        grid=(bsz, DF_HEADS, nq),
        in_specs=[pl.BlockSpec((1, DF_TQ, LANES), lambda b, h, i: (b, i, CB_DF_Q + h)),
                  pl.BlockSpec((1, s, LANES), lambda b, h, i: (b, 0, CB_DF_K + h)),
                  pl.BlockSpec((1, s, LANES), lambda b, h, i: (b, 0, CB_DF_V + h)),
                  pl.BlockSpec((1, DF_TQ, LANES), lambda b, h, i: (b, i, CB_DF_G + h)),
                  pl.BlockSpec((1, 1, LANES), lambda b, h, i: (h, 0, 0)),
                  vec, vec, vec, vec,
                  pl.BlockSpec((1, LANES), lambda b, h, i: (0, 0)),
                  vec, vec],
        out_specs=pl.BlockSpec((1, DF_TQ, LANES), lambda b, h, i: (b, i, h)),
        scratch_shapes=[stat, stat, stat],
        compiler_params=pltpu.CompilerParams(vmem_limit_bytes=VMEM_LIMIT),
        name="df_attn",
    )(proj, proj, proj, proj, slopes, lq1, lk1, lq2, lk2, subln_g, q_norm_g, k_norm_g)


def _out_proj_kernel(x_ref, sb_ref, df_ref, w_ref, gate_ref, o_ref):
    y = (jnp.dot(sb_ref[0], w_ref[:SB_WIDTH, :], preferred_element_type=jnp.float32)
         + jnp.dot(df_ref[0], w_ref[SB_WIDTH:, :], preferred_element_type=jnp.float32))
    o_ref[0] = x_ref[0] + gate_ref[0] * y


def _out_proj(x, sb_o, df_o, w_out_bf16, gate):
    bsz, s, d = x.shape
    tm = ROWS_OUT
    return pl.pallas_call(
        _out_proj_kernel,
        out_shape=jax.ShapeDtypeStruct((bsz, s, d), jnp.float32),
        grid=(bsz, s // tm),
        in_specs=[pl.BlockSpec((1, tm, d), lambda b, i: (b, i, 0)),
                  pl.BlockSpec((1, tm, SB_WIDTH), lambda b, i: (b, i, 0)),
                  pl.BlockSpec((1, tm, DF_WIDTH), lambda b, i: (b, i, 0)),
                  pl.BlockSpec((SB_WIDTH + DF_WIDTH, d), lambda b, i: (0, 0)),
                  pl.BlockSpec((1, 1, d), lambda b, i: (b, 0, 0))],
        out_specs=pl.BlockSpec((1, tm, d), lambda b, i: (b, i, 0)),
        compiler_params=pltpu.CompilerParams(vmem_limit_bytes=VMEM_LIMIT),
        name="out_proj",
    )(x, sb_o, df_o, w_out_bf16, gate.reshape(bsz, 1, d))


def _layer(x, c, layer_idx, norm_g, w_ada, b_ada, w_in, q_norm_g, k_norm_g,
           lambda_q1, lambda_k1, lambda_q2, lambda_k2, subln_g, w_out):
    d = x.shape[-1]
    lam_init = 0.8 - 0.6 * math.exp(-0.3 * layer_idx)
    mod = _adaln_mod(c, w_ada, b_ada)
    shift, scale, gate = mod[:, :d], mod[:, d:2 * d], mod[:, 2 * d:]
    proj = _in_proj(x, norm_g, shift, scale, w_in.astype(jnp.bfloat16), q_norm_g, k_norm_g)
    sb_o = _sb_attention(proj)
    slopes = jnp.asarray([2.0 ** (-8.0 * (h + 1) / DF_HEADS) for h in range(DF_HEADS)], jnp.float32)
    slopes = jnp.broadcast_to(slopes[:, None, None], (DF_HEADS, 1, LANES))
    row = lambda v: v.reshape(1, -1)
    df_o = _df_attention(proj, slopes, row(lambda_q1), row(lambda_k1), row(lambda_q2),
                         row(lambda_k2), row(subln_g), row(q_norm_g), row(k_norm_g), lam_init)
    return _out_proj(x, sb_o, df_o, w_out.astype(jnp.bfloat16), gate)


@jax.jit
def kernel(x, c, norm_g, w_ada, b_ada, w_in, q_norm_g, k_norm_g, lambda_q1, lambda_k1,
           lambda_q2, lambda_k2, subln_g, w_out):
    for l in range(norm_g.shape[0]):
        x = _layer(x, c, l, norm_g[l], w_ada[l], b_ada[l], w_in[l], q_norm_g[l], k_norm_g[l],
                   lambda_q1[l], lambda_k1[l], lambda_q2[l], lambda_k2[l], subln_g[l], w_out[l])
    return x
```

```python
import functools
import math

import jax
import jax.numpy as jnp
from jax import lax
from jax.experimental import pallas as pl
from jax.experimental.pallas import tpu as pltpu

D_MODEL = 1024
SB_HEADS = 8
DF_HEADS = 4
HEAD_DIM = 64
LANES = 128
SB_WIDTH = SB_HEADS * HEAD_DIM
DF_WIDTH = DF_HEADS * 2 * HEAD_DIM
IN_COLS = 4 * SB_WIDTH + 4 * DF_WIDTH
EPS = 1e-6
NEG_BIG = -1e30

CB_SB_Q, CB_SB_K, CB_SB_V, CB_SB_G = 0, 4, 8, 12
CB_DF_Q, CB_DF_K, CB_DF_V, CB_DF_G = 16, 20, 24, 28

ROWS_PROJ = 512
ROWS_OUT = 1024
TQ = 256
TK = 256
DF_TQ = 1024
DF_TK = 1024
SB_QT = 4
SB_FIRST = 3
SB_GROUP = 4
DF_GROUP = 1
DF_DIAG_PARTS = 4
LOG2E = math.log2(math.e)
SB_DEAD_LOG2 = -150.0
DF_DEAD_LOG2 = -150.0
DF_FIXED_SHIFT_MAX = 48.0
DF_BOUND_MARGIN = 1.02
PROJ_CHUNK = 512
VMEM_LIMIT = 48 * 1024 * 1024

_NT = (((1,), (1,)), ((), ()))


def _silu(g):
    return g / (1.0 + jnp.exp(-g))


def _adaln_kernel(ct_ref, w_ref, b_ref, o_ref):
    w = w_ref[...]
    rows = []
    for b in range(o_ref.shape[0]):
        col = ct_ref[:, b:b + 1]
        rows.append(jnp.sum(col * w, axis=0, keepdims=True))
    o_ref[...] = jnp.concatenate(rows, axis=0) + b_ref[...]


def _adaln_mod(c, w_ada, b_ada):
    bsz, d = c.shape
    n = w_ada.shape[1]
    tn = 512
    return pl.pallas_call(
        _adaln_kernel,
        out_shape=jax.ShapeDtypeStruct((bsz, n), jnp.float32),
        grid=(n // tn,),
        in_specs=[pl.BlockSpec((d, bsz), lambda j: (0, 0)),
                  pl.BlockSpec((d, tn), lambda j: (0, j)),
                  pl.BlockSpec((1, tn), lambda j: (0, j))],
        out_specs=pl.BlockSpec((bsz, tn), lambda j: (0, j)),
        name="adaln_mod",
    )(c.T, w_ada, b_ada.reshape(1, n))


def _group_rmsnorm(y, gain):
    outs = []
    lane = lax.broadcasted_iota(jnp.int32, (1, LANES), 1)
    lo = lane < HEAD_DIM
    for j in range(y.shape[1] // LANES):
        blk = y[:, j * LANES:(j + 1) * LANES]
        sq = blk * blk
        s_lo = jnp.sum(jnp.where(lo, sq, 0.0), axis=-1, keepdims=True)
        s_hi = jnp.sum(jnp.where(lo, 0.0, sq), axis=-1, keepdims=True)
        ms = jnp.where(lo, s_lo, s_hi) * (1.0 / HEAD_DIM)
        outs.append(blk * lax.rsqrt(ms + EPS) * gain)
    return jnp.concatenate(outs, axis=1)


def _in_proj_kernel(x_ref, ng_ref, shift_ref, scale_ref, w_ref, qg_ref, kg_ref, o_ref):
    x = x_ref[0]
    ms = jnp.mean(x * x, axis=-1, keepdims=True)
    h = x * lax.rsqrt(ms + EPS) * ng_ref[...]
    h = (h * (1.0 + scale_ref[0]) + shift_ref[0]).astype(jnp.bfloat16)
    inv = LOG2E / math.sqrt(HEAD_DIM)
    for ci in range(IN_COLS // PROJ_CHUNK):
        c0 = ci * PROJ_CHUNK
        y = jnp.dot(h, w_ref[:, c0:c0 + PROJ_CHUNK], preferred_element_type=jnp.float32)
        cb = c0 // LANES
        if cb == CB_SB_Q:
            y = y * inv
        elif cb == CB_DF_Q:
            y = _group_rmsnorm(y, qg_ref[...]) * inv
        elif cb == CB_DF_K:
            y = _group_rmsnorm(y, kg_ref[...])
        o_ref[0, :, c0:c0 + PROJ_CHUNK] = y.astype(o_ref.dtype)


def _in_proj(x, norm_g, shift, scale, w_in_bf16, q_norm_g, k_norm_g):
    bsz, s, d = x.shape
    tm = ROWS_PROJ
    assert SB_WIDTH % PROJ_CHUNK == 0 and DF_WIDTH % PROJ_CHUNK == 0 and s % tm == 0
    qg = jnp.tile(q_norm_g.reshape(1, HEAD_DIM), (1, 2))
    kg = jnp.tile(k_norm_g.reshape(1, HEAD_DIM), (1, 2))
    return pl.pallas_call(
        _in_proj_kernel,
        out_shape=jax.ShapeDtypeStruct((bsz, s, IN_COLS), jnp.bfloat16),
        grid=(bsz, s // tm),
        in_specs=[pl.BlockSpec((1, tm, d), lambda b, i: (b, i, 0)),
                  pl.BlockSpec((1, d), lambda b, i: (0, 0)),
                  pl.BlockSpec((1, 1, d), lambda b, i: (b, 0, 0)),
                  pl.BlockSpec((1, 1, d), lambda b, i: (b, 0, 0)),
                  pl.BlockSpec((d, IN_COLS), lambda b, i: (0, 0)),
                  pl.BlockSpec((1, LANES), lambda b, i: (0, 0)),
                  pl.BlockSpec((1, LANES), lambda b, i: (0, 0))],
        out_specs=pl.BlockSpec((1, tm, IN_COLS), lambda b, i: (b, i, 0)),
        compiler_params=pltpu.CompilerParams(vmem_limit_bytes=VMEM_LIMIT),
        name="in_proj",
    )(x, norm_g.reshape(1, d), shift.reshape(bsz, 1, d), scale.reshape(bsz, 1, d),
      w_in_bf16, qg, kg)


def _sb_tile(qm, kblk, vblk, cum, carry, mask):
    z = lax.dot_general(qm, kblk, _NT, preferred_element_type=jnp.float32)
    nz = -z
    l1p = jnp.log(1.0 + jnp.exp2(jnp.minimum(z, nz))) * LOG2E
    lg = jnp.minimum(nz, 0.0) - l1p
    if mask is not None:
        lg = jnp.where(mask, lg, 0.0)
    rem = jnp.dot(lg.astype(jnp.bfloat16), cum, preferred_element_type=jnp.float32)
    a = jnp.exp2(lg + z + rem + carry)
    if mask is not None:
        a = jnp.where(mask, a, 0.0)
    o = jnp.dot(a.astype(jnp.bfloat16), vblk, preferred_element_type=jnp.float32)
    return o, carry + jnp.sum(lg, axis=-1, keepdims=True)


def _sb_kernel(q_ref, k_ref, v_ref, g_ref, o_ref, acc):
    step = pl.program_id(2)
    lane = lax.broadcasted_iota(jnp.int32, (1, LANES), 1)
    first = lane < HEAD_DIM
    row = lax.broadcasted_iota(jnp.int32, (TK, TK), 0)
    col = lax.broadcasted_iota(jnp.int32, (TK, TK), 1)
    cum = (row > col).astype(jnp.bfloat16)
    strict = col < row
    diag_mask = jnp.concatenate([strict, strict], axis=0)

    def stacked(u):
        q = q_ref[0, u * TQ:(u + 1) * TQ, :]
        zero = jnp.zeros_like(q)
        return jnp.concatenate([jnp.where(first, q, zero), jnp.where(first, zero, q)], axis=0)

    qs = [stacked(u) for u in range(SB_QT)]
    acc_of = [acc.at[u * 2 * TQ:(u + 1) * 2 * TQ, :] for u in range(SB_QT)]

    def tiles(qm, kbs, carry, masks):
        total = None
        for kb, mask in zip(kbs, masks):
            start = pl.multiple_of(kb * TK, TK)
            o, carry = _sb_tile(qm, k_ref[0, pl.ds(start, TK), :], v_ref[0, pl.ds(start, TK), :],
                                cum, carry, mask)
            total = o if total is None else total + o
        return total, carry

    zero_carry = jnp.zeros((2 * TQ, 1), jnp.float32)

    def first_block(first_step):
        carries = []
        for u in range(SB_QT):
            qt = step * SB_QT + u
            n = min(SB_FIRST, u + 1) if first_step else SB_FIRST
            o, carry = tiles(qs[u], [qt - t for t in range(n)], zero_carry,
                             [diag_mask] + [None] * (n - 1))
            acc_of[u][...] = o
            carries.append(carry)
        return tuple(carries)

    carries = lax.cond(step > 0, lambda: first_block(False), lambda: first_block(True))

    def alive(carry):
        return jnp.max(carry) > SB_DEAD_LOG2

    def sweep(u, n_steps, kbs_of_step, carry):
        def cond(state):
            i, _, live = state
            return jnp.logical_and(i < n_steps, live)

        def body(state):
            i, carry, _ = state
            kbs = kbs_of_step(i)
            o, carry = tiles(qs[u], kbs, carry, [None] * len(kbs))
            acc_of[u][...] += o
            return i + 1, carry, alive(carry)

        return lax.while_loop(cond, body, (jnp.int32(0), carry, alive(carry)))[1]

    any_alive = alive(carries[0])
    for carry in carries[1:]:
        any_alive = jnp.logical_or(any_alive, alive(carry))

    @pl.when(any_alive)
    def _():
        for u in range(SB_QT):
            left = jnp.maximum(step * SB_QT + u - (SB_FIRST - 1), 0)
            rem = lax.rem(left, SB_GROUP)
            carry = sweep(u, rem, lambda i, left=left: [left - 1 - i], carries[u])
            sweep(u, left // SB_GROUP,
                  lambda i, left=left, rem=rem: [left - rem - i * SB_GROUP - 1 - t
                                                 for t in range(SB_GROUP)], carry)

    for u in range(SB_QT):
        out = jnp.where(first, acc_of[u][:TQ, :], acc_of[u][TQ:, :])
        g = g_ref[0, u * TQ:(u + 1) * TQ, :].astype(jnp.float32)
        o_ref[0, u * TQ:(u + 1) * TQ, :] = (out * _silu(g)).astype(o_ref.dtype)


def _sb_attention(proj):
    bsz, s, _ = proj.shape
    rows = SB_QT * TQ
    assert TQ == TK and s % rows == 0 and SB_QT >= SB_FIRST - 1
    return pl.pallas_call(
        _sb_kernel,
        out_shape=jax.ShapeDtypeStruct((bsz, s, SB_WIDTH), jnp.bfloat16),
        grid=(bsz, SB_WIDTH // LANES, s // rows),
        in_specs=[pl.BlockSpec((1, rows, LANES), lambda b, h, i: (b, i, CB_SB_Q + h)),
                  pl.BlockSpec((1, s, LANES), lambda b, h, i: (b, 0, CB_SB_K + h)),
                  pl.BlockSpec((1, s, LANES), lambda b, h, i: (b, 0, CB_SB_V + h)),
                  pl.BlockSpec((1, rows, LANES), lambda b, h, i: (b, i, CB_SB_G + h))],
        out_specs=pl.BlockSpec((1, rows, LANES), lambda b, h, i: (b, i, h)),
        scratch_shapes=[pltpu.VMEM((SB_QT * 2 * TQ, LANES), jnp.float32)],
        compiler_params=pltpu.CompilerParams(vmem_limit_bytes=VMEM_LIMIT),
        name="sb_attn",
    )(proj, proj, proj, proj)


def _df_kernel(q_ref, k_ref, v_ref, g_ref, slope_ref, lq1_ref, lk1_ref, lq2_ref, lk2_ref, sg_ref,
               qg_ref, kg_ref, o_ref, m_s, l_s, acc, *, lam_init):
    qi = pl.program_id(2)
    q = q_ref[0]
    lane = lax.broadcasted_iota(jnp.int32, (1, LANES), 1)
    first = lane < HEAD_DIM
    zero = jnp.zeros_like(q)
    qs = jnp.concatenate([jnp.where(first, q, zero), jnp.where(first, zero, q)], axis=0)
    slope = slope_ref[0] * LOG2E
    row = lax.broadcasted_iota(jnp.int32, (DF_TQ, DF_TK), 0)
    col = lax.broadcasted_iota(jnp.int32, (DF_TQ, DF_TK), 1)
    causal = col <= row
    diag_mask = jnp.concatenate([causal, causal], axis=0)

    qk_bound = (jnp.max(jnp.abs(qg_ref[...])) * jnp.max(jnp.abs(kg_ref[...]))
                * (math.sqrt(HEAD_DIM) * LOG2E * DF_BOUND_MARGIN))
    slope_s = jnp.max(slope)

    def load(first_kb, width, key_off=0):
        start = pl.multiple_of(first_kb * DF_TK + key_off, DF_TK // 2)
        rel = lax.broadcasted_iota(jnp.int32, (1, width), 1).astype(jnp.float32)
        off = ((first_kb - qi) * DF_TK + key_off).astype(jnp.float32)
        bias = jnp.concatenate([slope] * (width // LANES), axis=1) * (rel + off)
        return k_ref[0, pl.ds(start, width), :], v_ref[0, pl.ds(start, width), :], bias

    def last_rel(first_kb, width):
        return ((first_kb - qi) * DF_TK + (width - 1)).astype(jnp.float32)

    def sweep(n_steps, first_kb_of_step, width, tile, needed):
        def cond(state):
            i, live = state
            return jnp.logical_and(i < n_steps, live)

        def body(state):
            i, _ = state
            tile(first_kb_of_step(i), width, None)
            return i + 1, needed(first_kb_of_step(i + 1), width)

        lax.while_loop(cond, body, (jnp.int32(0), needed(first_kb_of_step(jnp.int32(0)), width)))

    def sweep_all(tile, needed, diagonal=None, near_half_only=None):
        if diagonal is None:
            tile(qi, DF_TK, diag_mask)
        else:
            diagonal()
        left = qi
        if near_half_only is not None:
            @pl.when(jnp.logical_and(near_half_only, qi > 0))
            def _():
                tile(qi - 1, DF_TK // 2, None, DF_TK // 2)

            left = jnp.where(near_half_only, 0, qi)
        rem = lax.rem(left, DF_GROUP)
        sweep(rem, lambda i: qi - 1 - i, DF_TK, tile, needed)
        sweep(left // DF_GROUP, lambda i: qi - rem - (i + 1) * DF_GROUP, DF_GROUP * DF_TK, tile, needed)

    def fixed_shift():
        rowpos = lax.broadcasted_iota(jnp.int32, (DF_TQ, LANES), 0).astype(jnp.float32)
        shift = qk_bound + slope * jnp.concatenate([rowpos, rowpos], axis=0)
        l_s[...] = jnp.zeros((2 * DF_TQ, LANES), jnp.float32)
        acc[...] = jnp.zeros((2 * DF_TQ, LANES), jnp.float32)

        def tile(first_kb, width, mask, key_off=0):
            reps = width // LANES
            kblk, vblk, bias = load(first_kb, width, key_off)
            s = (lax.dot_general(qs, kblk, _NT, preferred_element_type=jnp.float32)
                 + bias - jnp.concatenate([shift] * reps, axis=1))
            if mask is not None:
                s = jnp.where(mask, s, NEG_BIG)
            p = jnp.exp2(s)
            part = p[:, :LANES]
            for r in range(1, reps):
                part = part + p[:, r * LANES:(r + 1) * LANES]
            l_s[...] += part
            acc[...] += jnp.dot(p.astype(jnp.bfloat16), vblk, preferred_element_type=jnp.float32)

        def needed(first_kb, width):
            return slope_s * last_rel(first_kb, width) > DF_DEAD_LOG2

        part_rows = DF_TQ // DF_DIAG_PARTS
        hrow = lax.broadcasted_iota(jnp.int32, (part_rows, part_rows), 0)
        hcol = lax.broadcasted_iota(jnp.int32, (part_rows, part_rows), 1)
        tri = hcol <= hrow

        def diagonal_rows(r0, width, mask):
            half = part_rows
            both = lambda x: jnp.concatenate([x[r0:r0 + half], x[DF_TQ + r0:DF_TQ + r0 + half]], axis=0)
            reps = width // LANES
            start = pl.multiple_of(qi * DF_TK, DF_TK)
            kblk = k_ref[0, pl.ds(start, width), :]
            vblk = v_ref[0, pl.ds(start, width), :]
            rel = lax.broadcasted_iota(jnp.int32, (1, width), 1).astype(jnp.float32)
            bias = jnp.concatenate([slope] * reps, axis=1) * rel
            s = (lax.dot_general(both(qs), kblk, _NT, preferred_element_type=jnp.float32)
                 + bias - jnp.concatenate([both(shift)] * reps, axis=1))
            p = jnp.exp2(jnp.where(jnp.concatenate([mask, mask], axis=0), s, NEG_BIG))
            part = p[:, :LANES]
            for r in range(1, reps):
                part = part + p[:, r * LANES:(r + 1) * LANES]
            pv = jnp.dot(p.astype(jnp.bfloat16), vblk, preferred_element_type=jnp.float32)
            for m in range(2):
                rows = slice(m * DF_TQ + r0, m * DF_TQ + r0 + half)
                l_s[rows, :] += part[m * half:(m + 1) * half]
                acc[rows, :] += pv[m * half:(m + 1) * half]

        def diagonal():
            for i in range(DF_DIAG_PARTS):
                mask = jnp.concatenate([jnp.ones_like(tri)] * i + [tri], axis=1)
                diagonal_rows(i * part_rows, (i + 1) * part_rows, mask)

        near_half_only = slope_s * (-(DF_TK // 2) - 1.0) <= DF_DEAD_LOG2
        sweep_all(tile, needed, diagonal, near_half_only)
        l = jnp.sum(l_s[...], axis=-1, keepdims=True)
        return l[:DF_TQ], l[DF_TQ:]

    def running_max():
        m_s[...] = jnp.full((2 * DF_TQ, LANES), NEG_BIG, jnp.float32)
        l_s[...] = jnp.zeros((2 * DF_TQ, LANES), jnp.float32)
        acc[...] = jnp.zeros((2 * DF_TQ, LANES), jnp.float32)

        def tile(first_kb, width, mask):
            reps = width // LANES
            kblk, vblk, bias = load(first_kb, width)
            s = lax.dot_general(qs, kblk, _NT, preferred_element_type=jnp.float32) + bias
            if mask is not None:
                s = jnp.where(mask, s, NEG_BIG)
            m_old = m_s[...]
            m_new = jnp.maximum(m_old, jnp.max(s, axis=-1, keepdims=True))
            p = jnp.exp2(s - jnp.concatenate([m_new] * reps, axis=1))
            alpha = jnp.exp2(m_old - m_new)
            l_s[...] = alpha * l_s[...] + jnp.sum(p, axis=-1, keepdims=True)
            acc[...] = alpha * acc[...] + jnp.dot(p.astype(jnp.bfloat16), vblk,
                                                  preferred_element_type=jnp.float32)
            m_s[...] = m_new

        def needed(first_kb, width):
            return (qk_bound + slope_s * last_rel(first_kb, width) - jnp.min(m_s[...])
                    > DF_DEAD_LOG2)

        sweep_all(tile, needed)
        return l_s[:DF_TQ, :1], l_s[DF_TQ:, :1]

    l1, l2 = lax.cond(qk_bound < DF_FIXED_SHIFT_MAX, fixed_shift, running_max)

    lam = (jnp.exp(jnp.sum(lq1_ref[...] * lk1_ref[...], keepdims=True))
           - jnp.exp(jnp.sum(lq2_ref[...] * lk2_ref[...], keepdims=True)) + lam_init)
    o = acc[:DF_TQ, :] / l1 - lam * (acc[DF_TQ:, :] / l2)
    ms = jnp.mean(o * o, axis=-1, keepdims=True)
    o = o * lax.rsqrt(ms + EPS) * sg_ref[...] * (1.0 - lam_init)
    g = g_ref[0].astype(jnp.float32)
    o_ref[0] = (o * _silu(g)).astype(o_ref.dtype)


def _df_attention(proj, slopes, lq1, lk1, lq2, lk2, subln_g, q_norm_g, k_norm_g, lam_init):
    bsz, s, _ = proj.shape
    assert DF_TQ == DF_TK and s % DF_TQ == 0
    nq = s // DF_TQ
    vec = pl.BlockSpec((1, HEAD_DIM), lambda b, h, i: (0, 0))
    stat = pltpu.VMEM((2 * DF_TQ, LANES), jnp.float32)
    return pl.pallas_call(
        functools.partial(_df_kernel, lam_init=lam_init),
        out_shape=jax.ShapeDtypeStruct((bsz, s, DF_WIDTH), jnp.bfloat16),
        grid=(bsz, DF_HEADS, nq),
        in_specs=[pl.BlockSpec((1, DF_TQ, LANES), lambda b, h, i: (b, i, CB_DF_Q + h)),
                  pl.BlockSpec((1, s, LANES), lambda b, h, i: (b, 0, CB_DF_K + h)),
                  pl.BlockSpec((1, s, LANES), lambda b, h, i: (b, 0, CB_DF_V + h)),
                  pl.BlockSpec((1, DF_TQ, LANES), lambda b, h, i: (b, i, CB_DF_G + h)),
                  pl.BlockSpec((1, 1, LANES), lambda b, h, i: (h, 0, 0)),
                  vec, vec, vec, vec,
                  pl.BlockSpec((1, LANES), lambda b, h, i: (0, 0)),
                  vec, vec],
        out_specs=pl.BlockSpec((1, DF_TQ, LANES), lambda b, h, i: (b, i, h)),
        scratch_shapes=[stat, stat, stat],
        compiler_params=pltpu.CompilerParams(vmem_limit_bytes=VMEM_LIMIT),
        name="df_attn",
    )(proj, proj, proj, proj, slopes, lq1, lk1, lq2, lk2, subln_g, q_norm_g, k_norm_g)


def _out_proj_kernel(x_ref, sb_ref, df_ref, w_ref, gate_ref, o_ref):
    y = (jnp.dot(sb_ref[0], w_ref[:SB_WIDTH, :], preferred_element_type=jnp.float32)
         + jnp.dot(df_ref[0], w_ref[SB_WIDTH:, :], preferred_element_type=jnp.float32))
    o_ref[0] = x_ref[0] + gate_ref[0] * y


def _out_proj(x, sb_o, df_o, w_out_bf16, gate):
    bsz, s, d = x.shape
    tm = ROWS_OUT
    return pl.pallas_call(
        _out_proj_kernel,
        out_shape=jax.ShapeDtypeStruct((bsz, s, d), jnp.float32),
        grid=(bsz, s // tm),
        in_specs=[pl.BlockSpec((1, tm, d), lambda b, i: (b, i, 0)),
                  pl.BlockSpec((1, tm, SB_WIDTH), lambda b, i: (b, i, 0)),
                  pl.BlockSpec((1, tm, DF_WIDTH), lambda b, i: (b, i, 0)),
                  pl.BlockSpec((SB_WIDTH + DF_WIDTH, d), lambda b, i: (0, 0)),
                  pl.BlockSpec((1, 1, d), lambda b, i: (b, 0, 0))],
        out_specs=pl.BlockSpec((1, tm, d), lambda b, i: (b, i, 0)),
        compiler_params=pltpu.CompilerParams(vmem_limit_bytes=VMEM_LIMIT),
        name="out_proj",
    )(x, sb_o, df_o, w_out_bf16, gate.reshape(bsz, 1, d))


def _layer(x, c, layer_idx, norm_g, w_ada, b_ada, w_in, q_norm_g, k_norm_g,
           lambda_q1, lambda_k1, lambda_q2, lambda_k2, subln_g, w_out):
    d = x.shape[-1]
    lam_init = 0.8 - 0.6 * math.exp(-0.3 * layer_idx)
    mod = _adaln_mod(c, w_ada, b_ada)
    shift, scale, gate = mod[:, :d], mod[:, d:2 * d], mod[:, 2 * d:]
    proj = _in_proj(x, norm_g, shift, scale, w_in.astype(jnp.bfloat16), q_norm_g, k_norm_g)
    sb_o = _sb_attention(proj)
    slopes = jnp.asarray([2.0 ** (-8.0 * (h + 1) / DF_HEADS) for h in range(DF_HEADS)], jnp.float32)
    slopes = jnp.broadcast_to(slopes[:, None, None], (DF_HEADS, 1, LANES))
    row = lambda v: v.reshape(1, -1)
    df_o = _df_attention(proj, slopes, row(lambda_q1), row(lambda_k1), row(lambda_q2),
                         row(lambda_k2), row(subln_g), row(q_norm_g), row(k_norm_g), lam_init)
    return _out_proj(x, sb_o, df_o, w_out.astype(jnp.bfloat16), gate)


@jax.jit
def kernel(x, c, norm_g, w_ada, b_ada, w_in, q_norm_g, k_norm_g, lambda_q1, lambda_k1,
           lambda_q2, lambda_k2, subln_g, w_out):
    for l in range(norm_g.shape[0]):
        x = _layer(x, c, l, norm_g[l], w_ada[l], b_ada[l], w_in[l], q_norm_g[l], k_norm_g[l],
                   lambda_q1[l], lambda_k1[l], lambda_q2[l], lambda_k2[l], subln_g[l], w_out[l])
    return x
```

```python
import functools
import math

import jax
import jax.numpy as jnp
from jax import lax
from jax.experimental import pallas as pl
from jax.experimental.pallas import tpu as pltpu

D_MODEL = 1024
SB_HEADS = 8
DF_HEADS = 4
HEAD_DIM = 64
LANES = 128
SB_WIDTH = SB_HEADS * HEAD_DIM
DF_WIDTH = DF_HEADS * 2 * HEAD_DIM
IN_COLS = 4 * SB_WIDTH + 4 * DF_WIDTH
EPS = 1e-6
NEG_BIG = -1e30

CB_SB_Q, CB_SB_K, CB_SB_V, CB_SB_G = 0, 4, 8, 12
CB_DF_Q, CB_DF_K, CB_DF_V, CB_DF_G = 16, 20, 24, 28

ROWS_PROJ = 512
ROWS_OUT = 1024
TQ = 256
TK = 256
DF_TQ = 1024
DF_TK = 1024
SB_QT = 4
SB_FIRST = 3
SB_GROUP = 4
DF_GROUP = 1
DF_DIAG_PARTS = 4
LOG2E = math.log2(math.e)
SB_DEAD_LOG2 = -150.0
DF_DEAD_LOG2 = -150.0
DF_FIXED_SHIFT_MAX = 48.0
DF_BOUND_MARGIN = 1.02
PROJ_CHUNK = 512
VMEM_LIMIT = 48 * 1024 * 1024

_NT = (((1,), (1,)), ((), ()))


def _silu(g):
    return g / (1.0 + jnp.exp(-g))


def _adaln_kernel(ct_ref, w_ref, b_ref, o_ref):
    w = w_ref[...]
    rows = []
    for b in range(o_ref.shape[0]):
        col = ct_ref[:, b:b + 1]
        rows.append(jnp.sum(col * w, axis=0, keepdims=True))
    o_ref[...] = jnp.concatenate(rows, axis=0) + b_ref[...]


def _adaln_mod(c, w_ada, b_ada):
    bsz, d = c.shape
    n = w_ada.shape[1]
    tn = 512
    return pl.pallas_call(
        _adaln_kernel,
        out_shape=jax.ShapeDtypeStruct((bsz, n), jnp.float32),
        grid=(n // tn,),
        in_specs=[pl.BlockSpec((d, bsz), lambda j: (0, 0)),
                  pl.BlockSpec((d, tn), lambda j: (0, j)),
                  pl.BlockSpec((1, tn), lambda j: (0, j))],
        out_specs=pl.BlockSpec((bsz, tn), lambda j: (0, j)),
        name="adaln_mod",
    )(c.T, w_ada, b_ada.reshape(1, n))


def _group_rmsnorm(y, gain):
    outs = []
    lane = lax.broadcasted_iota(jnp.int32, (1, LANES), 1)
    lo = lane < HEAD_DIM
    for j in range(y.shape[1] // LANES):
        blk = y[:, j * LANES:(j + 1) * LANES]
        sq = blk * blk
        s_lo = jnp.sum(jnp.where(lo, sq, 0.0), axis=-1, keepdims=True)
        s_hi = jnp.sum(jnp.where(lo, 0.0, sq), axis=-1, keepdims=True)
        ms = jnp.where(lo, s_lo, s_hi) * (1.0 / HEAD_DIM)
        outs.append(blk * lax.rsqrt(ms + EPS) * gain)
    return jnp.concatenate(outs, axis=1)


def _in_proj_kernel(x_ref, ng_ref, shift_ref, scale_ref, w_ref, qg_ref, kg_ref, o_ref):
    x = x_ref[0]
    ms = jnp.mean(x * x, axis=-1, keepdims=True)
    h = x * lax.rsqrt(ms + EPS) * ng_ref[...]
    h = (h * (1.0 + scale_ref[0]) + shift_ref[0]).astype(jnp.bfloat16)
    inv = LOG2E / math.sqrt(HEAD_DIM)
    for ci in range(IN_COLS // PROJ_CHUNK):
        c0 = ci * PROJ_CHUNK
        y = jnp.dot(h, w_ref[:, c0:c0 + PROJ_CHUNK], preferred_element_type=jnp.float32)
        cb = c0 // LANES
        if cb == CB_SB_Q:
            y = y * inv
        elif cb == CB_DF_Q:
            y = _group_rmsnorm(y, qg_ref[...]) * inv
        elif cb == CB_DF_K:
            y = _group_rmsnorm(y, kg_ref[...])
        o_ref[0, :, c0:c0 + PROJ_CHUNK] = y.astype(o_ref.dtype)


def _in_proj(x, norm_g, shift, scale, w_in_bf16, q_norm_g, k_norm_g):
    bsz, s, d = x.shape
    tm = ROWS_PROJ
    assert SB_WIDTH % PROJ_CHUNK == 0 and DF_WIDTH % PROJ_CHUNK == 0 and s % tm == 0
    qg = jnp.tile(q_norm_g.reshape(1, HEAD_DIM), (1, 2))
    kg = jnp.tile(k_norm_g.reshape(1, HEAD_DIM), (1, 2))
    return pl.pallas_call(
        _in_proj_kernel,
        out_shape=jax.ShapeDtypeStruct((bsz, s, IN_COLS), jnp.bfloat16),
        grid=(bsz, s // tm),
        in_specs=[pl.BlockSpec((1, tm, d), lambda b, i: (b, i, 0)),
                  pl.BlockSpec((1, d), lambda b, i: (0, 0)),
                  pl.BlockSpec((1, 1, d), lambda b, i: (b, 0, 0)),
                  pl.BlockSpec((1, 1, d), lambda b, i: (b, 0, 0)),
                  pl.BlockSpec((d, IN_COLS), lambda b, i: (0, 0)),
                  pl.BlockSpec((1, LANES), lambda b, i: (0, 0)),
                  pl.BlockSpec((1, LANES), lambda b, i: (0, 0))],
        out_specs=pl.BlockSpec((1, tm, IN_COLS), lambda b, i: (b, i, 0)),
        compiler_params=pltpu.CompilerParams(vmem_limit_bytes=VMEM_LIMIT,
                                             allow_input_fusion=[False, False, False, False, True, False, False]),
        name="in_proj",
    )(x, norm_g.reshape(1, d), shift.reshape(bsz, 1, d), scale.reshape(bsz, 1, d),
      w_in_bf16, qg, kg)


def _sb_tile(qm, kblk, vblk, cum, carry, mask):
    z = lax.dot_general(qm, kblk, _NT, preferred_element_type=jnp.float32)
    nz = -z
    l1p = jnp.log(1.0 + jnp.exp2(jnp.minimum(z, nz))) * LOG2E
    lg = jnp.minimum(nz, 0.0) - l1p
    if mask is not None:
        lg = jnp.where(mask, lg, 0.0)
    rem = jnp.dot(lg.astype(jnp.bfloat16), cum, preferred_element_type=jnp.float32)
    a = jnp.exp2(lg + z + rem + carry)
    if mask is not None:
        a = jnp.where(mask, a, 0.0)
    o = jnp.dot(a.astype(jnp.bfloat16), vblk, preferred_element_type=jnp.float32)
    return o, carry + jnp.sum(lg, axis=-1, keepdims=True)


def _sb_kernel(q_ref, k_ref, v_ref, g_ref, o_ref, acc):
    step = pl.program_id(2)
    lane = lax.broadcasted_iota(jnp.int32, (1, LANES), 1)
    first = lane < HEAD_DIM
    row = lax.broadcasted_iota(jnp.int32, (TK, TK), 0)
    col = lax.broadcasted_iota(jnp.int32, (TK, TK), 1)
    cum = (row > col).astype(jnp.bfloat16)
    strict = col < row
    diag_mask = jnp.concatenate([strict, strict], axis=0)

    def stacked(u):
        q = q_ref[0, u * TQ:(u + 1) * TQ, :]
        zero = jnp.zeros_like(q)
        return jnp.concatenate([jnp.where(first, q, zero), jnp.where(first, zero, q)], axis=0)

    qs = [stacked(u) for u in range(SB_QT)]
    acc_of = [acc.at[u * 2 * TQ:(u + 1) * 2 * TQ, :] for u in range(SB_QT)]

    def tiles(qm, kbs, carry, masks):
        total = None
        for kb, mask in zip(kbs, masks):
            start = pl.multiple_of(kb * TK, TK)
            o, carry = _sb_tile(qm, k_ref[0, pl.ds(start, TK), :], v_ref[0, pl.ds(start, TK), :],
                                cum, carry, mask)
            total = o if total is None else total + o
        return total, carry

    zero_carry = jnp.zeros((2 * TQ, 1), jnp.float32)

    def first_block(first_step):
        carries = []
        for u in range(SB_QT):
            qt = step * SB_QT + u
            n = min(SB_FIRST, u + 1) if first_step else SB_FIRST
            o, carry = tiles(qs[u], [qt - t for t in range(n)], zero_carry,
                             [diag_mask] + [None] * (n - 1))
            acc_of[u][...] = o
            carries.append(carry)
        return tuple(carries)

    carries = lax.cond(step > 0, lambda: first_block(False), lambda: first_block(True))

    def alive(carry):
        return jnp.max(carry) > SB_DEAD_LOG2

    def sweep(u, n_steps, kbs_of_step, carry):
        def cond(state):
            i, _, live = state
            return jnp.logical_and(i < n_steps, live)

        def body(state):
            i, carry, _ = state
            kbs = kbs_of_step(i)
            o, carry = tiles(qs[u], kbs, carry, [None] * len(kbs))
            acc_of[u][...] += o
            return i + 1, carry, alive(carry)

        return lax.while_loop(cond, body, (jnp.int32(0), carry, alive(carry)))[1]

    any_alive = alive(carries[0])
    for carry in carries[1:]:
        any_alive = jnp.logical_or(any_alive, alive(carry))

    @pl.when(any_alive)
    def _():
        for u in range(SB_QT):
            left = jnp.maximum(step * SB_QT + u - (SB_FIRST - 1), 0)
            rem = lax.rem(left, SB_GROUP)
            carry = sweep(u, rem, lambda i, left=left: [left - 1 - i], carries[u])
            sweep(u, left // SB_GROUP,
                  lambda i, left=left, rem=rem: [left - rem - i * SB_GROUP - 1 - t
                                                 for t in range(SB_GROUP)], carry)

    for u in range(SB_QT):
        out = jnp.where(first, acc_of[u][:TQ, :], acc_of[u][TQ:, :])
        g = g_ref[0, u * TQ:(u + 1) * TQ, :].astype(jnp.float32)
        o_ref[0, u * TQ:(u + 1) * TQ, :] = (out * _silu(g)).astype(o_ref.dtype)


def _sb_attention(proj):
    bsz, s, _ = proj.shape
    rows = SB_QT * TQ
    assert TQ == TK and s % rows == 0 and SB_QT >= SB_FIRST - 1
    return pl.pallas_call(
        _sb_kernel,
        out_shape=jax.ShapeDtypeStruct((bsz, s, SB_WIDTH), jnp.bfloat16),
        grid=(bsz, SB_WIDTH // LANES, s // rows),
        in_specs=[pl.BlockSpec((1, rows, LANES), lambda b, h, i: (b, i, CB_SB_Q + h)),
                  pl.BlockSpec((1, s, LANES), lambda b, h, i: (b, 0, CB_SB_K + h)),
                  pl.BlockSpec((1, s, LANES), lambda b, h, i: (b, 0, CB_SB_V + h)),
                  pl.BlockSpec((1, rows, LANES), lambda b, h, i: (b, i, CB_SB_G + h))],
        out_specs=pl.BlockSpec((1, rows, LANES), lambda b, h, i: (b, i, h)),
        scratch_shapes=[pltpu.VMEM((SB_QT * 2 * TQ, LANES), jnp.float32)],
        compiler_params=pltpu.CompilerParams(vmem_limit_bytes=VMEM_LIMIT),
        name="sb_attn",
    )(proj, proj, proj, proj)


def _df_kernel(q_ref, k_ref, v_ref, g_ref, slope_ref, lq1_ref, lk1_ref, lq2_ref, lk2_ref, sg_ref,
               qg_ref, kg_ref, o_ref, m_s, l_s, acc, *, lam_init):
    qi = pl.program_id(2)
    q = q_ref[0]
    lane = lax.broadcasted_iota(jnp.int32, (1, LANES), 1)
    first = lane < HEAD_DIM
    zero = jnp.zeros_like(q)
    qs = jnp.concatenate([jnp.where(first, q, zero), jnp.where(first, zero, q)], axis=0)
    slope = slope_ref[0] * LOG2E
    row = lax.broadcasted_iota(jnp.int32, (DF_TQ, DF_TK), 0)
    col = lax.broadcasted_iota(jnp.int32, (DF_TQ, DF_TK), 1)
    causal = col <= row
    diag_mask = jnp.concatenate([causal, causal], axis=0)

    qk_bound = (jnp.max(jnp.abs(qg_ref[...])) * jnp.max(jnp.abs(kg_ref[...]))
                * (math.sqrt(HEAD_DIM) * LOG2E * DF_BOUND_MARGIN))
    slope_s = jnp.max(slope)

    def load(first_kb, width, key_off=0):
        start = pl.multiple_of(first_kb * DF_TK + key_off, DF_TK // 2)
        rel = lax.broadcasted_iota(jnp.int32, (1, width), 1).astype(jnp.float32)
        off = ((first_kb - qi) * DF_TK + key_off).astype(jnp.float32)
        bias = jnp.concatenate([slope] * (width // LANES), axis=1) * (rel + off)
        return k_ref[0, pl.ds(start, width), :], v_ref[0, pl.ds(start, width), :], bias

    def last_rel(first_kb, width):
        return ((first_kb - qi) * DF_TK + (width - 1)).astype(jnp.float32)

    def sweep(n_steps, first_kb_of_step, width, tile, needed):
        def cond(state):
            i, live = state
            return jnp.logical_and(i < n_steps, live)

        def body(state):
            i, _ = state
            tile(first_kb_of_step(i), width, None)
            return i + 1, needed(first_kb_of_step(i + 1), width)

        lax.while_loop(cond, body, (jnp.int32(0), needed(first_kb_of_step(jnp.int32(0)), width)))

    def sweep_all(tile, needed, diagonal=None, near_half_only=None):
        if diagonal is None:
            tile(qi, DF_TK, diag_mask)
        else:
            diagonal()
        left = qi
        if near_half_only is not None:
            @pl.when(jnp.logical_and(near_half_only, qi > 0))
            def _():
                tile(qi - 1, DF_TK // 2, None, DF_TK // 2)

            left = jnp.where(near_half_only, 0, qi)
        rem = lax.rem(left, DF_GROUP)
        sweep(rem, lambda i: qi - 1 - i, DF_TK, tile, needed)
        sweep(left // DF_GROUP, lambda i: qi - rem - (i + 1) * DF_GROUP, DF_GROUP * DF_TK, tile, needed)

    def fixed_shift():
        rowpos = lax.broadcasted_iota(jnp.int32, (DF_TQ, LANES), 0).astype(jnp.float32)
        shift = qk_bound + slope * jnp.concatenate([rowpos, rowpos], axis=0)
        l_s[...] = jnp.zeros((2 * DF_TQ, LANES), jnp.float32)
        acc[...] = jnp.zeros((2 * DF_TQ, LANES), jnp.float32)

        def tile(first_kb, width, mask, key_off=0):
            reps = width // LANES
            kblk, vblk, bias = load(first_kb, width, key_off)
            s = (lax.dot_general(qs, kblk, _NT, preferred_element_type=jnp.float32)
                 + bias - jnp.concatenate([shift] * reps, axis=1))
            if mask is not None:
                s = jnp.where(mask, s, NEG_BIG)
            p = jnp.exp2(s)
            part = p[:, :LANES]
            for r in range(1, reps):
                part = part + p[:, r * LANES:(r + 1) * LANES]
            l_s[...] += part
            acc[...] += jnp.dot(p.astype(jnp.bfloat16), vblk, preferred_element_type=jnp.float32)

        def needed(first_kb, width):
            return slope_s * last_rel(first_kb, width) > DF_DEAD_LOG2

        part_rows = DF_TQ // DF_DIAG_PARTS
        hrow = lax.broadcasted_iota(jnp.int32, (part_rows, part_rows), 0)
        hcol = lax.broadcasted_iota(jnp.int32, (part_rows, part_rows), 1)
        tri = hcol <= hrow

        def diagonal_rows(r0, width, mask):
            half = part_rows
            both = lambda x: jnp.concatenate([x[r0:r0 + half], x[DF_TQ + r0:DF_TQ + r0 + half]], axis=0)
            reps = width // LANES
            start = pl.multiple_of(qi * DF_TK, DF_TK)
            kblk = k_ref[0, pl.ds(start, width), :]
            vblk = v_ref[0, pl.ds(start, width), :]
            rel = lax.broadcasted_iota(jnp.int32, (1, width), 1).astype(jnp.float32)
            bias = jnp.concatenate([slope] * reps, axis=1) * rel
            s = (lax.dot_general(both(qs), kblk, _NT, preferred_element_type=jnp.float32)
                 + bias - jnp.concatenate([both(shift)] * reps, axis=1))
            p = jnp.exp2(jnp.where(jnp.concatenate([mask, mask], axis=0), s, NEG_BIG))
            part = p[:, :LANES]
            for r in range(1, reps):
                part = part + p[:, r * LANES:(r + 1) * LANES]
            pv = jnp.dot(p.astype(jnp.bfloat16), vblk, preferred_element_type=jnp.float32)
            for m in range(2):
                rows = slice(m * DF_TQ + r0, m * DF_TQ + r0 + half)
                l_s[rows, :] += part[m * half:(m + 1) * half]
                acc[rows, :] += pv[m * half:(m + 1) * half]

        def diagonal():
            for i in range(DF_DIAG_PARTS):
                mask = jnp.concatenate([jnp.ones_like(tri)] * i + [tri], axis=1)
                diagonal_rows(i * part_rows, (i + 1) * part_rows, mask)

        near_half_only = slope_s * (-(DF_TK // 2) - 1.0) <= DF_DEAD_LOG2
        sweep_all(tile, needed, diagonal, near_half_only)
        l = jnp.sum(l_s[...], axis=-1, keepdims=True)
        return l[:DF_TQ], l[DF_TQ:]

    def running_max():
        m_s[...] = jnp.full((2 * DF_TQ, LANES), NEG_BIG, jnp.float32)
        l_s[...] = jnp.zeros((2 * DF_TQ, LANES), jnp.float32)
        acc[...] = jnp.zeros((2 * DF_TQ, LANES), jnp.float32)

        def tile(first_kb, width, mask):
            reps = width // LANES
            kblk, vblk, bias = load(first_kb, width)
            s = lax.dot_general(qs, kblk, _NT, preferred_element_type=jnp.float32) + bias
            if mask is not None:
                s = jnp.where(mask, s, NEG_BIG)
            m_old = m_s[...]
            m_new = jnp.maximum(m_old, jnp.max(s, axis=-1, keepdims=True))
            p = jnp.exp2(s - jnp.concatenate([m_new] * reps, axis=1))
            alpha = jnp.exp2(m_old - m_new)
            l_s[...] = alpha * l_s[...] + jnp.sum(p, axis=-1, keepdims=True)
            acc[...] = alpha * acc[...] + jnp.dot(p.astype(jnp.bfloat16), vblk,
                                                  preferred_element_type=jnp.float32)
            m_s[...] = m_new

        def needed(first_kb, width):
            return (qk_bound + slope_s * last_rel(first_kb, width) - jnp.min(m_s[...])
                    > DF_DEAD_LOG2)

        sweep_all(tile, needed)
        return l_s[:DF_TQ, :1], l_s[DF_TQ:, :1]

    l1, l2 = lax.cond(qk_bound < DF_FIXED_SHIFT_MAX, fixed_shift, running_max)

    lam = (jnp.exp(jnp.sum(lq1_ref[...] * lk1_ref[...], keepdims=True))
           - jnp.exp(jnp.sum(lq2_ref[...] * lk2_ref[...], keepdims=True)) + lam_init)
    o = acc[:DF_TQ, :] / l1 - lam * (acc[DF_TQ:, :] / l2)
    ms = jnp.mean(o * o, axis=-1, keepdims=True)
    o = o * lax.rsqrt(ms + EPS) * sg_ref[...] * (1.0 - lam_init)
    g = g_ref[0].astype(jnp.float32)
    o_ref[0] = (o * _silu(g)).astype(o_ref.dtype)


def _df_attention(proj, slopes, lq1, lk1, lq2, lk2, subln_g, q_norm_g, k_norm_g, lam_init):
    bsz, s, _ = proj.shape
    assert DF_TQ == DF_TK and s % DF_TQ == 0
    nq = s // DF_TQ
    vec = pl.BlockSpec((1, HEAD_DIM), lambda b, h, i: (0, 0))
    stat = pltpu.VMEM((2 * DF_TQ, LANES), jnp.float32)
    return pl.pallas_call(
        functools.partial(_df_kernel, lam_init=lam_init),
        out_shape=jax.ShapeDtypeStruct((bsz, s, DF_WIDTH), jnp.bfloat16),
        grid=(bsz, DF_HEADS, nq),
        in_specs=[pl.BlockSpec((1, DF_TQ, LANES), lambda b, h, i: (b, i, CB_DF_Q + h)),
                  pl.BlockSpec((1, s, LANES), lambda b, h, i: (b, 0, CB_DF_K + h)),
                  pl.BlockSpec((1, s, LANES), lambda b, h, i: (b, 0, CB_DF_V + h)),
                  pl.BlockSpec((1, DF_TQ, LANES), lambda b, h, i: (b, i, CB_DF_G + h)),
                  pl.BlockSpec((1, 1, LANES), lambda b, h, i: (h, 0, 0)),
                  vec, vec, vec, vec,
                  pl.BlockSpec((1, LANES), lambda b, h, i: (0, 0)),
                  vec, vec],
        out_specs=pl.BlockSpec((1, DF_TQ, LANES), lambda b, h, i: (b, i, h)),
        scratch_shapes=[stat, stat, stat],
        compiler_params=pltpu.CompilerParams(vmem_limit_bytes=VMEM_LIMIT),
        name="df_attn",
    )(proj, proj, proj, proj, slopes, lq1, lk1, lq2, lk2, subln_g, q_norm_g, k_norm_g)


def _out_proj_kernel(x_ref, sb_ref, df_ref, w_ref, gate_ref, o_ref):
    y = (jnp.dot(sb_ref[0], w_ref[:SB_WIDTH, :], preferred_element_type=jnp.float32)
         + jnp.dot(df_ref[0], w_ref[SB_WIDTH:, :], preferred_element_type=jnp.float32))
    o_ref[0] = x_ref[0] + gate_ref[0] * y


def _out_proj(x, sb_o, df_o, w_out_bf16, gate):
    bsz, s, d = x.shape
    tm = ROWS_OUT
    return pl.pallas_call(
        _out_proj_kernel,
        out_shape=jax.ShapeDtypeStruct((bsz, s, d), jnp.float32),
        grid=(bsz, s // tm),
        in_specs=[pl.BlockSpec((1, tm, d), lambda b, i: (b, i, 0)),
                  pl.BlockSpec((1, tm, SB_WIDTH), lambda b, i: (b, i, 0)),
                  pl.BlockSpec((1, tm, DF_WIDTH), lambda b, i: (b, i, 0)),
                  pl.BlockSpec((SB_WIDTH + DF_WIDTH, d), lambda b, i: (0, 0)),
                  pl.BlockSpec((1, 1, d), lambda b, i: (b, 0, 0))],
        out_specs=pl.BlockSpec((1, tm, d), lambda b, i: (b, i, 0)),
        compiler_params=pltpu.CompilerParams(vmem_limit_bytes=VMEM_LIMIT),
        name="out_proj",
    )(x, sb_o, df_o, w_out_bf16, gate.reshape(bsz, 1, d))


def _layer(x, c, layer_idx, norm_g, w_ada, b_ada, w_in, q_norm_g, k_norm_g,
           lambda_q1, lambda_k1, lambda_q2, lambda_k2, subln_g, w_out):
    d = x.shape[-1]
    lam_init = 0.8 - 0.6 * math.exp(-0.3 * layer_idx)
    mod = _adaln_mod(c, w_ada, b_ada)
    shift, scale, gate = mod[:, :d], mod[:, d:2 * d], mod[:, 2 * d:]
    proj = _in_proj(x, norm_g, shift, scale, w_in.astype(jnp.bfloat16), q_norm_g, k_norm_g)
    sb_o = _sb_attention(proj)
    slopes = jnp.asarray([2.0 ** (-8.0 * (h + 1) / DF_HEADS) for h in range(DF_HEADS)], jnp.float32)
    slopes = jnp.broadcast_to(slopes[:, None, None], (DF_HEADS, 1, LANES))
    row = lambda v: v.reshape(1, -1)
    df_o = _df_attention(proj, slopes, row(lambda_q1), row(lambda_k1), row(lambda_q2),
                         row(lambda_k2), row(subln_g), row(q_norm_g), row(k_norm_g), lam_init)
    return _out_proj(x, sb_o, df_o, w_out.astype(jnp.bfloat16), gate)


@jax.jit
def kernel(x, c, norm_g, w_ada, b_ada, w_in, q_norm_g, k_norm_g, lambda_q1, lambda_k1,
           lambda_q2, lambda_k2, subln_g, w_out):
    for l in range(norm_g.shape[0]):
        x = _layer(x, c, l, norm_g[l], w_ada[l], b_ada[l], w_in[l], q_norm_g[l], k_norm_g[l],
                   lambda_q1[l], lambda_k1[l], lambda_q2[l], lambda_k2[l], subln_g[l], w_out[l])
    return x
```
